```python
import math
import jax, jax.numpy as jnp
from jax import lax
import numpy as np

D_MODEL = 1024
BATCH = 4
SEQ = 8192
DEPTH = 2

GRID_W = 64
Q_BLOCK = 128
RET_CHUNK = 128
NORM_EPS = 1e-6

DIFF_HEADS = 4
DIFF_HEAD_DIM = 64
DIFF_V_DIM = 2 * DIFF_HEAD_DIM
DIFF_WIDTH = DIFF_HEADS * DIFF_V_DIM
PARTIAL_ROT_DIM = DIFF_HEAD_DIM // 4
PARTIAL_ROPE_THETA = 500000.0

RET_HEADS = 4
RET_KEY_DIM = 128
RET_VAL_DIM = 128
RET_WIDTH = RET_HEADS * RET_VAL_DIM
RET_ROPE_THETA = 10000.0

GQA_Q_HEADS = 8
GQA_KV_HEADS = 2
GQA_GROUP = GQA_Q_HEADS // GQA_KV_HEADS
GQA_HEAD_DIM = 64
GQA_WIDTH = GQA_Q_HEADS * GQA_HEAD_DIM
AXIAL_DIM = GQA_HEAD_DIM // 2
AXIAL_ROPE_THETA = 10000.0

N_BRANCHES = 3
BRANCH_WIDTH = 512
D_FF = 4 * D_MODEL

IN_SPLITS = (
    DIFF_HEADS * 2 * DIFF_HEAD_DIM,
    DIFF_HEADS * 2 * DIFF_HEAD_DIM,
    DIFF_WIDTH,
    RET_HEADS * RET_KEY_DIM,
    RET_HEADS * RET_KEY_DIM,
    RET_WIDTH,
    RET_WIDTH,
    GQA_Q_HEADS * GQA_HEAD_DIM,
    GQA_KV_HEADS * GQA_HEAD_DIM,
    GQA_KV_HEADS * GQA_HEAD_DIM,
    N_BRANCHES * D_MODEL,
)
IN_COLS = 7424

kernel_name = "gated_parallel_diffattn_retention_axialgqa_encoder"


def rms_norm(x, gain, eps=NORM_EPS):
    xf = x.astype(jnp.float32)
    y = xf * lax.rsqrt(jnp.mean(xf * xf, axis=-1, keepdims=True) + eps)
    return (y * gain.astype(jnp.float32)).astype(x.dtype)


def rope_cos_sin(pos, dim, theta):
    inv_freq = theta ** (-jnp.arange(0, dim, 2, dtype=jnp.float32) / dim)
    ang = pos[:, None] * inv_freq[None, :]
    return jnp.cos(ang), jnp.sin(ang)


def apply_rope(x, cos_sin):
    cos, sin = cos_sin
    n = x.shape[-1] // 2
    shape = (1, cos.shape[0]) + (1,) * (x.ndim - 3) + (n,)
    c = cos.reshape(shape).astype(x.dtype)
    s = sin.reshape(shape).astype(x.dtype)
    x1, x2 = x[..., :n], x[..., n:]
    return jnp.concatenate([x1 * c - x2 * s, x2 * c + x1 * s], axis=-1)


def split_cols(y, sizes):
    out, start = [], 0
    for n in sizes:
        out.append(y[..., start:start + n])
        start += n
    return out


def sweep_query_blocks(fn, q):
    B, S = q.shape[:2]
    nb = S // Q_BLOCK
    qb = jnp.moveaxis(q.reshape((B, nb, Q_BLOCK) + q.shape[2:]), 1, 0)
    ob = lax.map(fn, qb)
    return jnp.moveaxis(ob, 0, 1).reshape((B, S) + ob.shape[3:])


def differential_attention(q, k, v, lam):
    scale = DIFF_HEAD_DIM ** -0.5

    def block(qb):
        s = jnp.einsum('bqhmd,bkhmd->bhmqk', qb, k).astype(jnp.float32) * scale
        p = jax.nn.softmax(s, axis=-1)
        w = (p[:, :, 0] - lam * p[:, :, 1]).astype(v.dtype)
        return jnp.einsum('bhqk,bkhe->bqhe', w, v)

    return sweep_query_blocks(block, q)


def grouped_query_attention(q, k, v):
    scale = GQA_HEAD_DIM ** -0.5

    def block(qb):
        s = jnp.einsum('bqgrd,bkgd->bgrqk', qb, k).astype(jnp.float32) * scale
        p = jax.nn.softmax(s, axis=-1).astype(v.dtype)
        return jnp.einsum('bgrqk,bkgd->bqgrd', p, v)

    return sweep_query_blocks(block, q)


def bidirectional_retention(q, k, v, log_fwd, log_bwd):
    B, S, H, dk = q.shape
    dv = v.shape[-1]
    C = RET_CHUNK
    N = S // C
    q = q.reshape(B, N, C, H, dk)
    k = k.reshape(B, N, C, H, dk)
    v = v.reshape(B, N, C, H, dv)
    i = jnp.arange(C, dtype=jnp.float32)
    diff = i[:, None] - i[None, :]
    decay_intra = jnp.where(
        diff[None] >= 0,
        jnp.exp(jnp.maximum(diff, 0.0)[None] * log_fwd[:, None, None]),
        jnp.exp(jnp.maximum(-diff, 0.0)[None] * log_bwd[:, None, None]))
    scores = jnp.einsum('bnihd,bnjhd->bnhij', q, k) * decay_intra
    out = jnp.einsum('bnhij,bnjhe->bnihe', scores, v)

    zeta_f = jnp.exp((C - 1 - i)[None] * log_fwd[:, None])
    xi_f = jnp.exp((i + 1)[None] * log_fwd[:, None])
    zeta_b = jnp.exp(i[None] * log_bwd[:, None])
    xi_b = jnp.exp((C - i)[None] * log_bwd[:, None])
    kv_f = jnp.einsum('bnjhd,bnjhe,hj->nbhde', k, v, zeta_f)
    kv_b = jnp.einsum('bnjhd,bnjhe,hj->nbhde', k, v, zeta_b)
    dec_f = jnp.exp(C * log_fwd)[None, :, None, None]
    dec_b = jnp.exp(C * log_bwd)[None, :, None, None]

    def chunk_states(kv, decay, reverse):
        def step(state, kv_n):
            return state * decay + kv_n, state
        _, states = lax.scan(step, jnp.zeros_like(kv[0]), kv, reverse=reverse)
        return states

    s_f = chunk_states(kv_f, dec_f, False)
    s_b = chunk_states(kv_b, dec_b, True)
    out = (out
           + jnp.einsum('bnihd,nbhde,hi->bnihe', q, s_f, xi_f)
           + jnp.einsum('bnihd,nbhde,hi->bnihe', q, s_b, xi_b))
    return out.reshape(B, S, H, dv)


def head_group_norm(x, gain, eps=1e-5):
    xf = x.astype(jnp.float32)
    mu = jnp.mean(xf, axis=-1, keepdims=True)
    var = jnp.mean(jnp.square(xf - mu), axis=-1, keepdims=True)
    y = ((xf - mu) * lax.rsqrt(var + eps)).reshape(x.shape[0], x.shape[1], -1)
    return y * gain.astype(jnp.float32)


def setup_inputs(seed: int = 0) -> dict:
    key = jax.random.key(seed)
    ks = jax.random.split(key, 21)
    L, D = DEPTH, D_MODEL
    f32 = jnp.float32

    def normal(k, shape, scale):
        return jax.random.normal(k, shape, f32) * scale

    def gain(k, shape):
        return 1.0 + 0.02 * jax.random.normal(k, shape, f32)

    base_log_decay = jnp.log(-jnp.log1p(-(2.0 ** (-5.0 - jnp.arange(RET_HEADS, dtype=f32)))))
    return {
        "x": jax.random.normal(ks[0], (BATCH, SEQ, D), f32),
        "attn_norm": gain(ks[1], (L, D)),
        "w_in": normal(ks[2], (L, D, IN_COLS), D ** -0.5),
        "diff_q_norm": gain(ks[3], (L, DIFF_HEAD_DIM)),
        "diff_k_norm": gain(ks[4], (L, DIFF_HEAD_DIM)),
        "diff_lam_q1": normal(ks[5], (L, DIFF_HEAD_DIM), 0.1),
        "diff_lam_k1": normal(ks[6], (L, DIFF_HEAD_DIM), 0.1),
        "diff_lam_q2": normal(ks[7], (L, DIFF_HEAD_DIM), 0.1),
        "diff_lam_k2": normal(ks[8], (L, DIFF_HEAD_DIM), 0.1),
        "diff_subln": gain(ks[9], (L, DIFF_V_DIM)),
        "ret_decay_fwd": base_log_decay[None] + normal(ks[10], (L, RET_HEADS), 0.1),
        "ret_decay_bwd": base_log_decay[None] + normal(ks[11], (L, RET_HEADS), 0.1),
        "ret_group_norm": gain(ks[12], (L, RET_WIDTH)),
        "gqa_q_norm": gain(ks[13], (L, GQA_HEAD_DIM)),
        "gqa_k_norm": gain(ks[14], (L, GQA_HEAD_DIM)),
        "w_branch": normal(ks[15], (L, N_BRANCHES, BRANCH_WIDTH, D), BRANCH_WIDTH ** -0.5),
        "w_out": normal(ks[16], (L, D, D), D ** -0.5),
        "mlp_norm": gain(ks[17], (L, D)),
        "w_mlp_in": normal(ks[18], (L, D, D_FF), D ** -0.5),
        "w_mlp_out": normal(ks[19], (L, D_FF, D), D_FF ** -0.5),
    }


def reference(x, attn_norm, w_in, diff_q_norm, diff_k_norm, diff_lam_q1, diff_lam_k1,
              diff_lam_q2, diff_lam_k2, diff_subln, ret_decay_fwd, ret_decay_bwd,
              ret_group_norm, gqa_q_norm, gqa_k_norm, w_branch, w_out, mlp_norm,
              w_mlp_in, w_mlp_out):
    B, S, D = x.shape
    f32 = jnp.float32
    rows = S // GRID_W
    seq_pos = jnp.arange(S, dtype=f32)
    row_pos = jnp.repeat(jnp.arange(rows, dtype=f32), GRID_W)
    col_pos = jnp.tile(jnp.arange(GRID_W, dtype=f32), rows)
    rope_partial = rope_cos_sin(seq_pos, PARTIAL_ROT_DIM, PARTIAL_ROPE_THETA)
    rope_ret = rope_cos_sin(seq_pos, RET_KEY_DIM, RET_ROPE_THETA)
    rope_row = rope_cos_sin(row_pos, AXIAL_DIM, AXIAL_ROPE_THETA)
    rope_col = rope_cos_sin(col_pos, AXIAL_DIM, AXIAL_ROPE_THETA)

    def partial_rope(t):
        return jnp.concatenate([apply_rope(t[..., :PARTIAL_ROT_DIM], rope_partial),
                                t[..., PARTIAL_ROT_DIM:]], axis=-1)

    def axial_rope(t):
        return jnp.concatenate([apply_rope(t[..., :AXIAL_DIM], rope_row),
                                apply_rope(t[..., AXIAL_DIM:], rope_col)], axis=-1)

    h = x
    for l in range(DEPTH):
        u = rms_norm(h, attn_norm[l])
        proj = jnp.einsum('bsd,dc->bsc', u, w_in[l])
        (dq, dk, dv, rq, rk, rv, rg, gq, gk, gv, gate_logits) = split_cols(proj, IN_SPLITS)

        lambda_init = 0.8 - 0.6 * math.exp(-0.3 * l)
        lam = (jnp.exp(jnp.sum(diff_lam_q1[l].astype(f32) * diff_lam_k1[l].astype(f32)))
               - jnp.exp(jnp.sum(diff_lam_q2[l].astype(f32) * diff_lam_k2[l].astype(f32)))
               + lambda_init)
        dq = partial_rope(rms_norm(dq.reshape(B, S, DIFF_HEADS, 2, DIFF_HEAD_DIM), diff_q_norm[l]))
        dk = partial_rope(rms_norm(dk.reshape(B, S, DIFF_HEADS, 2, DIFF_HEAD_DIM), diff_k_norm[l]))
        dv = dv.reshape(B, S, DIFF_HEADS, DIFF_V_DIM)
        a = differential_attention(dq, dk, dv, lam)
        a = (rms_norm(a, diff_subln[l], eps=1e-5) * (1.0 - lambda_init)).reshape(B, S, DIFF_WIDTH)

        rq = apply_rope(rq.reshape(B, S, RET_HEADS, RET_KEY_DIM), rope_ret).astype(f32)
        rk = (apply_rope(rk.reshape(B, S, RET_HEADS, RET_KEY_DIM), rope_ret).astype(f32)
              * (RET_KEY_DIM ** -0.5))
        rv = rv.reshape(B, S, RET_HEADS, RET_VAL_DIM).astype(f32)
        log_fwd = -jnp.exp(ret_decay_fwd[l].astype(f32))
        log_bwd = -jnp.exp(ret_decay_bwd[l].astype(f32))
        r = bidirectional_retention(rq, rk, rv, log_fwd, log_bwd)
        r = (jax.nn.silu(rg.astype(f32)) * head_group_norm(r, ret_group_norm[l])).astype(h.dtype)

        gq = axial_rope(rms_norm(gq.reshape(B, S, GQA_KV_HEADS, GQA_GROUP, GQA_HEAD_DIM), gqa_q_norm[l]))
        gk = axial_rope(rms_norm(gk.reshape(B, S, GQA_KV_HEADS, GQA_HEAD_DIM), gqa_k_norm[l]))
        gv = gv.reshape(B, S, GQA_KV_HEADS, GQA_HEAD_DIM)
        c = grouped_query_attention(gq, gk, gv).reshape(B, S, GQA_WIDTH)

        branches = jnp.stack([a, r, c], axis=2)
        projected = jnp.einsum('bsnw,nwd->bsnd', branches, w_branch[l])
        gates = jax.nn.sigmoid(gate_logits.reshape(B, S, N_BRANCHES, D))
        merged = jnp.sum(gates * projected, axis=2)
        h = h + jnp.einsum('bsd,de->bse', merged, w_out[l])

        m = jnp.einsum('bsd,df->bsf', rms_norm(h, mlp_norm[l]), w_mlp_in[l])
        m = jnp.square(jax.nn.relu(m))
        h = h + jnp.einsum('bsf,fd->bsd', m, w_mlp_out[l])
    return h
```

```python
import functools
import math

import jax
import jax.numpy as jnp
from jax import lax
from jax.experimental import pallas as pl
from jax.experimental.pallas import tpu as pltpu

F32 = jnp.float32
BF16 = jnp.bfloat16

D_MODEL = 1024
DEPTH = 2
GRID_W = 64
NORM_EPS = 1e-6
HEAD64 = 64
DIFF_HEADS = 4
RET_HEADS = 4
RET_CHUNK = 128
GQA_Q_HEADS = 8
GQA_GROUP = 4
N_BRANCHES = 3
BRANCH_WIDTH = 512
D_FF = 4 * D_MODEL
ATTN_COLS = 4352
LANES = 128

TM = 512
TK = 512
NQ = 1024
NC = 256
VMEM_LIMIT = 56 * 1024 * 1024
LOG2E = 1.4426950408889634


def _cparams(n_axes):
    return pltpu.CompilerParams(dimension_semantics=("arbitrary",) * n_axes,
                                vmem_limit_bytes=VMEM_LIMIT)


def _resident(shape):
    nd = len(shape)
    return pl.BlockSpec(shape, lambda *_: (0,) * nd, pipeline_mode=pl.Buffered(1))


def _proj_kernel(h_ref, an_ref, w_ref, g_ref, nrm_ref,
                 cd_ref, ad_ref, bd_ref, cg_ref, ag_ref, bg_ref, cr_ref, sr_ref,
                 u_ref, dqT_ref, dk_ref, dvT_ref, rq_ref, rk_ref, rv_ref, rg_ref,
                 gqT_ref, gk_ref, gvT_ref):
    x = h_ref[...]
    ms = jnp.mean(x * x, axis=-1, keepdims=True)
    u = (x * lax.rsqrt(ms + NORM_EPS) * an_ref[...]).astype(BF16)
    u_ref[...] = u

    def proj(c0, n):
        return jnp.dot(u, w_ref[:, c0:c0 + n], preferred_element_type=F32)

    gmat = g_ref[...]
    lo = lax.broadcasted_iota(jnp.int32, (1, LANES), 1) < HEAD64

    def qk_norm(y, gain):
        msq = jnp.dot((y * y).astype(BF16), gmat, preferred_element_type=F32)
        return y * lax.rsqrt(msq + NORM_EPS) * gain

    def rope(y, c, a, b, d):
        return y * c + pltpu.roll(y, LANES - d, 1) * a + pltpu.roll(y, d, 1) * b

    qscale = HEAD64 ** -0.5 * LOG2E
    cd, ad, bd = cd_ref[...], ad_ref[...], bd_ref[...]
    cg, ag, bg = cg_ref[...], ag_ref[...], bg_ref[...]
    cr, sr = cr_ref[...], sr_ref[...]

    p = proj(0, 512)
    for hd in range(DIFF_HEADS):
        y = rope(qk_norm(p[:, hd * LANES:(hd + 1) * LANES], nrm_ref[0:1, :]), cd, ad, bd, 8) * qscale
        dqT_ref[2 * hd] = jnp.where(lo, y, 0.0).T.astype(BF16)
        dqT_ref[2 * hd + 1] = jnp.where(lo, 0.0, y).T.astype(BF16)
    p = proj(512, 512)
    for hd in range(DIFF_HEADS):
        y = rope(qk_norm(p[:, hd * LANES:(hd + 1) * LANES], nrm_ref[1:2, :]), cd, ad, bd, 8)
        dk_ref[:, hd * LANES:(hd + 1) * LANES] = y.astype(BF16)
    p = proj(1024, 512)
    for hd in range(DIFF_HEADS):
        dvT_ref[hd] = p[:, hd * LANES:(hd + 1) * LANES].T.astype(BF16)

    p = proj(1536, 512)
    for hd in range(RET_HEADS):
        y = p[:, hd * LANES:(hd + 1) * LANES]
        rq_ref[:, hd * LANES:(hd + 1) * LANES] = (y * cr + pltpu.roll(y, 64, 1) * sr).astype(BF16)
    p = proj(2048, 512)
    kscale = LANES ** -0.5
    for hd in range(RET_HEADS):
        y = p[:, hd * LANES:(hd + 1) * LANES]
        rk_ref[:, hd * LANES:(hd + 1) * LANES] = (
            (y * cr + pltpu.roll(y, 64, 1) * sr) * kscale).astype(BF16)
    rv_ref[...] = proj(2560, 512).astype(BF16)
    rg_ref[...] = proj(3072, 512).astype(BF16)

    p = proj(3584, 512)
    for c in range(4):
        y = rope(qk_norm(p[:, c * LANES:(c + 1) * LANES], nrm_ref[2:3, :]), cg, ag, bg, 16) * qscale
        ysw = pltpu.roll(y, 64, 1)
        if c < 2:
            z_even, z_odd = jnp.where(lo, y, 0.0), jnp.where(lo, ysw, 0.0)
        else:
            z_even, z_odd = jnp.where(lo, 0.0, ysw), jnp.where(lo, 0.0, y)
        for hh, z in ((2 * c, z_even), (2 * c + 1, z_odd)):
            zt = z.T.astype(BF16)
            for qb in range(TM // LANES):
                gqT_ref[qb, :, hh * LANES:(hh + 1) * LANES] = zt[:, qb * LANES:(qb + 1) * LANES]
    p = proj(4096, 256)
    gk_ref[...] = rope(qk_norm(p[:, 0:LANES], nrm_ref[3:4, :]), cg, ag, bg, 16).astype(BF16)
    gvT_ref[...] = p[:, LANES:2 * LANES].T.astype(BF16)


def _proj_call(h, attn_norm, w_attn, gmat, nrm, tabs):
    T = h.shape[0]
    S = tabs[0].shape[0]
    nt = T // TM
    tab_spec = pl.BlockSpec((TM, LANES), lambda i: (i % (S // TM), 0))
    row = lambda w: pl.BlockSpec((TM, w), lambda i: (i, 0))
    out_shape = (
        jax.ShapeDtypeStruct((T, D_MODEL), BF16),
        jax.ShapeDtypeStruct((nt, 8, LANES, TM), BF16),
        jax.ShapeDtypeStruct((T, 512), BF16),
        jax.ShapeDtypeStruct((nt, 4, LANES, TM), BF16),
        jax.ShapeDtypeStruct((T, 512), BF16),
        jax.ShapeDtypeStruct((T, 512), BF16),
        jax.ShapeDtypeStruct((T, 512), BF16),
        jax.ShapeDtypeStruct((T, 512), BF16),
        jax.ShapeDtypeStruct((T // LANES, LANES, NQ), BF16),
        jax.ShapeDtypeStruct((T, LANES), BF16),
        jax.ShapeDtypeStruct((nt, LANES, TM), BF16),
    )
    out_specs = (
        row(D_MODEL),
        pl.BlockSpec((None, 8, LANES, TM), lambda i: (i, 0, 0, 0)),
        row(512),
        pl.BlockSpec((None, 4, LANES, TM), lambda i: (i, 0, 0, 0)),
        row(512), row(512), row(512), row(512),
        pl.BlockSpec((TM // LANES, LANES, NQ), lambda i: (i, 0, 0)),
        row(LANES),
        pl.BlockSpec((None, LANES, TM), lambda i: (i, 0, 0)),
    )
    in_specs = [row(D_MODEL), _resident((1, D_MODEL)), _resident(w_attn.shape),
                _resident(gmat.shape), _resident(nrm.shape)] + [tab_spec] * 8
    return pl.pallas_call(
        _proj_kernel, grid=(nt,), in_specs=in_specs, out_specs=out_specs,
        out_shape=out_shape, compiler_params=_cparams(1), name="proj",
    )(h, attn_norm, w_attn, gmat, nrm, *tabs)


def _attn_core(get_q, k_ref, vT_ref, m_ref, l_ref, acc_ref):
    m_ref[...] = jnp.full(m_ref.shape, -1e30, F32)
    l_ref[...] = jnp.zeros(l_ref.shape, F32)
    acc_ref[...] = jnp.zeros(acc_ref.shape, F32)
    n_steps = k_ref.shape[0] // TK

    def body(j, carry):
        k = k_ref[pl.ds(pl.multiple_of(j * TK, TK), TK), :]
        vT = vT_ref[j]
        for c in range(NQ // NC):
            cs = slice(c * NC, (c + 1) * NC)
            s = jnp.dot(k, get_q(c), preferred_element_type=F32)
            m_old = m_ref[:, cs]
            m_new = jnp.maximum(m_old, jnp.max(s, axis=0, keepdims=True))
            alpha = jnp.exp2(m_old - m_new)
            p = jnp.exp2(s - m_new)
            l_ref[:, cs] = alpha * l_ref[:, cs] + jnp.sum(p, axis=0, keepdims=True)
            pv = jnp.dot(vT, p.astype(BF16), preferred_element_type=F32)
            acc_ref[:, cs] = acc_ref[:, cs] * alpha + pv
            m_ref[:, cs] = m_new
        return carry

    lax.fori_loop(0, n_steps, body, 0)


def _diff_kernel(lam_ref, sub_ref, qT_ref, k_ref, vT_ref, o_ref, m_ref, l_ref, acc_ref, *,
                 lambda_init):
    def get_q(c):
        half = c % 2
        return qT_ref[c // 2, :, half * NC:(half + 1) * NC]

    _attn_core(get_q, k_ref, vT_ref, m_ref, l_ref, acc_ref)

    lv = lam_ref[...]
    lam = (jnp.exp(jnp.sum(lv[0:1] * lv[1:2], axis=-1, keepdims=True))
           - jnp.exp(jnp.sum(lv[2:3] * lv[3:4], axis=-1, keepdims=True)) + lambda_init)
    tq = NQ // 2
    o1 = acc_ref[:, 0:tq] / l_ref[:, 0:tq]
    o2 = acc_ref[:, tq:NQ] / l_ref[:, tq:NQ]
    d = o1 - lam * o2
    ms = jnp.mean(d * d, axis=0, keepdims=True)
    dn = d * lax.rsqrt(ms + 1e-5)
    o_ref[...] = (dn.T * (sub_ref[...] * (1.0 - lambda_init))).astype(o_ref.dtype)


def _diff_call(lamv, subln, dqT, dk, dvT, B, S, lambda_init):
    T = dk.shape[0]
    tq = NQ // 2
    nq = S // tq
    kern = functools.partial(_diff_kernel, lambda_init=lambda_init)
    return pl.pallas_call(
        kern, grid=(B, DIFF_HEADS, nq),
        in_specs=[
            _resident(lamv.shape), _resident(subln.shape),
            pl.BlockSpec((None, 2, LANES, tq), lambda b, h, i: (b * nq + i, h, 0, 0)),
            pl.BlockSpec((S, LANES), lambda b, h, i: (b, h)),
            pl.BlockSpec((S // TM, None, LANES, TM), lambda b, h, i: (b, h, 0, 0)),
        ],
        out_specs=pl.BlockSpec((tq, LANES), lambda b, h, i: (b * nq + i, h)),
        out_shape=jax.ShapeDtypeStruct((T, 512), BF16),
        scratch_shapes=[pltpu.VMEM((1, NQ), F32), pltpu.VMEM((1, NQ), F32),
                        pltpu.VMEM((LANES, NQ), F32)],
        compiler_params=_cparams(3), name="diff_attn",
    )(lamv, subln, dqT, dk, dvT)


def _gqa_kernel(qT_ref, k_ref, vT_ref, o_ref, m_ref, l_ref, acc_ref):
    _attn_core(lambda c: qT_ref[:, c * NC:(c + 1) * NC], k_ref, vT_ref, m_ref, l_ref, acc_ref)
    for c in range(GQA_Q_HEADS // 2):
        g = (2 * c) // GQA_GROUP
        rows = slice(g * HEAD64, (g + 1) * HEAD64)
        parts = []
        for hh in (2 * c, 2 * c + 1):
            cols = slice(hh * LANES, (hh + 1) * LANES)
            parts.append(acc_ref[rows, cols] / l_ref[:, cols])
        o_ref[:, c * LANES:(c + 1) * LANES] = jnp.concatenate(parts, axis=0).T.astype(o_ref.dtype)


def _gqa_call(gqT, gk, gvT, B, S):
    T = gk.shape[0]
    tq = NQ // GQA_Q_HEADS
    nq = S // tq
    return pl.pallas_call(
        _gqa_kernel, grid=(B, nq),
        in_specs=[
            pl.BlockSpec((None, LANES, NQ), lambda b, i: (b * nq + i, 0, 0)),
            pl.BlockSpec((S, LANES), lambda b, i: (b, 0)),
            pl.BlockSpec((S // TM, LANES, TM), lambda b, i: (b, 0, 0)),
        ],
        out_specs=pl.BlockSpec((tq, 512), lambda b, i: (b * nq + i, 0)),
        out_shape=jax.ShapeDtypeStruct((T, 512), BF16),
        scratch_shapes=[pltpu.VMEM((1, NQ), F32), pltpu.VMEM((1, NQ), F32),
                        pltpu.VMEM((LANES, NQ), F32)],
        compiler_params=_cparams(2), name="gqa_attn",
    )(gqT, gk, gvT)


def _ret_kernel(dec_ref, gain_ref, q_ref, k_ref, v_ref, g_ref, o_ref, of_ref, st_ref):
    C = RET_CHUNK
    hd = pl.program_id(1)
    lf = -jnp.exp(jnp.full((C, C), dec_ref[0, hd], F32))
    lb = -jnp.exp(jnp.full((C, C), dec_ref[1, hd], F32))
    ii = lax.broadcasted_iota(jnp.int32, (C, C), 0).astype(F32)
    jj = lax.broadcasted_iota(jnp.int32, (C, C), 1).astype(F32)
    diff = ii - jj
    decay = jnp.where(diff >= 0, jnp.exp(jnp.maximum(diff, 0.0) * lf),
                      jnp.exp(jnp.maximum(-diff, 0.0) * lb))
    xi_f, zeta_f = jnp.exp((ii + 1.0) * lf), jnp.exp((C - 1.0 - ii) * lf)
    xi_b, zeta_b = jnp.exp((C - ii) * lb), jnp.exp(ii * lb)
    dec_f, dec_b = jnp.exp(C * lf), jnp.exp(C * lb)
    n_chunks = q_ref.shape[0] // C
    nt = (((1,), (1,)), ((), ()))
    tn = (((0,), (0,)), ((), ()))

    def chunk(n):
        rows = pl.ds(pl.multiple_of(n * C, C), C)
        return rows, q_ref[rows, :], k_ref[rows, :], v_ref[rows, :]

    def state_update(k, v, zeta, dec):
        vz = (v.astype(F32) * zeta).astype(BF16)
        kv = lax.dot_general(k, vz, tn, preferred_element_type=F32)
        st_ref[...] = st_ref[...] * dec + kv

    def fwd(n, carry):
        rows, q, k, v = chunk(n)
        a = lax.dot_general(q, k, nt, preferred_element_type=F32) * decay
        o = jnp.dot(a.astype(BF16), v, preferred_element_type=F32)
        o += jnp.dot(q, st_ref[...].astype(BF16), preferred_element_type=F32) * xi_f
        of_ref[rows, :] = o
        state_update(k, v, zeta_f, dec_f)
        return carry

    def bwd(t, carry):
        rows, q, k, v = chunk(n_chunks - 1 - t)
        o = of_ref[rows, :] + jnp.dot(q, st_ref[...].astype(BF16), preferred_element_type=F32) * xi_b
        state_update(k, v, zeta_b, dec_b)
        mu = jnp.mean(o, axis=-1, keepdims=True)
        oc = o - mu
        var = jnp.mean(oc * oc, axis=-1, keepdims=True)
        y = oc * lax.rsqrt(var + 1e-5) * gain_ref[...]
        g = g_ref[rows, :].astype(F32)
        o_ref[rows, :] = (g / (1.0 + jnp.exp(-g)) * y).astype(o_ref.dtype)
        return carry

    st_ref[...] = jnp.zeros(st_ref.shape, F32)
    lax.fori_loop(0, n_chunks, fwd, 0)
    st_ref[...] = jnp.zeros(st_ref.shape, F32)
    lax.fori_loop(0, n_chunks, bwd, 0)


def _ret_call(dec, gain, rq, rk, rv, rg, B, S):
    T = rq.shape[0]
    blk = pl.BlockSpec((S, LANES), lambda b, h: (b, h))
    return pl.pallas_call(
        _ret_kernel, grid=(B, RET_HEADS),
        in_specs=[pl.BlockSpec(memory_space=pltpu.SMEM),
                  pl.BlockSpec((1, LANES), lambda b, h: (0, h)), blk, blk, blk, blk],
        out_specs=blk,
        out_shape=jax.ShapeDtypeStruct((T, 512), BF16),
        scratch_shapes=[pltpu.VMEM((S, LANES), F32), pltpu.VMEM((RET_CHUNK, LANES), F32)],
        compiler_params=_cparams(2), name="retention",
    )(dec, gain, rq, rk, rv, rg)


def _merge_kernel(h_ref, u_ref, a_ref, r_ref, c_ref, wg_ref, wb_ref, wo_ref, mn_ref,
                  hn_ref, u2_ref):
    u = u_ref[...]
    merged = None
    for n, br in enumerate((a_ref, r_ref, c_ref)):
        logits = jnp.dot(u, wg_ref[:, n * D_MODEL:(n + 1) * D_MODEL], preferred_element_type=F32)
        gate = 1.0 / (1.0 + jnp.exp(-logits))
        term = gate * jnp.dot(br[...], wb_ref[n], preferred_element_type=F32)
        merged = term if merged is None else merged + term
    hn = h_ref[...] + jnp.dot(merged.astype(BF16), wo_ref[...], preferred_element_type=F32)
    hn_ref[...] = hn
    ms = jnp.mean(hn * hn, axis=-1, keepdims=True)
    u2_ref[...] = (hn * lax.rsqrt(ms + NORM_EPS) * mn_ref[...]).astype(BF16)


def _merge_call(h, u, a, r, c, wg, wb, wo, mlp_norm):
    T = h.shape[0]
    row = lambda w: pl.BlockSpec((TM, w), lambda i: (i, 0))
    return pl.pallas_call(
        _merge_kernel, grid=(T // TM,),
        in_specs=[row(D_MODEL), row(D_MODEL), row(512), row(512), row(512),
                  _resident(wg.shape), _resident(wb.shape), _resident(wo.shape),
                  _resident((1, D_MODEL))],
        out_specs=(row(D_MODEL), row(D_MODEL)),
        out_shape=(jax.ShapeDtypeStruct((T, D_MODEL), F32),
                   jax.ShapeDtypeStruct((T, D_MODEL), BF16)),
        compiler_params=_cparams(1), name="merge",
    )(h, u, a, r, c, wg, wb, wo, mlp_norm)


def _mlp_kernel(h_ref, u_ref, w1_ref, w2_ref, o_ref):
    u = u_ref[...]
    acc = h_ref[...]
    for c in range(D_FF // D_MODEL):
        cs = slice(c * D_MODEL, (c + 1) * D_MODEL)
        m = jnp.maximum(jnp.dot(u, w1_ref[:, cs], preferred_element_type=F32), 0.0)
        acc = acc + jnp.dot((m * m).astype(BF16), w2_ref[cs, :], preferred_element_type=F32)
    o_ref[...] = acc


def _mlp_call(h, u2, w1, w2):
    T = h.shape[0]
    row = pl.BlockSpec((TM, D_MODEL), lambda i: (i, 0))
    return pl.pallas_call(
        _mlp_kernel, grid=(T // TM,),
        in_specs=[row, row, _resident(w1.shape), _resident(w2.shape)],
        out_specs=row,
        out_shape=jax.ShapeDtypeStruct((T, D_MODEL), F32),
        compiler_params=_cparams(1), name="mlp",
    )(h, u2, w1, w2)


def _rope_tables(S):
    pos = jnp.arange(S, dtype=F32)
    lane = jnp.arange(LANES)
    l64 = lane % 64

    inv = 500000.0 ** (-jnp.arange(0, 16, 2, dtype=F32) / 16)
    ang = pos[:, None] * inv[None, :]
    cos, sin = jnp.cos(ang)[:, l64 % 8], jnp.sin(ang)[:, l64 % 8]
    cd = jnp.where(l64 < 16, cos, 1.0)
    ad = jnp.where(l64 < 8, -sin, 0.0)
    bd = jnp.where((l64 >= 8) & (l64 < 16), sin, 0.0)

    inv = 10000.0 ** (-jnp.arange(0, 32, 2, dtype=F32) / 32)
    row_pos = jnp.floor(pos / GRID_W)
    col_pos = pos - row_pos * GRID_W
    p2 = jnp.where((l64 < 32)[None, :], row_pos[:, None], col_pos[:, None])
    ang = p2 * inv[l64 % 16][None, :]
    cg = jnp.cos(ang)
    ag = jnp.where(l64 % 32 < 16, -jnp.sin(ang), 0.0)
    bg = jnp.where(l64 % 32 >= 16, jnp.sin(ang), 0.0)

    inv = 10000.0 ** (-jnp.arange(0, 128, 2, dtype=F32) / 128)
    ang = pos[:, None] * inv[l64][None, :]
    cr = jnp.cos(ang)
    sr = jnp.where(lane < 64, -jnp.sin(ang), jnp.sin(ang))
    return (cd, ad, bd, cg, ag, bg, cr, sr)


def kernel(x, attn_norm, w_in, diff_q_norm, diff_k_norm, diff_lam_q1, diff_lam_k1, diff_lam_q2, diff_lam_k2, diff_subln, ret_decay_fwd, ret_decay_bwd, ret_group_norm, gqa_q_norm, gqa_k_norm, w_branch, w_out, mlp_norm, w_mlp_in, w_mlp_out):
    B, S, D = x.shape
    T = B * S
    tabs = _rope_tables(S)
    blk = jnp.arange(LANES) // HEAD64
    gmat = jnp.where(blk[:, None] == blk[None, :], 1.0 / HEAD64, 0.0).astype(BF16)
    tile2 = lambda v: jnp.concatenate([v, v]).astype(F32)

    h = x.reshape(T, D)
    for l in range(DEPTH):
        lambda_init = 0.8 - 0.6 * math.exp(-0.3 * l)
        w_bf = w_in[l].astype(BF16)
        nrm = jnp.stack([tile2(diff_q_norm[l]), tile2(diff_k_norm[l]),
                         tile2(gqa_q_norm[l]), tile2(gqa_k_norm[l])])
        (u, dqT, dk, dvT, rq, rk, rv, rg, gqT, gk, gvT) = _proj_call(
            h, attn_norm[l].reshape(1, D).astype(F32), w_bf[:, :ATTN_COLS], gmat, nrm, tabs)

        lamv = jnp.stack([diff_lam_q1[l], diff_lam_k1[l], diff_lam_q2[l], diff_lam_k2[l]]).astype(F32)
        a = _diff_call(lamv, diff_subln[l].reshape(1, LANES).astype(F32), dqT, dk, dvT, B, S,
                       lambda_init)
        dec = jnp.stack([ret_decay_fwd[l], ret_decay_bwd[l]]).astype(F32)
        r = _ret_call(dec, ret_group_norm[l].reshape(1, 512).astype(F32), rq, rk, rv, rg, B, S)
        c = _gqa_call(gqT, gk, gvT, B, S)

        h, u2 = _merge_call(h, u, a, r, c, w_bf[:, ATTN_COLS:], w_branch[l].astype(BF16),
                            w_out[l].astype(BF16), mlp_norm[l].reshape(1, D).astype(F32))
        h = _mlp_call(h, u2, w_mlp_in[l].astype(BF16), w_mlp_out[l].astype(BF16))
    return h.reshape(B, S, D)
```

```python
import functools
import math

import jax
import jax.numpy as jnp
from jax import lax
from jax.experimental import pallas as pl
from jax.experimental.pallas import tpu as pltpu

F32 = jnp.float32
BF16 = jnp.bfloat16

D_MODEL = 1024
DEPTH = 2
GRID_W = 64
NORM_EPS = 1e-6
HEAD64 = 64
DIFF_HEADS = 4
RET_HEADS = 4
RET_CHUNK = 128
GQA_Q_HEADS = 8
GQA_GROUP = 4
N_BRANCHES = 3
BRANCH_WIDTH = 512
D_FF = 4 * D_MODEL
ATTN_COLS = 4352
LANES = 128

TM = 512
TK = 512
NQ = 1024
NC = 256
KV_UNROLL = 4
VROWS = 144
FAST_SOFTMAX_BOUND = 64.0
VMEM_LIMIT = 56 * 1024 * 1024
LOG2E = 1.4426950408889634


def _cparams(n_axes):
    return pltpu.CompilerParams(dimension_semantics=("arbitrary",) * n_axes,
                                vmem_limit_bytes=VMEM_LIMIT)


def _resident(shape):
    nd = len(shape)
    return pl.BlockSpec(shape, lambda *_: (0,) * nd, pipeline_mode=pl.Buffered(1))


def _proj_kernel(h_ref, an_ref, w_ref, g_ref, nrm_ref,
                 cd_ref, ad_ref, bd_ref, cg_ref, ag_ref, bg_ref, cr_ref, sr_ref,
                 u_ref, dqT_ref, dk_ref, dvT_ref, rq_ref, rk_ref, rv_ref, rg_ref,
                 gqT_ref, gk_ref, gvT_ref):
    x = h_ref[...]
    ms = jnp.mean(x * x, axis=-1, keepdims=True)
    u = (x * lax.rsqrt(ms + NORM_EPS) * an_ref[...]).astype(BF16)
    u_ref[...] = u

    def proj(c0, n):
        return jnp.dot(u, w_ref[:, c0:c0 + n], preferred_element_type=F32)

    gmat = g_ref[...]
    lo = lax.broadcasted_iota(jnp.int32, (1, LANES), 1) < HEAD64
    first_row = lax.broadcasted_iota(jnp.int32, (VROWS - LANES, TM), 0) == 0
    ones_rows = jnp.where(first_row, 1.0, 0.0).astype(BF16)

    def qk_norm(y, gain):
        msq = jnp.dot((y * y).astype(BF16), gmat, preferred_element_type=F32)
        return y * lax.rsqrt(msq + NORM_EPS) * gain

    def rope(y, c, a, b, d):
        return y * c + pltpu.roll(y, LANES - d, 1) * a + pltpu.roll(y, d, 1) * b

    qscale = HEAD64 ** -0.5 * LOG2E
    cd, ad, bd = cd_ref[...], ad_ref[...], bd_ref[...]
    cg, ag, bg = cg_ref[...], ag_ref[...], bg_ref[...]
    cr, sr = cr_ref[...], sr_ref[...]

    p = proj(0, 512)
    for hd in range(DIFF_HEADS):
        y = rope(qk_norm(p[:, hd * LANES:(hd + 1) * LANES], nrm_ref[0:1, :]), cd, ad, bd, 8) * qscale
        dqT_ref[hd, :, 0:TM] = jnp.where(lo, y, 0.0).T.astype(BF16)
        dqT_ref[hd, :, TM:2 * TM] = jnp.where(lo, 0.0, y).T.astype(BF16)
    p = proj(512, 512)
    for hd in range(DIFF_HEADS):
        y = rope(qk_norm(p[:, hd * LANES:(hd + 1) * LANES], nrm_ref[1:2, :]), cd, ad, bd, 8)
        dk_ref[:, hd * LANES:(hd + 1) * LANES] = y.astype(BF16)
    p = proj(1024, 512)
    for hd in range(DIFF_HEADS):
        dvT_ref[hd, 0:LANES, :] = p[:, hd * LANES:(hd + 1) * LANES].T.astype(BF16)
        dvT_ref[hd, LANES:VROWS, :] = ones_rows

    p = proj(1536, 512)
    for hd in range(RET_HEADS):
        y = p[:, hd * LANES:(hd + 1) * LANES]
        rq_ref[:, hd * LANES:(hd + 1) * LANES] = (y * cr + pltpu.roll(y, 64, 1) * sr).astype(BF16)
    p = proj(2048, 512)
    kscale = LANES ** -0.5
    for hd in range(RET_HEADS):
        y = p[:, hd * LANES:(hd + 1) * LANES]
        rk_ref[:, hd * LANES:(hd + 1) * LANES] = (
            (y * cr + pltpu.roll(y, 64, 1) * sr) * kscale).astype(BF16)
    rv_ref[...] = proj(2560, 512).astype(BF16)
    rg_ref[...] = proj(3072, 512).astype(BF16)

    p = proj(3584, 512)
    for c in range(4):
        y = rope(qk_norm(p[:, c * LANES:(c + 1) * LANES], nrm_ref[2:3, :]), cg, ag, bg, 16) * qscale
        ysw = pltpu.roll(y, 64, 1)
        if c < 2:
            z_even, z_odd = jnp.where(lo, y, 0.0), jnp.where(lo, ysw, 0.0)
        else:
            z_even, z_odd = jnp.where(lo, 0.0, ysw), jnp.where(lo, 0.0, y)
        for hh, z in ((2 * c, z_even), (2 * c + 1, z_odd)):
            zt = z.T.astype(BF16)
            for qb in range(TM // LANES):
                gqT_ref[qb, :, hh * LANES:(hh + 1) * LANES] = zt[:, qb * LANES:(qb + 1) * LANES]
    p = proj(4096, 256)
    gk_ref[...] = rope(qk_norm(p[:, 0:LANES], nrm_ref[3:4, :]), cg, ag, bg, 16).astype(BF16)
    gvT_ref[0:LANES, :] = p[:, LANES:2 * LANES].T.astype(BF16)
    gvT_ref[LANES:VROWS, :] = ones_rows


def _proj_call(h, attn_norm, w_attn, gmat, nrm, tabs):
    T = h.shape[0]
    S = tabs[0].shape[0]
    nt = T // TM
    tab_spec = pl.BlockSpec((TM, LANES), lambda i: (i % (S // TM), 0))
    row = lambda w: pl.BlockSpec((TM, w), lambda i: (i, 0))
    out_shape = (
        jax.ShapeDtypeStruct((T, D_MODEL), BF16),
        jax.ShapeDtypeStruct((nt, 4, LANES, 2 * TM), BF16),
        jax.ShapeDtypeStruct((T, 512), BF16),
        jax.ShapeDtypeStruct((nt, 4, VROWS, TM), BF16),
        jax.ShapeDtypeStruct((T, 512), BF16),
        jax.ShapeDtypeStruct((T, 512), BF16),
        jax.ShapeDtypeStruct((T, 512), BF16),
        jax.ShapeDtypeStruct((T, 512), BF16),
        jax.ShapeDtypeStruct((T // LANES, LANES, NQ), BF16),
        jax.ShapeDtypeStruct((T, LANES), BF16),
        jax.ShapeDtypeStruct((nt, VROWS, TM), BF16),
    )
    out_specs = (
        row(D_MODEL),
        pl.BlockSpec((None, 4, LANES, 2 * TM), lambda i: (i, 0, 0, 0)),
        row(512),
        pl.BlockSpec((None, 4, VROWS, TM), lambda i: (i, 0, 0, 0)),
        row(512), row(512), row(512), row(512),
        pl.BlockSpec((TM // LANES, LANES, NQ), lambda i: (i, 0, 0)),
        row(LANES),
        pl.BlockSpec((None, VROWS, TM), lambda i: (i, 0, 0)),
    )
    in_specs = [row(D_MODEL), _resident((1, D_MODEL)), _resident(w_attn.shape),
                _resident(gmat.shape), _resident(nrm.shape)] + [tab_spec] * 8
    return pl.pallas_call(
        _proj_kernel, grid=(nt,), in_specs=in_specs, out_specs=out_specs,
        out_shape=out_shape, compiler_params=_cparams(1), name="proj",
    )(h, attn_norm, w_attn, gmat, nrm, *tabs)


def _attn_core(fast_ref, qT_ref, k_ref, vT_ref, m_ref, acc_ref):
    acc_ref[...] = jnp.zeros(acc_ref.shape, F32)
    n_steps = k_ref.shape[0] // TK

    def kv(j):
        return k_ref[pl.ds(pl.multiple_of(j * TK, TK), TK), :], vT_ref[j]

    @pl.when(fast_ref[0] == 1)
    def _():
        def body(jj, carry):
            pv = None
            for u in range(KV_UNROLL):
                k, vT = kv(jj * KV_UNROLL + u)
                s = jnp.dot(k, qT_ref[...], preferred_element_type=F32)
                t = jnp.dot(vT, jnp.exp2(s).astype(BF16), preferred_element_type=F32)
                pv = t if pv is None else pv + t
            acc_ref[...] += pv
            return carry

        lax.fori_loop(0, n_steps // KV_UNROLL, body, 0)

    @pl.when(fast_ref[0] == 0)
    def _():
        m_ref[...] = jnp.full(m_ref.shape, -1e30, F32)

        def body(j, carry):
            k, vT = kv(j)
            for c in range(NQ // NC):
                cs = slice(c * NC, (c + 1) * NC)
                s = jnp.dot(k, qT_ref[:, cs], preferred_element_type=F32)
                m_old = m_ref[:, cs]
                m_new = jnp.maximum(m_old, jnp.max(s, axis=0, keepdims=True))
                alpha = jnp.exp2(m_old - m_new)
                p = jnp.exp2(s - m_new).astype(BF16)
                acc_ref[:, cs] = acc_ref[:, cs] * alpha + jnp.dot(vT, p, preferred_element_type=F32)
                m_ref[:, cs] = m_new
            return carry

        lax.fori_loop(0, n_steps, body, 0)


def _diff_kernel(fast_ref, lam_ref, sub_ref, qT_ref, k_ref, vT_ref, o_ref, m_ref, acc_ref, *,
                 lambda_init):
    _attn_core(fast_ref, qT_ref, k_ref, vT_ref, m_ref, acc_ref)

    lv = lam_ref[...]
    lam = (jnp.exp(jnp.sum(lv[0:1] * lv[1:2], axis=-1, keepdims=True))
           - jnp.exp(jnp.sum(lv[2:3] * lv[3:4], axis=-1, keepdims=True)) + lambda_init)
    tq = NQ // 2
    o1 = acc_ref[0:LANES, 0:tq] / acc_ref[LANES:LANES + 1, 0:tq]
    o2 = acc_ref[0:LANES, tq:NQ] / acc_ref[LANES:LANES + 1, tq:NQ]
    d = o1 - lam * o2
    ms = jnp.mean(d * d, axis=0, keepdims=True)
    dn = d * lax.rsqrt(ms + 1e-5)
    o_ref[...] = (dn.T * (sub_ref[...] * (1.0 - lambda_init))).astype(o_ref.dtype)


_SMEM_SPEC = pl.BlockSpec(memory_space=pltpu.SMEM)
_ATTN_SCRATCH = [pltpu.VMEM((1, NQ), F32), pltpu.VMEM((VROWS, NQ), F32)]


def _diff_call(fast, lamv, subln, dqT, dk, dvT, B, S, lambda_init):
    T = dk.shape[0]
    tq = NQ // 2
    nq = S // tq
    kern = functools.partial(_diff_kernel, lambda_init=lambda_init)
    return pl.pallas_call(
        kern, grid=(B, DIFF_HEADS, nq),
        in_specs=[
            _SMEM_SPEC, _resident(lamv.shape), _resident(subln.shape),
            pl.BlockSpec((None, None, LANES, NQ), lambda b, h, i: (b * nq + i, h, 0, 0)),
            pl.BlockSpec((S, LANES), lambda b, h, i: (b, h)),
            pl.BlockSpec((S // TM, None, VROWS, TM), lambda b, h, i: (b, h, 0, 0)),
        ],
        out_specs=pl.BlockSpec((tq, LANES), lambda b, h, i: (b * nq + i, h)),
        out_shape=jax.ShapeDtypeStruct((T, 512), BF16),
        scratch_shapes=_ATTN_SCRATCH,
        compiler_params=_cparams(3), name="diff_attn",
    )(fast, lamv, subln, dqT, dk, dvT)


def _gqa_kernel(fast_ref, qT_ref, k_ref, vT_ref, o_ref, m_ref, acc_ref):
    _attn_core(fast_ref, qT_ref, k_ref, vT_ref, m_ref, acc_ref)
    for c in range(GQA_Q_HEADS // 2):
        g = (2 * c) // GQA_GROUP
        rows = slice(g * HEAD64, (g + 1) * HEAD64)
        parts = []
        for hh in (2 * c, 2 * c + 1):
            cols = slice(hh * LANES, (hh + 1) * LANES)
            parts.append(acc_ref[rows, cols] / acc_ref[LANES:LANES + 1, cols])
        o_ref[:, c * LANES:(c + 1) * LANES] = jnp.concatenate(parts, axis=0).T.astype(o_ref.dtype)


def _gqa_call(fast, gqT, gk, gvT, B, S):
    T = gk.shape[0]
    tq = NQ // GQA_Q_HEADS
    nq = S // tq
    return pl.pallas_call(
        _gqa_kernel, grid=(B, nq),
        in_specs=[
            _SMEM_SPEC,
            pl.BlockSpec((None, LANES, NQ), lambda b, i: (b * nq + i, 0, 0)),
            pl.BlockSpec((S, LANES), lambda b, i: (b, 0)),
            pl.BlockSpec((S // TM, VROWS, TM), lambda b, i: (b, 0, 0)),
        ],
        out_specs=pl.BlockSpec((tq, 512), lambda b, i: (b * nq + i, 0)),
        out_shape=jax.ShapeDtypeStruct((T, 512), BF16),
        scratch_shapes=_ATTN_SCRATCH,
        compiler_params=_cparams(2), name="gqa_attn",
    )(fast, gqT, gk, gvT)


def _ret_kernel(dec_ref, gain_ref, q_ref, k_ref, v_ref, g_ref, o_ref, of_ref, st_ref):
    C = RET_CHUNK
    hd = pl.program_id(1)
    lf = -jnp.exp(jnp.full((C, C), dec_ref[0, hd], F32))
    lb = -jnp.exp(jnp.full((C, C), dec_ref[1, hd], F32))
    ii = lax.broadcasted_iota(jnp.int32, (C, C), 0).astype(F32)
    jj = lax.broadcasted_iota(jnp.int32, (C, C), 1).astype(F32)
    diff = ii - jj
    decay = jnp.where(diff >= 0, jnp.exp(jnp.maximum(diff, 0.0) * lf),
                      jnp.exp(jnp.maximum(-diff, 0.0) * lb))
    xi_f, zeta_f = jnp.exp((ii + 1.0) * lf), jnp.exp((C - 1.0 - ii) * lf)
    xi_b, zeta_b = jnp.exp((C - ii) * lb), jnp.exp(ii * lb)
    dec_f, dec_b = jnp.exp(C * lf), jnp.exp(C * lb)
    n_chunks = q_ref.shape[0] // C
    nt = (((1,), (1,)), ((), ()))
    tn = (((0,), (0,)), ((), ()))

    def chunk(n):
        rows = pl.ds(pl.multiple_of(n * C, C), C)
        return rows, q_ref[rows, :], k_ref[rows, :], v_ref[rows, :]

    def state_update(k, v, zeta, dec):
        vz = (v.astype(F32) * zeta).astype(BF16)
        kv = lax.dot_general(k, vz, tn, preferred_element_type=F32)
        st_ref[...] = st_ref[...] * dec + kv

    def fwd(n, carry):
        rows, q, k, v = chunk(n)
        a = lax.dot_general(q, k, nt, preferred_element_type=F32) * decay
        o = jnp.dot(a.astype(BF16), v, preferred_element_type=F32)
        o += jnp.dot(q, st_ref[...].astype(BF16), preferred_element_type=F32) * xi_f
        of_ref[rows, :] = o
        state_update(k, v, zeta_f, dec_f)
        return carry

    def bwd(t, carry):
        rows, q, k, v = chunk(n_chunks - 1 - t)
        o = of_ref[rows, :] + jnp.dot(q, st_ref[...].astype(BF16), preferred_element_type=F32) * xi_b
        state_update(k, v, zeta_b, dec_b)
        mu = jnp.mean(o, axis=-1, keepdims=True)
        oc = o - mu
        var = jnp.mean(oc * oc, axis=-1, keepdims=True)
        y = oc * lax.rsqrt(var + 1e-5) * gain_ref[...]
        g = g_ref[rows, :].astype(F32)
        o_ref[rows, :] = (g / (1.0 + jnp.exp(-g)) * y).astype(o_ref.dtype)
        return carry

    st_ref[...] = jnp.zeros(st_ref.shape, F32)
    lax.fori_loop(0, n_chunks, fwd, 0)
    st_ref[...] = jnp.zeros(st_ref.shape, F32)
    lax.fori_loop(0, n_chunks, bwd, 0)


def _ret_call(dec, gain, rq, rk, rv, rg, B, S):
    T = rq.shape[0]
    blk = pl.BlockSpec((S, LANES), lambda b, h: (b, h))
    return pl.pallas_call(
        _ret_kernel, grid=(B, RET_HEADS),
        in_specs=[pl.BlockSpec(memory_space=pltpu.SMEM),
                  pl.BlockSpec((1, LANES), lambda b, h: (0, h)), blk, blk, blk, blk],
        out_specs=blk,
        out_shape=jax.ShapeDtypeStruct((T, 512), BF16),
        scratch_shapes=[pltpu.VMEM((S, LANES), F32), pltpu.VMEM((RET_CHUNK, LANES), F32)],
        compiler_params=_cparams(2), name="retention",
    )(dec, gain, rq, rk, rv, rg)


def _merge_kernel(h_ref, u_ref, a_ref, r_ref, c_ref, wg_ref, wb_ref, wo_ref, mn_ref,
                  hn_ref, u2_ref):
    u = u_ref[...]
    merged = None
    for n, br in enumerate((a_ref, r_ref, c_ref)):
        logits = jnp.dot(u, wg_ref[:, n * D_MODEL:(n + 1) * D_MODEL], preferred_element_type=F32)
        gate = 1.0 / (1.0 + jnp.exp(-logits))
        term = gate * jnp.dot(br[...], wb_ref[n], preferred_element_type=F32)
        merged = term if merged is None else merged + term
    hn = h_ref[...] + jnp.dot(merged.astype(BF16), wo_ref[...], preferred_element_type=F32)
    hn_ref[...] = hn
    ms = jnp.mean(hn * hn, axis=-1, keepdims=True)
    u2_ref[...] = (hn * lax.rsqrt(ms + NORM_EPS) * mn_ref[...]).astype(BF16)


def _merge_call(h, u, a, r, c, wg, wb, wo, mlp_norm):
    T = h.shape[0]
    row = lambda w: pl.BlockSpec((TM, w), lambda i: (i, 0))
    return pl.pallas_call(
        _merge_kernel, grid=(T // TM,),
        in_specs=[row(D_MODEL), row(D_MODEL), row(512), row(512), row(512),
                  _resident(wg.shape), _resident(wb.shape), _resident(wo.shape),
                  _resident((1, D_MODEL))],
        out_specs=(row(D_MODEL), row(D_MODEL)),
        out_shape=(jax.ShapeDtypeStruct((T, D_MODEL), F32),
                   jax.ShapeDtypeStruct((T, D_MODEL), BF16)),
        compiler_params=_cparams(1), name="merge",
    )(h, u, a, r, c, wg, wb, wo, mlp_norm)


def _mlp_kernel(h_ref, u_ref, w1_ref, w2_ref, o_ref):
    u = u_ref[...]
    acc = h_ref[...]
    for c in range(D_FF // D_MODEL):
        cs = slice(c * D_MODEL, (c + 1) * D_MODEL)
        m = jnp.maximum(jnp.dot(u, w1_ref[:, cs], preferred_element_type=F32), 0.0)
        acc = acc + jnp.dot((m * m).astype(BF16), w2_ref[cs, :], preferred_element_type=F32)
    o_ref[...] = acc


def _mlp_call(h, u2, w1, w2):
    T = h.shape[0]
    row = pl.BlockSpec((TM, D_MODEL), lambda i: (i, 0))
    return pl.pallas_call(
        _mlp_kernel, grid=(T // TM,),
        in_specs=[row, row, _resident(w1.shape), _resident(w2.shape)],
        out_specs=row,
        out_shape=jax.ShapeDtypeStruct((T, D_MODEL), F32),
        compiler_params=_cparams(1), name="mlp",
    )(h, u2, w1, w2)


def _rope_tables(S):
    pos = jnp.arange(S, dtype=F32)
    lane = jnp.arange(LANES)
    l64 = lane % 64

    inv = 500000.0 ** (-jnp.arange(0, 16, 2, dtype=F32) / 16)
    ang = pos[:, None] * inv[None, :]
    cos, sin = jnp.cos(ang)[:, l64 % 8], jnp.sin(ang)[:, l64 % 8]
    cd = jnp.where(l64 < 16, cos, 1.0)
    ad = jnp.where(l64 < 8, -sin, 0.0)
    bd = jnp.where((l64 >= 8) & (l64 < 16), sin, 0.0)

    inv = 10000.0 ** (-jnp.arange(0, 32, 2, dtype=F32) / 32)
    row_pos = jnp.floor(pos / GRID_W)
    col_pos = pos - row_pos * GRID_W
    p2 = jnp.where((l64 < 32)[None, :], row_pos[:, None], col_pos[:, None])
    ang = p2 * inv[l64 % 16][None, :]
    cg = jnp.cos(ang)
    ag = jnp.where(l64 % 32 < 16, -jnp.sin(ang), 0.0)
    bg = jnp.where(l64 % 32 >= 16, jnp.sin(ang), 0.0)

    inv = 10000.0 ** (-jnp.arange(0, 128, 2, dtype=F32) / 128)
    ang = pos[:, None] * inv[l64][None, :]
    cr = jnp.cos(ang)
    sr = jnp.where(lane < 64, -jnp.sin(ang), jnp.sin(ang))
    return (cd, ad, bd, cg, ag, bg, cr, sr)


def _fast_flag(q_gain, k_gain):
    bound = (HEAD64 ** 0.5 * LOG2E * 1.02) * jnp.max(jnp.abs(q_gain)) * jnp.max(jnp.abs(k_gain))
    return (bound <= FAST_SOFTMAX_BOUND).astype(jnp.int32).reshape(1)


def kernel(x, attn_norm, w_in, diff_q_norm, diff_k_norm, diff_lam_q1, diff_lam_k1, diff_lam_q2, diff_lam_k2, diff_subln, ret_decay_fwd, ret_decay_bwd, ret_group_norm, gqa_q_norm, gqa_k_norm, w_branch, w_out, mlp_norm, w_mlp_in, w_mlp_out):
    B, S, D = x.shape
    T = B * S
    tabs = _rope_tables(S)
    blk = jnp.arange(LANES) // HEAD64
    gmat = jnp.where(blk[:, None] == blk[None, :], 1.0 / HEAD64, 0.0).astype(BF16)
    tile2 = lambda v: jnp.concatenate([v, v]).astype(F32)

    h = x.reshape(T, D)
    for l in range(DEPTH):
        lambda_init = 0.8 - 0.6 * math.exp(-0.3 * l)
        w_bf = w_in[l].astype(BF16)
        nrm = jnp.stack([tile2(diff_q_norm[l]), tile2(diff_k_norm[l]),
                         tile2(gqa_q_norm[l]), tile2(gqa_k_norm[l])])
        (u, dqT, dk, dvT, rq, rk, rv, rg, gqT, gk, gvT) = _proj_call(
            h, attn_norm[l].reshape(1, D).astype(F32), w_bf[:, :ATTN_COLS], gmat, nrm, tabs)

        lamv = jnp.stack([diff_lam_q1[l], diff_lam_k1[l], diff_lam_q2[l], diff_lam_k2[l]]).astype(F32)
        a = _diff_call(_fast_flag(diff_q_norm[l], diff_k_norm[l]), lamv,
                       diff_subln[l].reshape(1, LANES).astype(F32), dqT, dk, dvT, B, S, lambda_init)
        dec = jnp.stack([ret_decay_fwd[l], ret_decay_bwd[l]]).astype(F32)
        r = _ret_call(dec, ret_group_norm[l].reshape(1, 512).astype(F32), rq, rk, rv, rg, B, S)
        c = _gqa_call(_fast_flag(gqa_q_norm[l], gqa_k_norm[l]), gqT, gk, gvT, B, S)

        h, u2 = _merge_call(h, u, a, r, c, w_bf[:, ATTN_COLS:], w_branch[l].astype(BF16),
                            w_out[l].astype(BF16), mlp_norm[l].reshape(1, D).astype(F32))
        h = _mlp_call(h, u2, w_mlp_in[l].astype(BF16), w_mlp_out[l].astype(BF16))
    return h.reshape(B, S, D)
```

```python
import functools
import math

import jax
import jax.numpy as jnp
from jax import lax
from jax.experimental import pallas as pl
from jax.experimental.pallas import tpu as pltpu

F32 = jnp.float32
BF16 = jnp.bfloat16

D_MODEL = 1024
DEPTH = 2
GRID_W = 64
NORM_EPS = 1e-6
HEAD64 = 64
DIFF_HEADS = 4
RET_HEADS = 4
RET_CHUNK = 128
GQA_Q_HEADS = 8
GQA_GROUP = 4
N_BRANCHES = 3
BRANCH_WIDTH = 512
D_FF = 4 * D_MODEL
ATTN_COLS = 4352
LANES = 128

TM = 512
TK = 512
NQ = 1024
NC = 256
KV_UNROLL = 8
ONES_ROWS = 16
VROWS = LANES + ONES_ROWS
GVROWS = HEAD64 + ONES_ROWS
FAST_SOFTMAX_BOUND = 64.0
VMEM_LIMIT = 56 * 1024 * 1024
LOG2E = 1.4426950408889634


def _cparams(n_axes):
    return pltpu.CompilerParams(dimension_semantics=("arbitrary",) * n_axes,
                                vmem_limit_bytes=VMEM_LIMIT)


def _resident(shape):
    nd = len(shape)
    return pl.BlockSpec(shape, lambda *_: (0,) * nd, pipeline_mode=pl.Buffered(1))


def _proj_kernel(h_ref, an_ref, w_ref, g_ref, nrm_ref,
                 cd_ref, ad_ref, bd_ref, cg_ref, ag_ref, bg_ref, cr_ref, sr_ref,
                 u_ref, dqT_ref, dk_ref, dvT_ref, rq_ref, rk_ref, rv_ref, rg_ref,
                 gqT_ref, gk_ref, gvT_ref):
    x = h_ref[...]
    ms = jnp.mean(x * x, axis=-1, keepdims=True)
    u = (x * lax.rsqrt(ms + NORM_EPS) * an_ref[...]).astype(BF16)
    u_ref[...] = u

    def proj(c0, n):
        return jnp.dot(u, w_ref[:, c0:c0 + n], preferred_element_type=F32)

    gmat = g_ref[...]
    lo = lax.broadcasted_iota(jnp.int32, (1, LANES), 1) < HEAD64
    first_row = lax.broadcasted_iota(jnp.int32, (ONES_ROWS, TM), 0) == 0
    ones_rows = jnp.where(first_row, 1.0, 0.0).astype(BF16)

    def qk_norm(y, gain):
        msq = jnp.dot((y * y).astype(BF16), gmat, preferred_element_type=F32)
        return y * lax.rsqrt(msq + NORM_EPS) * gain

    def rope(y, c, a, b, d):
        return y * c + pltpu.roll(y, LANES - d, 1) * a + pltpu.roll(y, d, 1) * b

    qscale = HEAD64 ** -0.5 * LOG2E
    cd, ad, bd = cd_ref[...], ad_ref[...], bd_ref[...]
    cg, ag, bg = cg_ref[...], ag_ref[...], bg_ref[...]
    cr, sr = cr_ref[...], sr_ref[...]

    p = proj(0, 512)
    for hd in range(DIFF_HEADS):
        y = rope(qk_norm(p[:, hd * LANES:(hd + 1) * LANES], nrm_ref[0:1, :]), cd, ad, bd, 8) * qscale
        dqT_ref[hd, :, 0:TM] = jnp.where(lo, y, 0.0).T.astype(BF16)
        dqT_ref[hd, :, TM:2 * TM] = jnp.where(lo, 0.0, y).T.astype(BF16)
    p = proj(512, 512)
    for hd in range(DIFF_HEADS):
        y = rope(qk_norm(p[:, hd * LANES:(hd + 1) * LANES], nrm_ref[1:2, :]), cd, ad, bd, 8)
        dk_ref[:, hd * LANES:(hd + 1) * LANES] = y.astype(BF16)
    p = proj(1024, 512)
    for hd in range(DIFF_HEADS):
        dvT_ref[hd, 0:LANES, :] = p[:, hd * LANES:(hd + 1) * LANES].T.astype(BF16)
        dvT_ref[hd, LANES:VROWS, :] = ones_rows

    p = proj(1536, 512)
    for hd in range(RET_HEADS):
        y = p[:, hd * LANES:(hd + 1) * LANES]
        rq_ref[:, hd * LANES:(hd + 1) * LANES] = (y * cr + pltpu.roll(y, 64, 1) * sr).astype(BF16)
    p = proj(2048, 512)
    kscale = LANES ** -0.5
    for hd in range(RET_HEADS):
        y = p[:, hd * LANES:(hd + 1) * LANES]
        rk_ref[:, hd * LANES:(hd + 1) * LANES] = (
            (y * cr + pltpu.roll(y, 64, 1) * sr) * kscale).astype(BF16)
    rv_ref[...] = proj(2560, 512).astype(BF16)
    rg_ref[...] = proj(3072, 512).astype(BF16)

    p = proj(3584, 512)
    for c in range(4):
        y = rope(qk_norm(p[:, c * LANES:(c + 1) * LANES], nrm_ref[2:3, :]), cg, ag, bg, 16) * qscale
        ysw = pltpu.roll(y, 64, 1)
        if c < 2:
            z_even, z_odd = jnp.where(lo, y, 0.0), jnp.where(lo, ysw, 0.0)
        else:
            z_even, z_odd = jnp.where(lo, 0.0, ysw), jnp.where(lo, 0.0, y)
        for hh, z in ((2 * c, z_even), (2 * c + 1, z_odd)):
            zt = z.T.astype(BF16)
            for qb in range(TM // LANES):
                gqT_ref[qb, :, hh * LANES:(hh + 1) * LANES] = zt[:, qb * LANES:(qb + 1) * LANES]
    p = proj(4096, 256)
    gk_ref[...] = rope(qk_norm(p[:, 0:LANES], nrm_ref[3:4, :]), cg, ag, bg, 16).astype(BF16)
    vt = p[:, LANES:2 * LANES].T.astype(BF16)
    for g in range(2):
        gvT_ref[g, 0:HEAD64, :] = vt[g * HEAD64:(g + 1) * HEAD64]
        gvT_ref[g, HEAD64:GVROWS, :] = ones_rows


def _proj_call(h, attn_norm, w_attn, gmat, nrm, tabs):
    T = h.shape[0]
    S = tabs[0].shape[0]
    nt = T // TM
    tab_spec = pl.BlockSpec((TM, LANES), lambda i: (i % (S // TM), 0))
    row = lambda w: pl.BlockSpec((TM, w), lambda i: (i, 0))
    out_shape = (
        jax.ShapeDtypeStruct((T, D_MODEL), BF16),
        jax.ShapeDtypeStruct((nt, 4, LANES, 2 * TM), BF16),
        jax.ShapeDtypeStruct((T, 512), BF16),
        jax.ShapeDtypeStruct((nt, 4, VROWS, TM), BF16),
        jax.ShapeDtypeStruct((T, 512), BF16),
        jax.ShapeDtypeStruct((T, 512), BF16),
        jax.ShapeDtypeStruct((T, 512), BF16),
        jax.ShapeDtypeStruct((T, 512), BF16),
        jax.ShapeDtypeStruct((T // LANES, LANES, NQ), BF16),
        jax.ShapeDtypeStruct((T, LANES), BF16),
        jax.ShapeDtypeStruct((nt, 2, GVROWS, TM), BF16),
    )
    out_specs = (
        row(D_MODEL),
        pl.BlockSpec((None, 4, LANES, 2 * TM), lambda i: (i, 0, 0, 0)),
        row(512),
        pl.BlockSpec((None, 4, VROWS, TM), lambda i: (i, 0, 0, 0)),
        row(512), row(512), row(512), row(512),
        pl.BlockSpec((TM // LANES, LANES, NQ), lambda i: (i, 0, 0)),
        row(LANES),
        pl.BlockSpec((None, 2, GVROWS, TM), lambda i: (i, 0, 0, 0)),
    )
    in_specs = [row(D_MODEL), _resident((1, D_MODEL)), _resident(w_attn.shape),
                _resident(gmat.shape), _resident(nrm.shape)] + [tab_spec] * 8
    return pl.pallas_call(
        _proj_kernel, grid=(nt,), in_specs=in_specs, out_specs=out_specs,
        out_shape=out_shape, compiler_params=_cparams(1), name="proj",
    )(h, attn_norm, w_attn, gmat, nrm, *tabs)


def _attn_core(fast_ref, qT_ref, k_ref, vT_ref, m_ref, acc_ref):
    acc_ref[...] = jnp.zeros(acc_ref.shape, F32)
    n_steps = k_ref.shape[0] // TK
    n_groups = vT_ref.shape[1]
    gw = NQ // n_groups

    def kv(j):
        return k_ref[pl.ds(pl.multiple_of(j * TK, TK), TK), :], vT_ref[j]

    @pl.when(fast_ref[0] == 1)
    def _():
        def body(jj, carry):
            pv = [None] * n_groups
            for u in range(KV_UNROLL):
                k, vT = kv(jj * KV_UNROLL + u)
                s = jnp.dot(k, qT_ref[...], preferred_element_type=F32)
                p = jnp.exp2(s).astype(BF16)
                for g in range(n_groups):
                    t = jnp.dot(vT[g], p[:, g * gw:(g + 1) * gw], preferred_element_type=F32)
                    pv[g] = t if pv[g] is None else pv[g] + t
            for g in range(n_groups):
                acc_ref[:, g * gw:(g + 1) * gw] += pv[g]
            return carry

        lax.fori_loop(0, n_steps // KV_UNROLL, body, 0)

    @pl.when(fast_ref[0] == 0)
    def _():
        m_ref[...] = jnp.full(m_ref.shape, -1e30, F32)

        def body(j, carry):
            k, vT = kv(j)
            for c in range(NQ // NC):
                cs = slice(c * NC, (c + 1) * NC)
                s = jnp.dot(k, qT_ref[:, cs], preferred_element_type=F32)
                m_old = m_ref[:, cs]
                m_new = jnp.maximum(m_old, jnp.max(s, axis=0, keepdims=True))
                alpha = jnp.exp2(m_old - m_new)
                p = jnp.exp2(s - m_new).astype(BF16)
                pv = jnp.dot(vT[(c * NC) // gw], p, preferred_element_type=F32)
                acc_ref[:, cs] = acc_ref[:, cs] * alpha + pv
                m_ref[:, cs] = m_new
            return carry

        lax.fori_loop(0, n_steps, body, 0)


def _diff_kernel(fast_ref, lam_ref, sub_ref, qT_ref, k_ref, vT_ref, o_ref, m_ref, acc_ref, *,
                 lambda_init):
    _attn_core(fast_ref, qT_ref, k_ref, vT_ref, m_ref, acc_ref)

    lv = lam_ref[...]
    lam = (jnp.exp(jnp.sum(lv[0:1] * lv[1:2], axis=-1, keepdims=True))
           - jnp.exp(jnp.sum(lv[2:3] * lv[3:4], axis=-1, keepdims=True)) + lambda_init)
    tq = NQ // 2
    o1 = acc_ref[0:LANES, 0:tq] / acc_ref[LANES:LANES + 1, 0:tq]
    o2 = acc_ref[0:LANES, tq:NQ] / acc_ref[LANES:LANES + 1, tq:NQ]
    d = o1 - lam * o2
    ms = jnp.mean(d * d, axis=0, keepdims=True)
    dn = d * lax.rsqrt(ms + 1e-5)
    o_ref[...] = (dn.T * (sub_ref[...] * (1.0 - lambda_init))).astype(o_ref.dtype)


_SMEM_SPEC = pl.BlockSpec(memory_space=pltpu.SMEM)

def _attn_scratch(rows):
    return [pltpu.VMEM((1, NQ), F32), pltpu.VMEM((rows, NQ), F32)]


def _diff_call(fast, lamv, subln, dqT, dk, dvT, B, S, lambda_init):
    T = dk.shape[0]
    tq = NQ // 2
    nq = S // tq
    kern = functools.partial(_diff_kernel, lambda_init=lambda_init)
    return pl.pallas_call(
        kern, grid=(B, DIFF_HEADS, nq),
        in_specs=[
            _SMEM_SPEC, _resident(lamv.shape), _resident(subln.shape),
            pl.BlockSpec((None, None, LANES, NQ), lambda b, h, i: (b * nq + i, h, 0, 0)),
            pl.BlockSpec((S, LANES), lambda b, h, i: (b, h)),
            pl.BlockSpec((S // TM, 1, VROWS, TM), lambda b, h, i: (b, h, 0, 0)),
        ],
        out_specs=pl.BlockSpec((tq, LANES), lambda b, h, i: (b * nq + i, h)),
        out_shape=jax.ShapeDtypeStruct((T, 512), BF16),
        scratch_shapes=_attn_scratch(VROWS),
        compiler_params=_cparams(3), name="diff_attn",
    )(fast, lamv, subln, dqT, dk, dvT)


def _gqa_kernel(fast_ref, qT_ref, k_ref, vT_ref, o_ref, m_ref, acc_ref):
    _attn_core(fast_ref, qT_ref, k_ref, vT_ref, m_ref, acc_ref)
    for c in range(GQA_Q_HEADS // 2):
        parts = []
        for hh in (2 * c, 2 * c + 1):
            cols = slice(hh * LANES, (hh + 1) * LANES)
            parts.append(acc_ref[0:HEAD64, cols] / acc_ref[HEAD64:HEAD64 + 1, cols])
        o_ref[:, c * LANES:(c + 1) * LANES] = jnp.concatenate(parts, axis=0).T.astype(o_ref.dtype)


def _gqa_call(fast, gqT, gk, gvT, B, S):
    T = gk.shape[0]
    tq = NQ // GQA_Q_HEADS
    nq = S // tq
    return pl.pallas_call(
        _gqa_kernel, grid=(B, nq),
        in_specs=[
            _SMEM_SPEC,
            pl.BlockSpec((None, LANES, NQ), lambda b, i: (b * nq + i, 0, 0)),
            pl.BlockSpec((S, LANES), lambda b, i: (b, 0)),
            pl.BlockSpec((S // TM, 2, GVROWS, TM), lambda b, i: (b, 0, 0, 0)),
        ],
        out_specs=pl.BlockSpec((tq, 512), lambda b, i: (b * nq + i, 0)),
        out_shape=jax.ShapeDtypeStruct((T, 512), BF16),
        scratch_shapes=_attn_scratch(GVROWS),
        compiler_params=_cparams(2), name="gqa_attn",
    )(fast, gqT, gk, gvT)


def _ret_kernel(dec_ref, gain_ref, q_ref, k_ref, v_ref, g_ref, o_ref, of_ref, st_ref):
    C = RET_CHUNK
    hd = pl.program_id(1)
    lf = -jnp.exp(jnp.full((C, C), dec_ref[0, hd], F32))
    lb = -jnp.exp(jnp.full((C, C), dec_ref[1, hd], F32))
    ii = lax.broadcasted_iota(jnp.int32, (C, C), 0).astype(F32)
    jj = lax.broadcasted_iota(jnp.int32, (C, C), 1).astype(F32)
    diff = ii - jj
    decay = jnp.where(diff >= 0, jnp.exp(jnp.maximum(diff, 0.0) * lf),
                      jnp.exp(jnp.maximum(-diff, 0.0) * lb))
    xi_f, zeta_f = jnp.exp((ii + 1.0) * lf), jnp.exp((C - 1.0 - ii) * lf)
    xi_b, zeta_b = jnp.exp((C - ii) * lb), jnp.exp(ii * lb)
    dec_f, dec_b = jnp.exp(C * lf), jnp.exp(C * lb)
    n_chunks = q_ref.shape[0] // C
    nt = (((1,), (1,)), ((), ()))
    tn = (((0,), (0,)), ((), ()))

    def chunk(n):
        rows = pl.ds(pl.multiple_of(n * C, C), C)
        return rows, q_ref[rows, :], k_ref[rows, :], v_ref[rows, :]

    def state_update(k, v, zeta, dec):
        vz = (v.astype(F32) * zeta).astype(BF16)
        kv = lax.dot_general(k, vz, tn, preferred_element_type=F32)
        st_ref[...] = st_ref[...] * dec + kv

    def fwd(n, carry):
        rows, q, k, v = chunk(n)
        a = lax.dot_general(q, k, nt, preferred_element_type=F32) * decay
        o = jnp.dot(a.astype(BF16), v, preferred_element_type=F32)
        o += jnp.dot(q, st_ref[...].astype(BF16), preferred_element_type=F32) * xi_f
        of_ref[rows, :] = o
        state_update(k, v, zeta_f, dec_f)
        return carry

    def bwd(t, carry):
        rows, q, k, v = chunk(n_chunks - 1 - t)
        o = of_ref[rows, :] + jnp.dot(q, st_ref[...].astype(BF16), preferred_element_type=F32) * xi_b
        state_update(k, v, zeta_b, dec_b)
        mu = jnp.mean(o, axis=-1, keepdims=True)
        oc = o - mu
        var = jnp.mean(oc * oc, axis=-1, keepdims=True)
        y = oc * lax.rsqrt(var + 1e-5) * gain_ref[...]
        g = g_ref[rows, :].astype(F32)
        o_ref[rows, :] = (g / (1.0 + jnp.exp(-g)) * y).astype(o_ref.dtype)
        return carry

    st_ref[...] = jnp.zeros(st_ref.shape, F32)
    lax.fori_loop(0, n_chunks, fwd, 0)
    st_ref[...] = jnp.zeros(st_ref.shape, F32)
    lax.fori_loop(0, n_chunks, bwd, 0)


def _ret_call(dec, gain, rq, rk, rv, rg, B, S):
    T = rq.shape[0]
    blk = pl.BlockSpec((S, LANES), lambda b, h: (b, h))
    return pl.pallas_call(
        _ret_kernel, grid=(B, RET_HEADS),
        in_specs=[pl.BlockSpec(memory_space=pltpu.SMEM),
                  pl.BlockSpec((1, LANES), lambda b, h: (0, h)), blk, blk, blk, blk],
        out_specs=blk,
        out_shape=jax.ShapeDtypeStruct((T, 512), BF16),
        scratch_shapes=[pltpu.VMEM((S, LANES), F32), pltpu.VMEM((RET_CHUNK, LANES), F32)],
        compiler_params=_cparams(2), name="retention",
    )(dec, gain, rq, rk, rv, rg)


def _merge_kernel(h_ref, u_ref, a_ref, r_ref, c_ref, wg_ref, wb_ref, wo_ref, mn_ref,
                  hn_ref, u2_ref):
    u = u_ref[...]
    merged = None
    for n, br in enumerate((a_ref, r_ref, c_ref)):
        logits = jnp.dot(u, wg_ref[:, n * D_MODEL:(n + 1) * D_MODEL], preferred_element_type=F32)
        gate = 1.0 / (1.0 + jnp.exp(-logits))
        term = gate * jnp.dot(br[...], wb_ref[n], preferred_element_type=F32)
        merged = term if merged is None else merged + term
    hn = h_ref[...] + jnp.dot(merged.astype(BF16), wo_ref[...], preferred_element_type=F32)
    hn_ref[...] = hn
    ms = jnp.mean(hn * hn, axis=-1, keepdims=True)
    u2_ref[...] = (hn * lax.rsqrt(ms + NORM_EPS) * mn_ref[...]).astype(BF16)


def _merge_call(h, u, a, r, c, wg, wb, wo, mlp_norm):
    T = h.shape[0]
    row = lambda w: pl.BlockSpec((TM, w), lambda i: (i, 0))
    return pl.pallas_call(
        _merge_kernel, grid=(T // TM,),
        in_specs=[row(D_MODEL), row(D_MODEL), row(512), row(512), row(512),
                  _resident(wg.shape), _resident(wb.shape), _resident(wo.shape),
                  _resident((1, D_MODEL))],
        out_specs=(row(D_MODEL), row(D_MODEL)),
        out_shape=(jax.ShapeDtypeStruct((T, D_MODEL), F32),
                   jax.ShapeDtypeStruct((T, D_MODEL), BF16)),
        compiler_params=_cparams(1), name="merge",
    )(h, u, a, r, c, wg, wb, wo, mlp_norm)


def _mlp_kernel(h_ref, u_ref, w1_ref, w2_ref, o_ref):
    u = u_ref[...]
    acc = h_ref[...]
    for c in range(D_FF // D_MODEL):
        cs = slice(c * D_MODEL, (c + 1) * D_MODEL)
        m = jnp.maximum(jnp.dot(u, w1_ref[:, cs], preferred_element_type=F32), 0.0)
        acc = acc + jnp.dot((m * m).astype(BF16), w2_ref[cs, :], preferred_element_type=F32)
    o_ref[...] = acc


def _mlp_call(h, u2, w1, w2):
    T = h.shape[0]
    row = pl.BlockSpec((TM, D_MODEL), lambda i: (i, 0))
    return pl.pallas_call(
        _mlp_kernel, grid=(T // TM,),
        in_specs=[row, row, _resident(w1.shape), _resident(w2.shape)],
        out_specs=row,
        out_shape=jax.ShapeDtypeStruct((T, D_MODEL), F32),
        compiler_params=_cparams(1), name="mlp",
    )(h, u2, w1, w2)


def _rope_tables(S):
    pos = jnp.arange(S, dtype=F32)
    lane = jnp.arange(LANES)
    l64 = lane % 64

    inv = 500000.0 ** (-jnp.arange(0, 16, 2, dtype=F32) / 16)
    ang = pos[:, None] * inv[None, :]
    cos, sin = jnp.cos(ang)[:, l64 % 8], jnp.sin(ang)[:, l64 % 8]
    cd = jnp.where(l64 < 16, cos, 1.0)
    ad = jnp.where(l64 < 8, -sin, 0.0)
    bd = jnp.where((l64 >= 8) & (l64 < 16), sin, 0.0)

    inv = 10000.0 ** (-jnp.arange(0, 32, 2, dtype=F32) / 32)
    row_pos = jnp.floor(pos / GRID_W)
    col_pos = pos - row_pos * GRID_W
    p2 = jnp.where((l64 < 32)[None, :], row_pos[:, None], col_pos[:, None])
    ang = p2 * inv[l64 % 16][None, :]
    cg = jnp.cos(ang)
    ag = jnp.where(l64 % 32 < 16, -jnp.sin(ang), 0.0)
    bg = jnp.where(l64 % 32 >= 16, jnp.sin(ang), 0.0)

    inv = 10000.0 ** (-jnp.arange(0, 128, 2, dtype=F32) / 128)
    ang = pos[:, None] * inv[l64][None, :]
    cr = jnp.cos(ang)
    sr = jnp.where(lane < 64, -jnp.sin(ang), jnp.sin(ang))
    return (cd, ad, bd, cg, ag, bg, cr, sr)


def _fast_flag(q_gain, k_gain):
    bound = (HEAD64 ** 0.5 * LOG2E * 1.02) * jnp.max(jnp.abs(q_gain)) * jnp.max(jnp.abs(k_gain))
    return (bound <= FAST_SOFTMAX_BOUND).astype(jnp.int32).reshape(1)


def kernel(x, attn_norm, w_in, diff_q_norm, diff_k_norm, diff_lam_q1, diff_lam_k1, diff_lam_q2, diff_lam_k2, diff_subln, ret_decay_fwd, ret_decay_bwd, ret_group_norm, gqa_q_norm, gqa_k_norm, w_branch, w_out, mlp_norm, w_mlp_in, w_mlp_out):
    B, S, D = x.shape
    T = B * S
    tabs = _rope_tables(S)
    blk = jnp.arange(LANES) // HEAD64
    gmat = jnp.where(blk[:, None] == blk[None, :], 1.0 / HEAD64, 0.0).astype(BF16)
    tile2 = lambda v: jnp.concatenate([v, v]).astype(F32)

    h = x.reshape(T, D)
    for l in range(DEPTH):
        lambda_init = 0.8 - 0.6 * math.exp(-0.3 * l)
        w_bf = w_in[l].astype(BF16)
        nrm = jnp.stack([tile2(diff_q_norm[l]), tile2(diff_k_norm[l]),
                         tile2(gqa_q_norm[l]), tile2(gqa_k_norm[l])])
        (u, dqT, dk, dvT, rq, rk, rv, rg, gqT, gk, gvT) = _proj_call(
            h, attn_norm[l].reshape(1, D).astype(F32), w_bf[:, :ATTN_COLS], gmat, nrm, tabs)

        lamv = jnp.stack([diff_lam_q1[l], diff_lam_k1[l], diff_lam_q2[l], diff_lam_k2[l]]).astype(F32)
        a = _diff_call(_fast_flag(diff_q_norm[l], diff_k_norm[l]), lamv,
                       diff_subln[l].reshape(1, LANES).astype(F32), dqT, dk, dvT, B, S, lambda_init)
        dec = jnp.stack([ret_decay_fwd[l], ret_decay_bwd[l]]).astype(F32)
        r = _ret_call(dec, ret_group_norm[l].reshape(1, 512).astype(F32), rq, rk, rv, rg, B, S)
        c = _gqa_call(_fast_flag(gqa_q_norm[l], gqa_k_norm[l]), gqT, gk, gvT, B, S)

        h, u2 = _merge_call(h, u, a, r, c, w_bf[:, ATTN_COLS:], w_branch[l].astype(BF16),
                            w_out[l].astype(BF16), mlp_norm[l].reshape(1, D).astype(F32))
        h = _mlp_call(h, u2, w_mlp_in[l].astype(BF16), w_mlp_out[l].astype(BF16))
    return h.reshape(B, S, D)
```

```python
import functools
import math

import jax
import jax.numpy as jnp
from jax import lax
from jax.experimental import pallas as pl
from jax.experimental.pallas import tpu as pltpu

F32 = jnp.float32
BF16 = jnp.bfloat16

D_MODEL = 1024
DEPTH = 2
GRID_W = 64
NORM_EPS = 1e-6
HEAD64 = 64
DIFF_HEADS = 4
RET_HEADS = 4
RET_CHUNK = 128
GQA_Q_HEADS = 8
GQA_GROUP = 4
N_BRANCHES = 3
BRANCH_WIDTH = 512
D_FF = 4 * D_MODEL
ATTN_COLS = 4352
LANES = 128

TM = 512
TK = 512
NQ = 1024
NC = 256
KV_UNROLL = 8
ONES_ROWS = 16
VROWS = LANES + ONES_ROWS
GVROWS = HEAD64 + ONES_ROWS
FAST_SOFTMAX_BOUND = 64.0
VMEM_LIMIT = 56 * 1024 * 1024
LOG2E = 1.4426950408889634


def _cparams(n_axes):
    return pltpu.CompilerParams(dimension_semantics=("arbitrary",) * n_axes,
                                vmem_limit_bytes=VMEM_LIMIT)


def _resident(shape):
    nd = len(shape)
    return pl.BlockSpec(shape, lambda *_: (0,) * nd, pipeline_mode=pl.Buffered(1))


def _proj_kernel(h_ref, an_ref, w_ref, g_ref, nrm_ref,
                 cd_ref, ad_ref, bd_ref, cg_ref, ag_ref, bg_ref, cr_ref, sr_ref,
                 u_ref, dqT_ref, dk_ref, dvT_ref, rq_ref, rk_ref, rv_ref, rg_ref,
                 gqT_ref, gk_ref, gvT_ref):
    x = h_ref[...]
    ms = jnp.mean(x * x, axis=-1, keepdims=True)
    u = (x * lax.rsqrt(ms + NORM_EPS) * an_ref[...]).astype(BF16)
    u_ref[...] = u

    def proj(c0, n):
        return jnp.dot(u, w_ref[:, c0:c0 + n], preferred_element_type=F32)

    gmat = g_ref[...]
    lo = lax.broadcasted_iota(jnp.int32, (1, LANES), 1) < HEAD64
    first_row = lax.broadcasted_iota(jnp.int32, (ONES_ROWS, TM), 0) == 0
    ones_rows = jnp.where(first_row, 1.0, 0.0).astype(BF16)

    def qk_norm(y, gain):
        msq = jnp.dot((y * y).astype(BF16), gmat, preferred_element_type=F32)
        return y * lax.rsqrt(msq + NORM_EPS) * gain

    def rope(y, c, a, b, d):
        return y * c + pltpu.roll(y, LANES - d, 1) * a + pltpu.roll(y, d, 1) * b

    qscale = HEAD64 ** -0.5 * LOG2E
    cd, ad, bd = cd_ref[...], ad_ref[...], bd_ref[...]
    cg, ag, bg = cg_ref[...], ag_ref[...], bg_ref[...]
    cr, sr = cr_ref[...], sr_ref[...]

    p = proj(0, 512)
    for hd in range(DIFF_HEADS):
        y = rope(qk_norm(p[:, hd * LANES:(hd + 1) * LANES], nrm_ref[0:1, :]), cd, ad, bd, 8) * qscale
        dqT_ref[hd, :, 0:TM] = jnp.where(lo, y, 0.0).T.astype(BF16)
        dqT_ref[hd, :, TM:2 * TM] = jnp.where(lo, 0.0, y).T.astype(BF16)
    p = proj(512, 512)
    for hd in range(DIFF_HEADS):
        y = rope(qk_norm(p[:, hd * LANES:(hd + 1) * LANES], nrm_ref[1:2, :]), cd, ad, bd, 8)
        dk_ref[:, hd * LANES:(hd + 1) * LANES] = y.astype(BF16)
    p = proj(1024, 512)
    for hd in range(DIFF_HEADS):
        dvT_ref[hd, 0:LANES, :] = p[:, hd * LANES:(hd + 1) * LANES].T.astype(BF16)
        dvT_ref[hd, LANES:VROWS, :] = ones_rows

    p = proj(1536, 512)
    for hd in range(RET_HEADS):
        y = p[:, hd * LANES:(hd + 1) * LANES]
        rq_ref[:, hd * LANES:(hd + 1) * LANES] = (y * cr + pltpu.roll(y, 64, 1) * sr).astype(BF16)
    p = proj(2048, 512)
    kscale = LANES ** -0.5
    for hd in range(RET_HEADS):
        y = p[:, hd * LANES:(hd + 1) * LANES]
        rk_ref[:, hd * LANES:(hd + 1) * LANES] = (
            (y * cr + pltpu.roll(y, 64, 1) * sr) * kscale).astype(BF16)
    rv_ref[...] = proj(2560, 512).astype(BF16)
    rg_ref[...] = proj(3072, 512).astype(BF16)

    p = proj(3584, 512)
    for c in range(4):
        y = rope(qk_norm(p[:, c * LANES:(c + 1) * LANES], nrm_ref[2:3, :]), cg, ag, bg, 16) * qscale
        ysw = pltpu.roll(y, 64, 1)
        if c < 2:
            z_even, z_odd = jnp.where(lo, y, 0.0), jnp.where(lo, ysw, 0.0)
        else:
            z_even, z_odd = jnp.where(lo, 0.0, ysw), jnp.where(lo, 0.0, y)
        for hh, z in ((2 * c, z_even), (2 * c + 1, z_odd)):
            zt = z.T.astype(BF16)
            for qb in range(TM // LANES):
                gqT_ref[qb, :, hh * LANES:(hh + 1) * LANES] = zt[:, qb * LANES:(qb + 1) * LANES]
    p = proj(4096, 256)
    gk_ref[...] = rope(qk_norm(p[:, 0:LANES], nrm_ref[3:4, :]), cg, ag, bg, 16).astype(BF16)
    vt = p[:, LANES:2 * LANES].T.astype(BF16)
    for g in range(2):
        gvT_ref[g, 0:HEAD64, :] = vt[g * HEAD64:(g + 1) * HEAD64]
        gvT_ref[g, HEAD64:GVROWS, :] = ones_rows


def _proj_call(h, attn_norm, w_attn, gmat, nrm, tabs):
    T = h.shape[0]
    S = tabs[0].shape[0]
    nt = T // TM
    tab_spec = pl.BlockSpec((TM, LANES), lambda i: (i % (S // TM), 0))
    row = lambda w: pl.BlockSpec((TM, w), lambda i: (i, 0))
    out_shape = (
        jax.ShapeDtypeStruct((T, D_MODEL), BF16),
        jax.ShapeDtypeStruct((nt, 4, LANES, 2 * TM), BF16),
        jax.ShapeDtypeStruct((T, 512), BF16),
        jax.ShapeDtypeStruct((nt, 4, VROWS, TM), BF16),
        jax.ShapeDtypeStruct((T, 512), BF16),
        jax.ShapeDtypeStruct((T, 512), BF16),
        jax.ShapeDtypeStruct((T, 512), BF16),
        jax.ShapeDtypeStruct((T, 512), BF16),
        jax.ShapeDtypeStruct((T // LANES, LANES, NQ), BF16),
        jax.ShapeDtypeStruct((T, LANES), BF16),
        jax.ShapeDtypeStruct((nt, 2, GVROWS, TM), BF16),
    )
    out_specs = (
        row(D_MODEL),
        pl.BlockSpec((None, 4, LANES, 2 * TM), lambda i: (i, 0, 0, 0)),
        row(512),
        pl.BlockSpec((None, 4, VROWS, TM), lambda i: (i, 0, 0, 0)),
        row(512), row(512), row(512), row(512),
        pl.BlockSpec((TM // LANES, LANES, NQ), lambda i: (i, 0, 0)),
        row(LANES),
        pl.BlockSpec((None, 2, GVROWS, TM), lambda i: (i, 0, 0, 0)),
    )
    in_specs = [row(D_MODEL), _resident((1, D_MODEL)), _resident(w_attn.shape),
                _resident(gmat.shape), _resident(nrm.shape)] + [tab_spec] * 8
    return pl.pallas_call(
        _proj_kernel, grid=(nt,), in_specs=in_specs, out_specs=out_specs,
        out_shape=out_shape, compiler_params=_cparams(1), name="proj",
    )(h, attn_norm, w_attn, gmat, nrm, *tabs)


def _attn_core(fast_ref, qT_ref, k_ref, vT_ref, m_ref, acc_ref):
    acc_ref[...] = jnp.zeros(acc_ref.shape, F32)
    n_steps = k_ref.shape[0] // TK
    n_groups = vT_ref.shape[1]
    gw = NQ // n_groups

    def kv(j):
        return k_ref[pl.ds(pl.multiple_of(j * TK, TK), TK), :], vT_ref[j]

    @pl.when(fast_ref[0] == 1)
    def _():
        def body(jj, carry):
            pv = [None] * n_groups
            for u in range(KV_UNROLL):
                k, vT = kv(jj * KV_UNROLL + u)
                s = jnp.dot(k, qT_ref[...], preferred_element_type=F32)
                p = jnp.exp2(s.astype(BF16))
                for g in range(n_groups):
                    t = jnp.dot(vT[g], p[:, g * gw:(g + 1) * gw], preferred_element_type=F32)
                    pv[g] = t if pv[g] is None else pv[g] + t
            for g in range(n_groups):
                acc_ref[:, g * gw:(g + 1) * gw] += pv[g]
            return carry

        lax.fori_loop(0, n_steps // KV_UNROLL, body, 0)

    @pl.when(fast_ref[0] == 0)
    def _():
        m_ref[...] = jnp.full(m_ref.shape, -1e30, F32)

        def body(j, carry):
            k, vT = kv(j)
            for c in range(NQ // NC):
                cs = slice(c * NC, (c + 1) * NC)
                s = jnp.dot(k, qT_ref[:, cs], preferred_element_type=F32)
                m_old = m_ref[:, cs]
                m_new = jnp.maximum(m_old, jnp.max(s, axis=0, keepdims=True))
                alpha = jnp.exp2(m_old - m_new)
                p = jnp.exp2(s - m_new).astype(BF16)
                pv = jnp.dot(vT[(c * NC) // gw], p, preferred_element_type=F32)
                acc_ref[:, cs] = acc_ref[:, cs] * alpha + pv
                m_ref[:, cs] = m_new
            return carry

        lax.fori_loop(0, n_steps, body, 0)


def _diff_kernel(fast_ref, lam_ref, sub_ref, qT_ref, k_ref, vT_ref, o_ref, m_ref, acc_ref, *,
                 lambda_init):
    _attn_core(fast_ref, qT_ref, k_ref, vT_ref, m_ref, acc_ref)

    lv = lam_ref[...]
    lam = (jnp.exp(jnp.sum(lv[0:1] * lv[1:2], axis=-1, keepdims=True))
           - jnp.exp(jnp.sum(lv[2:3] * lv[3:4], axis=-1, keepdims=True)) + lambda_init)
    tq = NQ // 2
    o1 = acc_ref[0:LANES, 0:tq] / acc_ref[LANES:LANES + 1, 0:tq]
    o2 = acc_ref[0:LANES, tq:NQ] / acc_ref[LANES:LANES + 1, tq:NQ]
    d = o1 - lam * o2
    ms = jnp.mean(d * d, axis=0, keepdims=True)
    dn = d * lax.rsqrt(ms + 1e-5)
    o_ref[...] = (dn.T * (sub_ref[...] * (1.0 - lambda_init))).astype(o_ref.dtype)


_SMEM_SPEC = pl.BlockSpec(memory_space=pltpu.SMEM)

def _attn_scratch(rows):
    return [pltpu.VMEM((1, NQ), F32), pltpu.VMEM((rows, NQ), F32)]


def _diff_call(fast, lamv, subln, dqT, dk, dvT, B, S, lambda_init):
    T = dk.shape[0]
    tq = NQ // 2
    nq = S // tq
    kern = functools.partial(_diff_kernel, lambda_init=lambda_init)
    return pl.pallas_call(
        kern, grid=(B, DIFF_HEADS, nq),
        in_specs=[
            _SMEM_SPEC, _resident(lamv.shape), _resident(subln.shape),
            pl.BlockSpec((None, None, LANES, NQ), lambda b, h, i: (b * nq + i, h, 0, 0)),
            pl.BlockSpec((S, LANES), lambda b, h, i: (b, h)),
            pl.BlockSpec((S // TM, 1, VROWS, TM), lambda b, h, i: (b, h, 0, 0)),
        ],
        out_specs=pl.BlockSpec((tq, LANES), lambda b, h, i: (b * nq + i, h)),
        out_shape=jax.ShapeDtypeStruct((T, 512), BF16),
        scratch_shapes=_attn_scratch(VROWS),
        compiler_params=_cparams(3), name="diff_attn",
    )(fast, lamv, subln, dqT, dk, dvT)


def _gqa_kernel(fast_ref, qT_ref, k_ref, vT_ref, o_ref, m_ref, acc_ref):
    _attn_core(fast_ref, qT_ref, k_ref, vT_ref, m_ref, acc_ref)
    for c in range(GQA_Q_HEADS // 2):
        parts = []
        for hh in (2 * c, 2 * c + 1):
            cols = slice(hh * LANES, (hh + 1) * LANES)
            parts.append(acc_ref[0:HEAD64, cols] / acc_ref[HEAD64:HEAD64 + 1, cols])
        o_ref[:, c * LANES:(c + 1) * LANES] = jnp.concatenate(parts, axis=0).T.astype(o_ref.dtype)


def _gqa_call(fast, gqT, gk, gvT, B, S):
    T = gk.shape[0]
    tq = NQ // GQA_Q_HEADS
    nq = S // tq
    return pl.pallas_call(
        _gqa_kernel, grid=(B, nq),
        in_specs=[
            _SMEM_SPEC,
            pl.BlockSpec((None, LANES, NQ), lambda b, i: (b * nq + i, 0, 0)),
            pl.BlockSpec((S, LANES), lambda b, i: (b, 0)),
            pl.BlockSpec((S // TM, 2, GVROWS, TM), lambda b, i: (b, 0, 0, 0)),
        ],
        out_specs=pl.BlockSpec((tq, 512), lambda b, i: (b * nq + i, 0)),
        out_shape=jax.ShapeDtypeStruct((T, 512), BF16),
        scratch_shapes=_attn_scratch(GVROWS),
        compiler_params=_cparams(2), name="gqa_attn",
    )(fast, gqT, gk, gvT)


def _ret_kernel(dec_ref, gain_ref, q_ref, k_ref, v_ref, g_ref, o_ref, of_ref, st_ref):
    C = RET_CHUNK
    hd = pl.program_id(1)
    lf = -jnp.exp(jnp.full((C, C), dec_ref[0, hd], F32))
    lb = -jnp.exp(jnp.full((C, C), dec_ref[1, hd], F32))
    ii = lax.broadcasted_iota(jnp.int32, (C, C), 0).astype(F32)
    jj = lax.broadcasted_iota(jnp.int32, (C, C), 1).astype(F32)
    diff = ii - jj
    decay = jnp.where(diff >= 0, jnp.exp(jnp.maximum(diff, 0.0) * lf),
                      jnp.exp(jnp.maximum(-diff, 0.0) * lb))
    xi_f, zeta_f = jnp.exp((ii + 1.0) * lf), jnp.exp((C - 1.0 - ii) * lf)
    xi_b, zeta_b = jnp.exp((C - ii) * lb), jnp.exp(ii * lb)
    dec_f, dec_b = jnp.exp(C * lf), jnp.exp(C * lb)
    n_chunks = q_ref.shape[0] // C
    nt = (((1,), (1,)), ((), ()))
    tn = (((0,), (0,)), ((), ()))

    def chunk(n):
        rows = pl.ds(pl.multiple_of(n * C, C), C)
        return rows, q_ref[rows, :], k_ref[rows, :], v_ref[rows, :]

    def state_update(k, v, zeta, dec):
        vz = (v.astype(F32) * zeta).astype(BF16)
        kv = lax.dot_general(k, vz, tn, preferred_element_type=F32)
        st_ref[...] = st_ref[...] * dec + kv

    def fwd(n, carry):
        rows, q, k, v = chunk(n)
        a = lax.dot_general(q, k, nt, preferred_element_type=F32) * decay
        o = jnp.dot(a.astype(BF16), v, preferred_element_type=F32)
        o += jnp.dot(q, st_ref[...].astype(BF16), preferred_element_type=F32) * xi_f
        of_ref[rows, :] = o
        state_update(k, v, zeta_f, dec_f)
        return carry

    def bwd(t, carry):
        rows, q, k, v = chunk(n_chunks - 1 - t)
        o = of_ref[rows, :] + jnp.dot(q, st_ref[...].astype(BF16), preferred_element_type=F32) * xi_b
        state_update(k, v, zeta_b, dec_b)
        mu = jnp.mean(o, axis=-1, keepdims=True)
        oc = o - mu
        var = jnp.mean(oc * oc, axis=-1, keepdims=True)
        y = oc * lax.rsqrt(var + 1e-5) * gain_ref[...]
        g = g_ref[rows, :].astype(F32)
        o_ref[rows, :] = (g / (1.0 + jnp.exp(-g)) * y).astype(o_ref.dtype)
        return carry

    st_ref[...] = jnp.zeros(st_ref.shape, F32)
    lax.fori_loop(0, n_chunks, fwd, 0)
    st_ref[...] = jnp.zeros(st_ref.shape, F32)
    lax.fori_loop(0, n_chunks, bwd, 0)


def _ret_call(dec, gain, rq, rk, rv, rg, B, S):
    T = rq.shape[0]
    blk = pl.BlockSpec((S, LANES), lambda b, h: (b, h))
    return pl.pallas_call(
        _ret_kernel, grid=(B, RET_HEADS),
        in_specs=[pl.BlockSpec(memory_space=pltpu.SMEM),
                  pl.BlockSpec((1, LANES), lambda b, h: (0, h)), blk, blk, blk, blk],
        out_specs=blk,
        out_shape=jax.ShapeDtypeStruct((T, 512), BF16),
        scratch_shapes=[pltpu.VMEM((S, LANES), F32), pltpu.VMEM((RET_CHUNK, LANES), F32)],
        compiler_params=_cparams(2), name="retention",
    )(dec, gain, rq, rk, rv, rg)


def _merge_kernel(h_ref, u_ref, a_ref, r_ref, c_ref, wg_ref, wb_ref, wo_ref, mn_ref,
                  hn_ref, u2_ref):
    u = u_ref[...]
    merged = None
    for n, br in enumerate((a_ref, r_ref, c_ref)):
        logits = jnp.dot(u, wg_ref[:, n * D_MODEL:(n + 1) * D_MODEL], preferred_element_type=F32)
        gate = 1.0 / (1.0 + jnp.exp(-logits))
        term = gate * jnp.dot(br[...], wb_ref[n], preferred_element_type=F32)
        merged = term if merged is None else merged + term
    hn = h_ref[...] + jnp.dot(merged.astype(BF16), wo_ref[...], preferred_element_type=F32)
    hn_ref[...] = hn
    ms = jnp.mean(hn * hn, axis=-1, keepdims=True)
    u2_ref[...] = (hn * lax.rsqrt(ms + NORM_EPS) * mn_ref[...]).astype(BF16)


def _merge_call(h, u, a, r, c, wg, wb, wo, mlp_norm):
    T = h.shape[0]
    row = lambda w: pl.BlockSpec((TM, w), lambda i: (i, 0))
    return pl.pallas_call(
        _merge_kernel, grid=(T // TM,),
        in_specs=[row(D_MODEL), row(D_MODEL), row(512), row(512), row(512),
                  _resident(wg.shape), _resident(wb.shape), _resident(wo.shape),
                  _resident((1, D_MODEL))],
        out_specs=(row(D_MODEL), row(D_MODEL)),
        out_shape=(jax.ShapeDtypeStruct((T, D_MODEL), F32),
                   jax.ShapeDtypeStruct((T, D_MODEL), BF16)),
        compiler_params=_cparams(1), name="merge",
    )(h, u, a, r, c, wg, wb, wo, mlp_norm)


def _mlp_kernel(h_ref, u_ref, w1_ref, w2_ref, o_ref):
    u = u_ref[...]
    acc = h_ref[...]
    for c in range(D_FF // D_MODEL):
        cs = slice(c * D_MODEL, (c + 1) * D_MODEL)
        m = jnp.maximum(jnp.dot(u, w1_ref[:, cs], preferred_element_type=F32), 0.0)
        acc = acc + jnp.dot((m * m).astype(BF16), w2_ref[cs, :], preferred_element_type=F32)
    o_ref[...] = acc


def _mlp_call(h, u2, w1, w2):
    T = h.shape[0]
    row = pl.BlockSpec((TM, D_MODEL), lambda i: (i, 0))
    return pl.pallas_call(
        _mlp_kernel, grid=(T // TM,),
        in_specs=[row, row, _resident(w1.shape), _resident(w2.shape)],
        out_specs=row,
        out_shape=jax.ShapeDtypeStruct((T, D_MODEL), F32),
        compiler_params=_cparams(1), name="mlp",
    )(h, u2, w1, w2)


def _rope_tables(S):
    pos = jnp.arange(S, dtype=F32)
    lane = jnp.arange(LANES)
    l64 = lane % 64

    inv = 500000.0 ** (-jnp.arange(0, 16, 2, dtype=F32) / 16)
    ang = pos[:, None] * inv[None, :]
    cos, sin = jnp.cos(ang)[:, l64 % 8], jnp.sin(ang)[:, l64 % 8]
    cd = jnp.where(l64 < 16, cos, 1.0)
    ad = jnp.where(l64 < 8, -sin, 0.0)
    bd = jnp.where((l64 >= 8) & (l64 < 16), sin, 0.0)

    inv = 10000.0 ** (-jnp.arange(0, 32, 2, dtype=F32) / 32)
    row_pos = jnp.floor(pos / GRID_W)
    col_pos = pos - row_pos * GRID_W
    p2 = jnp.where((l64 < 32)[None, :], row_pos[:, None], col_pos[:, None])
    ang = p2 * inv[l64 % 16][None, :]
    cg = jnp.cos(ang)
    ag = jnp.where(l64 % 32 < 16, -jnp.sin(ang), 0.0)
    bg = jnp.where(l64 % 32 >= 16, jnp.sin(ang), 0.0)

    inv = 10000.0 ** (-jnp.arange(0, 128, 2, dtype=F32) / 128)
    ang = pos[:, None] * inv[l64][None, :]
    cr = jnp.cos(ang)
    sr = jnp.where(lane < 64, -jnp.sin(ang), jnp.sin(ang))
    return (cd, ad, bd, cg, ag, bg, cr, sr)


def _fast_flag(q_gain, k_gain):
    bound = (HEAD64 ** 0.5 * LOG2E * 1.02) * jnp.max(jnp.abs(q_gain)) * jnp.max(jnp.abs(k_gain))
    return (bound <= FAST_SOFTMAX_BOUND).astype(jnp.int32).reshape(1)


def kernel(x, attn_norm, w_in, diff_q_norm, diff_k_norm, diff_lam_q1, diff_lam_k1, diff_lam_q2, diff_lam_k2, diff_subln, ret_decay_fwd, ret_decay_bwd, ret_group_norm, gqa_q_norm, gqa_k_norm, w_branch, w_out, mlp_norm, w_mlp_in, w_mlp_out):
    B, S, D = x.shape
    T = B * S
    tabs = _rope_tables(S)
    blk = jnp.arange(LANES) // HEAD64
    gmat = jnp.where(blk[:, None] == blk[None, :], 1.0 / HEAD64, 0.0).astype(BF16)
    tile2 = lambda v: jnp.concatenate([v, v]).astype(F32)

    h = x.reshape(T, D)
    for l in range(DEPTH):
        lambda_init = 0.8 - 0.6 * math.exp(-0.3 * l)
        w_bf = w_in[l].astype(BF16)
        nrm = jnp.stack([tile2(diff_q_norm[l]), tile2(diff_k_norm[l]),
                         tile2(gqa_q_norm[l]), tile2(gqa_k_norm[l])])
        (u, dqT, dk, dvT, rq, rk, rv, rg, gqT, gk, gvT) = _proj_call(
            h, attn_norm[l].reshape(1, D).astype(F32), w_bf[:, :ATTN_COLS], gmat, nrm, tabs)

        lamv = jnp.stack([diff_lam_q1[l], diff_lam_k1[l], diff_lam_q2[l], diff_lam_k2[l]]).astype(F32)
        a = _diff_call(_fast_flag(diff_q_norm[l], diff_k_norm[l]), lamv,
                       diff_subln[l].reshape(1, LANES).astype(F32), dqT, dk, dvT, B, S, lambda_init)
        dec = jnp.stack([ret_decay_fwd[l], ret_decay_bwd[l]]).astype(F32)
        r = _ret_call(dec, ret_group_norm[l].reshape(1, 512).astype(F32), rq, rk, rv, rg, B, S)
        c = _gqa_call(_fast_flag(gqa_q_norm[l], gqa_k_norm[l]), gqT, gk, gvT, B, S)

        h, u2 = _merge_call(h, u, a, r, c, w_bf[:, ATTN_COLS:], w_branch[l].astype(BF16),
                            w_out[l].astype(BF16), mlp_norm[l].reshape(1, D).astype(F32))
        h = _mlp_call(h, u2, w_mlp_in[l].astype(BF16), w_mlp_out[l].astype(BF16))
    return h.reshape(B, S, D)
```

```python
import functools
import math

import jax
import jax.numpy as jnp
from jax import lax
from jax.experimental import pallas as pl
from jax.experimental.pallas import tpu as pltpu

F32 = jnp.float32
BF16 = jnp.bfloat16

D_MODEL = 1024
DEPTH = 2
GRID_W = 64
NORM_EPS = 1e-6
HEAD64 = 64
DIFF_HEADS = 4
RET_HEADS = 4
RET_CHUNK = 128
GQA_Q_HEADS = 8
GQA_GROUP = 4
N_BRANCHES = 3
BRANCH_WIDTH = 512
D_FF = 4 * D_MODEL
ATTN_COLS = 4352
LANES = 128

TM = 512
TK = 512
NQ = 1024
NC = 256
KV_UNROLL = 8
ONES_ROWS = 16
VROWS = LANES + ONES_ROWS
GVROWS = HEAD64 + ONES_ROWS
FAST_SOFTMAX_BOUND = 64.0
VMEM_LIMIT = 56 * 1024 * 1024
LOG2E = 1.4426950408889634


def _cparams(n_axes):
    return pltpu.CompilerParams(dimension_semantics=("arbitrary",) * n_axes,
                                vmem_limit_bytes=VMEM_LIMIT)


def _resident(shape):
    nd = len(shape)
    return pl.BlockSpec(shape, lambda *_: (0,) * nd, pipeline_mode=pl.Buffered(1))


def _proj_kernel(h_ref, an_ref, w_ref, g_ref, nrm_ref,
                 cd_ref, ad_ref, bd_ref, cg_ref, ag_ref, bg_ref, cr_ref, sr_ref,
                 u_ref, dqT_ref, dk_ref, dvT_ref, rq_ref, rkT_ref, rv_ref, rg_ref,
                 gqT_ref, gk_ref, gvT_ref):
    x = h_ref[...]
    ms = jnp.mean(x * x, axis=-1, keepdims=True)
    u = (x * lax.rsqrt(ms + NORM_EPS) * an_ref[...]).astype(BF16)
    u_ref[...] = u

    def proj(c0, n):
        return jnp.dot(u, w_ref[:, c0:c0 + n], preferred_element_type=F32)

    gmat = g_ref[...]
    lo = lax.broadcasted_iota(jnp.int32, (1, LANES), 1) < HEAD64
    first_row = lax.broadcasted_iota(jnp.int32, (ONES_ROWS, TM), 0) == 0
    ones_rows = jnp.where(first_row, 1.0, 0.0).astype(BF16)

    def qk_norm(y, gain):
        msq = jnp.dot((y * y).astype(BF16), gmat, preferred_element_type=F32)
        return y * lax.rsqrt(msq + NORM_EPS) * gain

    def rope(y, c, a, b, d):
        return y * c + pltpu.roll(y, LANES - d, 1) * a + pltpu.roll(y, d, 1) * b

    qscale = HEAD64 ** -0.5 * LOG2E
    cd, ad, bd = cd_ref[...], ad_ref[...], bd_ref[...]
    cg, ag, bg = cg_ref[...], ag_ref[...], bg_ref[...]
    cr, sr = cr_ref[...], sr_ref[...]

    p = proj(0, 512)
    for hd in range(DIFF_HEADS):
        y = rope(qk_norm(p[:, hd * LANES:(hd + 1) * LANES], nrm_ref[0:1, :]), cd, ad, bd, 8) * qscale
        dqT_ref[hd, :, 0:TM] = jnp.where(lo, y, 0.0).T.astype(BF16)
        dqT_ref[hd, :, TM:2 * TM] = jnp.where(lo, 0.0, y).T.astype(BF16)
    p = proj(512, 512)
    for hd in range(DIFF_HEADS):
        y = rope(qk_norm(p[:, hd * LANES:(hd + 1) * LANES], nrm_ref[1:2, :]), cd, ad, bd, 8)
        dk_ref[:, hd * LANES:(hd + 1) * LANES] = y.astype(BF16)
    p = proj(1024, 512)
    for hd in range(DIFF_HEADS):
        dvT_ref[hd, 0:LANES, :] = p[:, hd * LANES:(hd + 1) * LANES].T.astype(BF16)
        dvT_ref[hd, LANES:VROWS, :] = ones_rows

    p = proj(1536, 512)
    for hd in range(RET_HEADS):
        y = p[:, hd * LANES:(hd + 1) * LANES]
        rq_ref[:, hd * LANES:(hd + 1) * LANES] = (y * cr + pltpu.roll(y, 64, 1) * sr).astype(BF16)
    p = proj(2048, 512)
    kscale = LANES ** -0.5
    for hd in range(RET_HEADS):
        y = p[:, hd * LANES:(hd + 1) * LANES]
        rkT_ref[hd] = ((y * cr + pltpu.roll(y, 64, 1) * sr) * kscale).T.astype(BF16)
    rv_ref[...] = proj(2560, 512).astype(BF16)
    rg_ref[...] = proj(3072, 512).astype(BF16)

    p = proj(3584, 512)
    for c in range(4):
        y = rope(qk_norm(p[:, c * LANES:(c + 1) * LANES], nrm_ref[2:3, :]), cg, ag, bg, 16) * qscale
        ysw = pltpu.roll(y, 64, 1)
        if c < 2:
            z_even, z_odd = jnp.where(lo, y, 0.0), jnp.where(lo, ysw, 0.0)
        else:
            z_even, z_odd = jnp.where(lo, 0.0, ysw), jnp.where(lo, 0.0, y)
        for hh, z in ((2 * c, z_even), (2 * c + 1, z_odd)):
            zt = z.T.astype(BF16)
            for qb in range(TM // LANES):
                gqT_ref[qb, :, hh * LANES:(hh + 1) * LANES] = zt[:, qb * LANES:(qb + 1) * LANES]
    p = proj(4096, 256)
    gk_ref[...] = rope(qk_norm(p[:, 0:LANES], nrm_ref[3:4, :]), cg, ag, bg, 16).astype(BF16)
    vt = p[:, LANES:2 * LANES].T.astype(BF16)
    for g in range(2):
        gvT_ref[g, 0:HEAD64, :] = vt[g * HEAD64:(g + 1) * HEAD64]
        gvT_ref[g, HEAD64:GVROWS, :] = ones_rows


def _proj_call(h, attn_norm, w_attn, gmat, nrm, tabs):
    T = h.shape[0]
    S = tabs[0].shape[0]
    nt = T // TM
    tab_spec = pl.BlockSpec((TM, LANES), lambda i: (i % (S // TM), 0))
    row = lambda w: pl.BlockSpec((TM, w), lambda i: (i, 0))
    out_shape = (
        jax.ShapeDtypeStruct((T, D_MODEL), BF16),
        jax.ShapeDtypeStruct((nt, 4, LANES, 2 * TM), BF16),
        jax.ShapeDtypeStruct((T, 512), BF16),
        jax.ShapeDtypeStruct((nt, 4, VROWS, TM), BF16),
        jax.ShapeDtypeStruct((T, 512), BF16),
        jax.ShapeDtypeStruct((nt, 4, LANES, TM), BF16),
        jax.ShapeDtypeStruct((T, 512), BF16),
        jax.ShapeDtypeStruct((T, 512), BF16),
        jax.ShapeDtypeStruct((T // LANES, LANES, NQ), BF16),
        jax.ShapeDtypeStruct((T, LANES), BF16),
        jax.ShapeDtypeStruct((nt, 2, GVROWS, TM), BF16),
    )
    out_specs = (
        row(D_MODEL),
        pl.BlockSpec((None, 4, LANES, 2 * TM), lambda i: (i, 0, 0, 0)),
        row(512),
        pl.BlockSpec((None, 4, VROWS, TM), lambda i: (i, 0, 0, 0)),
        row(512), pl.BlockSpec((None, 4, LANES, TM), lambda i: (i, 0, 0, 0)), row(512), row(512),
        pl.BlockSpec((TM // LANES, LANES, NQ), lambda i: (i, 0, 0)),
        row(LANES),
        pl.BlockSpec((None, 2, GVROWS, TM), lambda i: (i, 0, 0, 0)),
    )
    in_specs = [row(D_MODEL), _resident((1, D_MODEL)), _resident(w_attn.shape),
                _resident(gmat.shape), _resident(nrm.shape)] + [tab_spec] * 8
    return pl.pallas_call(
        _proj_kernel, grid=(nt,), in_specs=in_specs, out_specs=out_specs,
        out_shape=out_shape, compiler_params=_cparams(1), name="proj",
    )(h, attn_norm, w_attn, gmat, nrm, *tabs)


def _attn_core(fast_ref, qT_ref, k_ref, vT_ref, m_ref, acc_ref):
    acc_ref[...] = jnp.zeros(acc_ref.shape, F32)
    n_steps = k_ref.shape[0] // TK
    n_groups = vT_ref.shape[1]
    gw = NQ // n_groups

    def kv(j):
        return k_ref[pl.ds(pl.multiple_of(j * TK, TK), TK), :], vT_ref[j]

    @pl.when(fast_ref[0] == 1)
    def _():
        def body(jj, carry):
            pv = [None] * n_groups
            for u in range(KV_UNROLL):
                k, vT = kv(jj * KV_UNROLL + u)
                s = jnp.dot(k, qT_ref[...], preferred_element_type=F32)
                p = jnp.exp2(s).astype(BF16)
                for g in range(n_groups):
                    t = jnp.dot(vT[g], p[:, g * gw:(g + 1) * gw], preferred_element_type=F32)
                    pv[g] = t if pv[g] is None else pv[g] + t
            for g in range(n_groups):
                acc_ref[:, g * gw:(g + 1) * gw] += pv[g]
            return carry

        lax.fori_loop(0, n_steps // KV_UNROLL, body, 0)

    @pl.when(fast_ref[0] == 0)
    def _():
        m_ref[...] = jnp.full(m_ref.shape, -1e30, F32)

        def body(j, carry):
            k, vT = kv(j)
            for c in range(NQ // NC):
                cs = slice(c * NC, (c + 1) * NC)
                s = jnp.dot(k, qT_ref[:, cs], preferred_element_type=F32)
                m_old = m_ref[:, cs]
                m_new = jnp.maximum(m_old, jnp.max(s, axis=0, keepdims=True))
                alpha = jnp.exp2(m_old - m_new)
                p = jnp.exp2(s - m_new).astype(BF16)
                pv = jnp.dot(vT[(c * NC) // gw], p, preferred_element_type=F32)
                acc_ref[:, cs] = acc_ref[:, cs] * alpha + pv
                m_ref[:, cs] = m_new
            return carry

        lax.fori_loop(0, n_steps, body, 0)


def _diff_kernel(fast_ref, lam_ref, sub_ref, qT_ref, k_ref, vT_ref, o_ref, m_ref, acc_ref, *,
                 lambda_init):
    _attn_core(fast_ref, qT_ref, k_ref, vT_ref, m_ref, acc_ref)

    lv = lam_ref[...]
    lam = (jnp.exp(jnp.sum(lv[0:1] * lv[1:2], axis=-1, keepdims=True))
           - jnp.exp(jnp.sum(lv[2:3] * lv[3:4], axis=-1, keepdims=True)) + lambda_init)
    tq = NQ // 2
    o1 = acc_ref[0:LANES, 0:tq] / acc_ref[LANES:LANES + 1, 0:tq]
    o2 = acc_ref[0:LANES, tq:NQ] / acc_ref[LANES:LANES + 1, tq:NQ]
    d = o1 - lam * o2
    ms = jnp.mean(d * d, axis=0, keepdims=True)
    dn = d * lax.rsqrt(ms + 1e-5)
    o_ref[...] = (dn.T * (sub_ref[...] * (1.0 - lambda_init))).astype(o_ref.dtype)


_SMEM_SPEC = pl.BlockSpec(memory_space=pltpu.SMEM)

def _attn_scratch(rows):
    return [pltpu.VMEM((1, NQ), F32), pltpu.VMEM((rows, NQ), F32)]


def _diff_call(fast, lamv, subln, dqT, dk, dvT, B, S, lambda_init):
    T = dk.shape[0]
    tq = NQ // 2
    nq = S // tq
    kern = functools.partial(_diff_kernel, lambda_init=lambda_init)
    return pl.pallas_call(
        kern, grid=(B, DIFF_HEADS, nq),
        in_specs=[
            _SMEM_SPEC, _resident(lamv.shape), _resident(subln.shape),
            pl.BlockSpec((None, None, LANES, NQ), lambda b, h, i: (b * nq + i, h, 0, 0)),
            pl.BlockSpec((S, LANES), lambda b, h, i: (b, h)),
            pl.BlockSpec((S // TM, 1, VROWS, TM), lambda b, h, i: (b, h, 0, 0)),
        ],
        out_specs=pl.BlockSpec((tq, LANES), lambda b, h, i: (b * nq + i, h)),
        out_shape=jax.ShapeDtypeStruct((T, 512), BF16),
        scratch_shapes=_attn_scratch(VROWS),
        compiler_params=_cparams(3), name="diff_attn",
    )(fast, lamv, subln, dqT, dk, dvT)


def _gqa_kernel(fast_ref, qT_ref, k_ref, vT_ref, o_ref, m_ref, acc_ref):
    _attn_core(fast_ref, qT_ref, k_ref, vT_ref, m_ref, acc_ref)
    for c in range(GQA_Q_HEADS // 2):
        parts = []
        for hh in (2 * c, 2 * c + 1):
            cols = slice(hh * LANES, (hh + 1) * LANES)
            parts.append(acc_ref[0:HEAD64, cols] / acc_ref[HEAD64:HEAD64 + 1, cols])
        o_ref[:, c * LANES:(c + 1) * LANES] = jnp.concatenate(parts, axis=0).T.astype(o_ref.dtype)


def _gqa_call(fast, gqT, gk, gvT, B, S):
    T = gk.shape[0]
    tq = NQ // GQA_Q_HEADS
    nq = S // tq
    return pl.pallas_call(
        _gqa_kernel, grid=(B, nq),
        in_specs=[
            _SMEM_SPEC,
            pl.BlockSpec((None, LANES, NQ), lambda b, i: (b * nq + i, 0, 0)),
            pl.BlockSpec((S, LANES), lambda b, i: (b, 0)),
            pl.BlockSpec((S // TM, 2, GVROWS, TM), lambda b, i: (b, 0, 0, 0)),
        ],
        out_specs=pl.BlockSpec((tq, 512), lambda b, i: (b * nq + i, 0)),
        out_shape=jax.ShapeDtypeStruct((T, 512), BF16),
        scratch_shapes=_attn_scratch(GVROWS),
        compiler_params=_cparams(2), name="gqa_attn",
    )(fast, gqT, gk, gvT)


def _ret_kernel(dec_ref, gain_ref, q_ref, kT_ref, v_ref, g_ref, o_ref, sf_ref, sb_ref):
    C = RET_CHUNK
    hd = pl.program_id(1)
    lf = -jnp.exp(jnp.full((C, C), dec_ref[0, hd], F32))
    lb = -jnp.exp(jnp.full((C, C), dec_ref[1, hd], F32))
    ii = lax.broadcasted_iota(jnp.int32, (C, C), 0).astype(F32)
    jj = lax.broadcasted_iota(jnp.int32, (C, C), 1).astype(F32)
    diff = ii - jj
    decay = jnp.where(diff >= 0, jnp.exp(jnp.maximum(diff, 0.0) * lf),
                      jnp.exp(jnp.maximum(-diff, 0.0) * lb))
    xi_f, zeta_f = jnp.exp((ii + 1.0) * lf), jnp.exp((C - 1.0 - ii) * lf)
    xi_b, zeta_b = jnp.exp((C - ii) * lb), jnp.exp(ii * lb)
    dec_f, dec_b = jnp.exp(C * lf), jnp.exp(C * lb)
    n_tiles = kT_ref.shape[0]
    per_tile = TM // C

    def rows(t, c):
        return pl.ds(pl.multiple_of(t * TM + c * C, C), C)

    def scan(t, state, zeta, dec, s_ref, order):
        kvs = {}
        for c in order:
            vz = (v_ref[rows(t, c), :].astype(F32) * zeta).astype(BF16)
            kvs[c] = jnp.dot(kT_ref[t, :, c * C:(c + 1) * C], vz, preferred_element_type=F32)
        for c in order:
            s_ref[t * per_tile + c] = state.astype(BF16)
            state = state * dec + kvs[c]
        return state

    zero = jnp.zeros((C, C), F32)
    lax.fori_loop(0, n_tiles,
                  lambda t, st: scan(t, st, zeta_f, dec_f, sf_ref, range(per_tile)), zero)
    lax.fori_loop(0, n_tiles,
                  lambda i, st: scan(n_tiles - 1 - i, st, zeta_b, dec_b, sb_ref,
                                     range(per_tile - 1, -1, -1)), zero)

    def outputs(t, carry):
        for c in range(per_tile):
            r = rows(t, c)
            q, v = q_ref[r, :], v_ref[r, :]
            n = t * per_tile + c
            a = jnp.dot(q, kT_ref[t, :, c * C:(c + 1) * C], preferred_element_type=F32) * decay
            o = (jnp.dot(a.astype(BF16), v, preferred_element_type=F32)
                 + jnp.dot(q, sf_ref[n], preferred_element_type=F32) * xi_f
                 + jnp.dot(q, sb_ref[n], preferred_element_type=F32) * xi_b)
            mu = jnp.mean(o, axis=-1, keepdims=True)
            oc = o - mu
            var = jnp.mean(oc * oc, axis=-1, keepdims=True)
            y = oc * lax.rsqrt(var + 1e-5) * gain_ref[...]
            g = g_ref[r, :].astype(F32)
            o_ref[r, :] = (g / (1.0 + jnp.exp(-g)) * y).astype(o_ref.dtype)
        return carry

    lax.fori_loop(0, n_tiles, outputs, 0)


def _ret_call(dec, gain, rq, rkT, rv, rg, B, S):
    T = rq.shape[0]
    blk = pl.BlockSpec((S, LANES), lambda b, h: (b, h))
    n_chunks = S // RET_CHUNK
    return pl.pallas_call(
        _ret_kernel, grid=(B, RET_HEADS),
        in_specs=[pl.BlockSpec(memory_space=pltpu.SMEM),
                  pl.BlockSpec((1, LANES), lambda b, h: (0, h)), blk,
                  pl.BlockSpec((S // TM, None, LANES, TM), lambda b, h: (b, h, 0, 0)), blk, blk],
        out_specs=blk,
        out_shape=jax.ShapeDtypeStruct((T, 512), BF16),
        scratch_shapes=[pltpu.VMEM((n_chunks, RET_CHUNK, LANES), BF16),
                        pltpu.VMEM((n_chunks, RET_CHUNK, LANES), BF16)],
        compiler_params=_cparams(2), name="retention",
    )(dec, gain, rq, rkT, rv, rg)


def _merge_kernel(h_ref, u_ref, a_ref, r_ref, c_ref, wg_ref, wb_ref, wo_ref, mn_ref,
                  hn_ref, u2_ref):
    u = u_ref[...]
    merged = None
    for n, br in enumerate((a_ref, r_ref, c_ref)):
        logits = jnp.dot(u, wg_ref[:, n * D_MODEL:(n + 1) * D_MODEL], preferred_element_type=F32)
        gate = 1.0 / (1.0 + jnp.exp(-logits))
        term = gate * jnp.dot(br[...], wb_ref[n], preferred_element_type=F32)
        merged = term if merged is None else merged + term
    hn = h_ref[...] + jnp.dot(merged.astype(BF16), wo_ref[...], preferred_element_type=F32)
    hn_ref[...] = hn
    ms = jnp.mean(hn * hn, axis=-1, keepdims=True)
    u2_ref[...] = (hn * lax.rsqrt(ms + NORM_EPS) * mn_ref[...]).astype(BF16)


def _merge_call(h, u, a, r, c, wg, wb, wo, mlp_norm):
    T = h.shape[0]
    row = lambda w: pl.BlockSpec((TM, w), lambda i: (i, 0))
    return pl.pallas_call(
        _merge_kernel, grid=(T // TM,),
        in_specs=[row(D_MODEL), row(D_MODEL), row(512), row(512), row(512),
                  _resident(wg.shape), _resident(wb.shape), _resident(wo.shape),
                  _resident((1, D_MODEL))],
        out_specs=(row(D_MODEL), row(D_MODEL)),
        out_shape=(jax.ShapeDtypeStruct((T, D_MODEL), F32),
                   jax.ShapeDtypeStruct((T, D_MODEL), BF16)),
        compiler_params=_cparams(1), name="merge",
    )(h, u, a, r, c, wg, wb, wo, mlp_norm)


def _mlp_kernel(h_ref, u_ref, w1_ref, w2_ref, o_ref):
    u = u_ref[...]
    acc = h_ref[...]
    for c in range(D_FF // D_MODEL):
        cs = slice(c * D_MODEL, (c + 1) * D_MODEL)
        m = jnp.maximum(jnp.dot(u, w1_ref[:, cs], preferred_element_type=F32), 0.0)
        acc = acc + jnp.dot((m * m).astype(BF16), w2_ref[cs, :], preferred_element_type=F32)
    o_ref[...] = acc


def _mlp_call(h, u2, w1, w2):
    T = h.shape[0]
    row = pl.BlockSpec((TM, D_MODEL), lambda i: (i, 0))
    return pl.pallas_call(
        _mlp_kernel, grid=(T // TM,),
        in_specs=[row, row, _resident(w1.shape), _resident(w2.shape)],
        out_specs=row,
        out_shape=jax.ShapeDtypeStruct((T, D_MODEL), F32),
        compiler_params=_cparams(1), name="mlp",
    )(h, u2, w1, w2)


def _rope_tables(S):
    pos = jnp.arange(S, dtype=F32)
    lane = jnp.arange(LANES)
    l64 = lane % 64

    inv = 500000.0 ** (-jnp.arange(0, 16, 2, dtype=F32) / 16)
    ang = pos[:, None] * inv[None, :]
    cos, sin = jnp.cos(ang)[:, l64 % 8], jnp.sin(ang)[:, l64 % 8]
    cd = jnp.where(l64 < 16, cos, 1.0)
    ad = jnp.where(l64 < 8, -sin, 0.0)
    bd = jnp.where((l64 >= 8) & (l64 < 16), sin, 0.0)

    inv = 10000.0 ** (-jnp.arange(0, 32, 2, dtype=F32) / 32)
    row_pos = jnp.floor(pos / GRID_W)
    col_pos = pos - row_pos * GRID_W
    p2 = jnp.where((l64 < 32)[None, :], row_pos[:, None], col_pos[:, None])
    ang = p2 * inv[l64 % 16][None, :]
    cg = jnp.cos(ang)
    ag = jnp.where(l64 % 32 < 16, -jnp.sin(ang), 0.0)
    bg = jnp.where(l64 % 32 >= 16, jnp.sin(ang), 0.0)

    inv = 10000.0 ** (-jnp.arange(0, 128, 2, dtype=F32) / 128)
    ang = pos[:, None] * inv[l64][None, :]
    cr = jnp.cos(ang)
    sr = jnp.where(lane < 64, -jnp.sin(ang), jnp.sin(ang))
    return (cd, ad, bd, cg, ag, bg, cr, sr)


def _fast_flag(q_gain, k_gain):
    bound = (HEAD64 ** 0.5 * LOG2E * 1.02) * jnp.max(jnp.abs(q_gain)) * jnp.max(jnp.abs(k_gain))
    return (bound <= FAST_SOFTMAX_BOUND).astype(jnp.int32).reshape(1)


def kernel(x, attn_norm, w_in, diff_q_norm, diff_k_norm, diff_lam_q1, diff_lam_k1, diff_lam_q2, diff_lam_k2, diff_subln, ret_decay_fwd, ret_decay_bwd, ret_group_norm, gqa_q_norm, gqa_k_norm, w_branch, w_out, mlp_norm, w_mlp_in, w_mlp_out):
    B, S, D = x.shape
    T = B * S
    tabs = _rope_tables(S)
    blk = jnp.arange(LANES) // HEAD64
    gmat = jnp.where(blk[:, None] == blk[None, :], 1.0 / HEAD64, 0.0).astype(BF16)
    tile2 = lambda v: jnp.concatenate([v, v]).astype(F32)

    h = x.reshape(T, D)
    for l in range(DEPTH):
        lambda_init = 0.8 - 0.6 * math.exp(-0.3 * l)
        w_bf = w_in[l].astype(BF16)
        nrm = jnp.stack([tile2(diff_q_norm[l]), tile2(diff_k_norm[l]),
                         tile2(gqa_q_norm[l]), tile2(gqa_k_norm[l])])
        (u, dqT, dk, dvT, rq, rk, rv, rg, gqT, gk, gvT) = _proj_call(
            h, attn_norm[l].reshape(1, D).astype(F32), w_bf[:, :ATTN_COLS], gmat, nrm, tabs)

        lamv = jnp.stack([diff_lam_q1[l], diff_lam_k1[l], diff_lam_q2[l], diff_lam_k2[l]]).astype(F32)
        a = _diff_call(_fast_flag(diff_q_norm[l], diff_k_norm[l]), lamv,
                       diff_subln[l].reshape(1, LANES).astype(F32), dqT, dk, dvT, B, S, lambda_init)
        dec = jnp.stack([ret_decay_fwd[l], ret_decay_bwd[l]]).astype(F32)
        r = _ret_call(dec, ret_group_norm[l].reshape(1, 512).astype(F32), rq, rk, rv, rg, B, S)
        c = _gqa_call(_fast_flag(gqa_q_norm[l], gqa_k_norm[l]), gqT, gk, gvT, B, S)

        h, u2 = _merge_call(h, u, a, r, c, w_bf[:, ATTN_COLS:], w_branch[l].astype(BF16),
                            w_out[l].astype(BF16), mlp_norm[l].reshape(1, D).astype(F32))
        h = _mlp_call(h, u2, w_mlp_in[l].astype(BF16), w_mlp_out[l].astype(BF16))
    return h.reshape(B, S, D)
```

```python
import functools
import math

import jax
import jax.numpy as jnp
from jax import lax
from jax.experimental import pallas as pl
from jax.experimental.pallas import tpu as pltpu

F32 = jnp.float32
BF16 = jnp.bfloat16

D_MODEL = 1024
DEPTH = 2
GRID_W = 64
NORM_EPS = 1e-6
HEAD64 = 64
DIFF_HEADS = 4
RET_HEADS = 4
RET_CHUNK = 128
GQA_Q_HEADS = 8
GQA_GROUP = 4
N_BRANCHES = 3
BRANCH_WIDTH = 512
D_FF = 4 * D_MODEL
ATTN_COLS = 4352
LANES = 128

TM = 512
TK = 1024
NQ = 1024
NC = 256
KV_UNROLL = 4
ONES_ROWS = 16
VROWS = LANES + ONES_ROWS
GVROWS = HEAD64 + ONES_ROWS
FAST_SOFTMAX_BOUND = 64.0
VMEM_LIMIT = 56 * 1024 * 1024
LOG2E = 1.4426950408889634


def _cparams(n_axes):
    return pltpu.CompilerParams(dimension_semantics=("arbitrary",) * n_axes,
                                vmem_limit_bytes=VMEM_LIMIT)


def _resident(shape):
    nd = len(shape)
    return pl.BlockSpec(shape, lambda *_: (0,) * nd, pipeline_mode=pl.Buffered(1))


def _proj_kernel(h_ref, an_ref, w_ref, g_ref, nrm_ref,
                 cd_ref, ad_ref, bd_ref, cg_ref, ag_ref, bg_ref, cr_ref, sr_ref,
                 u_ref, dqT_ref, dk_ref, dvT_ref, rq_ref, rkT_ref, rv_ref, rg_ref,
                 gqT_ref, gk_ref, gvT_ref):
    x = h_ref[...]
    ms = jnp.mean(x * x, axis=-1, keepdims=True)
    u = (x * lax.rsqrt(ms + NORM_EPS) * an_ref[...]).astype(BF16)
    u_ref[...] = u

    def proj(c0, n):
        return jnp.dot(u, w_ref[:, c0:c0 + n], preferred_element_type=F32)

    gmat = g_ref[...]
    lo = lax.broadcasted_iota(jnp.int32, (1, LANES), 1) < HEAD64
    first_row = lax.broadcasted_iota(jnp.int32, (ONES_ROWS, TM), 0) == 0
    ones_rows = jnp.where(first_row, 1.0, 0.0).astype(BF16)

    def qk_norm(y, gain):
        msq = jnp.dot((y * y).astype(BF16), gmat, preferred_element_type=F32)
        return y * lax.rsqrt(msq + NORM_EPS) * gain

    def rope(y, c, a, b, d):
        return y * c + pltpu.roll(y, LANES - d, 1) * a + pltpu.roll(y, d, 1) * b

    qscale = HEAD64 ** -0.5 * LOG2E
    cd, ad, bd = cd_ref[...], ad_ref[...], bd_ref[...]
    cg, ag, bg = cg_ref[...], ag_ref[...], bg_ref[...]
    cr, sr = cr_ref[...], sr_ref[...]

    p = proj(0, 512)
    for hd in range(DIFF_HEADS):
        y = rope(qk_norm(p[:, hd * LANES:(hd + 1) * LANES], nrm_ref[0:1, :]), cd, ad, bd, 8) * qscale
        dqT_ref[hd, :, 0:TM] = jnp.where(lo, y, 0.0).T.astype(BF16)
        dqT_ref[hd, :, TM:2 * TM] = jnp.where(lo, 0.0, y).T.astype(BF16)
    p = proj(512, 512)
    for hd in range(DIFF_HEADS):
        y = rope(qk_norm(p[:, hd * LANES:(hd + 1) * LANES], nrm_ref[1:2, :]), cd, ad, bd, 8)
        dk_ref[:, hd * LANES:(hd + 1) * LANES] = y.astype(BF16)
    p = proj(1024, 512)
    for hd in range(DIFF_HEADS):
        dvT_ref[hd, 0:LANES, :] = p[:, hd * LANES:(hd + 1) * LANES].T.astype(BF16)
        dvT_ref[hd, LANES:VROWS, :] = ones_rows

    p = proj(1536, 512)
    for hd in range(RET_HEADS):
        y = p[:, hd * LANES:(hd + 1) * LANES]
        rq_ref[:, hd * LANES:(hd + 1) * LANES] = (y * cr + pltpu.roll(y, 64, 1) * sr).astype(BF16)
    p = proj(2048, 512)
    kscale = LANES ** -0.5
    for hd in range(RET_HEADS):
        y = p[:, hd * LANES:(hd + 1) * LANES]
        rkT_ref[hd] = ((y * cr + pltpu.roll(y, 64, 1) * sr) * kscale).T.astype(BF16)
    rv_ref[...] = proj(2560, 512).astype(BF16)
    rg_ref[...] = proj(3072, 512).astype(BF16)

    p = proj(3584, 512)
    for c in range(4):
        y = rope(qk_norm(p[:, c * LANES:(c + 1) * LANES], nrm_ref[2:3, :]), cg, ag, bg, 16) * qscale
        ysw = pltpu.roll(y, 64, 1)
        if c < 2:
            z_even, z_odd = jnp.where(lo, y, 0.0), jnp.where(lo, ysw, 0.0)
        else:
            z_even, z_odd = jnp.where(lo, 0.0, ysw), jnp.where(lo, 0.0, y)
        for hh, z in ((2 * c, z_even), (2 * c + 1, z_odd)):
            zt = z.T.astype(BF16)
            for qb in range(TM // LANES):
                gqT_ref[qb, :, hh * LANES:(hh + 1) * LANES] = zt[:, qb * LANES:(qb + 1) * LANES]
    p = proj(4096, 256)
    gk_ref[...] = rope(qk_norm(p[:, 0:LANES], nrm_ref[3:4, :]), cg, ag, bg, 16).astype(BF16)
    vt = p[:, LANES:2 * LANES].T.astype(BF16)
    for g in range(2):
        gvT_ref[g, 0:HEAD64, :] = vt[g * HEAD64:(g + 1) * HEAD64]
        gvT_ref[g, HEAD64:GVROWS, :] = ones_rows


def _proj_call(h, attn_norm, w_attn, gmat, nrm, tabs):
    T = h.shape[0]
    S = tabs[0].shape[0]
    nt = T // TM
    tab_spec = pl.BlockSpec((TM, LANES), lambda i: (i % (S // TM), 0))
    row = lambda w: pl.BlockSpec((TM, w), lambda i: (i, 0))
    out_shape = (
        jax.ShapeDtypeStruct((T, D_MODEL), BF16),
        jax.ShapeDtypeStruct((nt, 4, LANES, 2 * TM), BF16),
        jax.ShapeDtypeStruct((T, 512), BF16),
        jax.ShapeDtypeStruct((nt, 4, VROWS, TM), BF16),
        jax.ShapeDtypeStruct((T, 512), BF16),
        jax.ShapeDtypeStruct((nt, 4, LANES, TM), BF16),
        jax.ShapeDtypeStruct((T, 512), BF16),
        jax.ShapeDtypeStruct((T, 512), BF16),
        jax.ShapeDtypeStruct((T // LANES, LANES, NQ), BF16),
        jax.ShapeDtypeStruct((T, LANES), BF16),
        jax.ShapeDtypeStruct((nt, 2, GVROWS, TM), BF16),
    )
    out_specs = (
        row(D_MODEL),
        pl.BlockSpec((None, 4, LANES, 2 * TM), lambda i: (i, 0, 0, 0)),
        row(512),
        pl.BlockSpec((None, 4, VROWS, TM), lambda i: (i, 0, 0, 0)),
        row(512), pl.BlockSpec((None, 4, LANES, TM), lambda i: (i, 0, 0, 0)), row(512), row(512),
        pl.BlockSpec((TM // LANES, LANES, NQ), lambda i: (i, 0, 0)),
        row(LANES),
        pl.BlockSpec((None, 2, GVROWS, TM), lambda i: (i, 0, 0, 0)),
    )
    in_specs = [row(D_MODEL), _resident((1, D_MODEL)), _resident(w_attn.shape),
                _resident(gmat.shape), _resident(nrm.shape)] + [tab_spec] * 8
    return pl.pallas_call(
        _proj_kernel, grid=(nt,), in_specs=in_specs, out_specs=out_specs,
        out_shape=out_shape, compiler_params=_cparams(1), name="proj",
    )(h, attn_norm, w_attn, gmat, nrm, *tabs)


def _attn_core(fast_ref, qT_ref, k_ref, vT_ref, m_ref, acc_ref):
    acc_ref[...] = jnp.zeros(acc_ref.shape, F32)
    n_steps = k_ref.shape[0] // TK
    n_groups = vT_ref.shape[1]
    gw = NQ // n_groups

    def kv(j):
        k = k_ref[pl.ds(pl.multiple_of(j * TK, TK), TK), :]
        per = TK // TM
        vT = jnp.concatenate([vT_ref[j * per + i] for i in range(per)], axis=-1)
        return k, vT

    @pl.when(fast_ref[0] == 1)
    def _():
        def body(jj, carry):
            pv = [None] * n_groups
            for u in range(KV_UNROLL):
                k, vT = kv(jj * KV_UNROLL + u)
                s = jnp.dot(k, qT_ref[...], preferred_element_type=F32)
                p = jnp.exp2(s).astype(BF16)
                for g in range(n_groups):
                    t = jnp.dot(vT[g], p[:, g * gw:(g + 1) * gw], preferred_element_type=F32)
                    pv[g] = t if pv[g] is None else pv[g] + t
            for g in range(n_groups):
                acc_ref[:, g * gw:(g + 1) * gw] += pv[g]
            return carry

        lax.fori_loop(0, n_steps // KV_UNROLL, body, 0)

    @pl.when(fast_ref[0] == 0)
    def _():
        m_ref[...] = jnp.full(m_ref.shape, -1e30, F32)

        def body(j, carry):
            k, vT = kv(j)
            for c in range(NQ // NC):
                cs = slice(c * NC, (c + 1) * NC)
                s = jnp.dot(k, qT_ref[:, cs], preferred_element_type=F32)
                m_old = m_ref[:, cs]
                m_new = jnp.maximum(m_old, jnp.max(s, axis=0, keepdims=True))
                alpha = jnp.exp2(m_old - m_new)
                p = jnp.exp2(s - m_new).astype(BF16)
                pv = jnp.dot(vT[(c * NC) // gw], p, preferred_element_type=F32)
                acc_ref[:, cs] = acc_ref[:, cs] * alpha + pv
                m_ref[:, cs] = m_new
            return carry

        lax.fori_loop(0, n_steps, body, 0)


def _diff_kernel(fast_ref, lam_ref, sub_ref, qT_ref, k_ref, vT_ref, o_ref, m_ref, acc_ref, *,
                 lambda_init):
    _attn_core(fast_ref, qT_ref, k_ref, vT_ref, m_ref, acc_ref)

    lv = lam_ref[...]
    lam = (jnp.exp(jnp.sum(lv[0:1] * lv[1:2], axis=-1, keepdims=True))
           - jnp.exp(jnp.sum(lv[2:3] * lv[3:4], axis=-1, keepdims=True)) + lambda_init)
    tq = NQ // 2
    o1 = acc_ref[0:LANES, 0:tq] / acc_ref[LANES:LANES + 1, 0:tq]
    o2 = acc_ref[0:LANES, tq:NQ] / acc_ref[LANES:LANES + 1, tq:NQ]
    d = o1 - lam * o2
    ms = jnp.mean(d * d, axis=0, keepdims=True)
    dn = d * lax.rsqrt(ms + 1e-5)
    o_ref[...] = (dn.T * (sub_ref[...] * (1.0 - lambda_init))).astype(o_ref.dtype)


_SMEM_SPEC = pl.BlockSpec(memory_space=pltpu.SMEM)

def _attn_scratch(rows):
    return [pltpu.VMEM((1, NQ), F32), pltpu.VMEM((rows, NQ), F32)]


def _diff_call(fast, lamv, subln, dqT, dk, dvT, B, S, lambda_init):
    T = dk.shape[0]
    tq = NQ // 2
    nq = S // tq
    kern = functools.partial(_diff_kernel, lambda_init=lambda_init)
    return pl.pallas_call(
        kern, grid=(B, DIFF_HEADS, nq),
        in_specs=[
            _SMEM_SPEC, _resident(lamv.shape), _resident(subln.shape),
            pl.BlockSpec((None, None, LANES, NQ), lambda b, h, i: (b * nq + i, h, 0, 0)),
            pl.BlockSpec((S, LANES), lambda b, h, i: (b, h)),
            pl.BlockSpec((S // TM, 1, VROWS, TM), lambda b, h, i: (b, h, 0, 0)),
        ],
        out_specs=pl.BlockSpec((tq, LANES), lambda b, h, i: (b * nq + i, h)),
        out_shape=jax.ShapeDtypeStruct((T, 512), BF16),
        scratch_shapes=_attn_scratch(VROWS),
        compiler_params=_cparams(3), name="diff_attn",
    )(fast, lamv, subln, dqT, dk, dvT)


def _gqa_kernel(fast_ref, qT_ref, k_ref, vT_ref, o_ref, m_ref, acc_ref):
    _attn_core(fast_ref, qT_ref, k_ref, vT_ref, m_ref, acc_ref)
    for c in range(GQA_Q_HEADS // 2):
        parts = []
        for hh in (2 * c, 2 * c + 1):
            cols = slice(hh * LANES, (hh + 1) * LANES)
            parts.append(acc_ref[0:HEAD64, cols] / acc_ref[HEAD64:HEAD64 + 1, cols])
        o_ref[:, c * LANES:(c + 1) * LANES] = jnp.concatenate(parts, axis=0).T.astype(o_ref.dtype)


def _gqa_call(fast, gqT, gk, gvT, B, S):
    T = gk.shape[0]
    tq = NQ // GQA_Q_HEADS
    nq = S // tq
    return pl.pallas_call(
        _gqa_kernel, grid=(B, nq),
        in_specs=[
            _SMEM_SPEC,
            pl.BlockSpec((None, LANES, NQ), lambda b, i: (b * nq + i, 0, 0)),
            pl.BlockSpec((S, LANES), lambda b, i: (b, 0)),
            pl.BlockSpec((S // TM, 2, GVROWS, TM), lambda b, i: (b, 0, 0, 0)),
        ],
        out_specs=pl.BlockSpec((tq, 512), lambda b, i: (b * nq + i, 0)),
        out_shape=jax.ShapeDtypeStruct((T, 512), BF16),
        scratch_shapes=_attn_scratch(GVROWS),
        compiler_params=_cparams(2), name="gqa_attn",
    )(fast, gqT, gk, gvT)


def _ret_kernel(dec_ref, gain_ref, q_ref, kT_ref, v_ref, g_ref, o_ref, sf_ref, sb_ref):
    C = RET_CHUNK
    hd = pl.program_id(1)
    lf = -jnp.exp(jnp.full((C, C), dec_ref[0, hd], F32))
    lb = -jnp.exp(jnp.full((C, C), dec_ref[1, hd], F32))
    ii = lax.broadcasted_iota(jnp.int32, (C, C), 0).astype(F32)
    jj = lax.broadcasted_iota(jnp.int32, (C, C), 1).astype(F32)
    diff = ii - jj
    decay = jnp.where(diff >= 0, jnp.exp(jnp.maximum(diff, 0.0) * lf),
                      jnp.exp(jnp.maximum(-diff, 0.0) * lb))
    xi_f, zeta_f = jnp.exp((ii + 1.0) * lf), jnp.exp((C - 1.0 - ii) * lf)
    xi_b, zeta_b = jnp.exp((C - ii) * lb), jnp.exp(ii * lb)
    dec_f, dec_b = jnp.exp(C * lf), jnp.exp(C * lb)
    n_tiles = kT_ref.shape[0]
    per_tile = TM // C

    def rows(t, c):
        return pl.ds(pl.multiple_of(t * TM + c * C, C), C)

    def scan(t, state, zeta, dec, s_ref, order):
        kvs = {}
        for c in order:
            vz = (v_ref[rows(t, c), :].astype(F32) * zeta).astype(BF16)
            kvs[c] = jnp.dot(kT_ref[t, :, c * C:(c + 1) * C], vz, preferred_element_type=F32)
        for c in order:
            s_ref[t * per_tile + c] = state.astype(BF16)
            state = state * dec + kvs[c]
        return state

    zero = jnp.zeros((C, C), F32)
    lax.fori_loop(0, n_tiles,
                  lambda t, st: scan(t, st, zeta_f, dec_f, sf_ref, range(per_tile)), zero)
    lax.fori_loop(0, n_tiles,
                  lambda i, st: scan(n_tiles - 1 - i, st, zeta_b, dec_b, sb_ref,
                                     range(per_tile - 1, -1, -1)), zero)

    def outputs(t, carry):
        for c in range(per_tile):
            r = rows(t, c)
            q, v = q_ref[r, :], v_ref[r, :]
            n = t * per_tile + c
            a = jnp.dot(q, kT_ref[t, :, c * C:(c + 1) * C], preferred_element_type=F32) * decay
            o = (jnp.dot(a.astype(BF16), v, preferred_element_type=F32)
                 + jnp.dot(q, sf_ref[n], preferred_element_type=F32) * xi_f
                 + jnp.dot(q, sb_ref[n], preferred_element_type=F32) * xi_b)
            mu = jnp.mean(o, axis=-1, keepdims=True)
            oc = o - mu
            var = jnp.mean(oc * oc, axis=-1, keepdims=True)
            y = oc * lax.rsqrt(var + 1e-5) * gain_ref[...]
            g = g_ref[r, :].astype(F32)
            o_ref[r, :] = (g / (1.0 + jnp.exp(-g)) * y).astype(o_ref.dtype)
        return carry

    lax.fori_loop(0, n_tiles, outputs, 0)


def _ret_call(dec, gain, rq, rkT, rv, rg, B, S):
    T = rq.shape[0]
    blk = pl.BlockSpec((S, LANES), lambda b, h: (b, h))
    n_chunks = S // RET_CHUNK
    return pl.pallas_call(
        _ret_kernel, grid=(B, RET_HEADS),
        in_specs=[pl.BlockSpec(memory_space=pltpu.SMEM),
                  pl.BlockSpec((1, LANES), lambda b, h: (0, h)), blk,
                  pl.BlockSpec((S // TM, None, LANES, TM), lambda b, h: (b, h, 0, 0)), blk, blk],
        out_specs=blk,
        out_shape=jax.ShapeDtypeStruct((T, 512), BF16),
        scratch_shapes=[pltpu.VMEM((n_chunks, RET_CHUNK, LANES), BF16),
                        pltpu.VMEM((n_chunks, RET_CHUNK, LANES), BF16)],
        compiler_params=_cparams(2), name="retention",
    )(dec, gain, rq, rkT, rv, rg)


def _merge_kernel(h_ref, u_ref, a_ref, r_ref, c_ref, wg_ref, wb_ref, wo_ref, mn_ref,
                  hn_ref, u2_ref):
    u = u_ref[...]
    merged = None
    for n, br in enumerate((a_ref, r_ref, c_ref)):
        logits = jnp.dot(u, wg_ref[:, n * D_MODEL:(n + 1) * D_MODEL], preferred_element_type=F32)
        gate = 1.0 / (1.0 + jnp.exp(-logits))
        term = gate * jnp.dot(br[...], wb_ref[n], preferred_element_type=F32)
        merged = term if merged is None else merged + term
    hn = h_ref[...] + jnp.dot(merged.astype(BF16), wo_ref[...], preferred_element_type=F32)
    hn_ref[...] = hn
    ms = jnp.mean(hn * hn, axis=-1, keepdims=True)
    u2_ref[...] = (hn * lax.rsqrt(ms + NORM_EPS) * mn_ref[...]).astype(BF16)


def _merge_call(h, u, a, r, c, wg, wb, wo, mlp_norm):
    T = h.shape[0]
    row = lambda w: pl.BlockSpec((TM, w), lambda i: (i, 0))
    return pl.pallas_call(
        _merge_kernel, grid=(T // TM,),
        in_specs=[row(D_MODEL), row(D_MODEL), row(512), row(512), row(512),
                  _resident(wg.shape), _resident(wb.shape), _resident(wo.shape),
                  _resident((1, D_MODEL))],
        out_specs=(row(D_MODEL), row(D_MODEL)),
        out_shape=(jax.ShapeDtypeStruct((T, D_MODEL), F32),
                   jax.ShapeDtypeStruct((T, D_MODEL), BF16)),
        compiler_params=_cparams(1), name="merge",
    )(h, u, a, r, c, wg, wb, wo, mlp_norm)


def _mlp_kernel(h_ref, u_ref, w1_ref, w2_ref, o_ref):
    u = u_ref[...]
    acc = h_ref[...]
    for c in range(D_FF // D_MODEL):
        cs = slice(c * D_MODEL, (c + 1) * D_MODEL)
        m = jnp.maximum(jnp.dot(u, w1_ref[:, cs], preferred_element_type=F32), 0.0)
        acc = acc + jnp.dot((m * m).astype(BF16), w2_ref[cs, :], preferred_element_type=F32)
    o_ref[...] = acc


def _mlp_call(h, u2, w1, w2):
    T = h.shape[0]
    row = pl.BlockSpec((TM, D_MODEL), lambda i: (i, 0))
    return pl.pallas_call(
        _mlp_kernel, grid=(T // TM,),
        in_specs=[row, row, _resident(w1.shape), _resident(w2.shape)],
        out_specs=row,
        out_shape=jax.ShapeDtypeStruct((T, D_MODEL), F32),
        compiler_params=_cparams(1), name="mlp",
    )(h, u2, w1, w2)


def _rope_tables(S):
    pos = jnp.arange(S, dtype=F32)
    lane = jnp.arange(LANES)
    l64 = lane % 64

    inv = 500000.0 ** (-jnp.arange(0, 16, 2, dtype=F32) / 16)
    ang = pos[:, None] * inv[None, :]
    cos, sin = jnp.cos(ang)[:, l64 % 8], jnp.sin(ang)[:, l64 % 8]
    cd = jnp.where(l64 < 16, cos, 1.0)
    ad = jnp.where(l64 < 8, -sin, 0.0)
    bd = jnp.where((l64 >= 8) & (l64 < 16), sin, 0.0)

    inv = 10000.0 ** (-jnp.arange(0, 32, 2, dtype=F32) / 32)
    row_pos = jnp.floor(pos / GRID_W)
    col_pos = pos - row_pos * GRID_W
    p2 = jnp.where((l64 < 32)[None, :], row_pos[:, None], col_pos[:, None])
    ang = p2 * inv[l64 % 16][None, :]
    cg = jnp.cos(ang)
    ag = jnp.where(l64 % 32 < 16, -jnp.sin(ang), 0.0)
    bg = jnp.where(l64 % 32 >= 16, jnp.sin(ang), 0.0)

    inv = 10000.0 ** (-jnp.arange(0, 128, 2, dtype=F32) / 128)
    ang = pos[:, None] * inv[l64][None, :]
    cr = jnp.cos(ang)
    sr = jnp.where(lane < 64, -jnp.sin(ang), jnp.sin(ang))
    return (cd, ad, bd, cg, ag, bg, cr, sr)


def _fast_flag(q_gain, k_gain):
    bound = (HEAD64 ** 0.5 * LOG2E * 1.02) * jnp.max(jnp.abs(q_gain)) * jnp.max(jnp.abs(k_gain))
    return (bound <= FAST_SOFTMAX_BOUND).astype(jnp.int32).reshape(1)


def kernel(x, attn_norm, w_in, diff_q_norm, diff_k_norm, diff_lam_q1, diff_lam_k1, diff_lam_q2, diff_lam_k2, diff_subln, ret_decay_fwd, ret_decay_bwd, ret_group_norm, gqa_q_norm, gqa_k_norm, w_branch, w_out, mlp_norm, w_mlp_in, w_mlp_out):
    B, S, D = x.shape
    T = B * S
    tabs = _rope_tables(S)
    blk = jnp.arange(LANES) // HEAD64
    gmat = jnp.where(blk[:, None] == blk[None, :], 1.0 / HEAD64, 0.0).astype(BF16)
    tile2 = lambda v: jnp.concatenate([v, v]).astype(F32)

    h = x.reshape(T, D)
    for l in range(DEPTH):
        lambda_init = 0.8 - 0.6 * math.exp(-0.3 * l)
        w_bf = w_in[l].astype(BF16)
        nrm = jnp.stack([tile2(diff_q_norm[l]), tile2(diff_k_norm[l]),
                         tile2(gqa_q_norm[l]), tile2(gqa_k_norm[l])])
        (u, dqT, dk, dvT, rq, rk, rv, rg, gqT, gk, gvT) = _proj_call(
            h, attn_norm[l].reshape(1, D).astype(F32), w_bf[:, :ATTN_COLS], gmat, nrm, tabs)

        lamv = jnp.stack([diff_lam_q1[l], diff_lam_k1[l], diff_lam_q2[l], diff_lam_k2[l]]).astype(F32)
        a = _diff_call(_fast_flag(diff_q_norm[l], diff_k_norm[l]), lamv,
                       diff_subln[l].reshape(1, LANES).astype(F32), dqT, dk, dvT, B, S, lambda_init)
        dec = jnp.stack([ret_decay_fwd[l], ret_decay_bwd[l]]).astype(F32)
        r = _ret_call(dec, ret_group_norm[l].reshape(1, 512).astype(F32), rq, rk, rv, rg, B, S)
        c = _gqa_call(_fast_flag(gqa_q_norm[l], gqa_k_norm[l]), gqT, gk, gvT, B, S)

        h, u2 = _merge_call(h, u, a, r, c, w_bf[:, ATTN_COLS:], w_branch[l].astype(BF16),
                            w_out[l].astype(BF16), mlp_norm[l].reshape(1, D).astype(F32))
        h = _mlp_call(h, u2, w_mlp_in[l].astype(BF16), w_mlp_out[l].astype(BF16))
    return h.reshape(B, S, D)
```

```python
import functools
import math

import jax
import jax.numpy as jnp
from jax import lax
from jax.experimental import pallas as pl
from jax.experimental.pallas import tpu as pltpu

F32 = jnp.float32
BF16 = jnp.bfloat16

D_MODEL = 1024
DEPTH = 2
GRID_W = 64
NORM_EPS = 1e-6
HEAD64 = 64
DIFF_HEADS = 4
RET_HEADS = 4
RET_CHUNK = 128
GQA_Q_HEADS = 8
GQA_GROUP = 4
N_BRANCHES = 3
BRANCH_WIDTH = 512
D_FF = 4 * D_MODEL
ATTN_COLS = 4352
LANES = 128

TM = 512
TK = 2048
NQ = 1024
NC = 256
KV_UNROLL = 2
ONES_ROWS = 16
VROWS = LANES + ONES_ROWS
GVROWS = HEAD64 + ONES_ROWS
FAST_SOFTMAX_BOUND = 64.0
VMEM_LIMIT = 56 * 1024 * 1024
LOG2E = 1.4426950408889634


def _cparams(n_axes):
    return pltpu.CompilerParams(dimension_semantics=("arbitrary",) * n_axes,
                                vmem_limit_bytes=VMEM_LIMIT)


def _resident(shape):
    nd = len(shape)
    return pl.BlockSpec(shape, lambda *_: (0,) * nd, pipeline_mode=pl.Buffered(1))


def _proj_kernel(h_ref, an_ref, w_ref, g_ref, nrm_ref,
                 cd_ref, ad_ref, bd_ref, cg_ref, ag_ref, bg_ref, cr_ref, sr_ref,
                 u_ref, dqT_ref, dk_ref, dvT_ref, rq_ref, rkT_ref, rv_ref, rg_ref,
                 gqT_ref, gk_ref, gvT_ref):
    x = h_ref[...]
    ms = jnp.mean(x * x, axis=-1, keepdims=True)
    u = (x * lax.rsqrt(ms + NORM_EPS) * an_ref[...]).astype(BF16)
    u_ref[...] = u

    def proj(c0, n):
        return jnp.dot(u, w_ref[:, c0:c0 + n], preferred_element_type=F32)

    gmat = g_ref[...]
    lo = lax.broadcasted_iota(jnp.int32, (1, LANES), 1) < HEAD64
    first_row = lax.broadcasted_iota(jnp.int32, (ONES_ROWS, TM), 0) == 0
    ones_rows = jnp.where(first_row, 1.0, 0.0).astype(BF16)

    def qk_norm(y, gain):
        msq = jnp.dot((y * y).astype(BF16), gmat, preferred_element_type=F32)
        return y * lax.rsqrt(msq + NORM_EPS) * gain

    def rope(y, c, a, b, d):
        return y * c + pltpu.roll(y, LANES - d, 1) * a + pltpu.roll(y, d, 1) * b

    qscale = HEAD64 ** -0.5 * LOG2E
    cd, ad, bd = cd_ref[...], ad_ref[...], bd_ref[...]
    cg, ag, bg = cg_ref[...], ag_ref[...], bg_ref[...]
    cr, sr = cr_ref[...], sr_ref[...]

    p = proj(0, 512)
    for hd in range(DIFF_HEADS):
        y = rope(qk_norm(p[:, hd * LANES:(hd + 1) * LANES], nrm_ref[0:1, :]), cd, ad, bd, 8) * qscale
        dqT_ref[hd, :, 0:TM] = jnp.where(lo, y, 0.0).T.astype(BF16)
        dqT_ref[hd, :, TM:2 * TM] = jnp.where(lo, 0.0, y).T.astype(BF16)
    p = proj(512, 512)
    for hd in range(DIFF_HEADS):
        y = rope(qk_norm(p[:, hd * LANES:(hd + 1) * LANES], nrm_ref[1:2, :]), cd, ad, bd, 8)
        dk_ref[:, hd * LANES:(hd + 1) * LANES] = y.astype(BF16)
    p = proj(1024, 512)
    for hd in range(DIFF_HEADS):
        dvT_ref[hd, 0:LANES, :] = p[:, hd * LANES:(hd + 1) * LANES].T.astype(BF16)
        dvT_ref[hd, LANES:VROWS, :] = ones_rows

    p = proj(1536, 512)
    for hd in range(RET_HEADS):
        y = p[:, hd * LANES:(hd + 1) * LANES]
        rq_ref[:, hd * LANES:(hd + 1) * LANES] = (y * cr + pltpu.roll(y, 64, 1) * sr).astype(BF16)
    p = proj(2048, 512)
    kscale = LANES ** -0.5
    for hd in range(RET_HEADS):
        y = p[:, hd * LANES:(hd + 1) * LANES]
        rkT_ref[hd] = ((y * cr + pltpu.roll(y, 64, 1) * sr) * kscale).T.astype(BF16)
    rv_ref[...] = proj(2560, 512).astype(BF16)
    rg_ref[...] = proj(3072, 512).astype(BF16)

    p = proj(3584, 512)
    for c in range(4):
        y = rope(qk_norm(p[:, c * LANES:(c + 1) * LANES], nrm_ref[2:3, :]), cg, ag, bg, 16) * qscale
        ysw = pltpu.roll(y, 64, 1)
        if c < 2:
            z_even, z_odd = jnp.where(lo, y, 0.0), jnp.where(lo, ysw, 0.0)
        else:
            z_even, z_odd = jnp.where(lo, 0.0, ysw), jnp.where(lo, 0.0, y)
        for hh, z in ((2 * c, z_even), (2 * c + 1, z_odd)):
            zt = z.T.astype(BF16)
            for qb in range(TM // LANES):
                gqT_ref[qb, :, hh * LANES:(hh + 1) * LANES] = zt[:, qb * LANES:(qb + 1) * LANES]
    p = proj(4096, 256)
    gk_ref[...] = rope(qk_norm(p[:, 0:LANES], nrm_ref[3:4, :]), cg, ag, bg, 16).astype(BF16)
    vt = p[:, LANES:2 * LANES].T.astype(BF16)
    for g in range(2):
        gvT_ref[g, 0:HEAD64, :] = vt[g * HEAD64:(g + 1) * HEAD64]
        gvT_ref[g, HEAD64:GVROWS, :] = ones_rows


def _proj_call(h, attn_norm, w_attn, gmat, nrm, tabs):
    T = h.shape[0]
    S = tabs[0].shape[0]
    nt = T // TM
    tab_spec = pl.BlockSpec((TM, LANES), lambda i: (i % (S // TM), 0))
    row = lambda w: pl.BlockSpec((TM, w), lambda i: (i, 0))
    out_shape = (
        jax.ShapeDtypeStruct((T, D_MODEL), BF16),
        jax.ShapeDtypeStruct((nt, 4, LANES, 2 * TM), BF16),
        jax.ShapeDtypeStruct((T, 512), BF16),
        jax.ShapeDtypeStruct((nt, 4, VROWS, TM), BF16),
        jax.ShapeDtypeStruct((T, 512), BF16),
        jax.ShapeDtypeStruct((nt, 4, LANES, TM), BF16),
        jax.ShapeDtypeStruct((T, 512), BF16),
        jax.ShapeDtypeStruct((T, 512), BF16),
        jax.ShapeDtypeStruct((T // LANES, LANES, NQ), BF16),
        jax.ShapeDtypeStruct((T, LANES), BF16),
        jax.ShapeDtypeStruct((nt, 2, GVROWS, TM), BF16),
    )
    out_specs = (
        row(D_MODEL),
        pl.BlockSpec((None, 4, LANES, 2 * TM), lambda i: (i, 0, 0, 0)),
        row(512),
        pl.BlockSpec((None, 4, VROWS, TM), lambda i: (i, 0, 0, 0)),
        row(512), pl.BlockSpec((None, 4, LANES, TM), lambda i: (i, 0, 0, 0)), row(512), row(512),
        pl.BlockSpec((TM // LANES, LANES, NQ), lambda i: (i, 0, 0)),
        row(LANES),
        pl.BlockSpec((None, 2, GVROWS, TM), lambda i: (i, 0, 0, 0)),
    )
    in_specs = [row(D_MODEL), _resident((1, D_MODEL)), _resident(w_attn.shape),
                _resident(gmat.shape), _resident(nrm.shape)] + [tab_spec] * 8
    return pl.pallas_call(
        _proj_kernel, grid=(nt,), in_specs=in_specs, out_specs=out_specs,
        out_shape=out_shape, compiler_params=_cparams(1), name="proj",
    )(h, attn_norm, w_attn, gmat, nrm, *tabs)


def _attn_core(fast_ref, qT_ref, k_ref, vT_ref, m_ref, acc_ref):
    acc_ref[...] = jnp.zeros(acc_ref.shape, F32)
    n_steps = k_ref.shape[0] // TK
    n_groups = vT_ref.shape[1]
    gw = NQ // n_groups

    def kv(j):
        k = k_ref[pl.ds(pl.multiple_of(j * TK, TK), TK), :]
        per = TK // TM
        vT = jnp.concatenate([vT_ref[j * per + i] for i in range(per)], axis=-1)
        return k, vT

    @pl.when(fast_ref[0] == 1)
    def _():
        def body(jj, carry):
            pv = [None] * n_groups
            for u in range(KV_UNROLL):
                k, vT = kv(jj * KV_UNROLL + u)
                s = jnp.dot(k, qT_ref[...], preferred_element_type=F32)
                p = jnp.exp2(s).astype(BF16)
                for g in range(n_groups):
                    t = jnp.dot(vT[g], p[:, g * gw:(g + 1) * gw], preferred_element_type=F32)
                    pv[g] = t if pv[g] is None else pv[g] + t
            for g in range(n_groups):
                acc_ref[:, g * gw:(g + 1) * gw] += pv[g]
            return carry

        lax.fori_loop(0, n_steps // KV_UNROLL, body, 0)

    @pl.when(fast_ref[0] == 0)
    def _():
        m_ref[...] = jnp.full(m_ref.shape, -1e30, F32)

        def body(j, carry):
            k, vT = kv(j)
            for c in range(NQ // NC):
                cs = slice(c * NC, (c + 1) * NC)
                s = jnp.dot(k, qT_ref[:, cs], preferred_element_type=F32)
                m_old = m_ref[:, cs]
                m_new = jnp.maximum(m_old, jnp.max(s, axis=0, keepdims=True))
                alpha = jnp.exp2(m_old - m_new)
                p = jnp.exp2(s - m_new).astype(BF16)
                pv = jnp.dot(vT[(c * NC) // gw], p, preferred_element_type=F32)
                acc_ref[:, cs] = acc_ref[:, cs] * alpha + pv
                m_ref[:, cs] = m_new
            return carry

        lax.fori_loop(0, n_steps, body, 0)


def _diff_kernel(fast_ref, lam_ref, sub_ref, qT_ref, k_ref, vT_ref, o_ref, m_ref, acc_ref, *,
                 lambda_init):
    _attn_core(fast_ref, qT_ref, k_ref, vT_ref, m_ref, acc_ref)

    lv = lam_ref[...]
    lam = (jnp.exp(jnp.sum(lv[0:1] * lv[1:2], axis=-1, keepdims=True))
           - jnp.exp(jnp.sum(lv[2:3] * lv[3:4], axis=-1, keepdims=True)) + lambda_init)
    tq = NQ // 2
    o1 = acc_ref[0:LANES, 0:tq] / acc_ref[LANES:LANES + 1, 0:tq]
    o2 = acc_ref[0:LANES, tq:NQ] / acc_ref[LANES:LANES + 1, tq:NQ]
    d = o1 - lam * o2
    ms = jnp.mean(d * d, axis=0, keepdims=True)
    dn = d * lax.rsqrt(ms + 1e-5)
    o_ref[...] = (dn.T * (sub_ref[...] * (1.0 - lambda_init))).astype(o_ref.dtype)


_SMEM_SPEC = pl.BlockSpec(memory_space=pltpu.SMEM)

def _attn_scratch(rows):
    return [pltpu.VMEM((1, NQ), F32), pltpu.VMEM((rows, NQ), F32)]


def _diff_call(fast, lamv, subln, dqT, dk, dvT, B, S, lambda_init):
    T = dk.shape[0]
    tq = NQ // 2
    nq = S // tq
    kern = functools.partial(_diff_kernel, lambda_init=lambda_init)
    return pl.pallas_call(
        kern, grid=(B, DIFF_HEADS, nq),
        in_specs=[
            _SMEM_SPEC, _resident(lamv.shape), _resident(subln.shape),
            pl.BlockSpec((None, None, LANES, NQ), lambda b, h, i: (b * nq + i, h, 0, 0)),
            pl.BlockSpec((S, LANES), lambda b, h, i: (b, h)),
            pl.BlockSpec((S // TM, 1, VROWS, TM), lambda b, h, i: (b, h, 0, 0)),
        ],
        out_specs=pl.BlockSpec((tq, LANES), lambda b, h, i: (b * nq + i, h)),
        out_shape=jax.ShapeDtypeStruct((T, 512), BF16),
        scratch_shapes=_attn_scratch(VROWS),
        compiler_params=_cparams(3), name="diff_attn",
    )(fast, lamv, subln, dqT, dk, dvT)


def _gqa_kernel(fast_ref, qT_ref, k_ref, vT_ref, o_ref, m_ref, acc_ref):
    _attn_core(fast_ref, qT_ref, k_ref, vT_ref, m_ref, acc_ref)
    for c in range(GQA_Q_HEADS // 2):
        parts = []
        for hh in (2 * c, 2 * c + 1):
            cols = slice(hh * LANES, (hh + 1) * LANES)
            parts.append(acc_ref[0:HEAD64, cols] / acc_ref[HEAD64:HEAD64 + 1, cols])
        o_ref[:, c * LANES:(c + 1) * LANES] = jnp.concatenate(parts, axis=0).T.astype(o_ref.dtype)


def _gqa_call(fast, gqT, gk, gvT, B, S):
    T = gk.shape[0]
    tq = NQ // GQA_Q_HEADS
    nq = S // tq
    return pl.pallas_call(
        _gqa_kernel, grid=(B, nq),
        in_specs=[
            _SMEM_SPEC,
            pl.BlockSpec((None, LANES, NQ), lambda b, i: (b * nq + i, 0, 0)),
            pl.BlockSpec((S, LANES), lambda b, i: (b, 0)),
            pl.BlockSpec((S // TM, 2, GVROWS, TM), lambda b, i: (b, 0, 0, 0)),
        ],
        out_specs=pl.BlockSpec((tq, 512), lambda b, i: (b * nq + i, 0)),
        out_shape=jax.ShapeDtypeStruct((T, 512), BF16),
        scratch_shapes=_attn_scratch(GVROWS),
        compiler_params=_cparams(2), name="gqa_attn",
    )(fast, gqT, gk, gvT)


def _ret_kernel(dec_ref, gain_ref, q_ref, kT_ref, v_ref, g_ref, o_ref, sf_ref, sb_ref):
    C = RET_CHUNK
    hd = pl.program_id(1)
    lf = -jnp.exp(jnp.full((C, C), dec_ref[0, hd], F32))
    lb = -jnp.exp(jnp.full((C, C), dec_ref[1, hd], F32))
    ii = lax.broadcasted_iota(jnp.int32, (C, C), 0).astype(F32)
    jj = lax.broadcasted_iota(jnp.int32, (C, C), 1).astype(F32)
    diff = ii - jj
    decay = jnp.where(diff >= 0, jnp.exp(jnp.maximum(diff, 0.0) * lf),
                      jnp.exp(jnp.maximum(-diff, 0.0) * lb))
    xi_f, zeta_f = jnp.exp((ii + 1.0) * lf), jnp.exp((C - 1.0 - ii) * lf)
    xi_b, zeta_b = jnp.exp((C - ii) * lb), jnp.exp(ii * lb)
    dec_f, dec_b = jnp.exp(C * lf), jnp.exp(C * lb)
    n_tiles = kT_ref.shape[0]
    per_tile = TM // C

    def rows(t, c):
        return pl.ds(pl.multiple_of(t * TM + c * C, C), C)

    def scan(t, state, zeta, dec, s_ref, order):
        kvs = {}
        for c in order:
            vz = (v_ref[rows(t, c), :].astype(F32) * zeta).astype(BF16)
            kvs[c] = jnp.dot(kT_ref[t, :, c * C:(c + 1) * C], vz, preferred_element_type=F32)
        for c in order:
            s_ref[t * per_tile + c] = state.astype(BF16)
            state = state * dec + kvs[c]
        return state

    zero = jnp.zeros((C, C), F32)
    lax.fori_loop(0, n_tiles,
                  lambda t, st: scan(t, st, zeta_f, dec_f, sf_ref, range(per_tile)), zero)
    lax.fori_loop(0, n_tiles,
                  lambda i, st: scan(n_tiles - 1 - i, st, zeta_b, dec_b, sb_ref,
                                     range(per_tile - 1, -1, -1)), zero)

    def outputs(t, carry):
        for c in range(per_tile):
            r = rows(t, c)
            q, v = q_ref[r, :], v_ref[r, :]
            n = t * per_tile + c
            a = jnp.dot(q, kT_ref[t, :, c * C:(c + 1) * C], preferred_element_type=F32) * decay
            o = (jnp.dot(a.astype(BF16), v, preferred_element_type=F32)
                 + jnp.dot(q, sf_ref[n], preferred_element_type=F32) * xi_f
                 + jnp.dot(q, sb_ref[n], preferred_element_type=F32) * xi_b)
            mu = jnp.mean(o, axis=-1, keepdims=True)
            oc = o - mu
            var = jnp.mean(oc * oc, axis=-1, keepdims=True)
            y = oc * lax.rsqrt(var + 1e-5) * gain_ref[...]
            g = g_ref[r, :].astype(F32)
            o_ref[r, :] = (g / (1.0 + jnp.exp(-g)) * y).astype(o_ref.dtype)
        return carry

    lax.fori_loop(0, n_tiles, outputs, 0)


def _ret_call(dec, gain, rq, rkT, rv, rg, B, S):
    T = rq.shape[0]
    blk = pl.BlockSpec((S, LANES), lambda b, h: (b, h))
    n_chunks = S // RET_CHUNK
    return pl.pallas_call(
        _ret_kernel, grid=(B, RET_HEADS),
        in_specs=[pl.BlockSpec(memory_space=pltpu.SMEM),
                  pl.BlockSpec((1, LANES), lambda b, h: (0, h)), blk,
                  pl.BlockSpec((S // TM, None, LANES, TM), lambda b, h: (b, h, 0, 0)), blk, blk],
        out_specs=blk,
        out_shape=jax.ShapeDtypeStruct((T, 512), BF16),
        scratch_shapes=[pltpu.VMEM((n_chunks, RET_CHUNK, LANES), BF16),
                        pltpu.VMEM((n_chunks, RET_CHUNK, LANES), BF16)],
        compiler_params=_cparams(2), name="retention",
    )(dec, gain, rq, rkT, rv, rg)


def _merge_kernel(h_ref, u_ref, a_ref, r_ref, c_ref, wg_ref, wb_ref, wo_ref, mn_ref,
                  hn_ref, u2_ref):
    u = u_ref[...]
    merged = None
    for n, br in enumerate((a_ref, r_ref, c_ref)):
        logits = jnp.dot(u, wg_ref[:, n * D_MODEL:(n + 1) * D_MODEL], preferred_element_type=F32)
        gate = 1.0 / (1.0 + jnp.exp(-logits))
        term = gate * jnp.dot(br[...], wb_ref[n], preferred_element_type=F32)
        merged = term if merged is None else merged + term
    hn = h_ref[...] + jnp.dot(merged.astype(BF16), wo_ref[...], preferred_element_type=F32)
    hn_ref[...] = hn
    ms = jnp.mean(hn * hn, axis=-1, keepdims=True)
    u2_ref[...] = (hn * lax.rsqrt(ms + NORM_EPS) * mn_ref[...]).astype(BF16)


def _merge_call(h, u, a, r, c, wg, wb, wo, mlp_norm):
    T = h.shape[0]
    row = lambda w: pl.BlockSpec((TM, w), lambda i: (i, 0))
    return pl.pallas_call(
        _merge_kernel, grid=(T // TM,),
        in_specs=[row(D_MODEL), row(D_MODEL), row(512), row(512), row(512),
                  _resident(wg.shape), _resident(wb.shape), _resident(wo.shape),
                  _resident((1, D_MODEL))],
        out_specs=(row(D_MODEL), row(D_MODEL)),
        out_shape=(jax.ShapeDtypeStruct((T, D_MODEL), F32),
                   jax.ShapeDtypeStruct((T, D_MODEL), BF16)),
        compiler_params=_cparams(1), name="merge",
    )(h, u, a, r, c, wg, wb, wo, mlp_norm)


def _mlp_kernel(h_ref, u_ref, w1_ref, w2_ref, o_ref):
    u = u_ref[...]
    acc = h_ref[...]
    for c in range(D_FF // D_MODEL):
        cs = slice(c * D_MODEL, (c + 1) * D_MODEL)
        m = jnp.maximum(jnp.dot(u, w1_ref[:, cs], preferred_element_type=F32), 0.0)
        acc = acc + jnp.dot((m * m).astype(BF16), w2_ref[cs, :], preferred_element_type=F32)
    o_ref[...] = acc


def _mlp_call(h, u2, w1, w2):
    T = h.shape[0]
    row = pl.BlockSpec((TM, D_MODEL), lambda i: (i, 0))
    return pl.pallas_call(
        _mlp_kernel, grid=(T // TM,),
        in_specs=[row, row, _resident(w1.shape), _resident(w2.shape)],
        out_specs=row,
        out_shape=jax.ShapeDtypeStruct((T, D_MODEL), F32),
        compiler_params=_cparams(1), name="mlp",
    )(h, u2, w1, w2)


def _rope_tables(S):
    pos = jnp.arange(S, dtype=F32)
    lane = jnp.arange(LANES)
    l64 = lane % 64

    inv = 500000.0 ** (-jnp.arange(0, 16, 2, dtype=F32) / 16)
    ang = pos[:, None] * inv[None, :]
    cos, sin = jnp.cos(ang)[:, l64 % 8], jnp.sin(ang)[:, l64 % 8]
    cd = jnp.where(l64 < 16, cos, 1.0)
    ad = jnp.where(l64 < 8, -sin, 0.0)
    bd = jnp.where((l64 >= 8) & (l64 < 16), sin, 0.0)

    inv = 10000.0 ** (-jnp.arange(0, 32, 2, dtype=F32) / 32)
    row_pos = jnp.floor(pos / GRID_W)
    col_pos = pos - row_pos * GRID_W
    p2 = jnp.where((l64 < 32)[None, :], row_pos[:, None], col_pos[:, None])
    ang = p2 * inv[l64 % 16][None, :]
    cg = jnp.cos(ang)
    ag = jnp.where(l64 % 32 < 16, -jnp.sin(ang), 0.0)
    bg = jnp.where(l64 % 32 >= 16, jnp.sin(ang), 0.0)

    inv = 10000.0 ** (-jnp.arange(0, 128, 2, dtype=F32) / 128)
    ang = pos[:, None] * inv[l64][None, :]
    cr = jnp.cos(ang)
    sr = jnp.where(lane < 64, -jnp.sin(ang), jnp.sin(ang))
    return (cd, ad, bd, cg, ag, bg, cr, sr)


def _fast_flag(q_gain, k_gain):
    bound = (HEAD64 ** 0.5 * LOG2E * 1.02) * jnp.max(jnp.abs(q_gain)) * jnp.max(jnp.abs(k_gain))
    return (bound <= FAST_SOFTMAX_BOUND).astype(jnp.int32).reshape(1)


def kernel(x, attn_norm, w_in, diff_q_norm, diff_k_norm, diff_lam_q1, diff_lam_k1, diff_lam_q2, diff_lam_k2, diff_subln, ret_decay_fwd, ret_decay_bwd, ret_group_norm, gqa_q_norm, gqa_k_norm, w_branch, w_out, mlp_norm, w_mlp_in, w_mlp_out):
    B, S, D = x.shape
    T = B * S
    tabs = _rope_tables(S)
    blk = jnp.arange(LANES) // HEAD64
    gmat = jnp.where(blk[:, None] == blk[None, :], 1.0 / HEAD64, 0.0).astype(BF16)
    tile2 = lambda v: jnp.concatenate([v, v]).astype(F32)

    h = x.reshape(T, D)
    for l in range(DEPTH):
        lambda_init = 0.8 - 0.6 * math.exp(-0.3 * l)
        w_bf = w_in[l].astype(BF16)
        nrm = jnp.stack([tile2(diff_q_norm[l]), tile2(diff_k_norm[l]),
                         tile2(gqa_q_norm[l]), tile2(gqa_k_norm[l])])
        (u, dqT, dk, dvT, rq, rk, rv, rg, gqT, gk, gvT) = _proj_call(
            h, attn_norm[l].reshape(1, D).astype(F32), w_bf[:, :ATTN_COLS], gmat, nrm, tabs)

        lamv = jnp.stack([diff_lam_q1[l], diff_lam_k1[l], diff_lam_q2[l], diff_lam_k2[l]]).astype(F32)
        a = _diff_call(_fast_flag(diff_q_norm[l], diff_k_norm[l]), lamv,
                       diff_subln[l].reshape(1, LANES).astype(F32), dqT, dk, dvT, B, S, lambda_init)
        dec = jnp.stack([ret_decay_fwd[l], ret_decay_bwd[l]]).astype(F32)
        r = _ret_call(dec, ret_group_norm[l].reshape(1, 512).astype(F32), rq, rk, rv, rg, B, S)
        c = _gqa_call(_fast_flag(gqa_q_norm[l], gqa_k_norm[l]), gqT, gk, gvT, B, S)

        h, u2 = _merge_call(h, u, a, r, c, w_bf[:, ATTN_COLS:], w_branch[l].astype(BF16),
                            w_out[l].astype(BF16), mlp_norm[l].reshape(1, D).astype(F32))
        h = _mlp_call(h, u2, w_mlp_in[l].astype(BF16), w_mlp_out[l].astype(BF16))
    return h.reshape(B, S, D)
```

```python
import functools
import math

import jax
import jax.numpy as jnp
from jax import lax
from jax.experimental import pallas as pl
from jax.experimental.pallas import tpu as pltpu

F32 = jnp.float32
BF16 = jnp.bfloat16

D_MODEL = 1024
DEPTH = 2
GRID_W = 64
NORM_EPS = 1e-6
HEAD64 = 64
DIFF_HEADS = 4
RET_HEADS = 4
RET_CHUNK = 128
GQA_Q_HEADS = 8
GQA_GROUP = 4
N_BRANCHES = 3
BRANCH_WIDTH = 512
D_FF = 4 * D_MODEL
ATTN_COLS = 4352
LANES = 128

TM = 512
TK = 2048
NQ = 1024
NC = 256
SLAB = 256
ONES_ROWS = 16
VROWS = LANES + ONES_ROWS
GVROWS = HEAD64 + ONES_ROWS
FAST_SOFTMAX_BOUND = 64.0
VMEM_LIMIT = 56 * 1024 * 1024
LOG2E = 1.4426950408889634


def _cparams(n_axes):
    return pltpu.CompilerParams(dimension_semantics=("arbitrary",) * n_axes,
                                vmem_limit_bytes=VMEM_LIMIT)


def _resident(shape):
    nd = len(shape)
    return pl.BlockSpec(shape, lambda *_: (0,) * nd, pipeline_mode=pl.Buffered(1))


def _proj_kernel(h_ref, an_ref, w_ref, g_ref, nrm_ref,
                 cd_ref, ad_ref, bd_ref, cg_ref, ag_ref, bg_ref, cr_ref, sr_ref,
                 u_ref, dqT_ref, dk_ref, dvT_ref, rq_ref, rkT_ref, rv_ref, rg_ref,
                 gqT_ref, gk_ref, gvT_ref):
    x = h_ref[...]
    ms = jnp.mean(x * x, axis=-1, keepdims=True)
    u = (x * lax.rsqrt(ms + NORM_EPS) * an_ref[...]).astype(BF16)
    u_ref[...] = u

    def proj(c0, n):
        return jnp.dot(u, w_ref[:, c0:c0 + n], preferred_element_type=F32)

    gmat = g_ref[...]
    lo = lax.broadcasted_iota(jnp.int32, (1, LANES), 1) < HEAD64
    first_row = lax.broadcasted_iota(jnp.int32, (ONES_ROWS, TM), 0) == 0
    ones_rows = jnp.where(first_row, 1.0, 0.0).astype(BF16)

    def qk_norm(y, gain):
        msq = jnp.dot((y * y).astype(BF16), gmat, preferred_element_type=F32)
        return y * lax.rsqrt(msq + NORM_EPS) * gain

    def rope(y, c, a, b, d):
        return y * c + pltpu.roll(y, LANES - d, 1) * a + pltpu.roll(y, d, 1) * b

    qscale = HEAD64 ** -0.5 * LOG2E
    cd, ad, bd = cd_ref[...], ad_ref[...], bd_ref[...]
    cg, ag, bg = cg_ref[...], ag_ref[...], bg_ref[...]
    cr, sr = cr_ref[...], sr_ref[...]

    p = proj(0, 512)
    for hd in range(DIFF_HEADS):
        y = rope(qk_norm(p[:, hd * LANES:(hd + 1) * LANES], nrm_ref[0:1, :]), cd, ad, bd, 8) * qscale
        dqT_ref[hd, :, 0:TM] = jnp.where(lo, y, 0.0).T.astype(BF16)
        dqT_ref[hd, :, TM:2 * TM] = jnp.where(lo, 0.0, y).T.astype(BF16)
    p = proj(512, 512)
    for hd in range(DIFF_HEADS):
        y = rope(qk_norm(p[:, hd * LANES:(hd + 1) * LANES], nrm_ref[1:2, :]), cd, ad, bd, 8)
        dk_ref[:, hd * LANES:(hd + 1) * LANES] = y.astype(BF16)
    p = proj(1024, 512)
    for hd in range(DIFF_HEADS):
        dvT_ref[hd, 0:LANES, :] = p[:, hd * LANES:(hd + 1) * LANES].T.astype(BF16)
        dvT_ref[hd, LANES:VROWS, :] = ones_rows

    p = proj(1536, 512)
    for hd in range(RET_HEADS):
        y = p[:, hd * LANES:(hd + 1) * LANES]
        rq_ref[:, hd * LANES:(hd + 1) * LANES] = (y * cr + pltpu.roll(y, 64, 1) * sr).astype(BF16)
    p = proj(2048, 512)
    kscale = LANES ** -0.5
    for hd in range(RET_HEADS):
        y = p[:, hd * LANES:(hd + 1) * LANES]
        rkT_ref[hd] = ((y * cr + pltpu.roll(y, 64, 1) * sr) * kscale).T.astype(BF16)
    rv_ref[...] = proj(2560, 512).astype(BF16)
    rg_ref[...] = proj(3072, 512).astype(BF16)

    p = proj(3584, 512)
    for c in range(4):
        y = rope(qk_norm(p[:, c * LANES:(c + 1) * LANES], nrm_ref[2:3, :]), cg, ag, bg, 16) * qscale
        ysw = pltpu.roll(y, 64, 1)
        if c < 2:
            z_even, z_odd = jnp.where(lo, y, 0.0), jnp.where(lo, ysw, 0.0)
        else:
            z_even, z_odd = jnp.where(lo, 0.0, ysw), jnp.where(lo, 0.0, y)
        for hh, z in ((2 * c, z_even), (2 * c + 1, z_odd)):
            zt = z.T.astype(BF16)
            for qb in range(TM // LANES):
                gqT_ref[qb, :, hh * LANES:(hh + 1) * LANES] = zt[:, qb * LANES:(qb + 1) * LANES]
    p = proj(4096, 256)
    gk_ref[...] = rope(qk_norm(p[:, 0:LANES], nrm_ref[3:4, :]), cg, ag, bg, 16).astype(BF16)
    vt = p[:, LANES:2 * LANES].T.astype(BF16)
    for g in range(2):
        gvT_ref[g, 0:HEAD64, :] = vt[g * HEAD64:(g + 1) * HEAD64]
        gvT_ref[g, HEAD64:GVROWS, :] = ones_rows


def _proj_call(h, attn_norm, w_attn, gmat, nrm, tabs):
    T = h.shape[0]
    S = tabs[0].shape[0]
    nt = T // TM
    tab_spec = pl.BlockSpec((TM, LANES), lambda i: (i % (S // TM), 0))
    row = lambda w: pl.BlockSpec((TM, w), lambda i: (i, 0))
    out_shape = (
        jax.ShapeDtypeStruct((T, D_MODEL), BF16),
        jax.ShapeDtypeStruct((nt, 4, LANES, 2 * TM), BF16),
        jax.ShapeDtypeStruct((T, 512), BF16),
        jax.ShapeDtypeStruct((nt, 4, VROWS, TM), BF16),
        jax.ShapeDtypeStruct((T, 512), BF16),
        jax.ShapeDtypeStruct((nt, 4, LANES, TM), BF16),
        jax.ShapeDtypeStruct((T, 512), BF16),
        jax.ShapeDtypeStruct((T, 512), BF16),
        jax.ShapeDtypeStruct((T // LANES, LANES, NQ), BF16),
        jax.ShapeDtypeStruct((T, LANES), BF16),
        jax.ShapeDtypeStruct((nt, 2, GVROWS, TM), BF16),
    )
    out_specs = (
        row(D_MODEL),
        pl.BlockSpec((None, 4, LANES, 2 * TM), lambda i: (i, 0, 0, 0)),
        row(512),
        pl.BlockSpec((None, 4, VROWS, TM), lambda i: (i, 0, 0, 0)),
        row(512), pl.BlockSpec((None, 4, LANES, TM), lambda i: (i, 0, 0, 0)), row(512), row(512),
        pl.BlockSpec((TM // LANES, LANES, NQ), lambda i: (i, 0, 0)),
        row(LANES),
        pl.BlockSpec((None, 2, GVROWS, TM), lambda i: (i, 0, 0, 0)),
    )
    in_specs = [row(D_MODEL), _resident((1, D_MODEL)), _resident(w_attn.shape),
                _resident(gmat.shape), _resident(nrm.shape)] + [tab_spec] * 8
    return pl.pallas_call(
        _proj_kernel, grid=(nt,), in_specs=in_specs, out_specs=out_specs,
        out_shape=out_shape, compiler_params=_cparams(1), name="proj",
    )(h, attn_norm, w_attn, gmat, nrm, *tabs)


def _attn_core(fast_ref, qT_ref, k_ref, vT_ref, m_ref, acc_ref):
    acc_ref[...] = jnp.zeros(acc_ref.shape, F32)
    n_steps = k_ref.shape[0] // TK
    n_groups = vT_ref.shape[1]
    gw = NQ // n_groups

    def kv(j):
        k = k_ref[pl.ds(pl.multiple_of(j * TK, TK), TK), :]
        per = TK // TM
        vT = jnp.concatenate([vT_ref[j * per + i] for i in range(per)], axis=-1)
        return k, vT

    @pl.when(fast_ref[0] == 1)
    def _():
        n_slabs = TK // SLAB

        def body(j, carry):
            k, vT = kv(j)
            pv = [None] * n_groups
            prev = None
            for u in range(n_slabs + 1):
                if u < n_slabs:
                    s = jnp.dot(k[u * SLAB:(u + 1) * SLAB], qT_ref[...],
                                preferred_element_type=F32)
                if prev is not None:
                    pu, p = prev
                    for g in range(n_groups):
                        t = jnp.dot(vT[g, :, pu * SLAB:(pu + 1) * SLAB], p[:, g * gw:(g + 1) * gw],
                                    preferred_element_type=F32)
                        pv[g] = t if pv[g] is None else pv[g] + t
                if u < n_slabs:
                    prev = (u, jnp.exp2(s).astype(BF16))
            for g in range(n_groups):
                acc_ref[:, g * gw:(g + 1) * gw] += pv[g]
            return carry

        lax.fori_loop(0, n_steps, body, 0)

    @pl.when(fast_ref[0] == 0)
    def _():
        m_ref[...] = jnp.full(m_ref.shape, -1e30, F32)

        def body(j, carry):
            k, vT = kv(j)
            for c in range(NQ // NC):
                cs = slice(c * NC, (c + 1) * NC)
                s = jnp.dot(k, qT_ref[:, cs], preferred_element_type=F32)
                m_old = m_ref[:, cs]
                m_new = jnp.maximum(m_old, jnp.max(s, axis=0, keepdims=True))
                alpha = jnp.exp2(m_old - m_new)
                p = jnp.exp2(s - m_new).astype(BF16)
                pv = jnp.dot(vT[(c * NC) // gw], p, preferred_element_type=F32)
                acc_ref[:, cs] = acc_ref[:, cs] * alpha + pv
                m_ref[:, cs] = m_new
            return carry

        lax.fori_loop(0, n_steps, body, 0)


def _diff_kernel(fast_ref, lam_ref, sub_ref, qT_ref, k_ref, vT_ref, o_ref, m_ref, acc_ref, *,
                 lambda_init):
    _attn_core(fast_ref, qT_ref, k_ref, vT_ref, m_ref, acc_ref)

    lv = lam_ref[...]
    lam = (jnp.exp(jnp.sum(lv[0:1] * lv[1:2], axis=-1, keepdims=True))
           - jnp.exp(jnp.sum(lv[2:3] * lv[3:4], axis=-1, keepdims=True)) + lambda_init)
    tq = NQ // 2
    o1 = acc_ref[0:LANES, 0:tq] / acc_ref[LANES:LANES + 1, 0:tq]
    o2 = acc_ref[0:LANES, tq:NQ] / acc_ref[LANES:LANES + 1, tq:NQ]
    d = o1 - lam * o2
    ms = jnp.mean(d * d, axis=0, keepdims=True)
    dn = d * lax.rsqrt(ms + 1e-5)
    o_ref[...] = (dn.T * (sub_ref[...] * (1.0 - lambda_init))).astype(o_ref.dtype)


_SMEM_SPEC = pl.BlockSpec(memory_space=pltpu.SMEM)

def _attn_scratch(rows):
    return [pltpu.VMEM((1, NQ), F32), pltpu.VMEM((rows, NQ), F32)]


def _diff_call(fast, lamv, subln, dqT, dk, dvT, B, S, lambda_init):
    T = dk.shape[0]
    tq = NQ // 2
    nq = S // tq
    kern = functools.partial(_diff_kernel, lambda_init=lambda_init)
    return pl.pallas_call(
        kern, grid=(B, DIFF_HEADS, nq),
        in_specs=[
            _SMEM_SPEC, _resident(lamv.shape), _resident(subln.shape),
            pl.BlockSpec((None, None, LANES, NQ), lambda b, h, i: (b * nq + i, h, 0, 0)),
            pl.BlockSpec((S, LANES), lambda b, h, i: (b, h)),
            pl.BlockSpec((S // TM, 1, VROWS, TM), lambda b, h, i: (b, h, 0, 0)),
        ],
        out_specs=pl.BlockSpec((tq, LANES), lambda b, h, i: (b * nq + i, h)),
        out_shape=jax.ShapeDtypeStruct((T, 512), BF16),
        scratch_shapes=_attn_scratch(VROWS),
        compiler_params=_cparams(3), name="diff_attn",
    )(fast, lamv, subln, dqT, dk, dvT)


def _gqa_kernel(fast_ref, qT_ref, k_ref, vT_ref, o_ref, m_ref, acc_ref):
    _attn_core(fast_ref, qT_ref, k_ref, vT_ref, m_ref, acc_ref)
    for c in range(GQA_Q_HEADS // 2):
        parts = []
        for hh in (2 * c, 2 * c + 1):
            cols = slice(hh * LANES, (hh + 1) * LANES)
            parts.append(acc_ref[0:HEAD64, cols] / acc_ref[HEAD64:HEAD64 + 1, cols])
        o_ref[:, c * LANES:(c + 1) * LANES] = jnp.concatenate(parts, axis=0).T.astype(o_ref.dtype)


def _gqa_call(fast, gqT, gk, gvT, B, S):
    T = gk.shape[0]
    tq = NQ // GQA_Q_HEADS
    nq = S // tq
    return pl.pallas_call(
        _gqa_kernel, grid=(B, nq),
        in_specs=[
            _SMEM_SPEC,
            pl.BlockSpec((None, LANES, NQ), lambda b, i: (b * nq + i, 0, 0)),
            pl.BlockSpec((S, LANES), lambda b, i: (b, 0)),
            pl.BlockSpec((S // TM, 2, GVROWS, TM), lambda b, i: (b, 0, 0, 0)),
        ],
        out_specs=pl.BlockSpec((tq, 512), lambda b, i: (b * nq + i, 0)),
        out_shape=jax.ShapeDtypeStruct((T, 512), BF16),
        scratch_shapes=_attn_scratch(GVROWS),
        compiler_params=_cparams(2), name="gqa_attn",
    )(fast, gqT, gk, gvT)


def _ret_kernel(dec_ref, gain_ref, q_ref, kT_ref, v_ref, g_ref, o_ref, sf_ref, sb_ref):
    C = RET_CHUNK
    hd = pl.program_id(1)
    lf = -jnp.exp(jnp.full((C, C), dec_ref[0, hd], F32))
    lb = -jnp.exp(jnp.full((C, C), dec_ref[1, hd], F32))
    ii = lax.broadcasted_iota(jnp.int32, (C, C), 0).astype(F32)
    jj = lax.broadcasted_iota(jnp.int32, (C, C), 1).astype(F32)
    diff = ii - jj
    decay = jnp.where(diff >= 0, jnp.exp(jnp.maximum(diff, 0.0) * lf),
                      jnp.exp(jnp.maximum(-diff, 0.0) * lb))
    xi_f, zeta_f = jnp.exp((ii + 1.0) * lf), jnp.exp((C - 1.0 - ii) * lf)
    xi_b, zeta_b = jnp.exp((C - ii) * lb), jnp.exp(ii * lb)
    dec_f, dec_b = jnp.exp(C * lf), jnp.exp(C * lb)
    n_tiles = kT_ref.shape[0]
    per_tile = TM // C

    def rows(t, c):
        return pl.ds(pl.multiple_of(t * TM + c * C, C), C)

    def scan(t, state, zeta, dec, s_ref, order):
        kvs = {}
        for c in order:
            vz = (v_ref[rows(t, c), :].astype(F32) * zeta).astype(BF16)
            kvs[c] = jnp.dot(kT_ref[t, :, c * C:(c + 1) * C], vz, preferred_element_type=F32)
        for c in order:
            s_ref[t * per_tile + c] = state.astype(BF16)
            state = state * dec + kvs[c]
        return state

    zero = jnp.zeros((C, C), F32)
    lax.fori_loop(0, n_tiles,
                  lambda t, st: scan(t, st, zeta_f, dec_f, sf_ref, range(per_tile)), zero)
    lax.fori_loop(0, n_tiles,
                  lambda i, st: scan(n_tiles - 1 - i, st, zeta_b, dec_b, sb_ref,
                                     range(per_tile - 1, -1, -1)), zero)

    def outputs(t, carry):
        for c in range(per_tile):
            r = rows(t, c)
            q, v = q_ref[r, :], v_ref[r, :]
            n = t * per_tile + c
            a = jnp.dot(q, kT_ref[t, :, c * C:(c + 1) * C], preferred_element_type=F32) * decay
            o = (jnp.dot(a.astype(BF16), v, preferred_element_type=F32)
                 + jnp.dot(q, sf_ref[n], preferred_element_type=F32) * xi_f
                 + jnp.dot(q, sb_ref[n], preferred_element_type=F32) * xi_b)
            mu = jnp.mean(o, axis=-1, keepdims=True)
            oc = o - mu
            var = jnp.mean(oc * oc, axis=-1, keepdims=True)
            y = oc * lax.rsqrt(var + 1e-5) * gain_ref[...]
            g = g_ref[r, :].astype(F32)
            o_ref[r, :] = (g / (1.0 + jnp.exp(-g)) * y).astype(o_ref.dtype)
        return carry

    lax.fori_loop(0, n_tiles, outputs, 0)


def _ret_call(dec, gain, rq, rkT, rv, rg, B, S):
    T = rq.shape[0]
    blk = pl.BlockSpec((S, LANES), lambda b, h: (b, h))
    n_chunks = S // RET_CHUNK
    return pl.pallas_call(
        _ret_kernel, grid=(B, RET_HEADS),
        in_specs=[pl.BlockSpec(memory_space=pltpu.SMEM),
                  pl.BlockSpec((1, LANES), lambda b, h: (0, h)), blk,
                  pl.BlockSpec((S // TM, None, LANES, TM), lambda b, h: (b, h, 0, 0)), blk, blk],
        out_specs=blk,
        out_shape=jax.ShapeDtypeStruct((T, 512), BF16),
        scratch_shapes=[pltpu.VMEM((n_chunks, RET_CHUNK, LANES), BF16),
                        pltpu.VMEM((n_chunks, RET_CHUNK, LANES), BF16)],
        compiler_params=_cparams(2), name="retention",
    )(dec, gain, rq, rkT, rv, rg)


def _merge_kernel(h_ref, u_ref, a_ref, r_ref, c_ref, wg_ref, wb_ref, wo_ref, mn_ref,
                  hn_ref, u2_ref):
    u = u_ref[...]
    merged = None
    for n, br in enumerate((a_ref, r_ref, c_ref)):
        logits = jnp.dot(u, wg_ref[:, n * D_MODEL:(n + 1) * D_MODEL], preferred_element_type=F32)
        gate = 1.0 / (1.0 + jnp.exp(-logits))
        term = gate * jnp.dot(br[...], wb_ref[n], preferred_element_type=F32)
        merged = term if merged is None else merged + term
    hn = h_ref[...] + jnp.dot(merged.astype(BF16), wo_ref[...], preferred_element_type=F32)
    hn_ref[...] = hn
    ms = jnp.mean(hn * hn, axis=-1, keepdims=True)
    u2_ref[...] = (hn * lax.rsqrt(ms + NORM_EPS) * mn_ref[...]).astype(BF16)


def _merge_call(h, u, a, r, c, wg, wb, wo, mlp_norm):
    T = h.shape[0]
    row = lambda w: pl.BlockSpec((TM, w), lambda i: (i, 0))
    return pl.pallas_call(
        _merge_kernel, grid=(T // TM,),
        in_specs=[row(D_MODEL), row(D_MODEL), row(512), row(512), row(512),
                  _resident(wg.shape), _resident(wb.shape), _resident(wo.shape),
                  _resident((1, D_MODEL))],
        out_specs=(row(D_MODEL), row(D_MODEL)),
        out_shape=(jax.ShapeDtypeStruct((T, D_MODEL), F32),
                   jax.ShapeDtypeStruct((T, D_MODEL), BF16)),
        compiler_params=_cparams(1), name="merge",
    )(h, u, a, r, c, wg, wb, wo, mlp_norm)


def _mlp_kernel(h_ref, u_ref, w1_ref, w2_ref, o_ref):
    u = u_ref[...]
    acc = h_ref[...]
    for c in range(D_FF // D_MODEL):
        cs = slice(c * D_MODEL, (c + 1) * D_MODEL)
        m = jnp.maximum(jnp.dot(u, w1_ref[:, cs], preferred_element_type=F32), 0.0)
        acc = acc + jnp.dot((m * m).astype(BF16), w2_ref[cs, :], preferred_element_type=F32)
    o_ref[...] = acc


def _mlp_call(h, u2, w1, w2):
    T = h.shape[0]
    row = pl.BlockSpec((TM, D_MODEL), lambda i: (i, 0))
    return pl.pallas_call(
        _mlp_kernel, grid=(T // TM,),
        in_specs=[row, row, _resident(w1.shape), _resident(w2.shape)],
        out_specs=row,
        out_shape=jax.ShapeDtypeStruct((T, D_MODEL), F32),
        compiler_params=_cparams(1), name="mlp",
    )(h, u2, w1, w2)


def _rope_tables(S):
    pos = jnp.arange(S, dtype=F32)
    lane = jnp.arange(LANES)
    l64 = lane % 64

    inv = 500000.0 ** (-jnp.arange(0, 16, 2, dtype=F32) / 16)
    ang = pos[:, None] * inv[None, :]
    cos, sin = jnp.cos(ang)[:, l64 % 8], jnp.sin(ang)[:, l64 % 8]
    cd = jnp.where(l64 < 16, cos, 1.0)
    ad = jnp.where(l64 < 8, -sin, 0.0)
    bd = jnp.where((l64 >= 8) & (l64 < 16), sin, 0.0)

    inv = 10000.0 ** (-jnp.arange(0, 32, 2, dtype=F32) / 32)
    row_pos = jnp.floor(pos / GRID_W)
    col_pos = pos - row_pos * GRID_W
    p2 = jnp.where((l64 < 32)[None, :], row_pos[:, None], col_pos[:, None])
    ang = p2 * inv[l64 % 16][None, :]
    cg = jnp.cos(ang)
    ag = jnp.where(l64 % 32 < 16, -jnp.sin(ang), 0.0)
    bg = jnp.where(l64 % 32 >= 16, jnp.sin(ang), 0.0)

    inv = 10000.0 ** (-jnp.arange(0, 128, 2, dtype=F32) / 128)
    ang = pos[:, None] * inv[l64][None, :]
    cr = jnp.cos(ang)
    sr = jnp.where(lane < 64, -jnp.sin(ang), jnp.sin(ang))
    return (cd, ad, bd, cg, ag, bg, cr, sr)


def _fast_flag(q_gain, k_gain):
    bound = (HEAD64 ** 0.5 * LOG2E * 1.02) * jnp.max(jnp.abs(q_gain)) * jnp.max(jnp.abs(k_gain))
    return (bound <= FAST_SOFTMAX_BOUND).astype(jnp.int32).reshape(1)


def kernel(x, attn_norm, w_in, diff_q_norm, diff_k_norm, diff_lam_q1, diff_lam_k1, diff_lam_q2, diff_lam_k2, diff_subln, ret_decay_fwd, ret_decay_bwd, ret_group_norm, gqa_q_norm, gqa_k_norm, w_branch, w_out, mlp_norm, w_mlp_in, w_mlp_out):
    B, S, D = x.shape
    T = B * S
    tabs = _rope_tables(S)
    blk = jnp.arange(LANES) // HEAD64
    gmat = jnp.where(blk[:, None] == blk[None, :], 1.0 / HEAD64, 0.0).astype(BF16)
    tile2 = lambda v: jnp.concatenate([v, v]).astype(F32)

    h = x.reshape(T, D)
    for l in range(DEPTH):
        lambda_init = 0.8 - 0.6 * math.exp(-0.3 * l)
        w_bf = w_in[l].astype(BF16)
        nrm = jnp.stack([tile2(diff_q_norm[l]), tile2(diff_k_norm[l]),
                         tile2(gqa_q_norm[l]), tile2(gqa_k_norm[l])])
        (u, dqT, dk, dvT, rq, rk, rv, rg, gqT, gk, gvT) = _proj_call(
            h, attn_norm[l].reshape(1, D).astype(F32), w_bf[:, :ATTN_COLS], gmat, nrm, tabs)

        lamv = jnp.stack([diff_lam_q1[l], diff_lam_k1[l], diff_lam_q2[l], diff_lam_k2[l]]).astype(F32)
        a = _diff_call(_fast_flag(diff_q_norm[l], diff_k_norm[l]), lamv,
                       diff_subln[l].reshape(1, LANES).astype(F32), dqT, dk, dvT, B, S, lambda_init)
        dec = jnp.stack([ret_decay_fwd[l], ret_decay_bwd[l]]).astype(F32)
        r = _ret_call(dec, ret_group_norm[l].reshape(1, 512).astype(F32), rq, rk, rv, rg, B, S)
        c = _gqa_call(_fast_flag(gqa_q_norm[l], gqa_k_norm[l]), gqT, gk, gvT, B, S)

        h, u2 = _merge_call(h, u, a, r, c, w_bf[:, ATTN_COLS:], w_branch[l].astype(BF16),
                            w_out[l].astype(BF16), mlp_norm[l].reshape(1, D).astype(F32))
        h = _mlp_call(h, u2, w_mlp_in[l].astype(BF16), w_mlp_out[l].astype(BF16))
    return h.reshape(B, S, D)
```

```python
import functools
import math

import jax
import jax.numpy as jnp
from jax import lax
from jax.experimental import pallas as pl
from jax.experimental.pallas import tpu as pltpu

F32 = jnp.float32
BF16 = jnp.bfloat16

D_MODEL = 1024
DEPTH = 2
GRID_W = 64
NORM_EPS = 1e-6
HEAD64 = 64
DIFF_HEADS = 4
RET_HEADS = 4
RET_CHUNK = 128
GQA_Q_HEADS = 8
GQA_GROUP = 4
N_BRANCHES = 3
BRANCH_WIDTH = 512
D_FF = 4 * D_MODEL
ATTN_COLS = 4352
LANES = 128

TM = 512
TK = 2048
NQ = 1024
NC = 256
KV_UNROLL = 2
RET_OUT_TILES = 4
ONES_ROWS = 16
VROWS = LANES + ONES_ROWS
GVROWS = HEAD64 + ONES_ROWS
FAST_SOFTMAX_BOUND = 64.0
VMEM_LIMIT = 56 * 1024 * 1024
LOG2E = 1.4426950408889634


def _cparams(n_axes):
    return pltpu.CompilerParams(dimension_semantics=("arbitrary",) * n_axes,
                                vmem_limit_bytes=VMEM_LIMIT)


def _resident(shape):
    nd = len(shape)
    return pl.BlockSpec(shape, lambda *_: (0,) * nd, pipeline_mode=pl.Buffered(1))


def _proj_kernel(h_ref, an_ref, w_ref, g_ref, nrm_ref,
                 cd_ref, ad_ref, bd_ref, cg_ref, ag_ref, bg_ref, cr_ref, sr_ref,
                 u_ref, dqT_ref, dk_ref, dvT_ref, rq_ref, rkT_ref, rv_ref, rg_ref,
                 gqT_ref, gk_ref, gvT_ref):
    x = h_ref[...]
    ms = jnp.mean(x * x, axis=-1, keepdims=True)
    u = (x * lax.rsqrt(ms + NORM_EPS) * an_ref[...]).astype(BF16)
    u_ref[...] = u

    def proj(c0, n):
        return jnp.dot(u, w_ref[:, c0:c0 + n], preferred_element_type=F32)

    gmat = g_ref[...]
    lo = lax.broadcasted_iota(jnp.int32, (1, LANES), 1) < HEAD64
    first_row = lax.broadcasted_iota(jnp.int32, (ONES_ROWS, TM), 0) == 0
    ones_rows = jnp.where(first_row, 1.0, 0.0).astype(BF16)

    def qk_norm(y, gain):
        msq = jnp.dot((y * y).astype(BF16), gmat, preferred_element_type=F32)
        return y * lax.rsqrt(msq + NORM_EPS) * gain

    def rope(y, c, a, b, d):
        return y * c + pltpu.roll(y, LANES - d, 1) * a + pltpu.roll(y, d, 1) * b

    qscale = HEAD64 ** -0.5 * LOG2E
    cd, ad, bd = cd_ref[...], ad_ref[...], bd_ref[...]
    cg, ag, bg = cg_ref[...], ag_ref[...], bg_ref[...]
    cr, sr = cr_ref[...], sr_ref[...]

    p = proj(0, 512)
    for hd in range(DIFF_HEADS):
        y = rope(qk_norm(p[:, hd * LANES:(hd + 1) * LANES], nrm_ref[0:1, :]), cd, ad, bd, 8) * qscale
        dqT_ref[hd, :, 0:TM] = jnp.where(lo, y, 0.0).T.astype(BF16)
        dqT_ref[hd, :, TM:2 * TM] = jnp.where(lo, 0.0, y).T.astype(BF16)
    p = proj(512, 512)
    for hd in range(DIFF_HEADS):
        y = rope(qk_norm(p[:, hd * LANES:(hd + 1) * LANES], nrm_ref[1:2, :]), cd, ad, bd, 8)
        dk_ref[:, hd * LANES:(hd + 1) * LANES] = y.astype(BF16)
    p = proj(1024, 512)
    for hd in range(DIFF_HEADS):
        dvT_ref[hd, 0:LANES, :] = p[:, hd * LANES:(hd + 1) * LANES].T.astype(BF16)
        dvT_ref[hd, LANES:VROWS, :] = ones_rows

    p = proj(1536, 512)
    for hd in range(RET_HEADS):
        y = p[:, hd * LANES:(hd + 1) * LANES]
        rq_ref[:, hd * LANES:(hd + 1) * LANES] = (y * cr + pltpu.roll(y, 64, 1) * sr).astype(BF16)
    p = proj(2048, 512)
    kscale = LANES ** -0.5
    for hd in range(RET_HEADS):
        y = p[:, hd * LANES:(hd + 1) * LANES]
        rkT_ref[hd] = ((y * cr + pltpu.roll(y, 64, 1) * sr) * kscale).T.astype(BF16)
    rv_ref[...] = proj(2560, 512).astype(BF16)
    rg_ref[...] = proj(3072, 512).astype(BF16)

    p = proj(3584, 512)
    for c in range(4):
        y = rope(qk_norm(p[:, c * LANES:(c + 1) * LANES], nrm_ref[2:3, :]), cg, ag, bg, 16) * qscale
        ysw = pltpu.roll(y, 64, 1)
        if c < 2:
            z_even, z_odd = jnp.where(lo, y, 0.0), jnp.where(lo, ysw, 0.0)
        else:
            z_even, z_odd = jnp.where(lo, 0.0, ysw), jnp.where(lo, 0.0, y)
        for hh, z in ((2 * c, z_even), (2 * c + 1, z_odd)):
            zt = z.T.astype(BF16)
            for qb in range(TM // LANES):
                gqT_ref[qb, :, hh * LANES:(hh + 1) * LANES] = zt[:, qb * LANES:(qb + 1) * LANES]
    p = proj(4096, 256)
    gk_ref[...] = rope(qk_norm(p[:, 0:LANES], nrm_ref[3:4, :]), cg, ag, bg, 16).astype(BF16)
    vt = p[:, LANES:2 * LANES].T.astype(BF16)
    for g in range(2):
        gvT_ref[g, 0:HEAD64, :] = vt[g * HEAD64:(g + 1) * HEAD64]
        gvT_ref[g, HEAD64:GVROWS, :] = ones_rows


def _proj_call(h, attn_norm, w_attn, gmat, nrm, tabs):
    T = h.shape[0]
    S = tabs[0].shape[0]
    nt = T // TM
    tab_spec = pl.BlockSpec((TM, LANES), lambda i: (i % (S // TM), 0))
    row = lambda w: pl.BlockSpec((TM, w), lambda i: (i, 0))
    out_shape = (
        jax.ShapeDtypeStruct((T, D_MODEL), BF16),
        jax.ShapeDtypeStruct((nt, 4, LANES, 2 * TM), BF16),
        jax.ShapeDtypeStruct((T, 512), BF16),
        jax.ShapeDtypeStruct((nt, 4, VROWS, TM), BF16),
        jax.ShapeDtypeStruct((T, 512), BF16),
        jax.ShapeDtypeStruct((nt, 4, LANES, TM), BF16),
        jax.ShapeDtypeStruct((T, 512), BF16),
        jax.ShapeDtypeStruct((T, 512), BF16),
        jax.ShapeDtypeStruct((T // LANES, LANES, NQ), BF16),
        jax.ShapeDtypeStruct((T, LANES), BF16),
        jax.ShapeDtypeStruct((nt, 2, GVROWS, TM), BF16),
    )
    out_specs = (
        row(D_MODEL),
        pl.BlockSpec((None, 4, LANES, 2 * TM), lambda i: (i, 0, 0, 0)),
        row(512),
        pl.BlockSpec((None, 4, VROWS, TM), lambda i: (i, 0, 0, 0)),
        row(512), pl.BlockSpec((None, 4, LANES, TM), lambda i: (i, 0, 0, 0)), row(512), row(512),
        pl.BlockSpec((TM // LANES, LANES, NQ), lambda i: (i, 0, 0)),
        row(LANES),
        pl.BlockSpec((None, 2, GVROWS, TM), lambda i: (i, 0, 0, 0)),
    )
    in_specs = [row(D_MODEL), _resident((1, D_MODEL)), _resident(w_attn.shape),
                _resident(gmat.shape), _resident(nrm.shape)] + [tab_spec] * 8
    return pl.pallas_call(
        _proj_kernel, grid=(nt,), in_specs=in_specs, out_specs=out_specs,
        out_shape=out_shape, compiler_params=_cparams(1), name="proj",
    )(h, attn_norm, w_attn, gmat, nrm, *tabs)


def _attn_core(fast_ref, qT_ref, k_ref, vT_ref, m_ref, acc_ref):
    acc_ref[...] = jnp.zeros(acc_ref.shape, F32)
    n_steps = k_ref.shape[0] // TK
    n_groups = vT_ref.shape[1]
    gw = NQ // n_groups

    def kv(j):
        k = k_ref[pl.ds(pl.multiple_of(j * TK, TK), TK), :]
        per = TK // TM
        vT = jnp.concatenate([vT_ref[j * per + i] for i in range(per)], axis=-1)
        return k, vT

    @pl.when(fast_ref[0] == 1)
    def _():
        def body(jj, carry):
            pv = [None] * n_groups
            for u in range(KV_UNROLL):
                k, vT = kv(jj * KV_UNROLL + u)
                s = jnp.dot(k, qT_ref[...], preferred_element_type=F32)
                p = jnp.exp2(s).astype(BF16)
                for g in range(n_groups):
                    t = jnp.dot(vT[g], p[:, g * gw:(g + 1) * gw], preferred_element_type=F32)
                    pv[g] = t if pv[g] is None else pv[g] + t
            for g in range(n_groups):
                acc_ref[:, g * gw:(g + 1) * gw] += pv[g]
            return carry

        lax.fori_loop(0, n_steps // KV_UNROLL, body, 0)

    @pl.when(fast_ref[0] == 0)
    def _():
        m_ref[...] = jnp.full(m_ref.shape, -1e30, F32)

        def body(j, carry):
            k, vT = kv(j)
            for c in range(NQ // NC):
                cs = slice(c * NC, (c + 1) * NC)
                s = jnp.dot(k, qT_ref[:, cs], preferred_element_type=F32)
                m_old = m_ref[:, cs]
                m_new = jnp.maximum(m_old, jnp.max(s, axis=0, keepdims=True))
                alpha = jnp.exp2(m_old - m_new)
                p = jnp.exp2(s - m_new).astype(BF16)
                pv = jnp.dot(vT[(c * NC) // gw], p, preferred_element_type=F32)
                acc_ref[:, cs] = acc_ref[:, cs] * alpha + pv
                m_ref[:, cs] = m_new
            return carry

        lax.fori_loop(0, n_steps, body, 0)


def _diff_kernel(fast_ref, lam_ref, sub_ref, qT_ref, k_ref, vT_ref, o_ref, m_ref, acc_ref, *,
                 lambda_init):
    _attn_core(fast_ref, qT_ref, k_ref, vT_ref, m_ref, acc_ref)

    lv = lam_ref[...]
    lam = (jnp.exp(jnp.sum(lv[0:1] * lv[1:2], axis=-1, keepdims=True))
           - jnp.exp(jnp.sum(lv[2:3] * lv[3:4], axis=-1, keepdims=True)) + lambda_init)
    tq = NQ // 2
    o1 = acc_ref[0:LANES, 0:tq] / acc_ref[LANES:LANES + 1, 0:tq]
    o2 = acc_ref[0:LANES, tq:NQ] / acc_ref[LANES:LANES + 1, tq:NQ]
    d = o1 - lam * o2
    ms = jnp.mean(d * d, axis=0, keepdims=True)
    dn = d * lax.rsqrt(ms + 1e-5)
    o_ref[...] = (dn.T * (sub_ref[...] * (1.0 - lambda_init))).astype(o_ref.dtype)


_SMEM_SPEC = pl.BlockSpec(memory_space=pltpu.SMEM)

def _attn_scratch(rows):
    return [pltpu.VMEM((1, NQ), F32), pltpu.VMEM((rows, NQ), F32)]


def _diff_call(fast, lamv, subln, dqT, dk, dvT, B, S, lambda_init):
    T = dk.shape[0]
    tq = NQ // 2
    nq = S // tq
    kern = functools.partial(_diff_kernel, lambda_init=lambda_init)
    return pl.pallas_call(
        kern, grid=(B, DIFF_HEADS, nq),
        in_specs=[
            _SMEM_SPEC, _resident(lamv.shape), _resident(subln.shape),
            pl.BlockSpec((None, None, LANES, NQ), lambda b, h, i: (b * nq + i, h, 0, 0)),
            pl.BlockSpec((S, LANES), lambda b, h, i: (b, h)),
            pl.BlockSpec((S // TM, 1, VROWS, TM), lambda b, h, i: (b, h, 0, 0)),
        ],
        out_specs=pl.BlockSpec((tq, LANES), lambda b, h, i: (b * nq + i, h)),
        out_shape=jax.ShapeDtypeStruct((T, 512), BF16),
        scratch_shapes=_attn_scratch(VROWS),
        compiler_params=_cparams(3), name="diff_attn",
    )(fast, lamv, subln, dqT, dk, dvT)


def _gqa_kernel(fast_ref, qT_ref, k_ref, vT_ref, o_ref, m_ref, acc_ref):
    _attn_core(fast_ref, qT_ref, k_ref, vT_ref, m_ref, acc_ref)
    for c in range(GQA_Q_HEADS // 2):
        parts = []
        for hh in (2 * c, 2 * c + 1):
            cols = slice(hh * LANES, (hh + 1) * LANES)
            parts.append(acc_ref[0:HEAD64, cols] / acc_ref[HEAD64:HEAD64 + 1, cols])
        o_ref[:, c * LANES:(c + 1) * LANES] = jnp.concatenate(parts, axis=0).T.astype(o_ref.dtype)


def _gqa_call(fast, gqT, gk, gvT, B, S):
    T = gk.shape[0]
    tq = NQ // GQA_Q_HEADS
    nq = S // tq
    return pl.pallas_call(
        _gqa_kernel, grid=(B, nq),
        in_specs=[
            _SMEM_SPEC,
            pl.BlockSpec((None, LANES, NQ), lambda b, i: (b * nq + i, 0, 0)),
            pl.BlockSpec((S, LANES), lambda b, i: (b, 0)),
            pl.BlockSpec((S // TM, 2, GVROWS, TM), lambda b, i: (b, 0, 0, 0)),
        ],
        out_specs=pl.BlockSpec((tq, 512), lambda b, i: (b * nq + i, 0)),
        out_shape=jax.ShapeDtypeStruct((T, 512), BF16),
        scratch_shapes=_attn_scratch(GVROWS),
        compiler_params=_cparams(2), name="gqa_attn",
    )(fast, gqT, gk, gvT)


def _ret_kernel(dec_ref, gain_ref, q_ref, kT_ref, v_ref, g_ref, o_ref, sf_ref, sb_ref):
    C = RET_CHUNK
    hd = pl.program_id(1)
    lf = -jnp.exp(jnp.full((C, C), dec_ref[0, hd], F32))
    lb = -jnp.exp(jnp.full((C, C), dec_ref[1, hd], F32))
    ii = lax.broadcasted_iota(jnp.int32, (C, C), 0).astype(F32)
    jj = lax.broadcasted_iota(jnp.int32, (C, C), 1).astype(F32)
    diff = ii - jj
    decay = jnp.where(diff >= 0, jnp.exp(jnp.maximum(diff, 0.0) * lf),
                      jnp.exp(jnp.maximum(-diff, 0.0) * lb))
    xi_f, zeta_f = jnp.exp((ii + 1.0) * lf), jnp.exp((C - 1.0 - ii) * lf)
    xi_b, zeta_b = jnp.exp((C - ii) * lb), jnp.exp(ii * lb)
    dec_f, dec_b = jnp.exp(C * lf), jnp.exp(C * lb)
    n_tiles = kT_ref.shape[0]
    per_tile = TM // C

    def rows(t, c):
        return pl.ds(pl.multiple_of(t * TM + c * C, C), C)

    def chunk_kv(t, c, zeta):
        vz = (v_ref[rows(t, c), :].astype(F32) * zeta).astype(BF16)
        return jnp.dot(kT_ref[t, :, c * C:(c + 1) * C], vz, preferred_element_type=F32)

    def scans(i, states):
        st_f, st_b = states
        tf, tb = i, n_tiles - 1 - i
        kv_f = [chunk_kv(tf, c, zeta_f) for c in range(per_tile)]
        kv_b = [chunk_kv(tb, c, zeta_b) for c in range(per_tile)]
        for c in range(per_tile):
            sf_ref[tf * per_tile + c] = st_f.astype(BF16)
            st_f = st_f * dec_f + kv_f[c]
            cb = per_tile - 1 - c
            sb_ref[tb * per_tile + cb] = st_b.astype(BF16)
            st_b = st_b * dec_b + kv_b[cb]
        return st_f, st_b

    zero = jnp.zeros((C, C), F32)
    lax.fori_loop(0, n_tiles, scans, (zero, zero))

    def outputs(tt, carry):
        chunks = [(tt * RET_OUT_TILES + i, c) for i in range(RET_OUT_TILES) for c in range(per_tile)]
        rs = [rows(t, c) for t, c in chunks]
        qs = [q_ref[r, :] for r in rs]
        a = [jnp.dot(q, kT_ref[t, :, c * C:(c + 1) * C], preferred_element_type=F32)
             for q, (t, c) in zip(qs, chunks)]
        cross = [jnp.dot(q, sf_ref[t * per_tile + c], preferred_element_type=F32) * xi_f
                 + jnp.dot(q, sb_ref[t * per_tile + c], preferred_element_type=F32) * xi_b
                 for q, (t, c) in zip(qs, chunks)]
        a = [(x * decay).astype(BF16) for x in a]
        o = [jnp.dot(x, v_ref[r, :], preferred_element_type=F32) + y
             for x, r, y in zip(a, rs, cross)]
        mu = [jnp.mean(x, axis=-1, keepdims=True) for x in o]
        oc = [x - m for x, m in zip(o, mu)]
        var = [jnp.mean(x * x, axis=-1, keepdims=True) for x in oc]
        for x, s2, r in zip(oc, var, rs):
            y = x * lax.rsqrt(s2 + 1e-5) * gain_ref[...]
            g = g_ref[r, :].astype(F32)
            o_ref[r, :] = (g / (1.0 + jnp.exp(-g)) * y).astype(o_ref.dtype)
        return carry

    lax.fori_loop(0, n_tiles // RET_OUT_TILES, outputs, 0)


def _ret_call(dec, gain, rq, rkT, rv, rg, B, S):
    T = rq.shape[0]
    blk = pl.BlockSpec((S, LANES), lambda b, h: (b, h))
    n_chunks = S // RET_CHUNK
    return pl.pallas_call(
        _ret_kernel, grid=(B, RET_HEADS),
        in_specs=[pl.BlockSpec(memory_space=pltpu.SMEM),
                  pl.BlockSpec((1, LANES), lambda b, h: (0, h)), blk,
                  pl.BlockSpec((S // TM, None, LANES, TM), lambda b, h: (b, h, 0, 0)), blk, blk],
        out_specs=blk,
        out_shape=jax.ShapeDtypeStruct((T, 512), BF16),
        scratch_shapes=[pltpu.VMEM((n_chunks, RET_CHUNK, LANES), BF16),
                        pltpu.VMEM((n_chunks, RET_CHUNK, LANES), BF16)],
        compiler_params=_cparams(2), name="retention",
    )(dec, gain, rq, rkT, rv, rg)


def _merge_kernel(h_ref, u_ref, a_ref, r_ref, c_ref, wg_ref, wb_ref, wo_ref, mn_ref,
                  hn_ref, u2_ref):
    u = u_ref[...]
    merged = None
    for n, br in enumerate((a_ref, r_ref, c_ref)):
        logits = jnp.dot(u, wg_ref[:, n * D_MODEL:(n + 1) * D_MODEL], preferred_element_type=F32)
        gate = 1.0 / (1.0 + jnp.exp(-logits))
        term = gate * jnp.dot(br[...], wb_ref[n], preferred_element_type=F32)
        merged = term if merged is None else merged + term
    hn = h_ref[...] + jnp.dot(merged.astype(BF16), wo_ref[...], preferred_element_type=F32)
    hn_ref[...] = hn
    ms = jnp.mean(hn * hn, axis=-1, keepdims=True)
    u2_ref[...] = (hn * lax.rsqrt(ms + NORM_EPS) * mn_ref[...]).astype(BF16)


def _merge_call(h, u, a, r, c, wg, wb, wo, mlp_norm):
    T = h.shape[0]
    row = lambda w: pl.BlockSpec((TM, w), lambda i: (i, 0))
    return pl.pallas_call(
        _merge_kernel, grid=(T // TM,),
        in_specs=[row(D_MODEL), row(D_MODEL), row(512), row(512), row(512),
                  _resident(wg.shape), _resident(wb.shape), _resident(wo.shape),
                  _resident((1, D_MODEL))],
        out_specs=(row(D_MODEL), row(D_MODEL)),
        out_shape=(jax.ShapeDtypeStruct((T, D_MODEL), F32),
                   jax.ShapeDtypeStruct((T, D_MODEL), BF16)),
        compiler_params=_cparams(1), name="merge",
    )(h, u, a, r, c, wg, wb, wo, mlp_norm)


def _mlp_kernel(h_ref, u_ref, w1_ref, w2_ref, o_ref):
    u = u_ref[...]
    acc = h_ref[...]
    for c in range(D_FF // D_MODEL):
        cs = slice(c * D_MODEL, (c + 1) * D_MODEL)
        m = jnp.maximum(jnp.dot(u, w1_ref[:, cs], preferred_element_type=F32), 0.0)
        acc = acc + jnp.dot((m * m).astype(BF16), w2_ref[cs, :], preferred_element_type=F32)
    o_ref[...] = acc


def _mlp_call(h, u2, w1, w2):
    T = h.shape[0]
    row = pl.BlockSpec((TM, D_MODEL), lambda i: (i, 0))
    return pl.pallas_call(
        _mlp_kernel, grid=(T // TM,),
        in_specs=[row, row, _resident(w1.shape), _resident(w2.shape)],
        out_specs=row,
        out_shape=jax.ShapeDtypeStruct((T, D_MODEL), F32),
        compiler_params=_cparams(1), name="mlp",
    )(h, u2, w1, w2)


def _rope_tables(S):
    pos = jnp.arange(S, dtype=F32)
    lane = jnp.arange(LANES)
    l64 = lane % 64

    inv = 500000.0 ** (-jnp.arange(0, 16, 2, dtype=F32) / 16)
    ang = pos[:, None] * inv[None, :]
    cos, sin = jnp.cos(ang)[:, l64 % 8], jnp.sin(ang)[:, l64 % 8]
    cd = jnp.where(l64 < 16, cos, 1.0)
    ad = jnp.where(l64 < 8, -sin, 0.0)
    bd = jnp.where((l64 >= 8) & (l64 < 16), sin, 0.0)

    inv = 10000.0 ** (-jnp.arange(0, 32, 2, dtype=F32) / 32)
    row_pos = jnp.floor(pos / GRID_W)
    col_pos = pos - row_pos * GRID_W
    p2 = jnp.where((l64 < 32)[None, :], row_pos[:, None], col_pos[:, None])
    ang = p2 * inv[l64 % 16][None, :]
    cg = jnp.cos(ang)
    ag = jnp.where(l64 % 32 < 16, -jnp.sin(ang), 0.0)
    bg = jnp.where(l64 % 32 >= 16, jnp.sin(ang), 0.0)

    inv = 10000.0 ** (-jnp.arange(0, 128, 2, dtype=F32) / 128)
    ang = pos[:, None] * inv[l64][None, :]
    cr = jnp.cos(ang)
    sr = jnp.where(lane < 64, -jnp.sin(ang), jnp.sin(ang))
    return (cd, ad, bd, cg, ag, bg, cr, sr)


def _fast_flag(q_gain, k_gain):
    bound = (HEAD64 ** 0.5 * LOG2E * 1.02) * jnp.max(jnp.abs(q_gain)) * jnp.max(jnp.abs(k_gain))
    return (bound <= FAST_SOFTMAX_BOUND).astype(jnp.int32).reshape(1)


def kernel(x, attn_norm, w_in, diff_q_norm, diff_k_norm, diff_lam_q1, diff_lam_k1, diff_lam_q2, diff_lam_k2, diff_subln, ret_decay_fwd, ret_decay_bwd, ret_group_norm, gqa_q_norm, gqa_k_norm, w_branch, w_out, mlp_norm, w_mlp_in, w_mlp_out):
    B, S, D = x.shape
    T = B * S
    tabs = _rope_tables(S)
    blk = jnp.arange(LANES) // HEAD64
    gmat = jnp.where(blk[:, None] == blk[None, :], 1.0 / HEAD64, 0.0).astype(BF16)
    tile2 = lambda v: jnp.concatenate([v, v]).astype(F32)

    h = x.reshape(T, D)
    for l in range(DEPTH):
        lambda_init = 0.8 - 0.6 * math.exp(-0.3 * l)
        w_bf = w_in[l].astype(BF16)
        nrm = jnp.stack([tile2(diff_q_norm[l]), tile2(diff_k_norm[l]),
                         tile2(gqa_q_norm[l]), tile2(gqa_k_norm[l])])
        (u, dqT, dk, dvT, rq, rk, rv, rg, gqT, gk, gvT) = _proj_call(
            h, attn_norm[l].reshape(1, D).astype(F32), w_bf[:, :ATTN_COLS], gmat, nrm, tabs)

        lamv = jnp.stack([diff_lam_q1[l], diff_lam_k1[l], diff_lam_q2[l], diff_lam_k2[l]]).astype(F32)
        a = _diff_call(_fast_flag(diff_q_norm[l], diff_k_norm[l]), lamv,
                       diff_subln[l].reshape(1, LANES).astype(F32), dqT, dk, dvT, B, S, lambda_init)
        dec = jnp.stack([ret_decay_fwd[l], ret_decay_bwd[l]]).astype(F32)
        r = _ret_call(dec, ret_group_norm[l].reshape(1, 512).astype(F32), rq, rk, rv, rg, B, S)
        c = _gqa_call(_fast_flag(gqa_q_norm[l], gqa_k_norm[l]), gqT, gk, gvT, B, S)

        h, u2 = _merge_call(h, u, a, r, c, w_bf[:, ATTN_COLS:], w_branch[l].astype(BF16),
                            w_out[l].astype(BF16), mlp_norm[l].reshape(1, D).astype(F32))
        h = _mlp_call(h, u2, w_mlp_in[l].astype(BF16), w_mlp_out[l].astype(BF16))
    return h.reshape(B, S, D)
```

```python
import functools
import math

import jax
import jax.numpy as jnp
from jax import lax
from jax.experimental import pallas as pl
from jax.experimental.pallas import tpu as pltpu

F32 = jnp.float32
BF16 = jnp.bfloat16

D_MODEL = 1024
DEPTH = 2
GRID_W = 64
NORM_EPS = 1e-6
HEAD64 = 64
DIFF_HEADS = 4
RET_HEADS = 4
RET_CHUNK = 128
GQA_Q_HEADS = 8
GQA_GROUP = 4
N_BRANCHES = 3
BRANCH_WIDTH = 512
D_FF = 4 * D_MODEL
ATTN_COLS = 4352
LANES = 128

TM = 512
TK = 2048
NQ = 1024
NC = 256
KV_UNROLL = 2
RET_OUT_TILES = 4
ONES_ROWS = 16
VROWS = LANES + ONES_ROWS
GVROWS = HEAD64 + ONES_ROWS
FAST_SOFTMAX_BOUND = 64.0
VMEM_LIMIT = 56 * 1024 * 1024
LOG2E = 1.4426950408889634


def _cparams(n_axes):
    return pltpu.CompilerParams(dimension_semantics=("arbitrary",) * n_axes,
                                vmem_limit_bytes=VMEM_LIMIT)


def _resident(shape):
    nd = len(shape)
    return pl.BlockSpec(shape, lambda *_: (0,) * nd, pipeline_mode=pl.Buffered(1))


def _proj_kernel(h_ref, an_ref, w_ref, g_ref, nrm_ref,
                 cd_ref, ad_ref, bd_ref, cg_ref, ag_ref, bg_ref, cr_ref, sr_ref,
                 u_ref, dqT_ref, dk_ref, dvT_ref, rq_ref, rkT_ref, rv_ref, rg_ref,
                 gqT_ref, gk_ref, gvT_ref):
    x = h_ref[...]
    ms = jnp.mean(x * x, axis=-1, keepdims=True)
    u = (x * lax.rsqrt(ms + NORM_EPS) * an_ref[...]).astype(BF16)
    u_ref[...] = u

    def proj(c0, n):
        return jnp.dot(u, w_ref[:, c0:c0 + n], preferred_element_type=F32)

    gmat = g_ref[...]
    lo = lax.broadcasted_iota(jnp.int32, (1, LANES), 1) < HEAD64
    first_row = lax.broadcasted_iota(jnp.int32, (ONES_ROWS, TM), 0) == 0
    ones_rows = jnp.where(first_row, 1.0, 0.0).astype(BF16)

    def qk_norm(p, gain):
        out = []
        for c0 in range(0, p.shape[1], 2 * LANES):
            y = p[:, c0:c0 + 2 * LANES]
            w = y.shape[1]
            msq = jnp.dot((y * y).astype(BF16), gmat[0:w, 0:w], preferred_element_type=F32)
            y = y * lax.rsqrt(msq + NORM_EPS)
            out += [y[:, i:i + LANES] * gain for i in range(0, w, LANES)]
        return out

    def rope(y, c, a, b, d):
        return y * c + pltpu.roll(y, LANES - d, 1) * a + pltpu.roll(y, d, 1) * b

    def chunks(p):
        return [p[:, i:i + LANES] for i in range(0, p.shape[1], LANES)]

    qscale = HEAD64 ** -0.5 * LOG2E
    kscale = LANES ** -0.5
    cd, ad, bd = cd_ref[...], ad_ref[...], bd_ref[...]
    cg, ag, bg = cg_ref[...], ag_ref[...], bg_ref[...]
    cr, sr = cr_ref[...], sr_ref[...]

    def diff_q(p):
        for hd, y in enumerate(qk_norm(p, nrm_ref[0:1, :])):
            y = rope(y, cd, ad, bd, 8) * qscale
            dqT_ref[hd, :, 0:TM] = jnp.where(lo, y, 0.0).T.astype(BF16)
            dqT_ref[hd, :, TM:2 * TM] = jnp.where(lo, 0.0, y).T.astype(BF16)

    def diff_k(p):
        for hd, y in enumerate(qk_norm(p, nrm_ref[1:2, :])):
            dk_ref[:, hd * LANES:(hd + 1) * LANES] = rope(y, cd, ad, bd, 8).astype(BF16)

    def diff_v(p):
        for hd, y in enumerate(chunks(p)):
            dvT_ref[hd, 0:LANES, :] = y.T.astype(BF16)
            dvT_ref[hd, LANES:VROWS, :] = ones_rows

    def ret_q(p):
        for hd, y in enumerate(chunks(p)):
            rq_ref[:, hd * LANES:(hd + 1) * LANES] = (y * cr + pltpu.roll(y, 64, 1) * sr).astype(BF16)

    def ret_k(p):
        for hd, y in enumerate(chunks(p)):
            rkT_ref[hd] = ((y * cr + pltpu.roll(y, 64, 1) * sr) * kscale).T.astype(BF16)

    def ret_v(p):
        rv_ref[...] = p.astype(BF16)

    def ret_g(p):
        rg_ref[...] = p.astype(BF16)

    def gqa_q(p):
        for c, y in enumerate(qk_norm(p, nrm_ref[2:3, :])):
            y = rope(y, cg, ag, bg, 16) * qscale
            ysw = pltpu.roll(y, 64, 1)
            if c < 2:
                z_even, z_odd = jnp.where(lo, y, 0.0), jnp.where(lo, ysw, 0.0)
            else:
                z_even, z_odd = jnp.where(lo, 0.0, ysw), jnp.where(lo, 0.0, y)
            for hh, z in ((2 * c, z_even), (2 * c + 1, z_odd)):
                zt = z.T.astype(BF16)
                for qb in range(TM // LANES):
                    gqT_ref[qb, :, hh * LANES:(hh + 1) * LANES] = zt[:, qb * LANES:(qb + 1) * LANES]

    def gqa_kv(p):
        (y,) = qk_norm(p[:, 0:LANES], nrm_ref[3:4, :])
        gk_ref[...] = rope(y, cg, ag, bg, 16).astype(BF16)
        vt = p[:, LANES:2 * LANES].T.astype(BF16)
        for g in range(2):
            gvT_ref[g, 0:HEAD64, :] = vt[g * HEAD64:(g + 1) * HEAD64]
            gvT_ref[g, HEAD64:GVROWS, :] = ones_rows

    sections = [(3584, 512, gqa_q), (0, 512, diff_q), (2048, 512, ret_k), (1024, 512, diff_v),
                (512, 512, diff_k), (4096, 256, gqa_kv), (1536, 512, ret_q), (2560, 512, ret_v),
                (3072, 512, ret_g)]
    p_next = proj(*sections[0][:2])
    for i, (_, _, epilogue) in enumerate(sections):
        p_cur = p_next
        if i + 1 < len(sections):
            p_next = proj(*sections[i + 1][:2])
        epilogue(p_cur)


def _proj_call(h, attn_norm, w_attn, gmat, nrm, tabs):
    T = h.shape[0]
    S = tabs[0].shape[0]
    nt = T // TM
    tab_spec = pl.BlockSpec((TM, LANES), lambda i: (i % (S // TM), 0))
    row = lambda w: pl.BlockSpec((TM, w), lambda i: (i, 0))
    out_shape = (
        jax.ShapeDtypeStruct((T, D_MODEL), BF16),
        jax.ShapeDtypeStruct((nt, 4, LANES, 2 * TM), BF16),
        jax.ShapeDtypeStruct((T, 512), BF16),
        jax.ShapeDtypeStruct((nt, 4, VROWS, TM), BF16),
        jax.ShapeDtypeStruct((T, 512), BF16),
        jax.ShapeDtypeStruct((nt, 4, LANES, TM), BF16),
        jax.ShapeDtypeStruct((T, 512), BF16),
        jax.ShapeDtypeStruct((T, 512), BF16),
        jax.ShapeDtypeStruct((T // LANES, LANES, NQ), BF16),
        jax.ShapeDtypeStruct((T, LANES), BF16),
        jax.ShapeDtypeStruct((nt, 2, GVROWS, TM), BF16),
    )
    out_specs = (
        row(D_MODEL),
        pl.BlockSpec((None, 4, LANES, 2 * TM), lambda i: (i, 0, 0, 0)),
        row(512),
        pl.BlockSpec((None, 4, VROWS, TM), lambda i: (i, 0, 0, 0)),
        row(512), pl.BlockSpec((None, 4, LANES, TM), lambda i: (i, 0, 0, 0)), row(512), row(512),
        pl.BlockSpec((TM // LANES, LANES, NQ), lambda i: (i, 0, 0)),
        row(LANES),
        pl.BlockSpec((None, 2, GVROWS, TM), lambda i: (i, 0, 0, 0)),
    )
    in_specs = [row(D_MODEL), _resident((1, D_MODEL)), _resident(w_attn.shape),
                _resident(gmat.shape), _resident(nrm.shape)] + [tab_spec] * 8
    return pl.pallas_call(
        _proj_kernel, grid=(nt,), in_specs=in_specs, out_specs=out_specs,
        out_shape=out_shape, compiler_params=_cparams(1), name="proj",
    )(h, attn_norm, w_attn, gmat, nrm, *tabs)


def _attn_core(fast_ref, qT_ref, k_ref, vT_ref, m_ref, acc_ref):
    acc_ref[...] = jnp.zeros(acc_ref.shape, F32)
    n_steps = k_ref.shape[0] // TK
    n_groups = vT_ref.shape[1]
    gw = NQ // n_groups

    def kv(j):
        k = k_ref[pl.ds(pl.multiple_of(j * TK, TK), TK), :]
        per = TK // TM
        vT = jnp.concatenate([vT_ref[j * per + i] for i in range(per)], axis=-1)
        return k, vT

    @pl.when(fast_ref[0] == 1)
    def _():
        def body(jj, carry):
            pv = [None] * n_groups
            for u in range(KV_UNROLL):
                k, vT = kv(jj * KV_UNROLL + u)
                s = jnp.dot(k, qT_ref[...], preferred_element_type=F32)
                p = jnp.exp2(s).astype(BF16)
                for g in range(n_groups):
                    t = jnp.dot(vT[g], p[:, g * gw:(g + 1) * gw], preferred_element_type=F32)
                    pv[g] = t if pv[g] is None else pv[g] + t
            for g in range(n_groups):
                acc_ref[:, g * gw:(g + 1) * gw] += pv[g]
            return carry

        lax.fori_loop(0, n_steps // KV_UNROLL, body, 0)

    @pl.when(fast_ref[0] == 0)
    def _():
        m_ref[...] = jnp.full(m_ref.shape, -1e30, F32)

        def body(j, carry):
            k, vT = kv(j)
            for c in range(NQ // NC):
                cs = slice(c * NC, (c + 1) * NC)
                s = jnp.dot(k, qT_ref[:, cs], preferred_element_type=F32)
                m_old = m_ref[:, cs]
                m_new = jnp.maximum(m_old, jnp.max(s, axis=0, keepdims=True))
                alpha = jnp.exp2(m_old - m_new)
                p = jnp.exp2(s - m_new).astype(BF16)
                pv = jnp.dot(vT[(c * NC) // gw], p, preferred_element_type=F32)
                acc_ref[:, cs] = acc_ref[:, cs] * alpha + pv
                m_ref[:, cs] = m_new
            return carry

        lax.fori_loop(0, n_steps, body, 0)


def _diff_kernel(fast_ref, lam_ref, sub_ref, qT_ref, k_ref, vT_ref, o_ref, m_ref, acc_ref, *,
                 lambda_init):
    _attn_core(fast_ref, qT_ref, k_ref, vT_ref, m_ref, acc_ref)

    lv = lam_ref[...]
    lam = (jnp.exp(jnp.sum(lv[0:1] * lv[1:2], axis=-1, keepdims=True))
           - jnp.exp(jnp.sum(lv[2:3] * lv[3:4], axis=-1, keepdims=True)) + lambda_init)
    tq = NQ // 2
    o1 = acc_ref[0:LANES, 0:tq] / acc_ref[LANES:LANES + 1, 0:tq]
    o2 = acc_ref[0:LANES, tq:NQ] / acc_ref[LANES:LANES + 1, tq:NQ]
    d = o1 - lam * o2
    ms = jnp.mean(d * d, axis=0, keepdims=True)
    dn = d * lax.rsqrt(ms + 1e-5)
    o_ref[...] = (dn.T * (sub_ref[...] * (1.0 - lambda_init))).astype(o_ref.dtype)


_SMEM_SPEC = pl.BlockSpec(memory_space=pltpu.SMEM)

def _attn_scratch(rows):
    return [pltpu.VMEM((1, NQ), F32), pltpu.VMEM((rows, NQ), F32)]


def _diff_call(fast, lamv, subln, dqT, dk, dvT, B, S, lambda_init):
    T = dk.shape[0]
    tq = NQ // 2
    nq = S // tq
    kern = functools.partial(_diff_kernel, lambda_init=lambda_init)
    return pl.pallas_call(
        kern, grid=(B, DIFF_HEADS, nq),
        in_specs=[
            _SMEM_SPEC, _resident(lamv.shape), _resident(subln.shape),
            pl.BlockSpec((None, None, LANES, NQ), lambda b, h, i: (b * nq + i, h, 0, 0)),
            pl.BlockSpec((S, LANES), lambda b, h, i: (b, h)),
            pl.BlockSpec((S // TM, 1, VROWS, TM), lambda b, h, i: (b, h, 0, 0)),
        ],
        out_specs=pl.BlockSpec((tq, LANES), lambda b, h, i: (b * nq + i, h)),
        out_shape=jax.ShapeDtypeStruct((T, 512), BF16),
        scratch_shapes=_attn_scratch(VROWS),
        compiler_params=_cparams(3), name="diff_attn",
    )(fast, lamv, subln, dqT, dk, dvT)


def _gqa_kernel(fast_ref, qT_ref, k_ref, vT_ref, o_ref, m_ref, acc_ref):
    _attn_core(fast_ref, qT_ref, k_ref, vT_ref, m_ref, acc_ref)
    for c in range(GQA_Q_HEADS // 2):
        parts = []
        for hh in (2 * c, 2 * c + 1):
            cols = slice(hh * LANES, (hh + 1) * LANES)
            parts.append(acc_ref[0:HEAD64, cols] / acc_ref[HEAD64:HEAD64 + 1, cols])
        o_ref[:, c * LANES:(c + 1) * LANES] = jnp.concatenate(parts, axis=0).T.astype(o_ref.dtype)


def _gqa_call(fast, gqT, gk, gvT, B, S):
    T = gk.shape[0]
    tq = NQ // GQA_Q_HEADS
    nq = S // tq
    return pl.pallas_call(
        _gqa_kernel, grid=(B, nq),
        in_specs=[
            _SMEM_SPEC,
            pl.BlockSpec((None, LANES, NQ), lambda b, i: (b * nq + i, 0, 0)),
            pl.BlockSpec((S, LANES), lambda b, i: (b, 0)),
            pl.BlockSpec((S // TM, 2, GVROWS, TM), lambda b, i: (b, 0, 0, 0)),
        ],
        out_specs=pl.BlockSpec((tq, 512), lambda b, i: (b * nq + i, 0)),
        out_shape=jax.ShapeDtypeStruct((T, 512), BF16),
        scratch_shapes=_attn_scratch(GVROWS),
        compiler_params=_cparams(2), name="gqa_attn",
    )(fast, gqT, gk, gvT)


def _ret_kernel(dec_ref, gain_ref, q_ref, kT_ref, v_ref, g_ref, o_ref, sf_ref, sb_ref):
    C = RET_CHUNK
    hd = pl.program_id(1)
    lf = -jnp.exp(jnp.full((C, C), dec_ref[0, hd], F32))
    lb = -jnp.exp(jnp.full((C, C), dec_ref[1, hd], F32))
    ii = lax.broadcasted_iota(jnp.int32, (C, C), 0).astype(F32)
    jj = lax.broadcasted_iota(jnp.int32, (C, C), 1).astype(F32)
    diff = ii - jj
    decay = jnp.where(diff >= 0, jnp.exp(jnp.maximum(diff, 0.0) * lf),
                      jnp.exp(jnp.maximum(-diff, 0.0) * lb))
    xi_f, zeta_f = jnp.exp((ii + 1.0) * lf), jnp.exp((C - 1.0 - ii) * lf)
    xi_b, zeta_b = jnp.exp((C - ii) * lb), jnp.exp(ii * lb)
    dec_f, dec_b = jnp.exp(C * lf), jnp.exp(C * lb)
    n_tiles = kT_ref.shape[0]
    per_tile = TM // C

    def rows(t, c):
        return pl.ds(pl.multiple_of(t * TM + c * C, C), C)

    def chunk_kv(t, c, zeta):
        vz = (v_ref[rows(t, c), :].astype(F32) * zeta).astype(BF16)
        return jnp.dot(kT_ref[t, :, c * C:(c + 1) * C], vz, preferred_element_type=F32)

    def scans(i, states):
        st_f, st_b = states
        tf, tb = i, n_tiles - 1 - i
        kv_f = [chunk_kv(tf, c, zeta_f) for c in range(per_tile)]
        kv_b = [chunk_kv(tb, c, zeta_b) for c in range(per_tile)]
        for c in range(per_tile):
            sf_ref[tf * per_tile + c] = st_f.astype(BF16)
            st_f = st_f * dec_f + kv_f[c]
            cb = per_tile - 1 - c
            sb_ref[tb * per_tile + cb] = st_b.astype(BF16)
            st_b = st_b * dec_b + kv_b[cb]
        return st_f, st_b

    zero = jnp.zeros((C, C), F32)
    lax.fori_loop(0, n_tiles, scans, (zero, zero))

    def outputs(tt, carry):
        chunks = [(tt * RET_OUT_TILES + i, c) for i in range(RET_OUT_TILES) for c in range(per_tile)]
        rs = [rows(t, c) for t, c in chunks]
        qs = [q_ref[r, :] for r in rs]
        a = [jnp.dot(q, kT_ref[t, :, c * C:(c + 1) * C], preferred_element_type=F32)
             for q, (t, c) in zip(qs, chunks)]
        cross = [jnp.dot(q, sf_ref[t * per_tile + c], preferred_element_type=F32) * xi_f
                 + jnp.dot(q, sb_ref[t * per_tile + c], preferred_element_type=F32) * xi_b
                 for q, (t, c) in zip(qs, chunks)]
        a = [(x * decay).astype(BF16) for x in a]
        o = [jnp.dot(x, v_ref[r, :], preferred_element_type=F32) + y
             for x, r, y in zip(a, rs, cross)]
        mu = [jnp.mean(x, axis=-1, keepdims=True) for x in o]
        oc = [x - m for x, m in zip(o, mu)]
        var = [jnp.mean(x * x, axis=-1, keepdims=True) for x in oc]
        for x, s2, r in zip(oc, var, rs):
            y = x * lax.rsqrt(s2 + 1e-5) * gain_ref[...]
            g = g_ref[r, :].astype(F32)
            o_ref[r, :] = (g / (1.0 + jnp.exp(-g)) * y).astype(o_ref.dtype)
        return carry

    lax.fori_loop(0, n_tiles // RET_OUT_TILES, outputs, 0)


def _ret_call(dec, gain, rq, rkT, rv, rg, B, S):
    T = rq.shape[0]
    blk = pl.BlockSpec((S, LANES), lambda b, h: (b, h))
    n_chunks = S // RET_CHUNK
    return pl.pallas_call(
        _ret_kernel, grid=(B, RET_HEADS),
        in_specs=[pl.BlockSpec(memory_space=pltpu.SMEM),
                  pl.BlockSpec((1, LANES), lambda b, h: (0, h)), blk,
                  pl.BlockSpec((S // TM, None, LANES, TM), lambda b, h: (b, h, 0, 0)), blk, blk],
        out_specs=blk,
        out_shape=jax.ShapeDtypeStruct((T, 512), BF16),
        scratch_shapes=[pltpu.VMEM((n_chunks, RET_CHUNK, LANES), BF16),
                        pltpu.VMEM((n_chunks, RET_CHUNK, LANES), BF16)],
        compiler_params=_cparams(2), name="retention",
    )(dec, gain, rq, rkT, rv, rg)


def _merge_kernel(h_ref, u_ref, a_ref, r_ref, c_ref, wg_ref, wb_ref, wo_ref, mn_ref,
                  hn_ref, u2_ref):
    u = u_ref[...]
    merged = None
    for n, br in enumerate((a_ref, r_ref, c_ref)):
        logits = jnp.dot(u, wg_ref[:, n * D_MODEL:(n + 1) * D_MODEL], preferred_element_type=F32)
        gate = 1.0 / (1.0 + jnp.exp(-logits))
        term = gate * jnp.dot(br[...], wb_ref[n], preferred_element_type=F32)
        merged = term if merged is None else merged + term
    hn = h_ref[...] + jnp.dot(merged.astype(BF16), wo_ref[...], preferred_element_type=F32)
    hn_ref[...] = hn
    ms = jnp.mean(hn * hn, axis=-1, keepdims=True)
    u2_ref[...] = (hn * lax.rsqrt(ms + NORM_EPS) * mn_ref[...]).astype(BF16)


def _merge_call(h, u, a, r, c, wg, wb, wo, mlp_norm):
    T = h.shape[0]
    row = lambda w: pl.BlockSpec((TM, w), lambda i: (i, 0))
    return pl.pallas_call(
        _merge_kernel, grid=(T // TM,),
        in_specs=[row(D_MODEL), row(D_MODEL), row(512), row(512), row(512),
                  _resident(wg.shape), _resident(wb.shape), _resident(wo.shape),
                  _resident((1, D_MODEL))],
        out_specs=(row(D_MODEL), row(D_MODEL)),
        out_shape=(jax.ShapeDtypeStruct((T, D_MODEL), F32),
                   jax.ShapeDtypeStruct((T, D_MODEL), BF16)),
        compiler_params=_cparams(1), name="merge",
    )(h, u, a, r, c, wg, wb, wo, mlp_norm)


def _mlp_kernel(h_ref, u_ref, w1_ref, w2_ref, o_ref):
    u = u_ref[...]
    acc = h_ref[...]
    for c in range(D_FF // D_MODEL):
        cs = slice(c * D_MODEL, (c + 1) * D_MODEL)
        m = jnp.maximum(jnp.dot(u, w1_ref[:, cs], preferred_element_type=F32), 0.0)
        acc = acc + jnp.dot((m * m).astype(BF16), w2_ref[cs, :], preferred_element_type=F32)
    o_ref[...] = acc


def _mlp_call(h, u2, w1, w2):
    T = h.shape[0]
    row = pl.BlockSpec((TM, D_MODEL), lambda i: (i, 0))
    return pl.pallas_call(
        _mlp_kernel, grid=(T // TM,),
        in_specs=[row, row, _resident(w1.shape), _resident(w2.shape)],
        out_specs=row,
        out_shape=jax.ShapeDtypeStruct((T, D_MODEL), F32),
        compiler_params=_cparams(1), name="mlp",
    )(h, u2, w1, w2)


def _rope_tables(S):
    pos = jnp.arange(S, dtype=F32)
    lane = jnp.arange(LANES)
    l64 = lane % 64

    inv = 500000.0 ** (-jnp.arange(0, 16, 2, dtype=F32) / 16)
    ang = pos[:, None] * inv[None, :]
    cos, sin = jnp.cos(ang)[:, l64 % 8], jnp.sin(ang)[:, l64 % 8]
    cd = jnp.where(l64 < 16, cos, 1.0)
    ad = jnp.where(l64 < 8, -sin, 0.0)
    bd = jnp.where((l64 >= 8) & (l64 < 16), sin, 0.0)

    inv = 10000.0 ** (-jnp.arange(0, 32, 2, dtype=F32) / 32)
    row_pos = jnp.floor(pos / GRID_W)
    col_pos = pos - row_pos * GRID_W
    p2 = jnp.where((l64 < 32)[None, :], row_pos[:, None], col_pos[:, None])
    ang = p2 * inv[l64 % 16][None, :]
    cg = jnp.cos(ang)
    ag = jnp.where(l64 % 32 < 16, -jnp.sin(ang), 0.0)
    bg = jnp.where(l64 % 32 >= 16, jnp.sin(ang), 0.0)

    inv = 10000.0 ** (-jnp.arange(0, 128, 2, dtype=F32) / 128)
    ang = pos[:, None] * inv[l64][None, :]
    cr = jnp.cos(ang)
    sr = jnp.where(lane < 64, -jnp.sin(ang), jnp.sin(ang))
    return (cd, ad, bd, cg, ag, bg, cr, sr)


def _fast_flag(q_gain, k_gain):
    bound = (HEAD64 ** 0.5 * LOG2E * 1.02) * jnp.max(jnp.abs(q_gain)) * jnp.max(jnp.abs(k_gain))
    return (bound <= FAST_SOFTMAX_BOUND).astype(jnp.int32).reshape(1)


def kernel(x, attn_norm, w_in, diff_q_norm, diff_k_norm, diff_lam_q1, diff_lam_k1, diff_lam_q2, diff_lam_k2, diff_subln, ret_decay_fwd, ret_decay_bwd, ret_group_norm, gqa_q_norm, gqa_k_norm, w_branch, w_out, mlp_norm, w_mlp_in, w_mlp_out):
    B, S, D = x.shape
    T = B * S
    tabs = _rope_tables(S)
    blk = jnp.arange(2 * LANES) // HEAD64
    gmat = jnp.where(blk[:, None] == blk[None, :], 1.0 / HEAD64, 0.0).astype(BF16)
    tile2 = lambda v: jnp.concatenate([v, v]).astype(F32)

    h = x.reshape(T, D)
    for l in range(DEPTH):
        lambda_init = 0.8 - 0.6 * math.exp(-0.3 * l)
        w_bf = w_in[l].astype(BF16)
        nrm = jnp.stack([tile2(diff_q_norm[l]), tile2(diff_k_norm[l]),
                         tile2(gqa_q_norm[l]), tile2(gqa_k_norm[l])])
        (u, dqT, dk, dvT, rq, rk, rv, rg, gqT, gk, gvT) = _proj_call(
            h, attn_norm[l].reshape(1, D).astype(F32), w_bf[:, :ATTN_COLS], gmat, nrm, tabs)

        lamv = jnp.stack([diff_lam_q1[l], diff_lam_k1[l], diff_lam_q2[l], diff_lam_k2[l]]).astype(F32)
        a = _diff_call(_fast_flag(diff_q_norm[l], diff_k_norm[l]), lamv,
                       diff_subln[l].reshape(1, LANES).astype(F32), dqT, dk, dvT, B, S, lambda_init)
        dec = jnp.stack([ret_decay_fwd[l], ret_decay_bwd[l]]).astype(F32)
        r = _ret_call(dec, ret_group_norm[l].reshape(1, 512).astype(F32), rq, rk, rv, rg, B, S)
        c = _gqa_call(_fast_flag(gqa_q_norm[l], gqa_k_norm[l]), gqT, gk, gvT, B, S)

        h, u2 = _merge_call(h, u, a, r, c, w_bf[:, ATTN_COLS:], w_branch[l].astype(BF16),
                            w_out[l].astype(BF16), mlp_norm[l].reshape(1, D).astype(F32))
        h = _mlp_call(h, u2, w_mlp_in[l].astype(BF16), w_mlp_out[l].astype(BF16))
    return h.reshape(B, S, D)
```

```python
import functools
import math

import jax
import jax.numpy as jnp
from jax import lax
from jax.experimental import pallas as pl
from jax.experimental.pallas import tpu as pltpu

F32 = jnp.float32
BF16 = jnp.bfloat16

D_MODEL = 1024
DEPTH = 2
GRID_W = 64
NORM_EPS = 1e-6
HEAD64 = 64
DIFF_HEADS = 4
RET_HEADS = 4
RET_CHUNK = 128
GQA_Q_HEADS = 8
GQA_GROUP = 4
N_BRANCHES = 3
BRANCH_WIDTH = 512
D_FF = 4 * D_MODEL
ATTN_COLS = 4352
LANES = 128

TM = 512
TK = 2048
NQ = 1024
NC = 256
KV_UNROLL = 2
RET_OUT_TILES = 4
ONES_ROWS = 16
VROWS = LANES + ONES_ROWS
FAST_SOFTMAX_BOUND = 64.0
VMEM_LIMIT = 56 * 1024 * 1024
LOG2E = 1.4426950408889634


def _cparams(n_axes):
    return pltpu.CompilerParams(dimension_semantics=("arbitrary",) * n_axes,
                                vmem_limit_bytes=VMEM_LIMIT)


def _resident(shape):
    nd = len(shape)
    return pl.BlockSpec(shape, lambda *_: (0,) * nd, pipeline_mode=pl.Buffered(1))


def _proj_kernel(h_ref, an_ref, w_ref, g_ref, nrm_ref,
                 cd_ref, ad_ref, bd_ref, cg_ref, ag_ref, bg_ref, cr_ref, sr_ref,
                 u_ref, dqT_ref, dk_ref, dvT_ref, rq_ref, rkT_ref, rv_ref, rg_ref,
                 gqT_ref, gk_ref, gvT_ref):
    x = h_ref[...]
    ms = jnp.mean(x * x, axis=-1, keepdims=True)
    u = (x * lax.rsqrt(ms + NORM_EPS) * an_ref[...]).astype(BF16)
    u_ref[...] = u

    def proj(c0, n):
        return jnp.dot(u, w_ref[:, c0:c0 + n], preferred_element_type=F32)

    gmat = g_ref[...]
    lo = lax.broadcasted_iota(jnp.int32, (1, LANES), 1) < HEAD64
    first_row = lax.broadcasted_iota(jnp.int32, (ONES_ROWS, TM), 0) == 0
    ones_rows = jnp.where(first_row, 1.0, 0.0).astype(BF16)

    def qk_norm(p, gain):
        out = []
        for c0 in range(0, p.shape[1], 2 * LANES):
            y = p[:, c0:c0 + 2 * LANES]
            w = y.shape[1]
            msq = jnp.dot((y * y).astype(BF16), gmat[0:w, 0:w], preferred_element_type=F32)
            y = y * lax.rsqrt(msq + NORM_EPS)
            out += [y[:, i:i + LANES] * gain for i in range(0, w, LANES)]
        return out

    def rope(y, c, a, b, d):
        return y * c + pltpu.roll(y, LANES - d, 1) * a + pltpu.roll(y, d, 1) * b

    def chunks(p):
        return [p[:, i:i + LANES] for i in range(0, p.shape[1], LANES)]

    qscale = HEAD64 ** -0.5 * LOG2E
    kscale = LANES ** -0.5
    cd, ad, bd = cd_ref[...], ad_ref[...], bd_ref[...]
    cg, ag, bg = cg_ref[...], ag_ref[...], bg_ref[...]
    cr, sr = cr_ref[...], sr_ref[...]

    def diff_q(p):
        for hd, y in enumerate(qk_norm(p, nrm_ref[0:1, :])):
            y = rope(y, cd, ad, bd, 8) * qscale
            dqT_ref[hd, :, 0:TM] = jnp.where(lo, y, 0.0).T.astype(BF16)
            dqT_ref[hd, :, TM:2 * TM] = jnp.where(lo, 0.0, y).T.astype(BF16)

    def diff_k(p):
        for hd, y in enumerate(qk_norm(p, nrm_ref[1:2, :])):
            dk_ref[:, hd * LANES:(hd + 1) * LANES] = rope(y, cd, ad, bd, 8).astype(BF16)

    def diff_v(p):
        for hd, y in enumerate(chunks(p)):
            dvT_ref[hd, 0:LANES, :] = y.T.astype(BF16)
            dvT_ref[hd, LANES:VROWS, :] = ones_rows

    def ret_q(p):
        for hd, y in enumerate(chunks(p)):
            rq_ref[:, hd * LANES:(hd + 1) * LANES] = (y * cr + pltpu.roll(y, 64, 1) * sr).astype(BF16)

    def ret_k(p):
        for hd, y in enumerate(chunks(p)):
            rkT_ref[hd] = ((y * cr + pltpu.roll(y, 64, 1) * sr) * kscale).T.astype(BF16)

    def ret_v(p):
        rv_ref[...] = p.astype(BF16)

    def ret_g(p):
        rg_ref[...] = p.astype(BF16)

    def gqa_q(p):
        for c, y in enumerate(qk_norm(p, nrm_ref[2:3, :])):
            y = rope(y, cg, ag, bg, 16) * qscale
            ysw = pltpu.roll(y, 64, 1)
            if c < 2:
                z_even, z_odd = jnp.where(lo, y, 0.0), jnp.where(lo, ysw, 0.0)
            else:
                z_even, z_odd = jnp.where(lo, 0.0, ysw), jnp.where(lo, 0.0, y)
            for hh, z in ((2 * c, z_even), (2 * c + 1, z_odd)):
                zt = z.T.astype(BF16)
                for qb in range(TM // LANES):
                    gqT_ref[qb, :, hh * LANES:(hh + 1) * LANES] = zt[:, qb * LANES:(qb + 1) * LANES]

    def gqa_kv(p):
        (y,) = qk_norm(p[:, 0:LANES], nrm_ref[3:4, :])
        gk_ref[...] = rope(y, cg, ag, bg, 16).astype(BF16)
        gvT_ref[0, 0:LANES, :] = p[:, LANES:2 * LANES].T.astype(BF16)
        gvT_ref[0, LANES:VROWS, :] = ones_rows

    sections = [(3584, 512, gqa_q), (0, 512, diff_q), (2048, 512, ret_k), (1024, 512, diff_v),
                (512, 512, diff_k), (4096, 256, gqa_kv), (1536, 512, ret_q), (2560, 512, ret_v),
                (3072, 512, ret_g)]
    p_next = proj(*sections[0][:2])
    for i, (_, _, epilogue) in enumerate(sections):
        p_cur = p_next
        if i + 1 < len(sections):
            p_next = proj(*sections[i + 1][:2])
        epilogue(p_cur)


def _proj_call(h, attn_norm, w_attn, gmat, nrm, tabs):
    T = h.shape[0]
    S = tabs[0].shape[0]
    nt = T // TM
    tab_spec = pl.BlockSpec((TM, LANES), lambda i: (i % (S // TM), 0))
    row = lambda w: pl.BlockSpec((TM, w), lambda i: (i, 0))
    out_shape = (
        jax.ShapeDtypeStruct((T, D_MODEL), BF16),
        jax.ShapeDtypeStruct((nt, 4, LANES, 2 * TM), BF16),
        jax.ShapeDtypeStruct((T, 512), BF16),
        jax.ShapeDtypeStruct((nt, 4, VROWS, TM), BF16),
        jax.ShapeDtypeStruct((T, 512), BF16),
        jax.ShapeDtypeStruct((nt, 4, LANES, TM), BF16),
        jax.ShapeDtypeStruct((T, 512), BF16),
        jax.ShapeDtypeStruct((T, 512), BF16),
        jax.ShapeDtypeStruct((T // LANES, LANES, NQ), BF16),
        jax.ShapeDtypeStruct((T, LANES), BF16),
        jax.ShapeDtypeStruct((nt, 1, VROWS, TM), BF16),
    )
    out_specs = (
        row(D_MODEL),
        pl.BlockSpec((None, 4, LANES, 2 * TM), lambda i: (i, 0, 0, 0)),
        row(512),
        pl.BlockSpec((None, 4, VROWS, TM), lambda i: (i, 0, 0, 0)),
        row(512), pl.BlockSpec((None, 4, LANES, TM), lambda i: (i, 0, 0, 0)), row(512), row(512),
        pl.BlockSpec((TM // LANES, LANES, NQ), lambda i: (i, 0, 0)),
        row(LANES),
        pl.BlockSpec((None, 1, VROWS, TM), lambda i: (i, 0, 0, 0)),
    )
    in_specs = [row(D_MODEL), _resident((1, D_MODEL)), _resident(w_attn.shape),
                _resident(gmat.shape), _resident(nrm.shape)] + [tab_spec] * 8
    return pl.pallas_call(
        _proj_kernel, grid=(nt,), in_specs=in_specs, out_specs=out_specs,
        out_shape=out_shape, compiler_params=_cparams(1), name="proj",
    )(h, attn_norm, w_attn, gmat, nrm, *tabs)


def _attn_core(fast_ref, qT_ref, k_ref, vT_ref, m_ref, acc_ref):
    acc_ref[...] = jnp.zeros(acc_ref.shape, F32)
    n_steps = k_ref.shape[0] // TK
    n_groups = vT_ref.shape[1]
    gw = NQ // n_groups

    def kv(j):
        k = k_ref[pl.ds(pl.multiple_of(j * TK, TK), TK), :]
        per = TK // TM
        vT = jnp.concatenate([vT_ref[j * per + i] for i in range(per)], axis=-1)
        return k, vT

    @pl.when(fast_ref[0] == 1)
    def _():
        def body(jj, carry):
            pv = [None] * n_groups
            for u in range(KV_UNROLL):
                k, vT = kv(jj * KV_UNROLL + u)
                s = jnp.dot(k, qT_ref[...], preferred_element_type=F32)
                p = jnp.exp2(s).astype(BF16)
                for g in range(n_groups):
                    t = jnp.dot(vT[g], p[:, g * gw:(g + 1) * gw], preferred_element_type=F32)
                    pv[g] = t if pv[g] is None else pv[g] + t
            for g in range(n_groups):
                acc_ref[:, g * gw:(g + 1) * gw] += pv[g]
            return carry

        lax.fori_loop(0, n_steps // KV_UNROLL, body, 0)

    @pl.when(fast_ref[0] == 0)
    def _():
        m_ref[...] = jnp.full(m_ref.shape, -1e30, F32)

        def body(j, carry):
            k, vT = kv(j)
            for c in range(NQ // NC):
                cs = slice(c * NC, (c + 1) * NC)
                s = jnp.dot(k, qT_ref[:, cs], preferred_element_type=F32)
                m_old = m_ref[:, cs]
                m_new = jnp.maximum(m_old, jnp.max(s, axis=0, keepdims=True))
                alpha = jnp.exp2(m_old - m_new)
                p = jnp.exp2(s - m_new).astype(BF16)
                pv = jnp.dot(vT[(c * NC) // gw], p, preferred_element_type=F32)
                acc_ref[:, cs] = acc_ref[:, cs] * alpha + pv
                m_ref[:, cs] = m_new
            return carry

        lax.fori_loop(0, n_steps, body, 0)


def _diff_kernel(fast_ref, lam_ref, sub_ref, qT_ref, k_ref, vT_ref, o_ref, m_ref, acc_ref, *,
                 lambda_init):
    _attn_core(fast_ref, qT_ref, k_ref, vT_ref, m_ref, acc_ref)

    lv = lam_ref[...]
    lam = (jnp.exp(jnp.sum(lv[0:1] * lv[1:2], axis=-1, keepdims=True))
           - jnp.exp(jnp.sum(lv[2:3] * lv[3:4], axis=-1, keepdims=True)) + lambda_init)
    tq = NQ // 2
    o1 = acc_ref[0:LANES, 0:tq] / acc_ref[LANES:LANES + 1, 0:tq]
    o2 = acc_ref[0:LANES, tq:NQ] / acc_ref[LANES:LANES + 1, tq:NQ]
    d = o1 - lam * o2
    ms = jnp.mean(d * d, axis=0, keepdims=True)
    dn = d * lax.rsqrt(ms + 1e-5)
    o_ref[...] = (dn.T * (sub_ref[...] * (1.0 - lambda_init))).astype(o_ref.dtype)


_SMEM_SPEC = pl.BlockSpec(memory_space=pltpu.SMEM)

def _attn_scratch(rows):
    return [pltpu.VMEM((1, NQ), F32), pltpu.VMEM((rows, NQ), F32)]


def _diff_call(fast, lamv, subln, dqT, dk, dvT, B, S, lambda_init):
    T = dk.shape[0]
    tq = NQ // 2
    nq = S // tq
    kern = functools.partial(_diff_kernel, lambda_init=lambda_init)
    return pl.pallas_call(
        kern, grid=(B, DIFF_HEADS, nq),
        in_specs=[
            _SMEM_SPEC, _resident(lamv.shape), _resident(subln.shape),
            pl.BlockSpec((None, None, LANES, NQ), lambda b, h, i: (b * nq + i, h, 0, 0)),
            pl.BlockSpec((S, LANES), lambda b, h, i: (b, h)),
            pl.BlockSpec((S // TM, 1, VROWS, TM), lambda b, h, i: (b, h, 0, 0)),
        ],
        out_specs=pl.BlockSpec((tq, LANES), lambda b, h, i: (b * nq + i, h)),
        out_shape=jax.ShapeDtypeStruct((T, 512), BF16),
        scratch_shapes=_attn_scratch(VROWS),
        compiler_params=_cparams(3), name="diff_attn",
    )(fast, lamv, subln, dqT, dk, dvT)


def _gqa_kernel(fast_ref, qT_ref, k_ref, vT_ref, o_ref, m_ref, acc_ref):
    _attn_core(fast_ref, qT_ref, k_ref, vT_ref, m_ref, acc_ref)
    for c in range(GQA_Q_HEADS // 2):
        g = (2 * c) // GQA_GROUP
        rows = slice(g * HEAD64, (g + 1) * HEAD64)
        parts = []
        for hh in (2 * c, 2 * c + 1):
            cols = slice(hh * LANES, (hh + 1) * LANES)
            parts.append(acc_ref[rows, cols] / acc_ref[LANES:LANES + 1, cols])
        o_ref[:, c * LANES:(c + 1) * LANES] = jnp.concatenate(parts, axis=0).T.astype(o_ref.dtype)


def _gqa_call(fast, gqT, gk, gvT, B, S):
    T = gk.shape[0]
    tq = NQ // GQA_Q_HEADS
    nq = S // tq
    return pl.pallas_call(
        _gqa_kernel, grid=(B, nq),
        in_specs=[
            _SMEM_SPEC,
            pl.BlockSpec((None, LANES, NQ), lambda b, i: (b * nq + i, 0, 0)),
            pl.BlockSpec((S, LANES), lambda b, i: (b, 0)),
            pl.BlockSpec((S // TM, 1, VROWS, TM), lambda b, i: (b, 0, 0, 0)),
        ],
        out_specs=pl.BlockSpec((tq, 512), lambda b, i: (b * nq + i, 0)),
        out_shape=jax.ShapeDtypeStruct((T, 512), BF16),
        scratch_shapes=_attn_scratch(VROWS),
        compiler_params=_cparams(2), name="gqa_attn",
    )(fast, gqT, gk, gvT)


def _ret_kernel(dec_ref, gain_ref, q_ref, kT_ref, v_ref, g_ref, o_ref, sf_ref, sb_ref):
    C = RET_CHUNK
    hd = pl.program_id(1)
    lf = -jnp.exp(jnp.full((C, C), dec_ref[0, hd], F32))
    lb = -jnp.exp(jnp.full((C, C), dec_ref[1, hd], F32))
    ii = lax.broadcasted_iota(jnp.int32, (C, C), 0).astype(F32)
    jj = lax.broadcasted_iota(jnp.int32, (C, C), 1).astype(F32)
    diff = ii - jj
    decay = jnp.where(diff >= 0, jnp.exp(jnp.maximum(diff, 0.0) * lf),
                      jnp.exp(jnp.maximum(-diff, 0.0) * lb))
    xi_f, zeta_f = jnp.exp((ii + 1.0) * lf), jnp.exp((C - 1.0 - ii) * lf)
    xi_b, zeta_b = jnp.exp((C - ii) * lb), jnp.exp(ii * lb)
    dec_f, dec_b = jnp.exp(C * lf), jnp.exp(C * lb)
    n_tiles = kT_ref.shape[0]
    per_tile = TM // C

    def rows(t, c):
        return pl.ds(pl.multiple_of(t * TM + c * C, C), C)

    def chunk_kv(t, c, zeta):
        vz = (v_ref[rows(t, c), :].astype(F32) * zeta).astype(BF16)
        return jnp.dot(kT_ref[t, :, c * C:(c + 1) * C], vz, preferred_element_type=F32)

    def scans(i, states):
        st_f, st_b = states
        tf, tb = i, n_tiles - 1 - i
        kv_f = [chunk_kv(tf, c, zeta_f) for c in range(per_tile)]
        kv_b = [chunk_kv(tb, c, zeta_b) for c in range(per_tile)]
        for c in range(per_tile):
            sf_ref[tf * per_tile + c] = st_f.astype(BF16)
            st_f = st_f * dec_f + kv_f[c]
            cb = per_tile - 1 - c
            sb_ref[tb * per_tile + cb] = st_b.astype(BF16)
            st_b = st_b * dec_b + kv_b[cb]
        return st_f, st_b

    zero = jnp.zeros((C, C), F32)
    lax.fori_loop(0, n_tiles, scans, (zero, zero))

    def outputs(tt, carry):
        chunks = [(tt * RET_OUT_TILES + i, c) for i in range(RET_OUT_TILES) for c in range(per_tile)]
        rs = [rows(t, c) for t, c in chunks]
        qs = [q_ref[r, :] for r in rs]
        a = [jnp.dot(q, kT_ref[t, :, c * C:(c + 1) * C], preferred_element_type=F32)
             for q, (t, c) in zip(qs, chunks)]
        cross = [jnp.dot(q, sf_ref[t * per_tile + c], preferred_element_type=F32) * xi_f
                 + jnp.dot(q, sb_ref[t * per_tile + c], preferred_element_type=F32) * xi_b
                 for q, (t, c) in zip(qs, chunks)]
        a = [(x * decay).astype(BF16) for x in a]
        o = [jnp.dot(x, v_ref[r, :], preferred_element_type=F32) + y
             for x, r, y in zip(a, rs, cross)]
        mu = [jnp.mean(x, axis=-1, keepdims=True) for x in o]
        oc = [x - m for x, m in zip(o, mu)]
        var = [jnp.mean(x * x, axis=-1, keepdims=True) for x in oc]
        for x, s2, r in zip(oc, var, rs):
            y = x * lax.rsqrt(s2 + 1e-5) * gain_ref[...]
            g = g_ref[r, :].astype(F32)
            o_ref[r, :] = (g / (1.0 + jnp.exp(-g)) * y).astype(o_ref.dtype)
        return carry

    lax.fori_loop(0, n_tiles // RET_OUT_TILES, outputs, 0)


def _ret_call(dec, gain, rq, rkT, rv, rg, B, S):
    T = rq.shape[0]
    blk = pl.BlockSpec((S, LANES), lambda b, h: (b, h))
    n_chunks = S // RET_CHUNK
    return pl.pallas_call(
        _ret_kernel, grid=(B, RET_HEADS),
        in_specs=[pl.BlockSpec(memory_space=pltpu.SMEM),
                  pl.BlockSpec((1, LANES), lambda b, h: (0, h)), blk,
                  pl.BlockSpec((S // TM, None, LANES, TM), lambda b, h: (b, h, 0, 0)), blk, blk],
        out_specs=blk,
        out_shape=jax.ShapeDtypeStruct((T, 512), BF16),
        scratch_shapes=[pltpu.VMEM((n_chunks, RET_CHUNK, LANES), BF16),
                        pltpu.VMEM((n_chunks, RET_CHUNK, LANES), BF16)],
        compiler_params=_cparams(2), name="retention",
    )(dec, gain, rq, rkT, rv, rg)


def _merge_kernel(h_ref, u_ref, a_ref, r_ref, c_ref, wg_ref, wb_ref, wo_ref, mn_ref,
                  hn_ref, u2_ref):
    u = u_ref[...]
    merged = None
    for n, br in enumerate((a_ref, r_ref, c_ref)):
        logits = jnp.dot(u, wg_ref[:, n * D_MODEL:(n + 1) * D_MODEL], preferred_element_type=F32)
        gate = 1.0 / (1.0 + jnp.exp(-logits))
        term = gate * jnp.dot(br[...], wb_ref[n], preferred_element_type=F32)
        merged = term if merged is None else merged + term
    hn = h_ref[...] + jnp.dot(merged.astype(BF16), wo_ref[...], preferred_element_type=F32)
    hn_ref[...] = hn
    ms = jnp.mean(hn * hn, axis=-1, keepdims=True)
    u2_ref[...] = (hn * lax.rsqrt(ms + NORM_EPS) * mn_ref[...]).astype(BF16)


def _merge_call(h, u, a, r, c, wg, wb, wo, mlp_norm):
    T = h.shape[0]
    row = lambda w: pl.BlockSpec((TM, w), lambda i: (i, 0))
    return pl.pallas_call(
        _merge_kernel, grid=(T // TM,),
        in_specs=[row(D_MODEL), row(D_MODEL), row(512), row(512), row(512),
                  _resident(wg.shape), _resident(wb.shape), _resident(wo.shape),
                  _resident((1, D_MODEL))],
        out_specs=(row(D_MODEL), row(D_MODEL)),
        out_shape=(jax.ShapeDtypeStruct((T, D_MODEL), F32),
                   jax.ShapeDtypeStruct((T, D_MODEL), BF16)),
        compiler_params=_cparams(1), name="merge",
    )(h, u, a, r, c, wg, wb, wo, mlp_norm)


def _mlp_kernel(h_ref, u_ref, w1_ref, w2_ref, o_ref):
    u = u_ref[...]
    acc = h_ref[...]
    for c in range(D_FF // D_MODEL):
        cs = slice(c * D_MODEL, (c + 1) * D_MODEL)
        m = jnp.maximum(jnp.dot(u, w1_ref[:, cs], preferred_element_type=F32), 0.0)
        acc = acc + jnp.dot((m * m).astype(BF16), w2_ref[cs, :], preferred_element_type=F32)
    o_ref[...] = acc


def _mlp_call(h, u2, w1, w2):
    T = h.shape[0]
    row = pl.BlockSpec((TM, D_MODEL), lambda i: (i, 0))
    return pl.pallas_call(
        _mlp_kernel, grid=(T // TM,),
        in_specs=[row, row, _resident(w1.shape), _resident(w2.shape)],
        out_specs=row,
        out_shape=jax.ShapeDtypeStruct((T, D_MODEL), F32),
        compiler_params=_cparams(1), name="mlp",
    )(h, u2, w1, w2)


def _rope_tables(S):
    pos = jnp.arange(S, dtype=F32)
    lane = jnp.arange(LANES)
    l64 = lane % 64

    inv = 500000.0 ** (-jnp.arange(0, 16, 2, dtype=F32) / 16)
    ang = pos[:, None] * inv[None, :]
    cos, sin = jnp.cos(ang)[:, l64 % 8], jnp.sin(ang)[:, l64 % 8]
    cd = jnp.where(l64 < 16, cos, 1.0)
    ad = jnp.where(l64 < 8, -sin, 0.0)
    bd = jnp.where((l64 >= 8) & (l64 < 16), sin, 0.0)

    inv = 10000.0 ** (-jnp.arange(0, 32, 2, dtype=F32) / 32)
    row_pos = jnp.floor(pos / GRID_W)
    col_pos = pos - row_pos * GRID_W
    p2 = jnp.where((l64 < 32)[None, :], row_pos[:, None], col_pos[:, None])
    ang = p2 * inv[l64 % 16][None, :]
    cg = jnp.cos(ang)
    ag = jnp.where(l64 % 32 < 16, -jnp.sin(ang), 0.0)
    bg = jnp.where(l64 % 32 >= 16, jnp.sin(ang), 0.0)

    inv = 10000.0 ** (-jnp.arange(0, 128, 2, dtype=F32) / 128)
    ang = pos[:, None] * inv[l64][None, :]
    cr = jnp.cos(ang)
    sr = jnp.where(lane < 64, -jnp.sin(ang), jnp.sin(ang))
    return (cd, ad, bd, cg, ag, bg, cr, sr)


def _fast_flag(q_gain, k_gain):
    bound = (HEAD64 ** 0.5 * LOG2E * 1.02) * jnp.max(jnp.abs(q_gain)) * jnp.max(jnp.abs(k_gain))
    return (bound <= FAST_SOFTMAX_BOUND).astype(jnp.int32).reshape(1)


def kernel(x, attn_norm, w_in, diff_q_norm, diff_k_norm, diff_lam_q1, diff_lam_k1, diff_lam_q2, diff_lam_k2, diff_subln, ret_decay_fwd, ret_decay_bwd, ret_group_norm, gqa_q_norm, gqa_k_norm, w_branch, w_out, mlp_norm, w_mlp_in, w_mlp_out):
    B, S, D = x.shape
    T = B * S
    tabs = _rope_tables(S)
    blk = jnp.arange(2 * LANES) // HEAD64
    gmat = jnp.where(blk[:, None] == blk[None, :], 1.0 / HEAD64, 0.0).astype(BF16)
    tile2 = lambda v: jnp.concatenate([v, v]).astype(F32)

    h = x.reshape(T, D)
    for l in range(DEPTH):
        lambda_init = 0.8 - 0.6 * math.exp(-0.3 * l)
        w_attn = w_in[l, :, :ATTN_COLS].astype(BF16)
        w_gate = w_in[l, :, ATTN_COLS:].astype(BF16)
        nrm = jnp.stack([tile2(diff_q_norm[l]), tile2(diff_k_norm[l]),
                         tile2(gqa_q_norm[l]), tile2(gqa_k_norm[l])])
        (u, dqT, dk, dvT, rq, rk, rv, rg, gqT, gk, gvT) = _proj_call(
            h, attn_norm[l].reshape(1, D).astype(F32), w_attn, gmat, nrm, tabs)

        lamv = jnp.stack([diff_lam_q1[l], diff_lam_k1[l], diff_lam_q2[l], diff_lam_k2[l]]).astype(F32)
        a = _diff_call(_fast_flag(diff_q_norm[l], diff_k_norm[l]), lamv,
                       diff_subln[l].reshape(1, LANES).astype(F32), dqT, dk, dvT, B, S, lambda_init)
        dec = jnp.stack([ret_decay_fwd[l], ret_decay_bwd[l]]).astype(F32)
        r = _ret_call(dec, ret_group_norm[l].reshape(1, 512).astype(F32), rq, rk, rv, rg, B, S)
        c = _gqa_call(_fast_flag(gqa_q_norm[l], gqa_k_norm[l]), gqT, gk, gvT, B, S)

        h, u2 = _merge_call(h, u, a, r, c, w_gate, w_branch[l].astype(BF16),
                            w_out[l].astype(BF16), mlp_norm[l].reshape(1, D).astype(F32))
        h = _mlp_call(h, u2, w_mlp_in[l].astype(BF16), w_mlp_out[l].astype(BF16))
    return h.reshape(B, S, D)
```

```python
import functools
import math

import jax
import jax.numpy as jnp
from jax import lax
from jax.experimental import pallas as pl
from jax.experimental.pallas import tpu as pltpu

F32 = jnp.float32
BF16 = jnp.bfloat16

D_MODEL = 1024
DEPTH = 2
GRID_W = 64
NORM_EPS = 1e-6
HEAD64 = 64
DIFF_HEADS = 4
RET_HEADS = 4
RET_CHUNK = 128
GQA_Q_HEADS = 8
GQA_GROUP = 4
N_BRANCHES = 3
BRANCH_WIDTH = 512
D_FF = 4 * D_MODEL
ATTN_COLS = 4352
LANES = 128

TM = 512
TK = 2048
NQ = 1024
NC = 256
KV_UNROLL = 2
RET_OUT_TILES = 4
ONES_ROWS = 16
VROWS = LANES + ONES_ROWS
FAST_SOFTMAX_BOUND = 64.0
VMEM_LIMIT = 56 * 1024 * 1024
LOG2E = 1.4426950408889634


def _cparams(n_axes):
    return pltpu.CompilerParams(dimension_semantics=("arbitrary",) * n_axes,
                                vmem_limit_bytes=VMEM_LIMIT)


def _resident(shape):
    nd = len(shape)
    return pl.BlockSpec(shape, lambda *_: (0,) * nd, pipeline_mode=pl.Buffered(1))


def _proj_kernel(h_ref, an_ref, w_ref, g_ref, nrm_ref,
                 cd_ref, ad_ref, bd_ref, cg_ref, ag_ref, bg_ref, cr_ref, sr_ref,
                 u_ref, dqT_ref, dk_ref, dvT_ref, rq_ref, rkT_ref, rv_ref, rg_ref,
                 gqT_ref, gk_ref, gvT_ref):
    x = h_ref[...]
    ms = jnp.mean(x * x, axis=-1, keepdims=True)
    u = (x * lax.rsqrt(ms + NORM_EPS) * an_ref[...]).astype(BF16)
    u_ref[...] = u

    def proj(c0, n):
        return jnp.dot(u, w_ref[:, c0:c0 + n], preferred_element_type=F32)

    gmat = g_ref[...]
    lo = lax.broadcasted_iota(jnp.int32, (1, LANES), 1) < HEAD64
    first_row = lax.broadcasted_iota(jnp.int32, (ONES_ROWS, TM), 0) == 0
    ones_rows = jnp.where(first_row, 1.0, 0.0).astype(BF16)

    def qk_norm(p, gain):
        out = []
        for c0 in range(0, p.shape[1], 2 * LANES):
            y = p[:, c0:c0 + 2 * LANES]
            w = y.shape[1]
            msq = jnp.dot((y * y).astype(BF16), gmat[0:w, 0:w], preferred_element_type=F32)
            y = y * lax.rsqrt(msq + NORM_EPS)
            out += [y[:, i:i + LANES] * gain for i in range(0, w, LANES)]
        return out

    def rope(y, c, a, b, d):
        return y * c + pltpu.roll(y, LANES - d, 1) * a + pltpu.roll(y, d, 1) * b

    def chunks(p):
        return [p[:, i:i + LANES] for i in range(0, p.shape[1], LANES)]

    qscale = HEAD64 ** -0.5 * LOG2E
    kscale = LANES ** -0.5
    cd, ad, bd = cd_ref[...], ad_ref[...], bd_ref[...]
    cg, ag, bg = cg_ref[...], ag_ref[...], bg_ref[...]
    cr, sr = cr_ref[...], sr_ref[...]

    def diff_q(p):
        for hd, y in enumerate(qk_norm(p, nrm_ref[0:1, :])):
            y = rope(y, cd, ad, bd, 8) * qscale
            dqT_ref[hd, :, 0:TM] = jnp.where(lo, y, 0.0).T.astype(BF16)
            dqT_ref[hd, :, TM:2 * TM] = jnp.where(lo, 0.0, y).T.astype(BF16)

    def diff_k(p):
        for hd, y in enumerate(qk_norm(p, nrm_ref[1:2, :])):
            dk_ref[:, hd * LANES:(hd + 1) * LANES] = rope(y, cd, ad, bd, 8).astype(BF16)

    def diff_v(p):
        for hd, y in enumerate(chunks(p)):
            dvT_ref[hd, 0:LANES, :] = y.T.astype(BF16)
            dvT_ref[hd, LANES:VROWS, :] = ones_rows

    def ret_q(p):
        for hd, y in enumerate(chunks(p)):
            rq_ref[:, hd * LANES:(hd + 1) * LANES] = (y * cr + pltpu.roll(y, 64, 1) * sr).astype(BF16)

    def ret_k(p):
        for hd, y in enumerate(chunks(p)):
            rkT_ref[hd] = ((y * cr + pltpu.roll(y, 64, 1) * sr) * kscale).T.astype(BF16)

    def ret_v(p):
        rv_ref[...] = p.astype(BF16)

    def ret_g(p):
        rg_ref[...] = p.astype(BF16)

    def gqa_q(p):
        for c, y in enumerate(qk_norm(p, nrm_ref[2:3, :])):
            y = rope(y, cg, ag, bg, 16) * qscale
            ysw = pltpu.roll(y, 64, 1)
            if c < 2:
                z_even, z_odd = jnp.where(lo, y, 0.0), jnp.where(lo, ysw, 0.0)
            else:
                z_even, z_odd = jnp.where(lo, 0.0, ysw), jnp.where(lo, 0.0, y)
            for hh, z in ((2 * c, z_even), (2 * c + 1, z_odd)):
                zt = z.T.astype(BF16)
                for qb in range(TM // LANES):
                    gqT_ref[qb, :, hh * LANES:(hh + 1) * LANES] = zt[:, qb * LANES:(qb + 1) * LANES]

    def gqa_kv(p):
        (y,) = qk_norm(p[:, 0:LANES], nrm_ref[3:4, :])
        gk_ref[...] = rope(y, cg, ag, bg, 16).astype(BF16)
        gvT_ref[0, 0:LANES, :] = p[:, LANES:2 * LANES].T.astype(BF16)
        gvT_ref[0, LANES:VROWS, :] = ones_rows

    sections = [(3584, 512, gqa_q), (0, 512, diff_q), (2048, 512, ret_k), (1024, 512, diff_v),
                (512, 512, diff_k), (4096, 256, gqa_kv), (1536, 512, ret_q), (2560, 512, ret_v),
                (3072, 512, ret_g)]
    p_next = proj(*sections[0][:2])
    for i, (_, _, epilogue) in enumerate(sections):
        p_cur = p_next
        if i + 1 < len(sections):
            p_next = proj(*sections[i + 1][:2])
        epilogue(p_cur)


def _proj_call(h, attn_norm, w_attn, gmat, nrm, tabs):
    T = h.shape[0]
    S = tabs[0].shape[0]
    nt = T // TM
    tab_spec = pl.BlockSpec((TM, LANES), lambda i: (i % (S // TM), 0))
    row = lambda w: pl.BlockSpec((TM, w), lambda i: (i, 0))
    out_shape = (
        jax.ShapeDtypeStruct((T, D_MODEL), BF16),
        jax.ShapeDtypeStruct((nt, 4, LANES, 2 * TM), BF16),
        jax.ShapeDtypeStruct((T, 512), BF16),
        jax.ShapeDtypeStruct((nt, 4, VROWS, TM), BF16),
        jax.ShapeDtypeStruct((T, 512), BF16),
        jax.ShapeDtypeStruct((nt, 4, LANES, TM), BF16),
        jax.ShapeDtypeStruct((T, 512), BF16),
        jax.ShapeDtypeStruct((T, 512), BF16),
        jax.ShapeDtypeStruct((T // LANES, LANES, NQ), BF16),
        jax.ShapeDtypeStruct((T, LANES), BF16),
        jax.ShapeDtypeStruct((nt, 1, VROWS, TM), BF16),
    )
    out_specs = (
        row(D_MODEL),
        pl.BlockSpec((None, 4, LANES, 2 * TM), lambda i: (i, 0, 0, 0)),
        row(512),
        pl.BlockSpec((None, 4, VROWS, TM), lambda i: (i, 0, 0, 0)),
        row(512), pl.BlockSpec((None, 4, LANES, TM), lambda i: (i, 0, 0, 0)), row(512), row(512),
        pl.BlockSpec((TM // LANES, LANES, NQ), lambda i: (i, 0, 0)),
        row(LANES),
        pl.BlockSpec((None, 1, VROWS, TM), lambda i: (i, 0, 0, 0)),
    )
    in_specs = [row(D_MODEL), _resident((1, D_MODEL)), _resident(w_attn.shape),
                _resident(gmat.shape), _resident(nrm.shape)] + [tab_spec] * 8
    return pl.pallas_call(
        _proj_kernel, grid=(nt,), in_specs=in_specs, out_specs=out_specs,
        out_shape=out_shape, compiler_params=_cparams(1), name="proj",
    )(h, attn_norm, w_attn, gmat, nrm, *tabs)


def _attn_core(fast_ref, qT_ref, k_ref, vT_ref, m_ref, acc_ref):
    acc_ref[...] = jnp.zeros(acc_ref.shape, F32)
    n_steps = k_ref.shape[0] // TK
    n_groups = vT_ref.shape[1]
    gw = NQ // n_groups

    def kv(j):
        k = k_ref[pl.ds(pl.multiple_of(j * TK, TK), TK), :]
        per = TK // TM
        vT = jnp.concatenate([vT_ref[j * per + i] for i in range(per)], axis=-1)
        return k, vT

    @pl.when(fast_ref[0] == 1)
    def _():
        def body(jj, carry):
            pv = [None] * n_groups
            lsum = None
            for u in range(KV_UNROLL):
                k, vT = kv(jj * KV_UNROLL + u)
                s = jnp.dot(k, qT_ref[...], preferred_element_type=F32)
                pf = jnp.exp2(s)
                t = jnp.sum(pf, axis=0, keepdims=True)
                lsum = t if lsum is None else lsum + t
                p = pf.astype(BF16)
                for g in range(n_groups):
                    t = jnp.dot(vT[g, 0:LANES], p[:, g * gw:(g + 1) * gw], preferred_element_type=F32)
                    pv[g] = t if pv[g] is None else pv[g] + t
            for g in range(n_groups):
                acc_ref[0:LANES, g * gw:(g + 1) * gw] += pv[g]
            acc_ref[LANES:LANES + 1, :] += lsum
            return carry

        lax.fori_loop(0, n_steps // KV_UNROLL, body, 0)

    @pl.when(fast_ref[0] == 0)
    def _():
        m_ref[...] = jnp.full(m_ref.shape, -1e30, F32)

        def body(j, carry):
            k, vT = kv(j)
            for c in range(NQ // NC):
                cs = slice(c * NC, (c + 1) * NC)
                s = jnp.dot(k, qT_ref[:, cs], preferred_element_type=F32)
                m_old = m_ref[:, cs]
                m_new = jnp.maximum(m_old, jnp.max(s, axis=0, keepdims=True))
                alpha = jnp.exp2(m_old - m_new)
                p = jnp.exp2(s - m_new).astype(BF16)
                pv = jnp.dot(vT[(c * NC) // gw], p, preferred_element_type=F32)
                acc_ref[:, cs] = acc_ref[:, cs] * alpha + pv
                m_ref[:, cs] = m_new
            return carry

        lax.fori_loop(0, n_steps, body, 0)


def _diff_kernel(fast_ref, lam_ref, sub_ref, qT_ref, k_ref, vT_ref, o_ref, m_ref, acc_ref, *,
                 lambda_init):
    _attn_core(fast_ref, qT_ref, k_ref, vT_ref, m_ref, acc_ref)

    lv = lam_ref[...]
    lam = (jnp.exp(jnp.sum(lv[0:1] * lv[1:2], axis=-1, keepdims=True))
           - jnp.exp(jnp.sum(lv[2:3] * lv[3:4], axis=-1, keepdims=True)) + lambda_init)
    tq = NQ // 2
    o1 = acc_ref[0:LANES, 0:tq] / acc_ref[LANES:LANES + 1, 0:tq]
    o2 = acc_ref[0:LANES, tq:NQ] / acc_ref[LANES:LANES + 1, tq:NQ]
    d = o1 - lam * o2
    ms = jnp.mean(d * d, axis=0, keepdims=True)
    dn = d * lax.rsqrt(ms + 1e-5)
    o_ref[...] = (dn.T * (sub_ref[...] * (1.0 - lambda_init))).astype(o_ref.dtype)


_SMEM_SPEC = pl.BlockSpec(memory_space=pltpu.SMEM)

def _attn_scratch(rows):
    return [pltpu.VMEM((1, NQ), F32), pltpu.VMEM((rows, NQ), F32)]


def _diff_call(fast, lamv, subln, dqT, dk, dvT, B, S, lambda_init):
    T = dk.shape[0]
    tq = NQ // 2
    nq = S // tq
    kern = functools.partial(_diff_kernel, lambda_init=lambda_init)
    return pl.pallas_call(
        kern, grid=(B, DIFF_HEADS, nq),
        in_specs=[
            _SMEM_SPEC, _resident(lamv.shape), _resident(subln.shape),
            pl.BlockSpec((None, None, LANES, NQ), lambda b, h, i: (b * nq + i, h, 0, 0)),
            pl.BlockSpec((S, LANES), lambda b, h, i: (b, h)),
            pl.BlockSpec((S // TM, 1, VROWS, TM), lambda b, h, i: (b, h, 0, 0)),
        ],
        out_specs=pl.BlockSpec((tq, LANES), lambda b, h, i: (b * nq + i, h)),
        out_shape=jax.ShapeDtypeStruct((T, 512), BF16),
        scratch_shapes=_attn_scratch(VROWS),
        compiler_params=_cparams(3), name="diff_attn",
    )(fast, lamv, subln, dqT, dk, dvT)


def _gqa_kernel(fast_ref, qT_ref, k_ref, vT_ref, o_ref, m_ref, acc_ref):
    _attn_core(fast_ref, qT_ref, k_ref, vT_ref, m_ref, acc_ref)
    for c in range(GQA_Q_HEADS // 2):
        g = (2 * c) // GQA_GROUP
        rows = slice(g * HEAD64, (g + 1) * HEAD64)
        parts = []
        for hh in (2 * c, 2 * c + 1):
            cols = slice(hh * LANES, (hh + 1) * LANES)
            parts.append(acc_ref[rows, cols] / acc_ref[LANES:LANES + 1, cols])
        o_ref[:, c * LANES:(c + 1) * LANES] = jnp.concatenate(parts, axis=0).T.astype(o_ref.dtype)


def _gqa_call(fast, gqT, gk, gvT, B, S):
    T = gk.shape[0]
    tq = NQ // GQA_Q_HEADS
    nq = S // tq
    return pl.pallas_call(
        _gqa_kernel, grid=(B, nq),
        in_specs=[
            _SMEM_SPEC,
            pl.BlockSpec((None, LANES, NQ), lambda b, i: (b * nq + i, 0, 0)),
            pl.BlockSpec((S, LANES), lambda b, i: (b, 0)),
            pl.BlockSpec((S // TM, 1, VROWS, TM), lambda b, i: (b, 0, 0, 0)),
        ],
        out_specs=pl.BlockSpec((tq, 512), lambda b, i: (b * nq + i, 0)),
        out_shape=jax.ShapeDtypeStruct((T, 512), BF16),
        scratch_shapes=_attn_scratch(VROWS),
        compiler_params=_cparams(2), name="gqa_attn",
    )(fast, gqT, gk, gvT)


def _ret_kernel(dec_ref, gain_ref, q_ref, kT_ref, v_ref, g_ref, o_ref, sf_ref, sb_ref):
    C = RET_CHUNK
    hd = pl.program_id(1)
    lf = -jnp.exp(jnp.full((C, C), dec_ref[0, hd], F32))
    lb = -jnp.exp(jnp.full((C, C), dec_ref[1, hd], F32))
    ii = lax.broadcasted_iota(jnp.int32, (C, C), 0).astype(F32)
    jj = lax.broadcasted_iota(jnp.int32, (C, C), 1).astype(F32)
    diff = ii - jj
    decay = jnp.where(diff >= 0, jnp.exp(jnp.maximum(diff, 0.0) * lf),
                      jnp.exp(jnp.maximum(-diff, 0.0) * lb))
    xi_f, zeta_f = jnp.exp((ii + 1.0) * lf), jnp.exp((C - 1.0 - ii) * lf)
    xi_b, zeta_b = jnp.exp((C - ii) * lb), jnp.exp(ii * lb)
    dec_f, dec_b = jnp.exp(C * lf), jnp.exp(C * lb)
    n_tiles = kT_ref.shape[0]
    per_tile = TM // C

    def rows(t, c):
        return pl.ds(pl.multiple_of(t * TM + c * C, C), C)

    def chunk_kv(t, c, zeta):
        vz = (v_ref[rows(t, c), :].astype(F32) * zeta).astype(BF16)
        return jnp.dot(kT_ref[t, :, c * C:(c + 1) * C], vz, preferred_element_type=F32)

    def scans(i, states):
        st_f, st_b = states
        tf, tb = i, n_tiles - 1 - i
        kv_f = [chunk_kv(tf, c, zeta_f) for c in range(per_tile)]
        kv_b = [chunk_kv(tb, c, zeta_b) for c in range(per_tile)]
        for c in range(per_tile):
            sf_ref[tf * per_tile + c] = st_f.astype(BF16)
            st_f = st_f * dec_f + kv_f[c]
            cb = per_tile - 1 - c
            sb_ref[tb * per_tile + cb] = st_b.astype(BF16)
            st_b = st_b * dec_b + kv_b[cb]
        return st_f, st_b

    zero = jnp.zeros((C, C), F32)
    lax.fori_loop(0, n_tiles, scans, (zero, zero))

    def outputs(tt, carry):
        chunks = [(tt * RET_OUT_TILES + i, c) for i in range(RET_OUT_TILES) for c in range(per_tile)]
        rs = [rows(t, c) for t, c in chunks]
        qs = [q_ref[r, :] for r in rs]
        a = [jnp.dot(q, kT_ref[t, :, c * C:(c + 1) * C], preferred_element_type=F32)
             for q, (t, c) in zip(qs, chunks)]
        cross = [jnp.dot(q, sf_ref[t * per_tile + c], preferred_element_type=F32) * xi_f
                 + jnp.dot(q, sb_ref[t * per_tile + c], preferred_element_type=F32) * xi_b
                 for q, (t, c) in zip(qs, chunks)]
        a = [(x * decay).astype(BF16) for x in a]
        o = [jnp.dot(x, v_ref[r, :], preferred_element_type=F32) + y
             for x, r, y in zip(a, rs, cross)]
        mu = [jnp.mean(x, axis=-1, keepdims=True) for x in o]
        oc = [x - m for x, m in zip(o, mu)]
        var = [jnp.mean(x * x, axis=-1, keepdims=True) for x in oc]
        for x, s2, r in zip(oc, var, rs):
            y = x * lax.rsqrt(s2 + 1e-5) * gain_ref[...]
            g = g_ref[r, :].astype(F32)
            o_ref[r, :] = (g / (1.0 + jnp.exp(-g)) * y).astype(o_ref.dtype)
        return carry

    lax.fori_loop(0, n_tiles // RET_OUT_TILES, outputs, 0)


def _ret_call(dec, gain, rq, rkT, rv, rg, B, S):
    T = rq.shape[0]
    blk = pl.BlockSpec((S, LANES), lambda b, h: (b, h))
    n_chunks = S // RET_CHUNK
    return pl.pallas_call(
        _ret_kernel, grid=(B, RET_HEADS),
        in_specs=[pl.BlockSpec(memory_space=pltpu.SMEM),
                  pl.BlockSpec((1, LANES), lambda b, h: (0, h)), blk,
                  pl.BlockSpec((S // TM, None, LANES, TM), lambda b, h: (b, h, 0, 0)), blk, blk],
        out_specs=blk,
        out_shape=jax.ShapeDtypeStruct((T, 512), BF16),
        scratch_shapes=[pltpu.VMEM((n_chunks, RET_CHUNK, LANES), BF16),
                        pltpu.VMEM((n_chunks, RET_CHUNK, LANES), BF16)],
        compiler_params=_cparams(2), name="retention",
    )(dec, gain, rq, rkT, rv, rg)


def _merge_kernel(h_ref, u_ref, a_ref, r_ref, c_ref, wg_ref, wb_ref, wo_ref, mn_ref,
                  hn_ref, u2_ref):
    u = u_ref[...]
    merged = None
    for n, br in enumerate((a_ref, r_ref, c_ref)):
        logits = jnp.dot(u, wg_ref[:, n * D_MODEL:(n + 1) * D_MODEL], preferred_element_type=F32)
        gate = 1.0 / (1.0 + jnp.exp(-logits))
        term = gate * jnp.dot(br[...], wb_ref[n], preferred_element_type=F32)
        merged = term if merged is None else merged + term
    hn = h_ref[...] + jnp.dot(merged.astype(BF16), wo_ref[...], preferred_element_type=F32)
    hn_ref[...] = hn
    ms = jnp.mean(hn * hn, axis=-1, keepdims=True)
    u2_ref[...] = (hn * lax.rsqrt(ms + NORM_EPS) * mn_ref[...]).astype(BF16)


def _merge_call(h, u, a, r, c, wg, wb, wo, mlp_norm):
    T = h.shape[0]
    row = lambda w: pl.BlockSpec((TM, w), lambda i: (i, 0))
    return pl.pallas_call(
        _merge_kernel, grid=(T // TM,),
        in_specs=[row(D_MODEL), row(D_MODEL), row(512), row(512), row(512),
                  _resident(wg.shape), _resident(wb.shape), _resident(wo.shape),
                  _resident((1, D_MODEL))],
        out_specs=(row(D_MODEL), row(D_MODEL)),
        out_shape=(jax.ShapeDtypeStruct((T, D_MODEL), F32),
                   jax.ShapeDtypeStruct((T, D_MODEL), BF16)),
        compiler_params=_cparams(1), name="merge",
    )(h, u, a, r, c, wg, wb, wo, mlp_norm)


def _mlp_kernel(h_ref, u_ref, w1_ref, w2_ref, o_ref):
    u = u_ref[...]
    acc = h_ref[...]
    for c in range(D_FF // D_MODEL):
        cs = slice(c * D_MODEL, (c + 1) * D_MODEL)
        m = jnp.maximum(jnp.dot(u, w1_ref[:, cs], preferred_element_type=F32), 0.0)
        acc = acc + jnp.dot((m * m).astype(BF16), w2_ref[cs, :], preferred_element_type=F32)
    o_ref[...] = acc


def _mlp_call(h, u2, w1, w2):
    T = h.shape[0]
    row = pl.BlockSpec((TM, D_MODEL), lambda i: (i, 0))
    return pl.pallas_call(
        _mlp_kernel, grid=(T // TM,),
        in_specs=[row, row, _resident(w1.shape), _resident(w2.shape)],
        out_specs=row,
        out_shape=jax.ShapeDtypeStruct((T, D_MODEL), F32),
        compiler_params=_cparams(1), name="mlp",
    )(h, u2, w1, w2)


def _rope_tables(S):
    pos = jnp.arange(S, dtype=F32)
    lane = jnp.arange(LANES)
    l64 = lane % 64

    inv = 500000.0 ** (-jnp.arange(0, 16, 2, dtype=F32) / 16)
    ang = pos[:, None] * inv[None, :]
    cos, sin = jnp.cos(ang)[:, l64 % 8], jnp.sin(ang)[:, l64 % 8]
    cd = jnp.where(l64 < 16, cos, 1.0)
    ad = jnp.where(l64 < 8, -sin, 0.0)
    bd = jnp.where((l64 >= 8) & (l64 < 16), sin, 0.0)

    inv = 10000.0 ** (-jnp.arange(0, 32, 2, dtype=F32) / 32)
    row_pos = jnp.floor(pos / GRID_W)
    col_pos = pos - row_pos * GRID_W
    p2 = jnp.where((l64 < 32)[None, :], row_pos[:, None], col_pos[:, None])
    ang = p2 * inv[l64 % 16][None, :]
    cg = jnp.cos(ang)
    ag = jnp.where(l64 % 32 < 16, -jnp.sin(ang), 0.0)
    bg = jnp.where(l64 % 32 >= 16, jnp.sin(ang), 0.0)

    inv = 10000.0 ** (-jnp.arange(0, 128, 2, dtype=F32) / 128)
    ang = pos[:, None] * inv[l64][None, :]
    cr = jnp.cos(ang)
    sr = jnp.where(lane < 64, -jnp.sin(ang), jnp.sin(ang))
    return (cd, ad, bd, cg, ag, bg, cr, sr)


def _fast_flag(q_gain, k_gain):
    bound = (HEAD64 ** 0.5 * LOG2E * 1.02) * jnp.max(jnp.abs(q_gain)) * jnp.max(jnp.abs(k_gain))
    return (bound <= FAST_SOFTMAX_BOUND).astype(jnp.int32).reshape(1)


def kernel(x, attn_norm, w_in, diff_q_norm, diff_k_norm, diff_lam_q1, diff_lam_k1, diff_lam_q2, diff_lam_k2, diff_subln, ret_decay_fwd, ret_decay_bwd, ret_group_norm, gqa_q_norm, gqa_k_norm, w_branch, w_out, mlp_norm, w_mlp_in, w_mlp_out):
    B, S, D = x.shape
    T = B * S
    tabs = _rope_tables(S)
    blk = jnp.arange(2 * LANES) // HEAD64
    gmat = jnp.where(blk[:, None] == blk[None, :], 1.0 / HEAD64, 0.0).astype(BF16)
    tile2 = lambda v: jnp.concatenate([v, v]).astype(F32)

    h = x.reshape(T, D)
    for l in range(DEPTH):
        lambda_init = 0.8 - 0.6 * math.exp(-0.3 * l)
        w_attn = w_in[l, :, :ATTN_COLS].astype(BF16)
        w_gate = w_in[l, :, ATTN_COLS:].astype(BF16)
        nrm = jnp.stack([tile2(diff_q_norm[l]), tile2(diff_k_norm[l]),
                         tile2(gqa_q_norm[l]), tile2(gqa_k_norm[l])])
        (u, dqT, dk, dvT, rq, rk, rv, rg, gqT, gk, gvT) = _proj_call(
            h, attn_norm[l].reshape(1, D).astype(F32), w_attn, gmat, nrm, tabs)

        lamv = jnp.stack([diff_lam_q1[l], diff_lam_k1[l], diff_lam_q2[l], diff_lam_k2[l]]).astype(F32)
        a = _diff_call(_fast_flag(diff_q_norm[l], diff_k_norm[l]), lamv,
                       diff_subln[l].reshape(1, LANES).astype(F32), dqT, dk, dvT, B, S, lambda_init)
        dec = jnp.stack([ret_decay_fwd[l], ret_decay_bwd[l]]).astype(F32)
        r = _ret_call(dec, ret_group_norm[l].reshape(1, 512).astype(F32), rq, rk, rv, rg, B, S)
        c = _gqa_call(_fast_flag(gqa_q_norm[l], gqa_k_norm[l]), gqT, gk, gvT, B, S)

        h, u2 = _merge_call(h, u, a, r, c, w_gate, w_branch[l].astype(BF16),
                            w_out[l].astype(BF16), mlp_norm[l].reshape(1, D).astype(F32))
        h = _mlp_call(h, u2, w_mlp_in[l].astype(BF16), w_mlp_out[l].astype(BF16))
    return h.reshape(B, S, D)
```

```python
import functools
import math

import jax
import jax.numpy as jnp
from jax import lax
from jax.experimental import pallas as pl
from jax.experimental.pallas import tpu as pltpu

F32 = jnp.float32
BF16 = jnp.bfloat16

D_MODEL = 1024
DEPTH = 2
GRID_W = 64
NORM_EPS = 1e-6
HEAD64 = 64
DIFF_HEADS = 4
RET_HEADS = 4
RET_CHUNK = 128
GQA_Q_HEADS = 8
GQA_GROUP = 4
N_BRANCHES = 3
BRANCH_WIDTH = 512
D_FF = 4 * D_MODEL
ATTN_COLS = 4352
LANES = 128
MXU_DIM = 256

TM = 512
TK = 2048
NQ = 1024
NC = 256
KV_UNROLL = 2
RET_OUT_TILES = 4
ONES_ROWS = 16
VROWS = LANES + ONES_ROWS
FAST_SOFTMAX_BOUND = 64.0
VMEM_LIMIT = 56 * 1024 * 1024
LOG2E = 1.4426950408889634


def _cparams(n_axes):
    return pltpu.CompilerParams(dimension_semantics=("arbitrary",) * n_axes,
                                vmem_limit_bytes=VMEM_LIMIT)


def _resident(shape):
    nd = len(shape)
    return pl.BlockSpec(shape, lambda *_: (0,) * nd, pipeline_mode=pl.Buffered(1))


def _proj_kernel(h_ref, an_ref, w_ref, g_ref, nrm_ref,
                 cd_ref, ad_ref, bd_ref, cg_ref, ag_ref, bg_ref, cr_ref, sr_ref,
                 u_ref, dqT_ref, dk_ref, dvT_ref, rq_ref, rkT_ref, rv_ref, rg_ref,
                 gqT_ref, gk_ref, gvT_ref, gvT2_ref):
    x = h_ref[...]
    ms = jnp.mean(x * x, axis=-1, keepdims=True)
    u = (x * lax.rsqrt(ms + NORM_EPS) * an_ref[...]).astype(BF16)
    u_ref[...] = u

    def proj(c0, n):
        return jnp.dot(u, w_ref[:, c0:c0 + n], preferred_element_type=F32)

    gmat = g_ref[...]
    lo = lax.broadcasted_iota(jnp.int32, (1, LANES), 1) < HEAD64
    first_row = lax.broadcasted_iota(jnp.int32, (ONES_ROWS, TM), 0) == 0
    ones_rows = jnp.where(first_row, 1.0, 0.0).astype(BF16)

    def qk_norm(p, gain):
        out = []
        for c0 in range(0, p.shape[1], 2 * LANES):
            y = p[:, c0:c0 + 2 * LANES]
            w = y.shape[1]
            msq = jnp.dot((y * y).astype(BF16), gmat[0:w, 0:w], preferred_element_type=F32)
            y = y * lax.rsqrt(msq + NORM_EPS)
            out += [y[:, i:i + LANES] * gain for i in range(0, w, LANES)]
        return out

    def rope(y, c, a, b, d):
        return y * c + pltpu.roll(y, LANES - d, 1) * a + pltpu.roll(y, d, 1) * b

    def chunks(p):
        return [p[:, i:i + LANES] for i in range(0, p.shape[1], LANES)]

    qscale = HEAD64 ** -0.5 * LOG2E
    kscale = LANES ** -0.5
    cd, ad, bd = cd_ref[...], ad_ref[...], bd_ref[...]
    cg, ag, bg = cg_ref[...], ag_ref[...], bg_ref[...]
    cr, sr = cr_ref[...], sr_ref[...]

    def diff_q(p):
        for hd, y in enumerate(qk_norm(p, nrm_ref[0:1, :])):
            y = rope(y, cd, ad, bd, 8) * qscale
            dqT_ref[hd, :, 0:TM] = jnp.where(lo, y, 0.0).T.astype(BF16)
            dqT_ref[hd, :, TM:2 * TM] = jnp.where(lo, 0.0, y).T.astype(BF16)

    def diff_k(p):
        for hd, y in enumerate(qk_norm(p, nrm_ref[1:2, :])):
            dk_ref[:, hd * LANES:(hd + 1) * LANES] = rope(y, cd, ad, bd, 8).astype(BF16)

    def diff_v(p):
        for hd, y in enumerate(chunks(p)):
            dvT_ref[hd, 0:LANES, :] = y.T.astype(BF16)
            dvT_ref[hd, LANES:VROWS, :] = ones_rows

    def ret_q(p):
        for hd, y in enumerate(chunks(p)):
            rq_ref[:, hd * LANES:(hd + 1) * LANES] = (y * cr + pltpu.roll(y, 64, 1) * sr).astype(BF16)

    def ret_k(p):
        for hd, y in enumerate(chunks(p)):
            rkT_ref[hd] = ((y * cr + pltpu.roll(y, 64, 1) * sr) * kscale).T.astype(BF16)

    def ret_v(p):
        rv_ref[...] = p.astype(BF16)

    def ret_g(p):
        rg_ref[...] = p.astype(BF16)

    def gqa_q(p):
        for c, y in enumerate(qk_norm(p, nrm_ref[2:3, :])):
            y = rope(y, cg, ag, bg, 16) * qscale
            ysw = pltpu.roll(y, 64, 1)
            if c < 2:
                z_even, z_odd = jnp.where(lo, y, 0.0), jnp.where(lo, ysw, 0.0)
            else:
                z_even, z_odd = jnp.where(lo, 0.0, ysw), jnp.where(lo, 0.0, y)
            for hh, z in ((2 * c, z_even), (2 * c + 1, z_odd)):
                zt = z.T.astype(BF16)
                for qb in range(TM // LANES):
                    gqT_ref[qb, 0:LANES, hh * LANES:(hh + 1) * LANES] = zt[:, qb * LANES:(qb + 1) * LANES]
        gqT_ref[:, LANES:MXU_DIM, :] = jnp.zeros((TM // LANES, MXU_DIM - LANES, NQ), BF16)

    def gqa_kv(p):
        (y,) = qk_norm(p[:, 0:LANES], nrm_ref[3:4, :])
        gk_ref[:, 0:LANES] = rope(y, cg, ag, bg, 16).astype(BF16)
        gk_ref[:, LANES:MXU_DIM] = jnp.zeros((TM, MXU_DIM - LANES), BF16)
        vt = p[:, LANES:2 * LANES].T.astype(BF16)
        gvT_ref[0, 0:LANES, :] = vt
        gvT_ref[0, LANES:VROWS, :] = ones_rows
        for g in range(2):
            gvT2_ref[g] = vt[g * HEAD64:(g + 1) * HEAD64]

    sections = [(3584, 512, gqa_q), (0, 512, diff_q), (2048, 512, ret_k), (1024, 512, diff_v),
                (512, 512, diff_k), (4096, 256, gqa_kv), (1536, 512, ret_q), (2560, 512, ret_v),
                (3072, 512, ret_g)]
    p_next = proj(*sections[0][:2])
    for i, (_, _, epilogue) in enumerate(sections):
        p_cur = p_next
        if i + 1 < len(sections):
            p_next = proj(*sections[i + 1][:2])
        epilogue(p_cur)


def _proj_call(h, attn_norm, w_attn, gmat, nrm, tabs):
    T = h.shape[0]
    S = tabs[0].shape[0]
    nt = T // TM
    tab_spec = pl.BlockSpec((TM, LANES), lambda i: (i % (S // TM), 0))
    row = lambda w: pl.BlockSpec((TM, w), lambda i: (i, 0))
    out_shape = (
        jax.ShapeDtypeStruct((T, D_MODEL), BF16),
        jax.ShapeDtypeStruct((nt, 4, LANES, 2 * TM), BF16),
        jax.ShapeDtypeStruct((T, 512), BF16),
        jax.ShapeDtypeStruct((nt, 4, VROWS, TM), BF16),
        jax.ShapeDtypeStruct((T, 512), BF16),
        jax.ShapeDtypeStruct((nt, 4, LANES, TM), BF16),
        jax.ShapeDtypeStruct((T, 512), BF16),
        jax.ShapeDtypeStruct((T, 512), BF16),
        jax.ShapeDtypeStruct((T // LANES, MXU_DIM, NQ), BF16),
        jax.ShapeDtypeStruct((T, MXU_DIM), BF16),
        jax.ShapeDtypeStruct((nt, 1, VROWS, TM), BF16),
        jax.ShapeDtypeStruct((nt, 2, HEAD64, TM), BF16),
    )
    out_specs = (
        row(D_MODEL),
        pl.BlockSpec((None, 4, LANES, 2 * TM), lambda i: (i, 0, 0, 0)),
        row(512),
        pl.BlockSpec((None, 4, VROWS, TM), lambda i: (i, 0, 0, 0)),
        row(512), pl.BlockSpec((None, 4, LANES, TM), lambda i: (i, 0, 0, 0)), row(512), row(512),
        pl.BlockSpec((TM // LANES, MXU_DIM, NQ), lambda i: (i, 0, 0)),
        row(MXU_DIM),
        pl.BlockSpec((None, 1, VROWS, TM), lambda i: (i, 0, 0, 0)),
        pl.BlockSpec((None, 2, HEAD64, TM), lambda i: (i, 0, 0, 0)),
    )
    in_specs = [row(D_MODEL), _resident((1, D_MODEL)), _resident(w_attn.shape),
                _resident(gmat.shape), _resident(nrm.shape)] + [tab_spec] * 8
    return pl.pallas_call(
        _proj_kernel, grid=(nt,), in_specs=in_specs, out_specs=out_specs,
        out_shape=out_shape, compiler_params=_cparams(1), name="proj",
    )(h, attn_norm, w_attn, gmat, nrm, *tabs)


def _attn_core(fast_ref, qT_ref, k_ref, vT_ref, m_ref, acc_ref):
    acc_ref[...] = jnp.zeros(acc_ref.shape, F32)
    n_steps = k_ref.shape[0] // TK
    n_groups = vT_ref.shape[1]
    gw = NQ // n_groups

    def kv(j):
        k = k_ref[pl.ds(pl.multiple_of(j * TK, TK), TK), :]
        per = TK // TM
        vT = jnp.concatenate([vT_ref[j * per + i] for i in range(per)], axis=-1)
        return k, vT

    @pl.when(fast_ref[0] == 1)
    def _():
        def body(jj, carry):
            pv = [None] * n_groups
            lsum = None
            for u in range(KV_UNROLL):
                k, vT = kv(jj * KV_UNROLL + u)
                s = jnp.dot(k, qT_ref[...], preferred_element_type=F32)
                pf = jnp.exp2(s)
                t = jnp.sum(pf, axis=0, keepdims=True)
                lsum = t if lsum is None else lsum + t
                p = pf.astype(BF16)
                for g in range(n_groups):
                    t = jnp.dot(vT[g, 0:LANES], p[:, g * gw:(g + 1) * gw], preferred_element_type=F32)
                    pv[g] = t if pv[g] is None else pv[g] + t
            for g in range(n_groups):
                acc_ref[0:LANES, g * gw:(g + 1) * gw] += pv[g]
            acc_ref[LANES:LANES + 1, :] += lsum
            return carry

        lax.fori_loop(0, n_steps // KV_UNROLL, body, 0)

    @pl.when(fast_ref[0] == 0)
    def _():
        m_ref[...] = jnp.full(m_ref.shape, -1e30, F32)

        def body(j, carry):
            k, vT = kv(j)
            for c in range(NQ // NC):
                cs = slice(c * NC, (c + 1) * NC)
                s = jnp.dot(k, qT_ref[:, cs], preferred_element_type=F32)
                m_old = m_ref[:, cs]
                m_new = jnp.maximum(m_old, jnp.max(s, axis=0, keepdims=True))
                alpha = jnp.exp2(m_old - m_new)
                p = jnp.exp2(s - m_new).astype(BF16)
                pv = jnp.dot(vT[(c * NC) // gw], p, preferred_element_type=F32)
                acc_ref[:, cs] = acc_ref[:, cs] * alpha + pv
                m_ref[:, cs] = m_new
            return carry

        lax.fori_loop(0, n_steps, body, 0)


def _diff_kernel(fast_ref, lam_ref, sub_ref, qT_ref, k_ref, vT_ref, o_ref, m_ref, acc_ref, *,
                 lambda_init):
    _attn_core(fast_ref, qT_ref, k_ref, vT_ref, m_ref, acc_ref)

    lv = lam_ref[...]
    lam = (jnp.exp(jnp.sum(lv[0:1] * lv[1:2], axis=-1, keepdims=True))
           - jnp.exp(jnp.sum(lv[2:3] * lv[3:4], axis=-1, keepdims=True)) + lambda_init)
    tq = NQ // 2
    o1 = acc_ref[0:LANES, 0:tq] / acc_ref[LANES:LANES + 1, 0:tq]
    o2 = acc_ref[0:LANES, tq:NQ] / acc_ref[LANES:LANES + 1, tq:NQ]
    d = o1 - lam * o2
    ms = jnp.mean(d * d, axis=0, keepdims=True)
    dn = d * lax.rsqrt(ms + 1e-5)
    o_ref[...] = (dn.T * (sub_ref[...] * (1.0 - lambda_init))).astype(o_ref.dtype)


_SMEM_SPEC = pl.BlockSpec(memory_space=pltpu.SMEM)

def _attn_scratch(rows):
    return [pltpu.VMEM((1, NQ), F32), pltpu.VMEM((rows, NQ), F32)]


def _diff_call(fast, lamv, subln, dqT, dk, dvT, B, S, lambda_init):
    T = dk.shape[0]
    tq = NQ // 2
    nq = S // tq
    kern = functools.partial(_diff_kernel, lambda_init=lambda_init)
    return pl.pallas_call(
        kern, grid=(B, DIFF_HEADS, nq),
        in_specs=[
            _SMEM_SPEC, _resident(lamv.shape), _resident(subln.shape),
            pl.BlockSpec((None, None, LANES, NQ), lambda b, h, i: (b * nq + i, h, 0, 0)),
            pl.BlockSpec((S, LANES), lambda b, h, i: (b, h)),
            pl.BlockSpec((S // TM, 1, VROWS, TM), lambda b, h, i: (b, h, 0, 0)),
        ],
        out_specs=pl.BlockSpec((tq, LANES), lambda b, h, i: (b * nq + i, h)),
        out_shape=jax.ShapeDtypeStruct((T, 512), BF16),
        scratch_shapes=_attn_scratch(VROWS),
        compiler_params=_cparams(3), name="diff_attn",
    )(fast, lamv, subln, dqT, dk, dvT)


def _gqa_kernel(fast_ref, qT_ref, k_ref, vT_ref, o_ref, m_ref, acc_ref):
    _attn_core(fast_ref, qT_ref, k_ref, vT_ref, m_ref, acc_ref)
    for c in range(GQA_Q_HEADS // 2):
        g = (2 * c) // GQA_GROUP
        rows = slice(g * HEAD64, (g + 1) * HEAD64)
        parts = []
        for hh in (2 * c, 2 * c + 1):
            cols = slice(hh * LANES, (hh + 1) * LANES)
            parts.append(acc_ref[rows, cols] / acc_ref[LANES:LANES + 1, cols])
        o_ref[:, c * LANES:(c + 1) * LANES] = jnp.concatenate(parts, axis=0).T.astype(o_ref.dtype)


def _gqa_call(fast, gqT, gk, gvT, B, S):
    T = gk.shape[0]
    tq = NQ // GQA_Q_HEADS
    nq = S // tq
    return pl.pallas_call(
        _gqa_kernel, grid=(B, nq),
        in_specs=[
            _SMEM_SPEC,
            pl.BlockSpec((None, LANES, NQ), lambda b, i: (b * nq + i, 0, 0)),
            pl.BlockSpec((S, LANES), lambda b, i: (b, 0)),
            pl.BlockSpec((S // TM, 1, VROWS, TM), lambda b, i: (b, 0, 0, 0)),
        ],
        out_specs=pl.BlockSpec((tq, 512), lambda b, i: (b * nq + i, 0)),
        out_shape=jax.ShapeDtypeStruct((T, 512), BF16),
        scratch_shapes=_attn_scratch(VROWS),
        compiler_params=_cparams(2), name="gqa_attn",
    )(fast, gqT, gk, gvT)


S_REGIONS = (0, 64, 128)
O_REGIONS = (192, 208)
POP_LAG = 2


class _MxuStream:
    def __init__(self, mxu, first_unit, pending):
        self.mxu, self.unit, self.pending = mxu, first_unit, list(pending)

    def scores(self, q_tile, k_halves, sinks):
        pltpu.matmul_push_rhs(q_tile, staging_register=0, mxu_index=self.mxu)
        for half, (k, sink) in enumerate(zip(k_halves, sinks)):
            region = S_REGIONS[self.unit % len(S_REGIONS)]
            pltpu.matmul_acc_lhs(region, k, mxu_index=self.mxu,
                                 load_staged_rhs=0 if half == 0 else None)
            self.pending.append((region, sink))
            self.unit += 1
            if len(self.pending) > POP_LAG:
                self.pop()

    def pop(self):
        region, sink = self.pending.pop(0)
        sink(pltpu.matmul_pop(region, (MXU_DIM, MXU_DIM), F32, self.mxu))

    def values(self, o_region, p_halves, v_halves):
        pltpu.matmul_push_rhs(p_halves[0], staging_register=1, mxu_index=self.mxu)
        pltpu.matmul_push_rhs(p_halves[1], staging_register=0, mxu_index=self.mxu)
        pltpu.matmul_acc_lhs(o_region, v_halves[0], mxu_index=self.mxu, load_staged_rhs=1)
        pltpu.matmul_acc_lhs(o_region, v_halves[1], mxu_index=self.mxu, load_staged_rhs=0)


def _gqa_mxu_kernel(qT_ref, k_ref, vT_ref, o_ref, p_ref, l_ref):
    n_chunks = k_ref.shape[0] // TM
    per_trip = len(S_REGIONS)
    half = TM // 2
    assert half == MXU_DIM

    @pl.when((pl.program_id(0) == 0) & (pl.program_id(1) == 0))
    def _():
        for mxu in range(2):
            for region in S_REGIONS:
                pltpu.matmul_pop(region, (MXU_DIM, MXU_DIM), F32, mxu)
            for region in O_REGIONS:
                pltpu.matmul_pop(region, (HEAD64, MXU_DIM), F32, mxu)

    l_ref[...] = jnp.zeros(l_ref.shape, F32)

    def sink(slot, c, h):
        def store(s):
            pf = jnp.exp2(s)
            cs = slice(c * MXU_DIM, (c + 1) * MXU_DIM)
            l_ref[:, cs] += jnp.sum(pf, axis=0, keepdims=True)
            p_ref[slot, c, h * half:(h + 1) * half, :] = pf.astype(BF16)
        return store

    def k_halves(j):
        return [k_ref[pl.ds(j * TM + h * half, half), :] for h in range(2)]

    def chunk(streams, j, slot, do_scores, do_values):
        prev = (slot - 1) % per_trip
        for cc in range(2):
            if do_scores:
                for st in streams:
                    c = 2 * st.mxu + cc
                    st.scores(qT_ref[:, c * MXU_DIM:(c + 1) * MXU_DIM], k_halves(j),
                              [sink(slot, c, h) for h in range(2)])
            if do_values:
                for st in streams:
                    c = 2 * st.mxu + cc
                    st.values(O_REGIONS[cc],
                              [p_ref[prev, c, h * half:(h + 1) * half, :] for h in range(2)],
                              [vT_ref[j - 1, st.mxu, :, h * half:(h + 1) * half] for h in range(2)])

    sts = [_MxuStream(m, 0, []) for m in range(2)]
    for j in range(n_chunks):
        chunk(sts, j, j % per_trip, True, j > 0)
    for st in sts:
        while st.pending:
            st.pop()
    chunk(sts, n_chunks, n_chunks % per_trip, False, True)

    for st in sts:
        for cc in range(2):
            c = 2 * st.mxu + cc
            o = pltpu.matmul_pop(O_REGIONS[cc], (HEAD64, MXU_DIM), F32, st.mxu)
            o = o / l_ref[:, c * MXU_DIM:(c + 1) * MXU_DIM]
            pair = jnp.concatenate([o[:, 0:LANES], o[:, LANES:2 * LANES]], axis=0)
            o_ref[:, c * LANES:(c + 1) * LANES] = pair.T.astype(o_ref.dtype)


def _gqa_mxu_call(gqT, gk, gvT2, B, S):
    T = gk.shape[0]
    tq = NQ // GQA_Q_HEADS
    nq = S // tq
    return pl.pallas_call(
        _gqa_mxu_kernel, grid=(B, nq),
        in_specs=[
            pl.BlockSpec((None, MXU_DIM, NQ), lambda b, i: (b * nq + i, 0, 0)),
            pl.BlockSpec((S, MXU_DIM), lambda b, i: (b, 0)),
            pl.BlockSpec((S // TM, 2, HEAD64, TM), lambda b, i: (b, 0, 0, 0)),
        ],
        out_specs=pl.BlockSpec((tq, 512), lambda b, i: (b * nq + i, 0)),
        out_shape=jax.ShapeDtypeStruct((T, 512), BF16),
        scratch_shapes=[pltpu.VMEM((len(S_REGIONS), NQ // MXU_DIM, TM, MXU_DIM), BF16),
                        pltpu.VMEM((1, NQ), F32)],
        compiler_params=_cparams(2), name="gqa_attn_mxu",
    )(gqT, gk, gvT2)


def _ret_kernel(dec_ref, gain_ref, q_ref, kT_ref, v_ref, g_ref, o_ref, sf_ref, sb_ref):
    C = RET_CHUNK
    hd = pl.program_id(1)
    lf = -jnp.exp(jnp.full((C, C), dec_ref[0, hd], F32))
    lb = -jnp.exp(jnp.full((C, C), dec_ref[1, hd], F32))
    ii = lax.broadcasted_iota(jnp.int32, (C, C), 0).astype(F32)
    jj = lax.broadcasted_iota(jnp.int32, (C, C), 1).astype(F32)
    diff = ii - jj
    decay = jnp.where(diff >= 0, jnp.exp(jnp.maximum(diff, 0.0) * lf),
                      jnp.exp(jnp.maximum(-diff, 0.0) * lb))
    xi_f, zeta_f = jnp.exp((ii + 1.0) * lf), jnp.exp((C - 1.0 - ii) * lf)
    xi_b, zeta_b = jnp.exp((C - ii) * lb), jnp.exp(ii * lb)
    dec_f, dec_b = jnp.exp(C * lf), jnp.exp(C * lb)
    n_tiles = kT_ref.shape[0]
    per_tile = TM // C

    def rows(t, c):
        return pl.ds(pl.multiple_of(t * TM + c * C, C), C)

    def chunk_kv(t, c, zeta):
        vz = (v_ref[rows(t, c), :].astype(F32) * zeta).astype(BF16)
        return jnp.dot(kT_ref[t, :, c * C:(c + 1) * C], vz, preferred_element_type=F32)

    def scans(i, states):
        st_f, st_b = states
        tf, tb = i, n_tiles - 1 - i
        kv_f = [chunk_kv(tf, c, zeta_f) for c in range(per_tile)]
        kv_b = [chunk_kv(tb, c, zeta_b) for c in range(per_tile)]
        for c in range(per_tile):
            sf_ref[tf * per_tile + c] = st_f.astype(BF16)
            st_f = st_f * dec_f + kv_f[c]
            cb = per_tile - 1 - c
            sb_ref[tb * per_tile + cb] = st_b.astype(BF16)
            st_b = st_b * dec_b + kv_b[cb]
        return st_f, st_b

    zero = jnp.zeros((C, C), F32)
    lax.fori_loop(0, n_tiles, scans, (zero, zero))

    def outputs(tt, carry):
        chunks = [(tt * RET_OUT_TILES + i, c) for i in range(RET_OUT_TILES) for c in range(per_tile)]
        rs = [rows(t, c) for t, c in chunks]
        qs = [q_ref[r, :] for r in rs]
        a = [jnp.dot(q, kT_ref[t, :, c * C:(c + 1) * C], preferred_element_type=F32)
             for q, (t, c) in zip(qs, chunks)]
        cross = [jnp.dot(q, sf_ref[t * per_tile + c], preferred_element_type=F32) * xi_f
                 + jnp.dot(q, sb_ref[t * per_tile + c], preferred_element_type=F32) * xi_b
                 for q, (t, c) in zip(qs, chunks)]
        a = [(x * decay).astype(BF16) for x in a]
        o = [jnp.dot(x, v_ref[r, :], preferred_element_type=F32) + y
             for x, r, y in zip(a, rs, cross)]
        mu = [jnp.mean(x, axis=-1, keepdims=True) for x in o]
        oc = [x - m for x, m in zip(o, mu)]
        var = [jnp.mean(x * x, axis=-1, keepdims=True) for x in oc]
        for x, s2, r in zip(oc, var, rs):
            y = x * lax.rsqrt(s2 + 1e-5) * gain_ref[...]
            g = g_ref[r, :].astype(F32)
            o_ref[r, :] = (g / (1.0 + jnp.exp(-g)) * y).astype(o_ref.dtype)
        return carry

    lax.fori_loop(0, n_tiles // RET_OUT_TILES, outputs, 0)


def _ret_call(dec, gain, rq, rkT, rv, rg, B, S):
    T = rq.shape[0]
    blk = pl.BlockSpec((S, LANES), lambda b, h: (b, h))
    n_chunks = S // RET_CHUNK
    return pl.pallas_call(
        _ret_kernel, grid=(B, RET_HEADS),
        in_specs=[pl.BlockSpec(memory_space=pltpu.SMEM),
                  pl.BlockSpec((1, LANES), lambda b, h: (0, h)), blk,
                  pl.BlockSpec((S // TM, None, LANES, TM), lambda b, h: (b, h, 0, 0)), blk, blk],
        out_specs=blk,
        out_shape=jax.ShapeDtypeStruct((T, 512), BF16),
        scratch_shapes=[pltpu.VMEM((n_chunks, RET_CHUNK, LANES), BF16),
                        pltpu.VMEM((n_chunks, RET_CHUNK, LANES), BF16)],
        compiler_params=_cparams(2), name="retention",
    )(dec, gain, rq, rkT, rv, rg)


def _merge_kernel(h_ref, u_ref, a_ref, r_ref, c_ref, wg_ref, wb_ref, wo_ref, mn_ref,
                  hn_ref, u2_ref):
    u = u_ref[...]
    merged = None
    for n, br in enumerate((a_ref, r_ref, c_ref)):
        logits = jnp.dot(u, wg_ref[:, n * D_MODEL:(n + 1) * D_MODEL], preferred_element_type=F32)
        gate = 1.0 / (1.0 + jnp.exp(-logits))
        term = gate * jnp.dot(br[...], wb_ref[n], preferred_element_type=F32)
        merged = term if merged is None else merged + term
    hn = h_ref[...] + jnp.dot(merged.astype(BF16), wo_ref[...], preferred_element_type=F32)
    hn_ref[...] = hn
    ms = jnp.mean(hn * hn, axis=-1, keepdims=True)
    u2_ref[...] = (hn * lax.rsqrt(ms + NORM_EPS) * mn_ref[...]).astype(BF16)


def _merge_call(h, u, a, r, c, wg, wb, wo, mlp_norm):
    T = h.shape[0]
    row = lambda w: pl.BlockSpec((TM, w), lambda i: (i, 0))
    return pl.pallas_call(
        _merge_kernel, grid=(T // TM,),
        in_specs=[row(D_MODEL), row(D_MODEL), row(512), row(512), row(512),
                  _resident(wg.shape), _resident(wb.shape), _resident(wo.shape),
                  _resident((1, D_MODEL))],
        out_specs=(row(D_MODEL), row(D_MODEL)),
        out_shape=(jax.ShapeDtypeStruct((T, D_MODEL), F32),
                   jax.ShapeDtypeStruct((T, D_MODEL), BF16)),
        compiler_params=_cparams(1), name="merge",
    )(h, u, a, r, c, wg, wb, wo, mlp_norm)


def _mlp_kernel(h_ref, u_ref, w1_ref, w2_ref, o_ref):
    u = u_ref[...]
    acc = h_ref[...]
    for c in range(D_FF // D_MODEL):
        cs = slice(c * D_MODEL, (c + 1) * D_MODEL)
        m = jnp.maximum(jnp.dot(u, w1_ref[:, cs], preferred_element_type=F32), 0.0)
        acc = acc + jnp.dot((m * m).astype(BF16), w2_ref[cs, :], preferred_element_type=F32)
    o_ref[...] = acc


def _mlp_call(h, u2, w1, w2):
    T = h.shape[0]
    row = pl.BlockSpec((TM, D_MODEL), lambda i: (i, 0))
    return pl.pallas_call(
        _mlp_kernel, grid=(T // TM,),
        in_specs=[row, row, _resident(w1.shape), _resident(w2.shape)],
        out_specs=row,
        out_shape=jax.ShapeDtypeStruct((T, D_MODEL), F32),
        compiler_params=_cparams(1), name="mlp",
    )(h, u2, w1, w2)


def _rope_tables(S):
    pos = jnp.arange(S, dtype=F32)
    lane = jnp.arange(LANES)
    l64 = lane % 64

    inv = 500000.0 ** (-jnp.arange(0, 16, 2, dtype=F32) / 16)
    ang = pos[:, None] * inv[None, :]
    cos, sin = jnp.cos(ang)[:, l64 % 8], jnp.sin(ang)[:, l64 % 8]
    cd = jnp.where(l64 < 16, cos, 1.0)
    ad = jnp.where(l64 < 8, -sin, 0.0)
    bd = jnp.where((l64 >= 8) & (l64 < 16), sin, 0.0)

    inv = 10000.0 ** (-jnp.arange(0, 32, 2, dtype=F32) / 32)
    row_pos = jnp.floor(pos / GRID_W)
    col_pos = pos - row_pos * GRID_W
    p2 = jnp.where((l64 < 32)[None, :], row_pos[:, None], col_pos[:, None])
    ang = p2 * inv[l64 % 16][None, :]
    cg = jnp.cos(ang)
    ag = jnp.where(l64 % 32 < 16, -jnp.sin(ang), 0.0)
    bg = jnp.where(l64 % 32 >= 16, jnp.sin(ang), 0.0)

    inv = 10000.0 ** (-jnp.arange(0, 128, 2, dtype=F32) / 128)
    ang = pos[:, None] * inv[l64][None, :]
    cr = jnp.cos(ang)
    sr = jnp.where(lane < 64, -jnp.sin(ang), jnp.sin(ang))
    return (cd, ad, bd, cg, ag, bg, cr, sr)


def _fast_flag(q_gain, k_gain):
    bound = (HEAD64 ** 0.5 * LOG2E * 1.02) * jnp.max(jnp.abs(q_gain)) * jnp.max(jnp.abs(k_gain))
    return (bound <= FAST_SOFTMAX_BOUND).astype(jnp.int32).reshape(1)


def kernel(x, attn_norm, w_in, diff_q_norm, diff_k_norm, diff_lam_q1, diff_lam_k1, diff_lam_q2, diff_lam_k2, diff_subln, ret_decay_fwd, ret_decay_bwd, ret_group_norm, gqa_q_norm, gqa_k_norm, w_branch, w_out, mlp_norm, w_mlp_in, w_mlp_out):
    B, S, D = x.shape
    T = B * S
    tabs = _rope_tables(S)
    blk = jnp.arange(2 * LANES) // HEAD64
    gmat = jnp.where(blk[:, None] == blk[None, :], 1.0 / HEAD64, 0.0).astype(BF16)
    tile2 = lambda v: jnp.concatenate([v, v]).astype(F32)

    h = x.reshape(T, D)
    for l in range(DEPTH):
        lambda_init = 0.8 - 0.6 * math.exp(-0.3 * l)
        w_attn = w_in[l, :, :ATTN_COLS].astype(BF16)
        w_gate = w_in[l, :, ATTN_COLS:].astype(BF16)
        nrm = jnp.stack([tile2(diff_q_norm[l]), tile2(diff_k_norm[l]),
                         tile2(gqa_q_norm[l]), tile2(gqa_k_norm[l])])
        (u, dqT, dk, dvT, rq, rk, rv, rg, gqT, gk, gvT, gvT2) = _proj_call(
            h, attn_norm[l].reshape(1, D).astype(F32), w_attn, gmat, nrm, tabs)

        lamv = jnp.stack([diff_lam_q1[l], diff_lam_k1[l], diff_lam_q2[l], diff_lam_k2[l]]).astype(F32)
        a = _diff_call(_fast_flag(diff_q_norm[l], diff_k_norm[l]), lamv,
                       diff_subln[l].reshape(1, LANES).astype(F32), dqT, dk, dvT, B, S, lambda_init)
        dec = jnp.stack([ret_decay_fwd[l], ret_decay_bwd[l]]).astype(F32)
        r = _ret_call(dec, ret_group_norm[l].reshape(1, 512).astype(F32), rq, rk, rv, rg, B, S)
        gqa_fast = _fast_flag(gqa_q_norm[l], gqa_k_norm[l])
        c = lax.cond(gqa_fast[0] == 1,
                     lambda ops: _gqa_mxu_call(ops[0], ops[1], ops[3], B, S),
                     lambda ops: _gqa_call(gqa_fast, ops[0], ops[1], ops[2], B, S),
                     (gqT, gk, gvT, gvT2))

        h, u2 = _merge_call(h, u, a, r, c, w_gate, w_branch[l].astype(BF16),
                            w_out[l].astype(BF16), mlp_norm[l].reshape(1, D).astype(F32))
        h = _mlp_call(h, u2, w_mlp_in[l].astype(BF16), w_mlp_out[l].astype(BF16))
    return h.reshape(B, S, D)
```

```python
import functools
import math

import jax
import jax.numpy as jnp
from jax import lax
from jax.experimental import pallas as pl
from jax.experimental.pallas import tpu as pltpu

F32 = jnp.float32
BF16 = jnp.bfloat16

D_MODEL = 1024
DEPTH = 2
GRID_W = 64
NORM_EPS = 1e-6
HEAD64 = 64
DIFF_HEADS = 4
RET_HEADS = 4
RET_CHUNK = 128
GQA_Q_HEADS = 8
GQA_GROUP = 4
N_BRANCHES = 3
BRANCH_WIDTH = 512
D_FF = 4 * D_MODEL
ATTN_COLS = 4352
LANES = 128

TM = 512
TK = 2048
NQ = 1024
NC = 256
KV_UNROLL = 2
RET_OUT_TILES = 4
ONES_ROWS = 16
VROWS = LANES + ONES_ROWS
FAST_SOFTMAX_BOUND = 64.0
VMEM_LIMIT = 56 * 1024 * 1024
LOG2E = 1.4426950408889634


def _cparams(n_axes):
    return pltpu.CompilerParams(dimension_semantics=("arbitrary",) * n_axes,
                                vmem_limit_bytes=VMEM_LIMIT)


def _resident(shape):
    nd = len(shape)
    return pl.BlockSpec(shape, lambda *_: (0,) * nd, pipeline_mode=pl.Buffered(1))


def _proj_kernel(h_ref, an_ref, w_ref, g_ref, nrm_ref,
                 cd_ref, ad_ref, bd_ref, cg_ref, ag_ref, bg_ref, cr_ref, sr_ref,
                 u_ref, dqT_ref, dk_ref, dvT_ref, rq_ref, rkT_ref, rv_ref, rg_ref,
                 gqT_ref, gk_ref, gvT_ref):
    x = h_ref[...]
    ms = jnp.mean(x * x, axis=-1, keepdims=True)
    u = (x * lax.rsqrt(ms + NORM_EPS) * an_ref[...]).astype(BF16)
    u_ref[...] = u

    def proj(c0, n):
        return jnp.dot(u, w_ref[:, c0:c0 + n], preferred_element_type=F32)

    gmat = g_ref[...]
    lo = lax.broadcasted_iota(jnp.int32, (1, LANES), 1) < HEAD64
    first_row = lax.broadcasted_iota(jnp.int32, (ONES_ROWS, TM), 0) == 0
    ones_rows = jnp.where(first_row, 1.0, 0.0).astype(BF16)

    def qk_norm(p, gain):
        out = []
        for c0 in range(0, p.shape[1], 2 * LANES):
            y = p[:, c0:c0 + 2 * LANES]
            w = y.shape[1]
            msq = jnp.dot((y * y).astype(BF16), gmat[0:w, 0:w], preferred_element_type=F32)
            y = y * lax.rsqrt(msq + NORM_EPS)
            out += [y[:, i:i + LANES] * gain for i in range(0, w, LANES)]
        return out

    def rope(y, c, a, b, d):
        return y * c + pltpu.roll(y, LANES - d, 1) * a + pltpu.roll(y, d, 1) * b

    def chunks(p):
        return [p[:, i:i + LANES] for i in range(0, p.shape[1], LANES)]

    qscale = HEAD64 ** -0.5 * LOG2E
    kscale = LANES ** -0.5
    cd, ad, bd = cd_ref[...], ad_ref[...], bd_ref[...]
    cg, ag, bg = cg_ref[...], ag_ref[...], bg_ref[...]
    cr, sr = cr_ref[...], sr_ref[...]

    def diff_q(p):
        for hd, y in enumerate(qk_norm(p, nrm_ref[0:1, :])):
            y = rope(y, cd, ad, bd, 8) * qscale
            dqT_ref[hd, :, 0:TM] = jnp.where(lo, y, 0.0).T.astype(BF16)
            dqT_ref[hd, :, TM:2 * TM] = jnp.where(lo, 0.0, y).T.astype(BF16)

    def diff_k(p):
        for hd, y in enumerate(qk_norm(p, nrm_ref[1:2, :])):
            dk_ref[:, hd * LANES:(hd + 1) * LANES] = rope(y, cd, ad, bd, 8).astype(BF16)

    def diff_v(p):
        for hd, y in enumerate(chunks(p)):
            dvT_ref[hd, 0:LANES, :] = y.T.astype(BF16)
            dvT_ref[hd, LANES:VROWS, :] = ones_rows

    def ret_q(p):
        for hd, y in enumerate(chunks(p)):
            rq_ref[:, hd * LANES:(hd + 1) * LANES] = (y * cr + pltpu.roll(y, 64, 1) * sr).astype(BF16)

    def ret_k(p):
        for hd, y in enumerate(chunks(p)):
            rkT_ref[hd] = ((y * cr + pltpu.roll(y, 64, 1) * sr) * kscale).T.astype(BF16)

    def ret_v(p):
        rv_ref[...] = p.astype(BF16)

    def ret_g(p):
        rg_ref[...] = p.astype(BF16)

    def gqa_q(p):
        for c, y in enumerate(qk_norm(p, nrm_ref[2:3, :])):
            y = rope(y, cg, ag, bg, 16) * qscale
            ysw = pltpu.roll(y, 64, 1)
            if c < 2:
                z_even, z_odd = jnp.where(lo, y, 0.0), jnp.where(lo, ysw, 0.0)
            else:
                z_even, z_odd = jnp.where(lo, 0.0, ysw), jnp.where(lo, 0.0, y)
            for hh, z in ((2 * c, z_even), (2 * c + 1, z_odd)):
                zt = z.T.astype(BF16)
                for qb in range(TM // LANES):
                    gqT_ref[qb, :, hh * LANES:(hh + 1) * LANES] = zt[:, qb * LANES:(qb + 1) * LANES]

    def gqa_kv(p):
        (y,) = qk_norm(p[:, 0:LANES], nrm_ref[3:4, :])
        gk_ref[...] = rope(y, cg, ag, bg, 16).astype(BF16)
        gvT_ref[0:LANES, :] = p[:, LANES:2 * LANES].T.astype(BF16)
        gvT_ref[LANES:VROWS, :] = ones_rows

    sections = [(3584, 512, gqa_q), (0, 512, diff_q), (2048, 512, ret_k), (1024, 512, diff_v),
                (512, 512, diff_k), (4096, 256, gqa_kv), (1536, 512, ret_q), (2560, 512, ret_v),
                (3072, 512, ret_g)]
    p_next = proj(*sections[0][:2])
    for i, (_, _, epilogue) in enumerate(sections):
        p_cur = p_next
        if i + 1 < len(sections):
            p_next = proj(*sections[i + 1][:2])
        epilogue(p_cur)


def _proj_call(h, attn_norm, w_attn, gmat, nrm, tabs):
    T = h.shape[0]
    S = tabs[0].shape[0]
    nt = T // TM
    tab_spec = pl.BlockSpec((TM, LANES), lambda i: (i % (S // TM), 0))
    row = lambda w: pl.BlockSpec((TM, w), lambda i: (i, 0))
    out_shape = (
        jax.ShapeDtypeStruct((T, D_MODEL), BF16),
        jax.ShapeDtypeStruct((nt, 4, LANES, 2 * TM), BF16),
        jax.ShapeDtypeStruct((T, 512), BF16),
        jax.ShapeDtypeStruct((nt, 4, VROWS, TM), BF16),
        jax.ShapeDtypeStruct((T, 512), BF16),
        jax.ShapeDtypeStruct((nt, 4, LANES, TM), BF16),
        jax.ShapeDtypeStruct((T, 512), BF16),
        jax.ShapeDtypeStruct((T, 512), BF16),
        jax.ShapeDtypeStruct((T // LANES, LANES, NQ), BF16),
        jax.ShapeDtypeStruct((T, LANES), BF16),
        jax.ShapeDtypeStruct((nt, VROWS, TM), BF16),
    )
    out_specs = (
        row(D_MODEL),
        pl.BlockSpec((None, 4, LANES, 2 * TM), lambda i: (i, 0, 0, 0)),
        row(512),
        pl.BlockSpec((None, 4, VROWS, TM), lambda i: (i, 0, 0, 0)),
        row(512), pl.BlockSpec((None, 4, LANES, TM), lambda i: (i, 0, 0, 0)), row(512), row(512),
        pl.BlockSpec((TM // LANES, LANES, NQ), lambda i: (i, 0, 0)),
        row(LANES),
        pl.BlockSpec((None, VROWS, TM), lambda i: (i, 0, 0)),
    )
    in_specs = [row(D_MODEL), _resident((1, D_MODEL)), _resident(w_attn.shape),
                _resident(gmat.shape), _resident(nrm.shape)] + [tab_spec] * 8
    return pl.pallas_call(
        _proj_kernel, grid=(nt,), in_specs=in_specs, out_specs=out_specs,
        out_shape=out_shape, compiler_params=_cparams(1), name="proj",
    )(h, attn_norm, w_attn, gmat, nrm, *tabs)


def _attn_core(fast_ref, qT_ref, k_ref, vT_ref, m_ref, acc_ref):
    acc_ref[...] = jnp.zeros(acc_ref.shape, F32)
    n_steps = k_ref.shape[0] // TK

    def kv(j):
        k = k_ref[pl.ds(pl.multiple_of(j * TK, TK), TK), :]
        per = TK // TM
        vT = jnp.concatenate([vT_ref[j * per + i] for i in range(per)], axis=-1)
        return k, vT

    @pl.when(fast_ref[0] == 1)
    def _():
        def body(jj, carry):
            pv = lsum = None
            for u in range(KV_UNROLL):
                k, vT = kv(jj * KV_UNROLL + u)
                s = jnp.dot(k, qT_ref[...], preferred_element_type=F32)
                pf = jnp.exp2(s)
                ls = jnp.sum(pf, axis=0, keepdims=True)
                t = jnp.dot(vT[0:LANES], pf.astype(BF16), preferred_element_type=F32)
                pv, lsum = (t, ls) if pv is None else (pv + t, lsum + ls)
            acc_ref[0:LANES, :] += pv
            acc_ref[LANES:LANES + 1, :] += lsum
            return carry

        lax.fori_loop(0, n_steps // KV_UNROLL, body, 0)

    @pl.when(fast_ref[0] == 0)
    def _():
        m_ref[...] = jnp.full(m_ref.shape, -1e30, F32)

        def body(j, carry):
            k, vT = kv(j)
            for c in range(NQ // NC):
                cs = slice(c * NC, (c + 1) * NC)
                s = jnp.dot(k, qT_ref[:, cs], preferred_element_type=F32)
                m_old = m_ref[:, cs]
                m_new = jnp.maximum(m_old, jnp.max(s, axis=0, keepdims=True))
                alpha = jnp.exp2(m_old - m_new)
                p = jnp.exp2(s - m_new).astype(BF16)
                pv = jnp.dot(vT, p, preferred_element_type=F32)
                acc_ref[:, cs] = acc_ref[:, cs] * alpha + pv
                m_ref[:, cs] = m_new
            return carry

        lax.fori_loop(0, n_steps, body, 0)


def _diff_kernel(fast_ref, lam_ref, sub_ref, qT_ref, k_ref, vT_ref, o_ref, m_ref, acc_ref, *,
                 lambda_init):
    _attn_core(fast_ref, qT_ref, k_ref, vT_ref, m_ref, acc_ref)

    lv = lam_ref[...]
    lam = (jnp.exp(jnp.sum(lv[0:1] * lv[1:2], axis=-1, keepdims=True))
           - jnp.exp(jnp.sum(lv[2:3] * lv[3:4], axis=-1, keepdims=True)) + lambda_init)
    tq = NQ // 2
    o1 = acc_ref[0:LANES, 0:tq] / acc_ref[LANES:LANES + 1, 0:tq]
    o2 = acc_ref[0:LANES, tq:NQ] / acc_ref[LANES:LANES + 1, tq:NQ]
    d = o1 - lam * o2
    ms = jnp.mean(d * d, axis=0, keepdims=True)
    dn = d * lax.rsqrt(ms + 1e-5)
    o_ref[...] = (dn.T * (sub_ref[...] * (1.0 - lambda_init))).astype(o_ref.dtype)


_SMEM_SPEC = pl.BlockSpec(memory_space=pltpu.SMEM)
_ATTN_SCRATCH = [pltpu.VMEM((1, NQ), F32), pltpu.VMEM((VROWS, NQ), F32)]


def _diff_call(fast, lamv, subln, dqT, dk, dvT, B, S, lambda_init):
    T = dk.shape[0]
    tq = NQ // 2
    nq = S // tq
    kern = functools.partial(_diff_kernel, lambda_init=lambda_init)
    return pl.pallas_call(
        kern, grid=(B, DIFF_HEADS, nq),
        in_specs=[
            _SMEM_SPEC, _resident(lamv.shape), _resident(subln.shape),
            pl.BlockSpec((None, None, LANES, NQ), lambda b, h, i: (b * nq + i, h, 0, 0)),
            pl.BlockSpec((S, LANES), lambda b, h, i: (b, h)),
            pl.BlockSpec((S // TM, None, VROWS, TM), lambda b, h, i: (b, h, 0, 0)),
        ],
        out_specs=pl.BlockSpec((tq, LANES), lambda b, h, i: (b * nq + i, h)),
        out_shape=jax.ShapeDtypeStruct((T, 512), BF16),
        scratch_shapes=_ATTN_SCRATCH,
        compiler_params=_cparams(3), name="diff_attn",
    )(fast, lamv, subln, dqT, dk, dvT)


def _gqa_kernel(fast_ref, qT_ref, k_ref, vT_ref, o_ref, m_ref, acc_ref):
    _attn_core(fast_ref, qT_ref, k_ref, vT_ref, m_ref, acc_ref)
    for c in range(GQA_Q_HEADS // 2):
        g = (2 * c) // GQA_GROUP
        rows = slice(g * HEAD64, (g + 1) * HEAD64)
        parts = []
        for hh in (2 * c, 2 * c + 1):
            cols = slice(hh * LANES, (hh + 1) * LANES)
            parts.append(acc_ref[rows, cols] / acc_ref[LANES:LANES + 1, cols])
        o_ref[:, c * LANES:(c + 1) * LANES] = jnp.concatenate(parts, axis=0).T.astype(o_ref.dtype)


def _gqa_call(fast, gqT, gk, gvT, B, S):
    T = gk.shape[0]
    tq = NQ // GQA_Q_HEADS
    nq = S // tq
    return pl.pallas_call(
        _gqa_kernel, grid=(B, nq),
        in_specs=[
            _SMEM_SPEC,
            pl.BlockSpec((None, LANES, NQ), lambda b, i: (b * nq + i, 0, 0)),
            pl.BlockSpec((S, LANES), lambda b, i: (b, 0)),
            pl.BlockSpec((S // TM, VROWS, TM), lambda b, i: (b, 0, 0)),
        ],
        out_specs=pl.BlockSpec((tq, 512), lambda b, i: (b * nq + i, 0)),
        out_shape=jax.ShapeDtypeStruct((T, 512), BF16),
        scratch_shapes=_ATTN_SCRATCH,
        compiler_params=_cparams(2), name="gqa_attn",
    )(fast, gqT, gk, gvT)


def _ret_kernel(dec_ref, gain_ref, q_ref, kT_ref, v_ref, g_ref, o_ref, sf_ref, sb_ref):
    C = RET_CHUNK
    hd = pl.program_id(1)
    lf = -jnp.exp(jnp.full((C, C), dec_ref[0, hd], F32))
    lb = -jnp.exp(jnp.full((C, C), dec_ref[1, hd], F32))
    ii = lax.broadcasted_iota(jnp.int32, (C, C), 0).astype(F32)
    jj = lax.broadcasted_iota(jnp.int32, (C, C), 1).astype(F32)
    diff = ii - jj
    decay = jnp.where(diff >= 0, jnp.exp(jnp.maximum(diff, 0.0) * lf),
                      jnp.exp(jnp.maximum(-diff, 0.0) * lb))
    xi_f, zeta_f = jnp.exp((ii + 1.0) * lf), jnp.exp((C - 1.0 - ii) * lf)
    xi_b, zeta_b = jnp.exp((C - ii) * lb), jnp.exp(ii * lb)
    dec_f, dec_b = jnp.exp(C * lf), jnp.exp(C * lb)
    n_tiles = kT_ref.shape[0]
    per_tile = TM // C

    def rows(t, c):
        return pl.ds(pl.multiple_of(t * TM + c * C, C), C)

    def chunk_kv(t, c, zeta):
        vz = (v_ref[rows(t, c), :].astype(F32) * zeta).astype(BF16)
        return jnp.dot(kT_ref[t, :, c * C:(c + 1) * C], vz, preferred_element_type=F32)

    def scans(i, states):
        st_f, st_b = states
        tf, tb = i, n_tiles - 1 - i
        kv_f = [chunk_kv(tf, c, zeta_f) for c in range(per_tile)]
        kv_b = [chunk_kv(tb, c, zeta_b) for c in range(per_tile)]
        for c in range(per_tile):
            sf_ref[tf * per_tile + c] = st_f.astype(BF16)
            st_f = st_f * dec_f + kv_f[c]
            cb = per_tile - 1 - c
            sb_ref[tb * per_tile + cb] = st_b.astype(BF16)
            st_b = st_b * dec_b + kv_b[cb]
        return st_f, st_b

    zero = jnp.zeros((C, C), F32)
    lax.fori_loop(0, n_tiles, scans, (zero, zero))

    def outputs(tt, carry):
        chunks = [(tt * RET_OUT_TILES + i, c) for i in range(RET_OUT_TILES) for c in range(per_tile)]
        rs = [rows(t, c) for t, c in chunks]
        qs = [q_ref[r, :] for r in rs]
        a = [jnp.dot(q, kT_ref[t, :, c * C:(c + 1) * C], preferred_element_type=F32)
             for q, (t, c) in zip(qs, chunks)]
        cross = [jnp.dot(q, sf_ref[t * per_tile + c], preferred_element_type=F32) * xi_f
                 + jnp.dot(q, sb_ref[t * per_tile + c], preferred_element_type=F32) * xi_b
                 for q, (t, c) in zip(qs, chunks)]
        a = [(x * decay).astype(BF16) for x in a]
        o = [jnp.dot(x, v_ref[r, :], preferred_element_type=F32) + y
             for x, r, y in zip(a, rs, cross)]
        mu = [jnp.mean(x, axis=-1, keepdims=True) for x in o]
        oc = [x - m for x, m in zip(o, mu)]
        var = [jnp.mean(x * x, axis=-1, keepdims=True) for x in oc]
        for x, s2, r in zip(oc, var, rs):
            y = x * lax.rsqrt(s2 + 1e-5) * gain_ref[...]
            g = g_ref[r, :].astype(F32)
            o_ref[r, :] = (g / (1.0 + jnp.exp(-g)) * y).astype(o_ref.dtype)
        return carry

    lax.fori_loop(0, n_tiles // RET_OUT_TILES, outputs, 0)


def _ret_call(dec, gain, rq, rkT, rv, rg, B, S):
    T = rq.shape[0]
    blk = pl.BlockSpec((S, LANES), lambda b, h: (b, h))
    n_chunks = S // RET_CHUNK
    return pl.pallas_call(
        _ret_kernel, grid=(B, RET_HEADS),
        in_specs=[pl.BlockSpec(memory_space=pltpu.SMEM),
                  pl.BlockSpec((1, LANES), lambda b, h: (0, h)), blk,
                  pl.BlockSpec((S // TM, None, LANES, TM), lambda b, h: (b, h, 0, 0)), blk, blk],
        out_specs=blk,
        out_shape=jax.ShapeDtypeStruct((T, 512), BF16),
        scratch_shapes=[pltpu.VMEM((n_chunks, RET_CHUNK, LANES), BF16),
                        pltpu.VMEM((n_chunks, RET_CHUNK, LANES), BF16)],
        compiler_params=_cparams(2), name="retention",
    )(dec, gain, rq, rkT, rv, rg)


def _merge_kernel(h_ref, u_ref, a_ref, r_ref, c_ref, wg_ref, wb_ref, wo_ref, mn_ref,
                  hn_ref, u2_ref):
    u = u_ref[...]
    merged = None
    for n, br in enumerate((a_ref, r_ref, c_ref)):
        logits = jnp.dot(u, wg_ref[:, n * D_MODEL:(n + 1) * D_MODEL], preferred_element_type=F32)
        gate = 1.0 / (1.0 + jnp.exp(-logits))
        term = gate * jnp.dot(br[...], wb_ref[n], preferred_element_type=F32)
        merged = term if merged is None else merged + term
    hn = h_ref[...] + jnp.dot(merged.astype(BF16), wo_ref[...], preferred_element_type=F32)
    hn_ref[...] = hn
    ms = jnp.mean(hn * hn, axis=-1, keepdims=True)
    u2_ref[...] = (hn * lax.rsqrt(ms + NORM_EPS) * mn_ref[...]).astype(BF16)


def _merge_call(h, u, a, r, c, wg, wb, wo, mlp_norm):
    T = h.shape[0]
    row = lambda w: pl.BlockSpec((TM, w), lambda i: (i, 0))
    return pl.pallas_call(
        _merge_kernel, grid=(T // TM,),
        in_specs=[row(D_MODEL), row(D_MODEL), row(512), row(512), row(512),
                  _resident(wg.shape), _resident(wb.shape), _resident(wo.shape),
                  _resident((1, D_MODEL))],
        out_specs=(row(D_MODEL), row(D_MODEL)),
        out_shape=(jax.ShapeDtypeStruct((T, D_MODEL), F32),
                   jax.ShapeDtypeStruct((T, D_MODEL), BF16)),
        compiler_params=_cparams(1), name="merge",
    )(h, u, a, r, c, wg, wb, wo, mlp_norm)


def _mlp_kernel(h_ref, u_ref, w1_ref, w2_ref, o_ref):
    u = u_ref[...]
    acc = h_ref[...]
    for c in range(D_FF // D_MODEL):
        cs = slice(c * D_MODEL, (c + 1) * D_MODEL)
        m = jnp.maximum(jnp.dot(u, w1_ref[:, cs], preferred_element_type=F32), 0.0)
        acc = acc + jnp.dot((m * m).astype(BF16), w2_ref[cs, :], preferred_element_type=F32)
    o_ref[...] = acc


def _mlp_call(h, u2, w1, w2):
    T = h.shape[0]
    row = pl.BlockSpec((TM, D_MODEL), lambda i: (i, 0))
    return pl.pallas_call(
        _mlp_kernel, grid=(T // TM,),
        in_specs=[row, row, _resident(w1.shape), _resident(w2.shape)],
        out_specs=row,
        out_shape=jax.ShapeDtypeStruct((T, D_MODEL), F32),
        compiler_params=_cparams(1), name="mlp",
    )(h, u2, w1, w2)


def _rope_tables(S):
    pos = jnp.arange(S, dtype=F32)
    lane = jnp.arange(LANES)
    l64 = lane % 64

    inv = 500000.0 ** (-jnp.arange(0, 16, 2, dtype=F32) / 16)
    ang = pos[:, None] * inv[None, :]
    cos, sin = jnp.cos(ang)[:, l64 % 8], jnp.sin(ang)[:, l64 % 8]
    cd = jnp.where(l64 < 16, cos, 1.0)
    ad = jnp.where(l64 < 8, -sin, 0.0)
    bd = jnp.where((l64 >= 8) & (l64 < 16), sin, 0.0)

    inv = 10000.0 ** (-jnp.arange(0, 32, 2, dtype=F32) / 32)
    row_pos = jnp.floor(pos / GRID_W)
    col_pos = pos - row_pos * GRID_W
    p2 = jnp.where((l64 < 32)[None, :], row_pos[:, None], col_pos[:, None])
    ang = p2 * inv[l64 % 16][None, :]
    cg = jnp.cos(ang)
    ag = jnp.where(l64 % 32 < 16, -jnp.sin(ang), 0.0)
    bg = jnp.where(l64 % 32 >= 16, jnp.sin(ang), 0.0)

    inv = 10000.0 ** (-jnp.arange(0, 128, 2, dtype=F32) / 128)
    ang = pos[:, None] * inv[l64][None, :]
    cr = jnp.cos(ang)
    sr = jnp.where(lane < 64, -jnp.sin(ang), jnp.sin(ang))
    return (cd, ad, bd, cg, ag, bg, cr, sr)


def _fast_flag(q_gain, k_gain):
    bound = (HEAD64 ** 0.5 * LOG2E * 1.02) * jnp.max(jnp.abs(q_gain)) * jnp.max(jnp.abs(k_gain))
    return (bound <= FAST_SOFTMAX_BOUND).astype(jnp.int32).reshape(1)


def kernel(x, attn_norm, w_in, diff_q_norm, diff_k_norm, diff_lam_q1, diff_lam_k1, diff_lam_q2, diff_lam_k2, diff_subln, ret_decay_fwd, ret_decay_bwd, ret_group_norm, gqa_q_norm, gqa_k_norm, w_branch, w_out, mlp_norm, w_mlp_in, w_mlp_out):
    B, S, D = x.shape
    T = B * S
    tabs = _rope_tables(S)
    blk = jnp.arange(2 * LANES) // HEAD64
    gmat = jnp.where(blk[:, None] == blk[None, :], 1.0 / HEAD64, 0.0).astype(BF16)
    tile2 = lambda v: jnp.concatenate([v, v]).astype(F32)

    h = x.reshape(T, D)
    for l in range(DEPTH):
        lambda_init = 0.8 - 0.6 * math.exp(-0.3 * l)
        w_attn = w_in[l, :, :ATTN_COLS].astype(BF16)
        w_gate = w_in[l, :, ATTN_COLS:].astype(BF16)
        nrm = jnp.stack([tile2(diff_q_norm[l]), tile2(diff_k_norm[l]),
                         tile2(gqa_q_norm[l]), tile2(gqa_k_norm[l])])
        (u, dqT, dk, dvT, rq, rkT, rv, rg, gqT, gk, gvT) = _proj_call(
            h, attn_norm[l].reshape(1, D).astype(F32), w_attn, gmat, nrm, tabs)

        lamv = jnp.stack([diff_lam_q1[l], diff_lam_k1[l], diff_lam_q2[l], diff_lam_k2[l]]).astype(F32)
        a = _diff_call(_fast_flag(diff_q_norm[l], diff_k_norm[l]), lamv,
                       diff_subln[l].reshape(1, LANES).astype(F32), dqT, dk, dvT, B, S, lambda_init)
        dec = jnp.stack([ret_decay_fwd[l], ret_decay_bwd[l]]).astype(F32)
        r = _ret_call(dec, ret_group_norm[l].reshape(1, 512).astype(F32), rq, rkT, rv, rg, B, S)
        c = _gqa_call(_fast_flag(gqa_q_norm[l], gqa_k_norm[l]), gqT, gk, gvT, B, S)

        h, u2 = _merge_call(h, u, a, r, c, w_gate, w_branch[l].astype(BF16),
                            w_out[l].astype(BF16), mlp_norm[l].reshape(1, D).astype(F32))
        h = _mlp_call(h, u2, w_mlp_in[l].astype(BF16), w_mlp_out[l].astype(BF16))
    return h.reshape(B, S, D)
```

```python
import functools
import math

import jax
import jax.numpy as jnp
from jax import lax
from jax.experimental import pallas as pl
from jax.experimental.pallas import tpu as pltpu

F32 = jnp.float32
BF16 = jnp.bfloat16

D_MODEL = 1024
DEPTH = 2
GRID_W = 64
NORM_EPS = 1e-6
HEAD64 = 64
DIFF_HEADS = 4
RET_HEADS = 4
RET_CHUNK = 128
GQA_Q_HEADS = 8
GQA_GROUP = 4
N_BRANCHES = 3
BRANCH_WIDTH = 512
D_FF = 4 * D_MODEL
ATTN_COLS = 4352
LANES = 128

TM = 512
TK = 2048
NQ = 1024
NC = 256
KV_UNROLL = 4
RET_OUT_TILES = 4
ONES_ROWS = 16
VROWS = LANES + ONES_ROWS
FAST_SOFTMAX_BOUND = 64.0
VMEM_LIMIT = 56 * 1024 * 1024
LOG2E = 1.4426950408889634


def _cparams(n_axes):
    return pltpu.CompilerParams(dimension_semantics=("arbitrary",) * n_axes,
                                vmem_limit_bytes=VMEM_LIMIT)


def _resident(shape):
    nd = len(shape)
    return pl.BlockSpec(shape, lambda *_: (0,) * nd, pipeline_mode=pl.Buffered(1))


def _proj_kernel(h_ref, an_ref, w_ref, g_ref, nrm_ref,
                 cd_ref, ad_ref, bd_ref, cg_ref, ag_ref, bg_ref, cr_ref, sr_ref,
                 u_ref, dqT_ref, dk_ref, dvT_ref, rq_ref, rkT_ref, rv_ref, rg_ref,
                 gqT_ref, gk_ref, gvT_ref):
    x = h_ref[...]
    ms = jnp.mean(x * x, axis=-1, keepdims=True)
    u = (x * lax.rsqrt(ms + NORM_EPS) * an_ref[...]).astype(BF16)
    u_ref[...] = u

    def proj(c0, n):
        return jnp.dot(u, w_ref[:, c0:c0 + n], preferred_element_type=F32)

    gmat = g_ref[...]
    lo = lax.broadcasted_iota(jnp.int32, (1, LANES), 1) < HEAD64
    first_row = lax.broadcasted_iota(jnp.int32, (ONES_ROWS, TM), 0) == 0
    ones_rows = jnp.where(first_row, 1.0, 0.0).astype(BF16)

    def qk_norm(p, gain):
        out = []
        for c0 in range(0, p.shape[1], 2 * LANES):
            y = p[:, c0:c0 + 2 * LANES]
            w = y.shape[1]
            msq = jnp.dot((y * y).astype(BF16), gmat[0:w, 0:w], preferred_element_type=F32)
            y = y * lax.rsqrt(msq + NORM_EPS)
            out += [y[:, i:i + LANES] * gain for i in range(0, w, LANES)]
        return out

    def rope(y, c, a, b, d):
        return y * c + pltpu.roll(y, LANES - d, 1) * a + pltpu.roll(y, d, 1) * b

    def chunks(p):
        return [p[:, i:i + LANES] for i in range(0, p.shape[1], LANES)]

    qscale = HEAD64 ** -0.5 * LOG2E
    kscale = LANES ** -0.5
    cd, ad, bd = cd_ref[...], ad_ref[...], bd_ref[...]
    cg, ag, bg = cg_ref[...], ag_ref[...], bg_ref[...]
    cr, sr = cr_ref[...], sr_ref[...]

    def diff_q(p):
        for hd, y in enumerate(qk_norm(p, nrm_ref[0:1, :])):
            y = rope(y, cd, ad, bd, 8) * qscale
            dqT_ref[hd, :, 0:TM] = jnp.where(lo, y, 0.0).T.astype(BF16)
            dqT_ref[hd, :, TM:2 * TM] = jnp.where(lo, 0.0, y).T.astype(BF16)

    def diff_k(p):
        for hd, y in enumerate(qk_norm(p, nrm_ref[1:2, :])):
            dk_ref[:, hd * LANES:(hd + 1) * LANES] = rope(y, cd, ad, bd, 8).astype(BF16)

    def diff_v(p):
        for hd, y in enumerate(chunks(p)):
            dvT_ref[hd, 0:LANES, :] = y.T.astype(BF16)
            dvT_ref[hd, LANES:VROWS, :] = ones_rows

    def ret_q(p):
        for hd, y in enumerate(chunks(p)):
            rq_ref[:, hd * LANES:(hd + 1) * LANES] = (y * cr + pltpu.roll(y, 64, 1) * sr).astype(BF16)

    def ret_k(p):
        for hd, y in enumerate(chunks(p)):
            rkT_ref[hd] = ((y * cr + pltpu.roll(y, 64, 1) * sr) * kscale).T.astype(BF16)

    def ret_v(p):
        rv_ref[...] = p.astype(BF16)

    def ret_g(p):
        rg_ref[...] = p.astype(BF16)

    def gqa_q(p):
        for c, y in enumerate(qk_norm(p, nrm_ref[2:3, :])):
            y = rope(y, cg, ag, bg, 16) * qscale
            ysw = pltpu.roll(y, 64, 1)
            if c < 2:
                z_even, z_odd = jnp.where(lo, y, 0.0), jnp.where(lo, ysw, 0.0)
            else:
                z_even, z_odd = jnp.where(lo, 0.0, ysw), jnp.where(lo, 0.0, y)
            for hh, z in ((2 * c, z_even), (2 * c + 1, z_odd)):
                zt = z.T.astype(BF16)
                for qb in range(TM // LANES):
                    gqT_ref[qb, :, hh * LANES:(hh + 1) * LANES] = zt[:, qb * LANES:(qb + 1) * LANES]

    def gqa_kv(p):
        (y,) = qk_norm(p[:, 0:LANES], nrm_ref[3:4, :])
        gk_ref[...] = rope(y, cg, ag, bg, 16).astype(BF16)
        gvT_ref[0:LANES, :] = p[:, LANES:2 * LANES].T.astype(BF16)
        gvT_ref[LANES:VROWS, :] = ones_rows

    sections = [(3584, 512, gqa_q), (0, 512, diff_q), (2048, 512, ret_k), (1024, 512, diff_v),
                (512, 512, diff_k), (4096, 256, gqa_kv), (1536, 512, ret_q), (2560, 512, ret_v),
                (3072, 512, ret_g)]
    p_next = proj(*sections[0][:2])
    for i, (_, _, epilogue) in enumerate(sections):
        p_cur = p_next
        if i + 1 < len(sections):
            p_next = proj(*sections[i + 1][:2])
        epilogue(p_cur)


def _proj_call(h, attn_norm, w_attn, gmat, nrm, tabs):
    T = h.shape[0]
    S = tabs[0].shape[0]
    nt = T // TM
    tab_spec = pl.BlockSpec((TM, LANES), lambda i: (i % (S // TM), 0))
    row = lambda w: pl.BlockSpec((TM, w), lambda i: (i, 0))
    out_shape = (
        jax.ShapeDtypeStruct((T, D_MODEL), BF16),
        jax.ShapeDtypeStruct((nt, 4, LANES, 2 * TM), BF16),
        jax.ShapeDtypeStruct((T, 512), BF16),
        jax.ShapeDtypeStruct((nt, 4, VROWS, TM), BF16),
        jax.ShapeDtypeStruct((T, 512), BF16),
        jax.ShapeDtypeStruct((nt, 4, LANES, TM), BF16),
        jax.ShapeDtypeStruct((T, 512), BF16),
        jax.ShapeDtypeStruct((T, 512), BF16),
        jax.ShapeDtypeStruct((T // LANES, LANES, NQ), BF16),
        jax.ShapeDtypeStruct((T, LANES), BF16),
        jax.ShapeDtypeStruct((nt, VROWS, TM), BF16),
    )
    out_specs = (
        row(D_MODEL),
        pl.BlockSpec((None, 4, LANES, 2 * TM), lambda i: (i, 0, 0, 0)),
        row(512),
        pl.BlockSpec((None, 4, VROWS, TM), lambda i: (i, 0, 0, 0)),
        row(512), pl.BlockSpec((None, 4, LANES, TM), lambda i: (i, 0, 0, 0)), row(512), row(512),
        pl.BlockSpec((TM // LANES, LANES, NQ), lambda i: (i, 0, 0)),
        row(LANES),
        pl.BlockSpec((None, VROWS, TM), lambda i: (i, 0, 0)),
    )
    in_specs = [row(D_MODEL), _resident((1, D_MODEL)), _resident(w_attn.shape),
                _resident(gmat.shape), _resident(nrm.shape)] + [tab_spec] * 8
    return pl.pallas_call(
        _proj_kernel, grid=(nt,), in_specs=in_specs, out_specs=out_specs,
        out_shape=out_shape, compiler_params=_cparams(1), name="proj",
    )(h, attn_norm, w_attn, gmat, nrm, *tabs)


def _attn_core(fast_ref, qT_ref, k_ref, vT_ref, m_ref, acc_ref):
    acc_ref[...] = jnp.zeros(acc_ref.shape, F32)
    n_steps = k_ref.shape[0] // TK

    def kv(j):
        k = k_ref[pl.ds(pl.multiple_of(j * TK, TK), TK), :]
        per = TK // TM
        vT = jnp.concatenate([vT_ref[j * per + i] for i in range(per)], axis=-1)
        return k, vT

    @pl.when(fast_ref[0] == 1)
    def _():
        def body(jj, carry):
            pv = lsum = None
            for u in range(KV_UNROLL):
                k, vT = kv(jj * KV_UNROLL + u)
                s = jnp.dot(k, qT_ref[...], preferred_element_type=F32)
                pf = jnp.exp2(s)
                ls = jnp.sum(pf, axis=0, keepdims=True)
                t = jnp.dot(vT[0:LANES], pf.astype(BF16), preferred_element_type=F32)
                pv, lsum = (t, ls) if pv is None else (pv + t, lsum + ls)
            acc_ref[0:LANES, :] += pv
            acc_ref[LANES:LANES + 1, :] += lsum
            return carry

        lax.fori_loop(0, n_steps // KV_UNROLL, body, 0)

    @pl.when(fast_ref[0] == 0)
    def _():
        m_ref[...] = jnp.full(m_ref.shape, -1e30, F32)

        def body(j, carry):
            k, vT = kv(j)
            for c in range(NQ // NC):
                cs = slice(c * NC, (c + 1) * NC)
                s = jnp.dot(k, qT_ref[:, cs], preferred_element_type=F32)
                m_old = m_ref[:, cs]
                m_new = jnp.maximum(m_old, jnp.max(s, axis=0, keepdims=True))
                alpha = jnp.exp2(m_old - m_new)
                p = jnp.exp2(s - m_new).astype(BF16)
                pv = jnp.dot(vT, p, preferred_element_type=F32)
                acc_ref[:, cs] = acc_ref[:, cs] * alpha + pv
                m_ref[:, cs] = m_new
            return carry

        lax.fori_loop(0, n_steps, body, 0)


def _diff_kernel(fast_ref, lam_ref, sub_ref, qT_ref, k_ref, vT_ref, o_ref, m_ref, acc_ref, *,
                 lambda_init):
    _attn_core(fast_ref, qT_ref, k_ref, vT_ref, m_ref, acc_ref)

    lv = lam_ref[...]
    lam = (jnp.exp(jnp.sum(lv[0:1] * lv[1:2], axis=-1, keepdims=True))
           - jnp.exp(jnp.sum(lv[2:3] * lv[3:4], axis=-1, keepdims=True)) + lambda_init)
    tq = NQ // 2
    o1 = acc_ref[0:LANES, 0:tq] / acc_ref[LANES:LANES + 1, 0:tq]
    o2 = acc_ref[0:LANES, tq:NQ] / acc_ref[LANES:LANES + 1, tq:NQ]
    d = o1 - lam * o2
    ms = jnp.mean(d * d, axis=0, keepdims=True)
    dn = d * lax.rsqrt(ms + 1e-5)
    o_ref[...] = (dn.T * (sub_ref[...] * (1.0 - lambda_init))).astype(o_ref.dtype)


_SMEM_SPEC = pl.BlockSpec(memory_space=pltpu.SMEM)
_ATTN_SCRATCH = [pltpu.VMEM((1, NQ), F32), pltpu.VMEM((VROWS, NQ), F32)]


def _diff_call(fast, lamv, subln, dqT, dk, dvT, B, S, lambda_init):
    T = dk.shape[0]
    tq = NQ // 2
    nq = S // tq
    kern = functools.partial(_diff_kernel, lambda_init=lambda_init)
    return pl.pallas_call(
        kern, grid=(B, DIFF_HEADS, nq),
        in_specs=[
            _SMEM_SPEC, _resident(lamv.shape), _resident(subln.shape),
            pl.BlockSpec((None, None, LANES, NQ), lambda b, h, i: (b * nq + i, h, 0, 0)),
            pl.BlockSpec((S, LANES), lambda b, h, i: (b, h)),
            pl.BlockSpec((S // TM, None, VROWS, TM), lambda b, h, i: (b, h, 0, 0)),
        ],
        out_specs=pl.BlockSpec((tq, LANES), lambda b, h, i: (b * nq + i, h)),
        out_shape=jax.ShapeDtypeStruct((T, 512), BF16),
        scratch_shapes=_ATTN_SCRATCH,
        compiler_params=_cparams(3), name="diff_attn",
    )(fast, lamv, subln, dqT, dk, dvT)


def _gqa_kernel(fast_ref, qT_ref, k_ref, vT_ref, o_ref, m_ref, acc_ref):
    _attn_core(fast_ref, qT_ref, k_ref, vT_ref, m_ref, acc_ref)
    for c in range(GQA_Q_HEADS // 2):
        g = (2 * c) // GQA_GROUP
        rows = slice(g * HEAD64, (g + 1) * HEAD64)
        parts = []
        for hh in (2 * c, 2 * c + 1):
            cols = slice(hh * LANES, (hh + 1) * LANES)
            parts.append(acc_ref[rows, cols] / acc_ref[LANES:LANES + 1, cols])
        o_ref[:, c * LANES:(c + 1) * LANES] = jnp.concatenate(parts, axis=0).T.astype(o_ref.dtype)


def _gqa_call(fast, gqT, gk, gvT, B, S):
    T = gk.shape[0]
    tq = NQ // GQA_Q_HEADS
    nq = S // tq
    return pl.pallas_call(
        _gqa_kernel, grid=(B, nq),
        in_specs=[
            _SMEM_SPEC,
            pl.BlockSpec((None, LANES, NQ), lambda b, i: (b * nq + i, 0, 0)),
            pl.BlockSpec((S, LANES), lambda b, i: (b, 0)),
            pl.BlockSpec((S // TM, VROWS, TM), lambda b, i: (b, 0, 0)),
        ],
        out_specs=pl.BlockSpec((tq, 512), lambda b, i: (b * nq + i, 0)),
        out_shape=jax.ShapeDtypeStruct((T, 512), BF16),
        scratch_shapes=_ATTN_SCRATCH,
        compiler_params=_cparams(2), name="gqa_attn",
    )(fast, gqT, gk, gvT)


def _ret_kernel(dec_ref, gain_ref, q_ref, kT_ref, v_ref, g_ref, o_ref, sf_ref, sb_ref):
    C = RET_CHUNK
    hd = pl.program_id(1)
    lf = -jnp.exp(jnp.full((C, C), dec_ref[0, hd], F32))
    lb = -jnp.exp(jnp.full((C, C), dec_ref[1, hd], F32))
    ii = lax.broadcasted_iota(jnp.int32, (C, C), 0).astype(F32)
    jj = lax.broadcasted_iota(jnp.int32, (C, C), 1).astype(F32)
    diff = ii - jj
    decay = jnp.where(diff >= 0, jnp.exp(jnp.maximum(diff, 0.0) * lf),
                      jnp.exp(jnp.maximum(-diff, 0.0) * lb))
    xi_f, zeta_f = jnp.exp((ii + 1.0) * lf), jnp.exp((C - 1.0 - ii) * lf)
    xi_b, zeta_b = jnp.exp((C - ii) * lb), jnp.exp(ii * lb)
    dec_f, dec_b = jnp.exp(C * lf), jnp.exp(C * lb)
    n_tiles = kT_ref.shape[0]
    per_tile = TM // C

    def rows(t, c):
        return pl.ds(pl.multiple_of(t * TM + c * C, C), C)

    def chunk_kv(t, c, zeta):
        vz = (v_ref[rows(t, c), :].astype(F32) * zeta).astype(BF16)
        return jnp.dot(kT_ref[t, :, c * C:(c + 1) * C], vz, preferred_element_type=F32)

    def scans(i, states):
        st_f, st_b = states
        tf, tb = i, n_tiles - 1 - i
        kv_f = [chunk_kv(tf, c, zeta_f) for c in range(per_tile)]
        kv_b = [chunk_kv(tb, c, zeta_b) for c in range(per_tile)]
        for c in range(per_tile):
            sf_ref[tf * per_tile + c] = st_f.astype(BF16)
            st_f = st_f * dec_f + kv_f[c]
            cb = per_tile - 1 - c
            sb_ref[tb * per_tile + cb] = st_b.astype(BF16)
            st_b = st_b * dec_b + kv_b[cb]
        return st_f, st_b

    zero = jnp.zeros((C, C), F32)
    lax.fori_loop(0, n_tiles, scans, (zero, zero))

    def outputs(tt, carry):
        chunks = [(tt * RET_OUT_TILES + i, c) for i in range(RET_OUT_TILES) for c in range(per_tile)]
        rs = [rows(t, c) for t, c in chunks]
        qs = [q_ref[r, :] for r in rs]
        a = [jnp.dot(q, kT_ref[t, :, c * C:(c + 1) * C], preferred_element_type=F32)
             for q, (t, c) in zip(qs, chunks)]
        cross = [jnp.dot(q, sf_ref[t * per_tile + c], preferred_element_type=F32) * xi_f
                 + jnp.dot(q, sb_ref[t * per_tile + c], preferred_element_type=F32) * xi_b
                 for q, (t, c) in zip(qs, chunks)]
        a = [(x * decay).astype(BF16) for x in a]
        o = [jnp.dot(x, v_ref[r, :], preferred_element_type=F32) + y
             for x, r, y in zip(a, rs, cross)]
        mu = [jnp.mean(x, axis=-1, keepdims=True) for x in o]
        oc = [x - m for x, m in zip(o, mu)]
        var = [jnp.mean(x * x, axis=-1, keepdims=True) for x in oc]
        for x, s2, r in zip(oc, var, rs):
            y = x * lax.rsqrt(s2 + 1e-5) * gain_ref[...]
            g = g_ref[r, :].astype(F32)
            o_ref[r, :] = (g / (1.0 + jnp.exp(-g)) * y).astype(o_ref.dtype)
        return carry

    lax.fori_loop(0, n_tiles // RET_OUT_TILES, outputs, 0)


def _ret_call(dec, gain, rq, rkT, rv, rg, B, S):
    T = rq.shape[0]
    blk = pl.BlockSpec((S, LANES), lambda b, h: (b, h))
    n_chunks = S // RET_CHUNK
    return pl.pallas_call(
        _ret_kernel, grid=(B, RET_HEADS),
        in_specs=[pl.BlockSpec(memory_space=pltpu.SMEM),
                  pl.BlockSpec((1, LANES), lambda b, h: (0, h)), blk,
                  pl.BlockSpec((S // TM, None, LANES, TM), lambda b, h: (b, h, 0, 0)), blk, blk],
        out_specs=blk,
        out_shape=jax.ShapeDtypeStruct((T, 512), BF16),
        scratch_shapes=[pltpu.VMEM((n_chunks, RET_CHUNK, LANES), BF16),
                        pltpu.VMEM((n_chunks, RET_CHUNK, LANES), BF16)],
        compiler_params=_cparams(2), name="retention",
    )(dec, gain, rq, rkT, rv, rg)


def _merge_kernel(h_ref, u_ref, a_ref, r_ref, c_ref, wg_ref, wb_ref, wo_ref, mn_ref,
                  hn_ref, u2_ref):
    u = u_ref[...]
    merged = None
    for n, br in enumerate((a_ref, r_ref, c_ref)):
        logits = jnp.dot(u, wg_ref[:, n * D_MODEL:(n + 1) * D_MODEL], preferred_element_type=F32)
        gate = 1.0 / (1.0 + jnp.exp(-logits))
        term = gate * jnp.dot(br[...], wb_ref[n], preferred_element_type=F32)
        merged = term if merged is None else merged + term
    hn = h_ref[...] + jnp.dot(merged.astype(BF16), wo_ref[...], preferred_element_type=F32)
    hn_ref[...] = hn
    ms = jnp.mean(hn * hn, axis=-1, keepdims=True)
    u2_ref[...] = (hn * lax.rsqrt(ms + NORM_EPS) * mn_ref[...]).astype(BF16)


def _merge_call(h, u, a, r, c, wg, wb, wo, mlp_norm):
    T = h.shape[0]
    row = lambda w: pl.BlockSpec((TM, w), lambda i: (i, 0))
    return pl.pallas_call(
        _merge_kernel, grid=(T // TM,),
        in_specs=[row(D_MODEL), row(D_MODEL), row(512), row(512), row(512),
                  _resident(wg.shape), _resident(wb.shape), _resident(wo.shape),
                  _resident((1, D_MODEL))],
        out_specs=(row(D_MODEL), row(D_MODEL)),
        out_shape=(jax.ShapeDtypeStruct((T, D_MODEL), F32),
                   jax.ShapeDtypeStruct((T, D_MODEL), BF16)),
        compiler_params=_cparams(1), name="merge",
    )(h, u, a, r, c, wg, wb, wo, mlp_norm)


def _mlp_kernel(h_ref, u_ref, w1_ref, w2_ref, o_ref):
    u = u_ref[...]
    acc = h_ref[...]
    for c in range(D_FF // D_MODEL):
        cs = slice(c * D_MODEL, (c + 1) * D_MODEL)
        m = jnp.maximum(jnp.dot(u, w1_ref[:, cs], preferred_element_type=F32), 0.0)
        acc = acc + jnp.dot((m * m).astype(BF16), w2_ref[cs, :], preferred_element_type=F32)
    o_ref[...] = acc


def _mlp_call(h, u2, w1, w2):
    T = h.shape[0]
    row = pl.BlockSpec((TM, D_MODEL), lambda i: (i, 0))
    return pl.pallas_call(
        _mlp_kernel, grid=(T // TM,),
        in_specs=[row, row, _resident(w1.shape), _resident(w2.shape)],
        out_specs=row,
        out_shape=jax.ShapeDtypeStruct((T, D_MODEL), F32),
        compiler_params=_cparams(1), name="mlp",
    )(h, u2, w1, w2)


def _rope_tables(S):
    pos = jnp.arange(S, dtype=F32)
    lane = jnp.arange(LANES)
    l64 = lane % 64

    inv = 500000.0 ** (-jnp.arange(0, 16, 2, dtype=F32) / 16)
    ang = pos[:, None] * inv[None, :]
    cos, sin = jnp.cos(ang)[:, l64 % 8], jnp.sin(ang)[:, l64 % 8]
    cd = jnp.where(l64 < 16, cos, 1.0)
    ad = jnp.where(l64 < 8, -sin, 0.0)
    bd = jnp.where((l64 >= 8) & (l64 < 16), sin, 0.0)

    inv = 10000.0 ** (-jnp.arange(0, 32, 2, dtype=F32) / 32)
    row_pos = jnp.floor(pos / GRID_W)
    col_pos = pos - row_pos * GRID_W
    p2 = jnp.where((l64 < 32)[None, :], row_pos[:, None], col_pos[:, None])
    ang = p2 * inv[l64 % 16][None, :]
    cg = jnp.cos(ang)
    ag = jnp.where(l64 % 32 < 16, -jnp.sin(ang), 0.0)
    bg = jnp.where(l64 % 32 >= 16, jnp.sin(ang), 0.0)

    inv = 10000.0 ** (-jnp.arange(0, 128, 2, dtype=F32) / 128)
    ang = pos[:, None] * inv[l64][None, :]
    cr = jnp.cos(ang)
    sr = jnp.where(lane < 64, -jnp.sin(ang), jnp.sin(ang))
    return (cd, ad, bd, cg, ag, bg, cr, sr)


def _fast_flag(q_gain, k_gain):
    bound = (HEAD64 ** 0.5 * LOG2E * 1.02) * jnp.max(jnp.abs(q_gain)) * jnp.max(jnp.abs(k_gain))
    return (bound <= FAST_SOFTMAX_BOUND).astype(jnp.int32).reshape(1)


def kernel(x, attn_norm, w_in, diff_q_norm, diff_k_norm, diff_lam_q1, diff_lam_k1, diff_lam_q2, diff_lam_k2, diff_subln, ret_decay_fwd, ret_decay_bwd, ret_group_norm, gqa_q_norm, gqa_k_norm, w_branch, w_out, mlp_norm, w_mlp_in, w_mlp_out):
    B, S, D = x.shape
    T = B * S
    tabs = _rope_tables(S)
    blk = jnp.arange(2 * LANES) // HEAD64
    gmat = jnp.where(blk[:, None] == blk[None, :], 1.0 / HEAD64, 0.0).astype(BF16)
    tile2 = lambda v: jnp.concatenate([v, v]).astype(F32)

    h = x.reshape(T, D)
    for l in range(DEPTH):
        lambda_init = 0.8 - 0.6 * math.exp(-0.3 * l)
        w_attn = w_in[l, :, :ATTN_COLS].astype(BF16)
        w_gate = w_in[l, :, ATTN_COLS:].astype(BF16)
        nrm = jnp.stack([tile2(diff_q_norm[l]), tile2(diff_k_norm[l]),
                         tile2(gqa_q_norm[l]), tile2(gqa_k_norm[l])])
        (u, dqT, dk, dvT, rq, rkT, rv, rg, gqT, gk, gvT) = _proj_call(
            h, attn_norm[l].reshape(1, D).astype(F32), w_attn, gmat, nrm, tabs)

        lamv = jnp.stack([diff_lam_q1[l], diff_lam_k1[l], diff_lam_q2[l], diff_lam_k2[l]]).astype(F32)
        a = _diff_call(_fast_flag(diff_q_norm[l], diff_k_norm[l]), lamv,
                       diff_subln[l].reshape(1, LANES).astype(F32), dqT, dk, dvT, B, S, lambda_init)
        dec = jnp.stack([ret_decay_fwd[l], ret_decay_bwd[l]]).astype(F32)
        r = _ret_call(dec, ret_group_norm[l].reshape(1, 512).astype(F32), rq, rkT, rv, rg, B, S)
        c = _gqa_call(_fast_flag(gqa_q_norm[l], gqa_k_norm[l]), gqT, gk, gvT, B, S)

        h, u2 = _merge_call(h, u, a, r, c, w_gate, w_branch[l].astype(BF16),
                            w_out[l].astype(BF16), mlp_norm[l].reshape(1, D).astype(F32))
        h = _mlp_call(h, u2, w_mlp_in[l].astype(BF16), w_mlp_out[l].astype(BF16))
    return h.reshape(B, S, D)
```

```python
import functools
import math

import jax
import jax.numpy as jnp
from jax import lax
from jax.experimental import pallas as pl
from jax.experimental.pallas import tpu as pltpu

F32 = jnp.float32
BF16 = jnp.bfloat16

D_MODEL = 1024
DEPTH = 2
GRID_W = 64
NORM_EPS = 1e-6
HEAD64 = 64
DIFF_HEADS = 4
RET_HEADS = 4
RET_CHUNK = 128
GQA_Q_HEADS = 8
GQA_GROUP = 4
N_BRANCHES = 3
BRANCH_WIDTH = 512
D_FF = 4 * D_MODEL
ATTN_COLS = 4352
LANES = 128

TM = 512
TK = 4096
NQ = 1024
NC = 256
KV_UNROLL = 2
RET_OUT_TILES = 4
ONES_ROWS = 16
VROWS = LANES + ONES_ROWS
FAST_SOFTMAX_BOUND = 64.0
VMEM_LIMIT = 56 * 1024 * 1024
LOG2E = 1.4426950408889634


def _cparams(n_axes):
    return pltpu.CompilerParams(dimension_semantics=("arbitrary",) * n_axes,
                                vmem_limit_bytes=VMEM_LIMIT)


def _resident(shape):
    nd = len(shape)
    return pl.BlockSpec(shape, lambda *_: (0,) * nd, pipeline_mode=pl.Buffered(1))


def _proj_kernel(h_ref, an_ref, w_ref, g_ref, nrm_ref,
                 cd_ref, ad_ref, bd_ref, cg_ref, ag_ref, bg_ref, cr_ref, sr_ref,
                 u_ref, dqT_ref, dk_ref, dvT_ref, rq_ref, rkT_ref, rv_ref, rg_ref,
                 gqT_ref, gk_ref, gvT_ref):
    x = h_ref[...]
    ms = jnp.mean(x * x, axis=-1, keepdims=True)
    u = (x * lax.rsqrt(ms + NORM_EPS) * an_ref[...]).astype(BF16)
    u_ref[...] = u

    def proj(c0, n):
        return jnp.dot(u, w_ref[:, c0:c0 + n], preferred_element_type=F32)

    gmat = g_ref[...]
    lo = lax.broadcasted_iota(jnp.int32, (1, LANES), 1) < HEAD64
    first_row = lax.broadcasted_iota(jnp.int32, (ONES_ROWS, TM), 0) == 0
    ones_rows = jnp.where(first_row, 1.0, 0.0).astype(BF16)

    def qk_norm(p, gain):
        out = []
        for c0 in range(0, p.shape[1], 2 * LANES):
            y = p[:, c0:c0 + 2 * LANES]
            w = y.shape[1]
            msq = jnp.dot((y * y).astype(BF16), gmat[0:w, 0:w], preferred_element_type=F32)
            y = y * lax.rsqrt(msq + NORM_EPS)
            out += [y[:, i:i + LANES] * gain for i in range(0, w, LANES)]
        return out

    def rope(y, c, a, b, d):
        return y * c + pltpu.roll(y, LANES - d, 1) * a + pltpu.roll(y, d, 1) * b

    def chunks(p):
        return [p[:, i:i + LANES] for i in range(0, p.shape[1], LANES)]

    qscale = HEAD64 ** -0.5 * LOG2E
    kscale = LANES ** -0.5
    cd, ad, bd = cd_ref[...], ad_ref[...], bd_ref[...]
    cg, ag, bg = cg_ref[...], ag_ref[...], bg_ref[...]
    cr, sr = cr_ref[...], sr_ref[...]

    def diff_q(p):
        for hd, y in enumerate(qk_norm(p, nrm_ref[0:1, :])):
            y = rope(y, cd, ad, bd, 8) * qscale
            dqT_ref[hd, :, 0:TM] = jnp.where(lo, y, 0.0).T.astype(BF16)
            dqT_ref[hd, :, TM:2 * TM] = jnp.where(lo, 0.0, y).T.astype(BF16)

    def diff_k(p):
        for hd, y in enumerate(qk_norm(p, nrm_ref[1:2, :])):
            dk_ref[:, hd * LANES:(hd + 1) * LANES] = rope(y, cd, ad, bd, 8).astype(BF16)

    def diff_v(p):
        for hd, y in enumerate(chunks(p)):
            dvT_ref[hd, 0:LANES, :] = y.T.astype(BF16)
            dvT_ref[hd, LANES:VROWS, :] = ones_rows

    def ret_q(p):
        for hd, y in enumerate(chunks(p)):
            rq_ref[:, hd * LANES:(hd + 1) * LANES] = (y * cr + pltpu.roll(y, 64, 1) * sr).astype(BF16)

    def ret_k(p):
        for hd, y in enumerate(chunks(p)):
            rkT_ref[hd] = ((y * cr + pltpu.roll(y, 64, 1) * sr) * kscale).T.astype(BF16)

    def ret_v(p):
        rv_ref[...] = p.astype(BF16)

    def ret_g(p):
        rg_ref[...] = p.astype(BF16)

    def gqa_q(p):
        for c, y in enumerate(qk_norm(p, nrm_ref[2:3, :])):
            y = rope(y, cg, ag, bg, 16) * qscale
            ysw = pltpu.roll(y, 64, 1)
            if c < 2:
                z_even, z_odd = jnp.where(lo, y, 0.0), jnp.where(lo, ysw, 0.0)
            else:
                z_even, z_odd = jnp.where(lo, 0.0, ysw), jnp.where(lo, 0.0, y)
            for hh, z in ((2 * c, z_even), (2 * c + 1, z_odd)):
                zt = z.T.astype(BF16)
                for qb in range(TM // LANES):
                    gqT_ref[qb, :, hh * LANES:(hh + 1) * LANES] = zt[:, qb * LANES:(qb + 1) * LANES]

    def gqa_kv(p):
        (y,) = qk_norm(p[:, 0:LANES], nrm_ref[3:4, :])
        gk_ref[...] = rope(y, cg, ag, bg, 16).astype(BF16)
        gvT_ref[0:LANES, :] = p[:, LANES:2 * LANES].T.astype(BF16)
        gvT_ref[LANES:VROWS, :] = ones_rows

    sections = [(3584, 512, gqa_q), (0, 512, diff_q), (2048, 512, ret_k), (1024, 512, diff_v),
                (512, 512, diff_k), (4096, 256, gqa_kv), (1536, 512, ret_q), (2560, 512, ret_v),
                (3072, 512, ret_g)]
    p_next = proj(*sections[0][:2])
    for i, (_, _, epilogue) in enumerate(sections):
        p_cur = p_next
        if i + 1 < len(sections):
            p_next = proj(*sections[i + 1][:2])
        epilogue(p_cur)


def _proj_call(h, attn_norm, w_attn, gmat, nrm, tabs):
    T = h.shape[0]
    S = tabs[0].shape[0]
    nt = T // TM
    tab_spec = pl.BlockSpec((TM, LANES), lambda i: (i % (S // TM), 0))
    row = lambda w: pl.BlockSpec((TM, w), lambda i: (i, 0))
    out_shape = (
        jax.ShapeDtypeStruct((T, D_MODEL), BF16),
        jax.ShapeDtypeStruct((nt, 4, LANES, 2 * TM), BF16),
        jax.ShapeDtypeStruct((T, 512), BF16),
        jax.ShapeDtypeStruct((nt, 4, VROWS, TM), BF16),
        jax.ShapeDtypeStruct((T, 512), BF16),
        jax.ShapeDtypeStruct((nt, 4, LANES, TM), BF16),
        jax.ShapeDtypeStruct((T, 512), BF16),
        jax.ShapeDtypeStruct((T, 512), BF16),
        jax.ShapeDtypeStruct((T // LANES, LANES, NQ), BF16),
        jax.ShapeDtypeStruct((T, LANES), BF16),
        jax.ShapeDtypeStruct((nt, VROWS, TM), BF16),
    )
    out_specs = (
        row(D_MODEL),
        pl.BlockSpec((None, 4, LANES, 2 * TM), lambda i: (i, 0, 0, 0)),
        row(512),
        pl.BlockSpec((None, 4, VROWS, TM), lambda i: (i, 0, 0, 0)),
        row(512), pl.BlockSpec((None, 4, LANES, TM), lambda i: (i, 0, 0, 0)), row(512), row(512),
        pl.BlockSpec((TM // LANES, LANES, NQ), lambda i: (i, 0, 0)),
        row(LANES),
        pl.BlockSpec((None, VROWS, TM), lambda i: (i, 0, 0)),
    )
    in_specs = [row(D_MODEL), _resident((1, D_MODEL)), _resident(w_attn.shape),
                _resident(gmat.shape), _resident(nrm.shape)] + [tab_spec] * 8
    return pl.pallas_call(
        _proj_kernel, grid=(nt,), in_specs=in_specs, out_specs=out_specs,
        out_shape=out_shape, compiler_params=_cparams(1), name="proj",
    )(h, attn_norm, w_attn, gmat, nrm, *tabs)


def _attn_core(fast_ref, qT_ref, k_ref, vT_ref, m_ref, acc_ref):
    acc_ref[...] = jnp.zeros(acc_ref.shape, F32)
    n_steps = k_ref.shape[0] // TK

    def kv(j):
        k = k_ref[pl.ds(pl.multiple_of(j * TK, TK), TK), :]
        per = TK // TM
        vT = jnp.concatenate([vT_ref[j * per + i] for i in range(per)], axis=-1)
        return k, vT

    @pl.when(fast_ref[0] == 1)
    def _():
        def body(jj, carry):
            pv = lsum = None
            for u in range(KV_UNROLL):
                k, vT = kv(jj * KV_UNROLL + u)
                s = jnp.dot(k, qT_ref[...], preferred_element_type=F32)
                pf = jnp.exp2(s)
                ls = jnp.sum(pf, axis=0, keepdims=True)
                t = jnp.dot(vT[0:LANES], pf.astype(BF16), preferred_element_type=F32)
                pv, lsum = (t, ls) if pv is None else (pv + t, lsum + ls)
            acc_ref[0:LANES, :] += pv
            acc_ref[LANES:LANES + 1, :] += lsum
            return carry

        lax.fori_loop(0, n_steps // KV_UNROLL, body, 0)

    @pl.when(fast_ref[0] == 0)
    def _():
        m_ref[...] = jnp.full(m_ref.shape, -1e30, F32)

        def body(j, carry):
            k, vT = kv(j)
            for c in range(NQ // NC):
                cs = slice(c * NC, (c + 1) * NC)
                s = jnp.dot(k, qT_ref[:, cs], preferred_element_type=F32)
                m_old = m_ref[:, cs]
                m_new = jnp.maximum(m_old, jnp.max(s, axis=0, keepdims=True))
                alpha = jnp.exp2(m_old - m_new)
                p = jnp.exp2(s - m_new).astype(BF16)
                pv = jnp.dot(vT, p, preferred_element_type=F32)
                acc_ref[:, cs] = acc_ref[:, cs] * alpha + pv
                m_ref[:, cs] = m_new
            return carry

        lax.fori_loop(0, n_steps, body, 0)


def _diff_kernel(fast_ref, lam_ref, sub_ref, qT_ref, k_ref, vT_ref, o_ref, m_ref, acc_ref, *,
                 lambda_init):
    _attn_core(fast_ref, qT_ref, k_ref, vT_ref, m_ref, acc_ref)

    lv = lam_ref[...]
    lam = (jnp.exp(jnp.sum(lv[0:1] * lv[1:2], axis=-1, keepdims=True))
           - jnp.exp(jnp.sum(lv[2:3] * lv[3:4], axis=-1, keepdims=True)) + lambda_init)
    tq = NQ // 2
    o1 = acc_ref[0:LANES, 0:tq] / acc_ref[LANES:LANES + 1, 0:tq]
    o2 = acc_ref[0:LANES, tq:NQ] / acc_ref[LANES:LANES + 1, tq:NQ]
    d = o1 - lam * o2
    ms = jnp.mean(d * d, axis=0, keepdims=True)
    dn = d * lax.rsqrt(ms + 1e-5)
    o_ref[...] = (dn.T * (sub_ref[...] * (1.0 - lambda_init))).astype(o_ref.dtype)


_SMEM_SPEC = pl.BlockSpec(memory_space=pltpu.SMEM)
_ATTN_SCRATCH = [pltpu.VMEM((1, NQ), F32), pltpu.VMEM((VROWS, NQ), F32)]


def _diff_call(fast, lamv, subln, dqT, dk, dvT, B, S, lambda_init):
    T = dk.shape[0]
    tq = NQ // 2
    nq = S // tq
    kern = functools.partial(_diff_kernel, lambda_init=lambda_init)
    return pl.pallas_call(
        kern, grid=(B, DIFF_HEADS, nq),
        in_specs=[
            _SMEM_SPEC, _resident(lamv.shape), _resident(subln.shape),
            pl.BlockSpec((None, None, LANES, NQ), lambda b, h, i: (b * nq + i, h, 0, 0)),
            pl.BlockSpec((S, LANES), lambda b, h, i: (b, h)),
            pl.BlockSpec((S // TM, None, VROWS, TM), lambda b, h, i: (b, h, 0, 0)),
        ],
        out_specs=pl.BlockSpec((tq, LANES), lambda b, h, i: (b * nq + i, h)),
        out_shape=jax.ShapeDtypeStruct((T, 512), BF16),
        scratch_shapes=_ATTN_SCRATCH,
        compiler_params=_cparams(3), name="diff_attn",
    )(fast, lamv, subln, dqT, dk, dvT)


def _gqa_kernel(fast_ref, qT_ref, k_ref, vT_ref, o_ref, m_ref, acc_ref):
    _attn_core(fast_ref, qT_ref, k_ref, vT_ref, m_ref, acc_ref)
    for c in range(GQA_Q_HEADS // 2):
        g = (2 * c) // GQA_GROUP
        rows = slice(g * HEAD64, (g + 1) * HEAD64)
        parts = []
        for hh in (2 * c, 2 * c + 1):
            cols = slice(hh * LANES, (hh + 1) * LANES)
            parts.append(acc_ref[rows, cols] / acc_ref[LANES:LANES + 1, cols])
        o_ref[:, c * LANES:(c + 1) * LANES] = jnp.concatenate(parts, axis=0).T.astype(o_ref.dtype)


def _gqa_call(fast, gqT, gk, gvT, B, S):
    T = gk.shape[0]
    tq = NQ // GQA_Q_HEADS
    nq = S // tq
    return pl.pallas_call(
        _gqa_kernel, grid=(B, nq),
        in_specs=[
            _SMEM_SPEC,
            pl.BlockSpec((None, LANES, NQ), lambda b, i: (b * nq + i, 0, 0)),
            pl.BlockSpec((S, LANES), lambda b, i: (b, 0)),
            pl.BlockSpec((S // TM, VROWS, TM), lambda b, i: (b, 0, 0)),
        ],
        out_specs=pl.BlockSpec((tq, 512), lambda b, i: (b * nq + i, 0)),
        out_shape=jax.ShapeDtypeStruct((T, 512), BF16),
        scratch_shapes=_ATTN_SCRATCH,
        compiler_params=_cparams(2), name="gqa_attn",
    )(fast, gqT, gk, gvT)


def _ret_kernel(dec_ref, gain_ref, q_ref, kT_ref, v_ref, g_ref, o_ref, sf_ref, sb_ref):
    C = RET_CHUNK
    hd = pl.program_id(1)
    lf = -jnp.exp(jnp.full((C, C), dec_ref[0, hd], F32))
    lb = -jnp.exp(jnp.full((C, C), dec_ref[1, hd], F32))
    ii = lax.broadcasted_iota(jnp.int32, (C, C), 0).astype(F32)
    jj = lax.broadcasted_iota(jnp.int32, (C, C), 1).astype(F32)
    diff = ii - jj
    decay = jnp.where(diff >= 0, jnp.exp(jnp.maximum(diff, 0.0) * lf),
                      jnp.exp(jnp.maximum(-diff, 0.0) * lb))
    xi_f, zeta_f = jnp.exp((ii + 1.0) * lf), jnp.exp((C - 1.0 - ii) * lf)
    xi_b, zeta_b = jnp.exp((C - ii) * lb), jnp.exp(ii * lb)
    dec_f, dec_b = jnp.exp(C * lf), jnp.exp(C * lb)
    n_tiles = kT_ref.shape[0]
    per_tile = TM // C

    def rows(t, c):
        return pl.ds(pl.multiple_of(t * TM + c * C, C), C)

    def chunk_kv(t, c, zeta):
        vz = (v_ref[rows(t, c), :].astype(F32) * zeta).astype(BF16)
        return jnp.dot(kT_ref[t, :, c * C:(c + 1) * C], vz, preferred_element_type=F32)

    def scans(i, states):
        st_f, st_b = states
        tf, tb = i, n_tiles - 1 - i
        kv_f = [chunk_kv(tf, c, zeta_f) for c in range(per_tile)]
        kv_b = [chunk_kv(tb, c, zeta_b) for c in range(per_tile)]
        for c in range(per_tile):
            sf_ref[tf * per_tile + c] = st_f.astype(BF16)
            st_f = st_f * dec_f + kv_f[c]
            cb = per_tile - 1 - c
            sb_ref[tb * per_tile + cb] = st_b.astype(BF16)
            st_b = st_b * dec_b + kv_b[cb]
        return st_f, st_b

    zero = jnp.zeros((C, C), F32)
    lax.fori_loop(0, n_tiles, scans, (zero, zero))

    def outputs(tt, carry):
        chunks = [(tt * RET_OUT_TILES + i, c) for i in range(RET_OUT_TILES) for c in range(per_tile)]
        rs = [rows(t, c) for t, c in chunks]
        qs = [q_ref[r, :] for r in rs]
        a = [jnp.dot(q, kT_ref[t, :, c * C:(c + 1) * C], preferred_element_type=F32)
             for q, (t, c) in zip(qs, chunks)]
        cross = [jnp.dot(q, sf_ref[t * per_tile + c], preferred_element_type=F32) * xi_f
                 + jnp.dot(q, sb_ref[t * per_tile + c], preferred_element_type=F32) * xi_b
                 for q, (t, c) in zip(qs, chunks)]
        a = [(x * decay).astype(BF16) for x in a]
        o = [jnp.dot(x, v_ref[r, :], preferred_element_type=F32) + y
             for x, r, y in zip(a, rs, cross)]
        mu = [jnp.mean(x, axis=-1, keepdims=True) for x in o]
        oc = [x - m for x, m in zip(o, mu)]
        var = [jnp.mean(x * x, axis=-1, keepdims=True) for x in oc]
        for x, s2, r in zip(oc, var, rs):
            y = x * lax.rsqrt(s2 + 1e-5) * gain_ref[...]
            g = g_ref[r, :].astype(F32)
            o_ref[r, :] = (g / (1.0 + jnp.exp(-g)) * y).astype(o_ref.dtype)
        return carry

    lax.fori_loop(0, n_tiles // RET_OUT_TILES, outputs, 0)


def _ret_call(dec, gain, rq, rkT, rv, rg, B, S):
    T = rq.shape[0]
    blk = pl.BlockSpec((S, LANES), lambda b, h: (b, h))
    n_chunks = S // RET_CHUNK
    return pl.pallas_call(
        _ret_kernel, grid=(B, RET_HEADS),
        in_specs=[pl.BlockSpec(memory_space=pltpu.SMEM),
                  pl.BlockSpec((1, LANES), lambda b, h: (0, h)), blk,
                  pl.BlockSpec((S // TM, None, LANES, TM), lambda b, h: (b, h, 0, 0)), blk, blk],
        out_specs=blk,
        out_shape=jax.ShapeDtypeStruct((T, 512), BF16),
        scratch_shapes=[pltpu.VMEM((n_chunks, RET_CHUNK, LANES), BF16),
                        pltpu.VMEM((n_chunks, RET_CHUNK, LANES), BF16)],
        compiler_params=_cparams(2), name="retention",
    )(dec, gain, rq, rkT, rv, rg)


def _merge_kernel(h_ref, u_ref, a_ref, r_ref, c_ref, wg_ref, wb_ref, wo_ref, mn_ref,
                  hn_ref, u2_ref):
    u = u_ref[...]
    merged = None
    for n, br in enumerate((a_ref, r_ref, c_ref)):
        logits = jnp.dot(u, wg_ref[:, n * D_MODEL:(n + 1) * D_MODEL], preferred_element_type=F32)
        gate = 1.0 / (1.0 + jnp.exp(-logits))
        term = gate * jnp.dot(br[...], wb_ref[n], preferred_element_type=F32)
        merged = term if merged is None else merged + term
    hn = h_ref[...] + jnp.dot(merged.astype(BF16), wo_ref[...], preferred_element_type=F32)
    hn_ref[...] = hn
    ms = jnp.mean(hn * hn, axis=-1, keepdims=True)
    u2_ref[...] = (hn * lax.rsqrt(ms + NORM_EPS) * mn_ref[...]).astype(BF16)


def _merge_call(h, u, a, r, c, wg, wb, wo, mlp_norm):
    T = h.shape[0]
    row = lambda w: pl.BlockSpec((TM, w), lambda i: (i, 0))
    return pl.pallas_call(
        _merge_kernel, grid=(T // TM,),
        in_specs=[row(D_MODEL), row(D_MODEL), row(512), row(512), row(512),
                  _resident(wg.shape), _resident(wb.shape), _resident(wo.shape),
                  _resident((1, D_MODEL))],
        out_specs=(row(D_MODEL), row(D_MODEL)),
        out_shape=(jax.ShapeDtypeStruct((T, D_MODEL), F32),
                   jax.ShapeDtypeStruct((T, D_MODEL), BF16)),
        compiler_params=_cparams(1), name="merge",
    )(h, u, a, r, c, wg, wb, wo, mlp_norm)


def _mlp_kernel(h_ref, u_ref, w1_ref, w2_ref, o_ref):
    u = u_ref[...]
    acc = h_ref[...]
    for c in range(D_FF // D_MODEL):
        cs = slice(c * D_MODEL, (c + 1) * D_MODEL)
        m = jnp.maximum(jnp.dot(u, w1_ref[:, cs], preferred_element_type=F32), 0.0)
        acc = acc + jnp.dot((m * m).astype(BF16), w2_ref[cs, :], preferred_element_type=F32)
    o_ref[...] = acc


def _mlp_call(h, u2, w1, w2):
    T = h.shape[0]
    row = pl.BlockSpec((TM, D_MODEL), lambda i: (i, 0))
    return pl.pallas_call(
        _mlp_kernel, grid=(T // TM,),
        in_specs=[row, row, _resident(w1.shape), _resident(w2.shape)],
        out_specs=row,
        out_shape=jax.ShapeDtypeStruct((T, D_MODEL), F32),
        compiler_params=_cparams(1), name="mlp",
    )(h, u2, w1, w2)


def _rope_tables(S):
    pos = jnp.arange(S, dtype=F32)
    lane = jnp.arange(LANES)
    l64 = lane % 64

    inv = 500000.0 ** (-jnp.arange(0, 16, 2, dtype=F32) / 16)
    ang = pos[:, None] * inv[None, :]
    cos, sin = jnp.cos(ang)[:, l64 % 8], jnp.sin(ang)[:, l64 % 8]
    cd = jnp.where(l64 < 16, cos, 1.0)
    ad = jnp.where(l64 < 8, -sin, 0.0)
    bd = jnp.where((l64 >= 8) & (l64 < 16), sin, 0.0)

    inv = 10000.0 ** (-jnp.arange(0, 32, 2, dtype=F32) / 32)
    row_pos = jnp.floor(pos / GRID_W)
    col_pos = pos - row_pos * GRID_W
    p2 = jnp.where((l64 < 32)[None, :], row_pos[:, None], col_pos[:, None])
    ang = p2 * inv[l64 % 16][None, :]
    cg = jnp.cos(ang)
    ag = jnp.where(l64 % 32 < 16, -jnp.sin(ang), 0.0)
    bg = jnp.where(l64 % 32 >= 16, jnp.sin(ang), 0.0)

    inv = 10000.0 ** (-jnp.arange(0, 128, 2, dtype=F32) / 128)
    ang = pos[:, None] * inv[l64][None, :]
    cr = jnp.cos(ang)
    sr = jnp.where(lane < 64, -jnp.sin(ang), jnp.sin(ang))
    return (cd, ad, bd, cg, ag, bg, cr, sr)


def _fast_flag(q_gain, k_gain):
    bound = (HEAD64 ** 0.5 * LOG2E * 1.02) * jnp.max(jnp.abs(q_gain)) * jnp.max(jnp.abs(k_gain))
    return (bound <= FAST_SOFTMAX_BOUND).astype(jnp.int32).reshape(1)


def kernel(x, attn_norm, w_in, diff_q_norm, diff_k_norm, diff_lam_q1, diff_lam_k1, diff_lam_q2, diff_lam_k2, diff_subln, ret_decay_fwd, ret_decay_bwd, ret_group_norm, gqa_q_norm, gqa_k_norm, w_branch, w_out, mlp_norm, w_mlp_in, w_mlp_out):
    B, S, D = x.shape
    T = B * S
    tabs = _rope_tables(S)
    blk = jnp.arange(2 * LANES) // HEAD64
    gmat = jnp.where(blk[:, None] == blk[None, :], 1.0 / HEAD64, 0.0).astype(BF16)
    tile2 = lambda v: jnp.concatenate([v, v]).astype(F32)

    h = x.reshape(T, D)
    for l in range(DEPTH):
        lambda_init = 0.8 - 0.6 * math.exp(-0.3 * l)
        w_attn = w_in[l, :, :ATTN_COLS].astype(BF16)
        w_gate = w_in[l, :, ATTN_COLS:].astype(BF16)
        nrm = jnp.stack([tile2(diff_q_norm[l]), tile2(diff_k_norm[l]),
                         tile2(gqa_q_norm[l]), tile2(gqa_k_norm[l])])
        (u, dqT, dk, dvT, rq, rkT, rv, rg, gqT, gk, gvT) = _proj_call(
            h, attn_norm[l].reshape(1, D).astype(F32), w_attn, gmat, nrm, tabs)

        lamv = jnp.stack([diff_lam_q1[l], diff_lam_k1[l], diff_lam_q2[l], diff_lam_k2[l]]).astype(F32)
        a = _diff_call(_fast_flag(diff_q_norm[l], diff_k_norm[l]), lamv,
                       diff_subln[l].reshape(1, LANES).astype(F32), dqT, dk, dvT, B, S, lambda_init)
        dec = jnp.stack([ret_decay_fwd[l], ret_decay_bwd[l]]).astype(F32)
        r = _ret_call(dec, ret_group_norm[l].reshape(1, 512).astype(F32), rq, rkT, rv, rg, B, S)
        c = _gqa_call(_fast_flag(gqa_q_norm[l], gqa_k_norm[l]), gqT, gk, gvT, B, S)

        h, u2 = _merge_call(h, u, a, r, c, w_gate, w_branch[l].astype(BF16),
                            w_out[l].astype(BF16), mlp_norm[l].reshape(1, D).astype(F32))
        h = _mlp_call(h, u2, w_mlp_in[l].astype(BF16), w_mlp_out[l].astype(BF16))
    return h.reshape(B, S, D)
```

```python
import functools
import math

import jax
import jax.numpy as jnp
from jax import lax
from jax.experimental import pallas as pl
from jax.experimental.pallas import tpu as pltpu

F32 = jnp.float32
BF16 = jnp.bfloat16

D_MODEL = 1024
DEPTH = 2
GRID_W = 64
NORM_EPS = 1e-6
HEAD64 = 64
DIFF_HEADS = 4
RET_HEADS = 4
RET_CHUNK = 128
GQA_Q_HEADS = 8
GQA_GROUP = 4
N_BRANCHES = 3
BRANCH_WIDTH = 512
D_FF = 4 * D_MODEL
ATTN_COLS = 4352
LANES = 128

TM = 512
TK = 8192
NQ = 1024
NC = 256
KV_UNROLL = 1
RET_OUT_TILES = 4
ONES_ROWS = 16
VROWS = LANES + ONES_ROWS
FAST_SOFTMAX_BOUND = 64.0
VMEM_LIMIT = 56 * 1024 * 1024
LOG2E = 1.4426950408889634


def _cparams(n_axes):
    return pltpu.CompilerParams(dimension_semantics=("arbitrary",) * n_axes,
                                vmem_limit_bytes=VMEM_LIMIT)


def _resident(shape):
    nd = len(shape)
    return pl.BlockSpec(shape, lambda *_: (0,) * nd, pipeline_mode=pl.Buffered(1))


def _proj_kernel(h_ref, an_ref, w_ref, g_ref, nrm_ref,
                 cd_ref, ad_ref, bd_ref, cg_ref, ag_ref, bg_ref, cr_ref, sr_ref,
                 u_ref, dqT_ref, dk_ref, dvT_ref, rq_ref, rkT_ref, rv_ref, rg_ref,
                 gqT_ref, gk_ref, gvT_ref):
    x = h_ref[...]
    ms = jnp.mean(x * x, axis=-1, keepdims=True)
    u = (x * lax.rsqrt(ms + NORM_EPS) * an_ref[...]).astype(BF16)
    u_ref[...] = u

    def proj(c0, n):
        return jnp.dot(u, w_ref[:, c0:c0 + n], preferred_element_type=F32)

    gmat = g_ref[...]
    lo = lax.broadcasted_iota(jnp.int32, (1, LANES), 1) < HEAD64
    first_row = lax.broadcasted_iota(jnp.int32, (ONES_ROWS, TM), 0) == 0
    ones_rows = jnp.where(first_row, 1.0, 0.0).astype(BF16)

    def qk_norm(p, gain):
        out = []
        for c0 in range(0, p.shape[1], 2 * LANES):
            y = p[:, c0:c0 + 2 * LANES]
            w = y.shape[1]
            msq = jnp.dot((y * y).astype(BF16), gmat[0:w, 0:w], preferred_element_type=F32)
            y = y * lax.rsqrt(msq + NORM_EPS)
            out += [y[:, i:i + LANES] * gain for i in range(0, w, LANES)]
        return out

    def rope(y, c, a, b, d):
        return y * c + pltpu.roll(y, LANES - d, 1) * a + pltpu.roll(y, d, 1) * b

    def chunks(p):
        return [p[:, i:i + LANES] for i in range(0, p.shape[1], LANES)]

    qscale = HEAD64 ** -0.5 * LOG2E
    kscale = LANES ** -0.5
    cd, ad, bd = cd_ref[...], ad_ref[...], bd_ref[...]
    cg, ag, bg = cg_ref[...], ag_ref[...], bg_ref[...]
    cr, sr = cr_ref[...], sr_ref[...]

    def diff_q(p):
        for hd, y in enumerate(qk_norm(p, nrm_ref[0:1, :])):
            y = rope(y, cd, ad, bd, 8) * qscale
            dqT_ref[hd, :, 0:TM] = jnp.where(lo, y, 0.0).T.astype(BF16)
            dqT_ref[hd, :, TM:2 * TM] = jnp.where(lo, 0.0, y).T.astype(BF16)

    def diff_k(p):
        for hd, y in enumerate(qk_norm(p, nrm_ref[1:2, :])):
            dk_ref[:, hd * LANES:(hd + 1) * LANES] = rope(y, cd, ad, bd, 8).astype(BF16)

    def diff_v(p):
        for hd, y in enumerate(chunks(p)):
            dvT_ref[hd, 0:LANES, :] = y.T.astype(BF16)
            dvT_ref[hd, LANES:VROWS, :] = ones_rows

    def ret_q(p):
        for hd, y in enumerate(chunks(p)):
            rq_ref[:, hd * LANES:(hd + 1) * LANES] = (y * cr + pltpu.roll(y, 64, 1) * sr).astype(BF16)

    def ret_k(p):
        for hd, y in enumerate(chunks(p)):
            rkT_ref[hd] = ((y * cr + pltpu.roll(y, 64, 1) * sr) * kscale).T.astype(BF16)

    def ret_v(p):
        rv_ref[...] = p.astype(BF16)

    def ret_g(p):
        rg_ref[...] = p.astype(BF16)

    def gqa_q(p):
        for c, y in enumerate(qk_norm(p, nrm_ref[2:3, :])):
            y = rope(y, cg, ag, bg, 16) * qscale
            ysw = pltpu.roll(y, 64, 1)
            if c < 2:
                z_even, z_odd = jnp.where(lo, y, 0.0), jnp.where(lo, ysw, 0.0)
            else:
                z_even, z_odd = jnp.where(lo, 0.0, ysw), jnp.where(lo, 0.0, y)
            for hh, z in ((2 * c, z_even), (2 * c + 1, z_odd)):
                zt = z.T.astype(BF16)
                for qb in range(TM // LANES):
                    gqT_ref[qb, :, hh * LANES:(hh + 1) * LANES] = zt[:, qb * LANES:(qb + 1) * LANES]

    def gqa_kv(p):
        (y,) = qk_norm(p[:, 0:LANES], nrm_ref[3:4, :])
        gk_ref[...] = rope(y, cg, ag, bg, 16).astype(BF16)
        gvT_ref[0:LANES, :] = p[:, LANES:2 * LANES].T.astype(BF16)
        gvT_ref[LANES:VROWS, :] = ones_rows

    sections = [(3584, 512, gqa_q), (0, 512, diff_q), (2048, 512, ret_k), (1024, 512, diff_v),
                (512, 512, diff_k), (4096, 256, gqa_kv), (1536, 512, ret_q), (2560, 512, ret_v),
                (3072, 512, ret_g)]
    p_next = proj(*sections[0][:2])
    for i, (_, _, epilogue) in enumerate(sections):
        p_cur = p_next
        if i + 1 < len(sections):
            p_next = proj(*sections[i + 1][:2])
        epilogue(p_cur)


def _proj_call(h, attn_norm, w_attn, gmat, nrm, tabs):
    T = h.shape[0]
    S = tabs[0].shape[0]
    nt = T // TM
    tab_spec = pl.BlockSpec((TM, LANES), lambda i: (i % (S // TM), 0))
    row = lambda w: pl.BlockSpec((TM, w), lambda i: (i, 0))
    out_shape = (
        jax.ShapeDtypeStruct((T, D_MODEL), BF16),
        jax.ShapeDtypeStruct((nt, 4, LANES, 2 * TM), BF16),
        jax.ShapeDtypeStruct((T, 512), BF16),
        jax.ShapeDtypeStruct((nt, 4, VROWS, TM), BF16),
        jax.ShapeDtypeStruct((T, 512), BF16),
        jax.ShapeDtypeStruct((nt, 4, LANES, TM), BF16),
        jax.ShapeDtypeStruct((T, 512), BF16),
        jax.ShapeDtypeStruct((T, 512), BF16),
        jax.ShapeDtypeStruct((T // LANES, LANES, NQ), BF16),
        jax.ShapeDtypeStruct((T, LANES), BF16),
        jax.ShapeDtypeStruct((nt, VROWS, TM), BF16),
    )
    out_specs = (
        row(D_MODEL),
        pl.BlockSpec((None, 4, LANES, 2 * TM), lambda i: (i, 0, 0, 0)),
        row(512),
        pl.BlockSpec((None, 4, VROWS, TM), lambda i: (i, 0, 0, 0)),
        row(512), pl.BlockSpec((None, 4, LANES, TM), lambda i: (i, 0, 0, 0)), row(512), row(512),
        pl.BlockSpec((TM // LANES, LANES, NQ), lambda i: (i, 0, 0)),
        row(LANES),
        pl.BlockSpec((None, VROWS, TM), lambda i: (i, 0, 0)),
    )
    in_specs = [row(D_MODEL), _resident((1, D_MODEL)), _resident(w_attn.shape),
                _resident(gmat.shape), _resident(nrm.shape)] + [tab_spec] * 8
    return pl.pallas_call(
        _proj_kernel, grid=(nt,), in_specs=in_specs, out_specs=out_specs,
        out_shape=out_shape, compiler_params=_cparams(1), name="proj",
    )(h, attn_norm, w_attn, gmat, nrm, *tabs)


def _attn_core(fast_ref, qT_ref, k_ref, vT_ref, m_ref, acc_ref):
    acc_ref[...] = jnp.zeros(acc_ref.shape, F32)
    n_steps = k_ref.shape[0] // TK

    def kv(j):
        k = k_ref[pl.ds(pl.multiple_of(j * TK, TK), TK), :]
        per = TK // TM
        vT = jnp.concatenate([vT_ref[j * per + i] for i in range(per)], axis=-1)
        return k, vT

    @pl.when(fast_ref[0] == 1)
    def _():
        def body(jj, carry):
            pv = lsum = None
            for u in range(KV_UNROLL):
                k, vT = kv(jj * KV_UNROLL + u)
                s = jnp.dot(k, qT_ref[...], preferred_element_type=F32)
                pf = jnp.exp2(s)
                ls = jnp.sum(pf, axis=0, keepdims=True)
                t = jnp.dot(vT[0:LANES], pf.astype(BF16), preferred_element_type=F32)
                pv, lsum = (t, ls) if pv is None else (pv + t, lsum + ls)
            acc_ref[0:LANES, :] += pv
            acc_ref[LANES:LANES + 1, :] += lsum
            return carry

        lax.fori_loop(0, n_steps // KV_UNROLL, body, 0)

    @pl.when(fast_ref[0] == 0)
    def _():
        m_ref[...] = jnp.full(m_ref.shape, -1e30, F32)

        def body(j, carry):
            k, vT = kv(j)
            for c in range(NQ // NC):
                cs = slice(c * NC, (c + 1) * NC)
                s = jnp.dot(k, qT_ref[:, cs], preferred_element_type=F32)
                m_old = m_ref[:, cs]
                m_new = jnp.maximum(m_old, jnp.max(s, axis=0, keepdims=True))
                alpha = jnp.exp2(m_old - m_new)
                p = jnp.exp2(s - m_new).astype(BF16)
                pv = jnp.dot(vT, p, preferred_element_type=F32)
                acc_ref[:, cs] = acc_ref[:, cs] * alpha + pv
                m_ref[:, cs] = m_new
            return carry

        lax.fori_loop(0, n_steps, body, 0)


def _diff_kernel(fast_ref, lam_ref, sub_ref, qT_ref, k_ref, vT_ref, o_ref, m_ref, acc_ref, *,
                 lambda_init):
    _attn_core(fast_ref, qT_ref, k_ref, vT_ref, m_ref, acc_ref)

    lv = lam_ref[...]
    lam = (jnp.exp(jnp.sum(lv[0:1] * lv[1:2], axis=-1, keepdims=True))
           - jnp.exp(jnp.sum(lv[2:3] * lv[3:4], axis=-1, keepdims=True)) + lambda_init)
    tq = NQ // 2
    o1 = acc_ref[0:LANES, 0:tq] / acc_ref[LANES:LANES + 1, 0:tq]
    o2 = acc_ref[0:LANES, tq:NQ] / acc_ref[LANES:LANES + 1, tq:NQ]
    d = o1 - lam * o2
    ms = jnp.mean(d * d, axis=0, keepdims=True)
    dn = d * lax.rsqrt(ms + 1e-5)
    o_ref[...] = (dn.T * (sub_ref[...] * (1.0 - lambda_init))).astype(o_ref.dtype)


_SMEM_SPEC = pl.BlockSpec(memory_space=pltpu.SMEM)
_ATTN_SCRATCH = [pltpu.VMEM((1, NQ), F32), pltpu.VMEM((VROWS, NQ), F32)]


def _diff_call(fast, lamv, subln, dqT, dk, dvT, B, S, lambda_init):
    T = dk.shape[0]
    tq = NQ // 2
    nq = S // tq
    kern = functools.partial(_diff_kernel, lambda_init=lambda_init)
    return pl.pallas_call(
        kern, grid=(B, DIFF_HEADS, nq),
        in_specs=[
            _SMEM_SPEC, _resident(lamv.shape), _resident(subln.shape),
            pl.BlockSpec((None, None, LANES, NQ), lambda b, h, i: (b * nq + i, h, 0, 0)),
            pl.BlockSpec((S, LANES), lambda b, h, i: (b, h)),
            pl.BlockSpec((S // TM, None, VROWS, TM), lambda b, h, i: (b, h, 0, 0)),
        ],
        out_specs=pl.BlockSpec((tq, LANES), lambda b, h, i: (b * nq + i, h)),
        out_shape=jax.ShapeDtypeStruct((T, 512), BF16),
        scratch_shapes=_ATTN_SCRATCH,
        compiler_params=_cparams(3), name="diff_attn",
    )(fast, lamv, subln, dqT, dk, dvT)


def _gqa_kernel(fast_ref, qT_ref, k_ref, vT_ref, o_ref, m_ref, acc_ref):
    _attn_core(fast_ref, qT_ref, k_ref, vT_ref, m_ref, acc_ref)
    for c in range(GQA_Q_HEADS // 2):
        g = (2 * c) // GQA_GROUP
        rows = slice(g * HEAD64, (g + 1) * HEAD64)
        parts = []
        for hh in (2 * c, 2 * c + 1):
            cols = slice(hh * LANES, (hh + 1) * LANES)
            parts.append(acc_ref[rows, cols] / acc_ref[LANES:LANES + 1, cols])
        o_ref[:, c * LANES:(c + 1) * LANES] = jnp.concatenate(parts, axis=0).T.astype(o_ref.dtype)


def _gqa_call(fast, gqT, gk, gvT, B, S):
    T = gk.shape[0]
    tq = NQ // GQA_Q_HEADS
    nq = S // tq
    return pl.pallas_call(
        _gqa_kernel, grid=(B, nq),
        in_specs=[
            _SMEM_SPEC,
            pl.BlockSpec((None, LANES, NQ), lambda b, i: (b * nq + i, 0, 0)),
            pl.BlockSpec((S, LANES), lambda b, i: (b, 0)),
            pl.BlockSpec((S // TM, VROWS, TM), lambda b, i: (b, 0, 0)),
        ],
        out_specs=pl.BlockSpec((tq, 512), lambda b, i: (b * nq + i, 0)),
        out_shape=jax.ShapeDtypeStruct((T, 512), BF16),
        scratch_shapes=_ATTN_SCRATCH,
        compiler_params=_cparams(2), name="gqa_attn",
    )(fast, gqT, gk, gvT)


def _ret_kernel(dec_ref, gain_ref, q_ref, kT_ref, v_ref, g_ref, o_ref, sf_ref, sb_ref):
    C = RET_CHUNK
    hd = pl.program_id(1)
    lf = -jnp.exp(jnp.full((C, C), dec_ref[0, hd], F32))
    lb = -jnp.exp(jnp.full((C, C), dec_ref[1, hd], F32))
    ii = lax.broadcasted_iota(jnp.int32, (C, C), 0).astype(F32)
    jj = lax.broadcasted_iota(jnp.int32, (C, C), 1).astype(F32)
    diff = ii - jj
    decay = jnp.where(diff >= 0, jnp.exp(jnp.maximum(diff, 0.0) * lf),
                      jnp.exp(jnp.maximum(-diff, 0.0) * lb))
    xi_f, zeta_f = jnp.exp((ii + 1.0) * lf), jnp.exp((C - 1.0 - ii) * lf)
    xi_b, zeta_b = jnp.exp((C - ii) * lb), jnp.exp(ii * lb)
    dec_f, dec_b = jnp.exp(C * lf), jnp.exp(C * lb)
    n_tiles = kT_ref.shape[0]
    per_tile = TM // C

    def rows(t, c):
        return pl.ds(pl.multiple_of(t * TM + c * C, C), C)

    def chunk_kv(t, c, zeta):
        vz = (v_ref[rows(t, c), :].astype(F32) * zeta).astype(BF16)
        return jnp.dot(kT_ref[t, :, c * C:(c + 1) * C], vz, preferred_element_type=F32)

    def scans(i, states):
        st_f, st_b = states
        tf, tb = i, n_tiles - 1 - i
        kv_f = [chunk_kv(tf, c, zeta_f) for c in range(per_tile)]
        kv_b = [chunk_kv(tb, c, zeta_b) for c in range(per_tile)]
        for c in range(per_tile):
            sf_ref[tf * per_tile + c] = st_f.astype(BF16)
            st_f = st_f * dec_f + kv_f[c]
            cb = per_tile - 1 - c
            sb_ref[tb * per_tile + cb] = st_b.astype(BF16)
            st_b = st_b * dec_b + kv_b[cb]
        return st_f, st_b

    zero = jnp.zeros((C, C), F32)
    lax.fori_loop(0, n_tiles, scans, (zero, zero))

    def outputs(tt, carry):
        chunks = [(tt * RET_OUT_TILES + i, c) for i in range(RET_OUT_TILES) for c in range(per_tile)]
        rs = [rows(t, c) for t, c in chunks]
        qs = [q_ref[r, :] for r in rs]
        a = [jnp.dot(q, kT_ref[t, :, c * C:(c + 1) * C], preferred_element_type=F32)
             for q, (t, c) in zip(qs, chunks)]
        cross = [jnp.dot(q, sf_ref[t * per_tile + c], preferred_element_type=F32) * xi_f
                 + jnp.dot(q, sb_ref[t * per_tile + c], preferred_element_type=F32) * xi_b
                 for q, (t, c) in zip(qs, chunks)]
        a = [(x * decay).astype(BF16) for x in a]
        o = [jnp.dot(x, v_ref[r, :], preferred_element_type=F32) + y
             for x, r, y in zip(a, rs, cross)]
        mu = [jnp.mean(x, axis=-1, keepdims=True) for x in o]
        oc = [x - m for x, m in zip(o, mu)]
        var = [jnp.mean(x * x, axis=-1, keepdims=True) for x in oc]
        for x, s2, r in zip(oc, var, rs):
            y = x * lax.rsqrt(s2 + 1e-5) * gain_ref[...]
            g = g_ref[r, :].astype(F32)
            o_ref[r, :] = (g / (1.0 + jnp.exp(-g)) * y).astype(o_ref.dtype)
        return carry

    lax.fori_loop(0, n_tiles // RET_OUT_TILES, outputs, 0)


def _ret_call(dec, gain, rq, rkT, rv, rg, B, S):
    T = rq.shape[0]
    blk = pl.BlockSpec((S, LANES), lambda b, h: (b, h))
    n_chunks = S // RET_CHUNK
    return pl.pallas_call(
        _ret_kernel, grid=(B, RET_HEADS),
        in_specs=[pl.BlockSpec(memory_space=pltpu.SMEM),
                  pl.BlockSpec((1, LANES), lambda b, h: (0, h)), blk,
                  pl.BlockSpec((S // TM, None, LANES, TM), lambda b, h: (b, h, 0, 0)), blk, blk],
        out_specs=blk,
        out_shape=jax.ShapeDtypeStruct((T, 512), BF16),
        scratch_shapes=[pltpu.VMEM((n_chunks, RET_CHUNK, LANES), BF16),
                        pltpu.VMEM((n_chunks, RET_CHUNK, LANES), BF16)],
        compiler_params=_cparams(2), name="retention",
    )(dec, gain, rq, rkT, rv, rg)


def _merge_kernel(h_ref, u_ref, a_ref, r_ref, c_ref, wg_ref, wb_ref, wo_ref, mn_ref,
                  hn_ref, u2_ref):
    u = u_ref[...]
    merged = None
    for n, br in enumerate((a_ref, r_ref, c_ref)):
        logits = jnp.dot(u, wg_ref[:, n * D_MODEL:(n + 1) * D_MODEL], preferred_element_type=F32)
        gate = 1.0 / (1.0 + jnp.exp(-logits))
        term = gate * jnp.dot(br[...], wb_ref[n], preferred_element_type=F32)
        merged = term if merged is None else merged + term
    hn = h_ref[...] + jnp.dot(merged.astype(BF16), wo_ref[...], preferred_element_type=F32)
    hn_ref[...] = hn
    ms = jnp.mean(hn * hn, axis=-1, keepdims=True)
    u2_ref[...] = (hn * lax.rsqrt(ms + NORM_EPS) * mn_ref[...]).astype(BF16)


def _merge_call(h, u, a, r, c, wg, wb, wo, mlp_norm):
    T = h.shape[0]
    row = lambda w: pl.BlockSpec((TM, w), lambda i: (i, 0))
    return pl.pallas_call(
        _merge_kernel, grid=(T // TM,),
        in_specs=[row(D_MODEL), row(D_MODEL), row(512), row(512), row(512),
                  _resident(wg.shape), _resident(wb.shape), _resident(wo.shape),
                  _resident((1, D_MODEL))],
        out_specs=(row(D_MODEL), row(D_MODEL)),
        out_shape=(jax.ShapeDtypeStruct((T, D_MODEL), F32),
                   jax.ShapeDtypeStruct((T, D_MODEL), BF16)),
        compiler_params=_cparams(1), name="merge",
    )(h, u, a, r, c, wg, wb, wo, mlp_norm)


def _mlp_kernel(h_ref, u_ref, w1_ref, w2_ref, o_ref):
    u = u_ref[...]
    acc = h_ref[...]
    for c in range(D_FF // D_MODEL):
        cs = slice(c * D_MODEL, (c + 1) * D_MODEL)
        m = jnp.maximum(jnp.dot(u, w1_ref[:, cs], preferred_element_type=F32), 0.0)
        acc = acc + jnp.dot((m * m).astype(BF16), w2_ref[cs, :], preferred_element_type=F32)
    o_ref[...] = acc


def _mlp_call(h, u2, w1, w2):
    T = h.shape[0]
    row = pl.BlockSpec((TM, D_MODEL), lambda i: (i, 0))
    return pl.pallas_call(
        _mlp_kernel, grid=(T // TM,),
        in_specs=[row, row, _resident(w1.shape), _resident(w2.shape)],
        out_specs=row,
        out_shape=jax.ShapeDtypeStruct((T, D_MODEL), F32),
        compiler_params=_cparams(1), name="mlp",
    )(h, u2, w1, w2)


def _rope_tables(S):
    pos = jnp.arange(S, dtype=F32)
    lane = jnp.arange(LANES)
    l64 = lane % 64

    inv = 500000.0 ** (-jnp.arange(0, 16, 2, dtype=F32) / 16)
    ang = pos[:, None] * inv[None, :]
    cos, sin = jnp.cos(ang)[:, l64 % 8], jnp.sin(ang)[:, l64 % 8]
    cd = jnp.where(l64 < 16, cos, 1.0)
    ad = jnp.where(l64 < 8, -sin, 0.0)
    bd = jnp.where((l64 >= 8) & (l64 < 16), sin, 0.0)

    inv = 10000.0 ** (-jnp.arange(0, 32, 2, dtype=F32) / 32)
    row_pos = jnp.floor(pos / GRID_W)
    col_pos = pos - row_pos * GRID_W
    p2 = jnp.where((l64 < 32)[None, :], row_pos[:, None], col_pos[:, None])
    ang = p2 * inv[l64 % 16][None, :]
    cg = jnp.cos(ang)
    ag = jnp.where(l64 % 32 < 16, -jnp.sin(ang), 0.0)
    bg = jnp.where(l64 % 32 >= 16, jnp.sin(ang), 0.0)

    inv = 10000.0 ** (-jnp.arange(0, 128, 2, dtype=F32) / 128)
    ang = pos[:, None] * inv[l64][None, :]
    cr = jnp.cos(ang)
    sr = jnp.where(lane < 64, -jnp.sin(ang), jnp.sin(ang))
    return (cd, ad, bd, cg, ag, bg, cr, sr)


def _fast_flag(q_gain, k_gain):
    bound = (HEAD64 ** 0.5 * LOG2E * 1.02) * jnp.max(jnp.abs(q_gain)) * jnp.max(jnp.abs(k_gain))
    return (bound <= FAST_SOFTMAX_BOUND).astype(jnp.int32).reshape(1)


def kernel(x, attn_norm, w_in, diff_q_norm, diff_k_norm, diff_lam_q1, diff_lam_k1, diff_lam_q2, diff_lam_k2, diff_subln, ret_decay_fwd, ret_decay_bwd, ret_group_norm, gqa_q_norm, gqa_k_norm, w_branch, w_out, mlp_norm, w_mlp_in, w_mlp_out):
    B, S, D = x.shape
    T = B * S
    tabs = _rope_tables(S)
    blk = jnp.arange(2 * LANES) // HEAD64
    gmat = jnp.where(blk[:, None] == blk[None, :], 1.0 / HEAD64, 0.0).astype(BF16)
    tile2 = lambda v: jnp.concatenate([v, v]).astype(F32)

    h = x.reshape(T, D)
    for l in range(DEPTH):
        lambda_init = 0.8 - 0.6 * math.exp(-0.3 * l)
        w_attn = w_in[l, :, :ATTN_COLS].astype(BF16)
        w_gate = w_in[l, :, ATTN_COLS:].astype(BF16)
        nrm = jnp.stack([tile2(diff_q_norm[l]), tile2(diff_k_norm[l]),
                         tile2(gqa_q_norm[l]), tile2(gqa_k_norm[l])])
        (u, dqT, dk, dvT, rq, rkT, rv, rg, gqT, gk, gvT) = _proj_call(
            h, attn_norm[l].reshape(1, D).astype(F32), w_attn, gmat, nrm, tabs)

        lamv = jnp.stack([diff_lam_q1[l], diff_lam_k1[l], diff_lam_q2[l], diff_lam_k2[l]]).astype(F32)
        a = _diff_call(_fast_flag(diff_q_norm[l], diff_k_norm[l]), lamv,
                       diff_subln[l].reshape(1, LANES).astype(F32), dqT, dk, dvT, B, S, lambda_init)
        dec = jnp.stack([ret_decay_fwd[l], ret_decay_bwd[l]]).astype(F32)
        r = _ret_call(dec, ret_group_norm[l].reshape(1, 512).astype(F32), rq, rkT, rv, rg, B, S)
        c = _gqa_call(_fast_flag(gqa_q_norm[l], gqa_k_norm[l]), gqT, gk, gvT, B, S)

        h, u2 = _merge_call(h, u, a, r, c, w_gate, w_branch[l].astype(BF16),
                            w_out[l].astype(BF16), mlp_norm[l].reshape(1, D).astype(F32))
        h = _mlp_call(h, u2, w_mlp_in[l].astype(BF16), w_mlp_out[l].astype(BF16))
    return h.reshape(B, S, D)
```

```python
import functools
import math

import jax
import jax.numpy as jnp
from jax import lax
from jax.experimental import pallas as pl
from jax.experimental.pallas import tpu as pltpu

F32 = jnp.float32
BF16 = jnp.bfloat16

D_MODEL = 1024
DEPTH = 2
GRID_W = 64
NORM_EPS = 1e-6
HEAD64 = 64
DIFF_HEADS = 4
RET_HEADS = 4
RET_CHUNK = 128
GQA_Q_HEADS = 8
GQA_GROUP = 4
N_BRANCHES = 3
BRANCH_WIDTH = 512
D_FF = 4 * D_MODEL
ATTN_COLS = 4352
LANES = 128

TM = 512
TK = 2048
NQ = 1024
Q_BLOCKS = 2
NCOLS = Q_BLOCKS * NQ
NC = 256
KV_UNROLL = 4
RET_OUT_TILES = 4
ONES_ROWS = 16
VROWS = LANES + ONES_ROWS
FAST_SOFTMAX_BOUND = 64.0
VMEM_LIMIT = 56 * 1024 * 1024
LOG2E = 1.4426950408889634


def _cparams(n_axes):
    return pltpu.CompilerParams(dimension_semantics=("arbitrary",) * n_axes,
                                vmem_limit_bytes=VMEM_LIMIT)


def _resident(shape):
    nd = len(shape)
    return pl.BlockSpec(shape, lambda *_: (0,) * nd, pipeline_mode=pl.Buffered(1))


def _proj_kernel(h_ref, an_ref, w_ref, g_ref, nrm_ref,
                 cd_ref, ad_ref, bd_ref, cg_ref, ag_ref, bg_ref, cr_ref, sr_ref,
                 u_ref, dqT_ref, dk_ref, dvT_ref, rq_ref, rkT_ref, rv_ref, rg_ref,
                 gqT_ref, gk_ref, gvT_ref):
    x = h_ref[...]
    ms = jnp.mean(x * x, axis=-1, keepdims=True)
    u = (x * lax.rsqrt(ms + NORM_EPS) * an_ref[...]).astype(BF16)
    u_ref[...] = u

    def proj(c0, n):
        return jnp.dot(u, w_ref[:, c0:c0 + n], preferred_element_type=F32)

    gmat = g_ref[...]
    lo = lax.broadcasted_iota(jnp.int32, (1, LANES), 1) < HEAD64
    first_row = lax.broadcasted_iota(jnp.int32, (ONES_ROWS, TM), 0) == 0
    ones_rows = jnp.where(first_row, 1.0, 0.0).astype(BF16)

    def qk_norm(p, gain):
        out = []
        for c0 in range(0, p.shape[1], 2 * LANES):
            y = p[:, c0:c0 + 2 * LANES]
            w = y.shape[1]
            msq = jnp.dot((y * y).astype(BF16), gmat[0:w, 0:w], preferred_element_type=F32)
            y = y * lax.rsqrt(msq + NORM_EPS)
            out += [y[:, i:i + LANES] * gain for i in range(0, w, LANES)]
        return out

    def rope(y, c, a, b, d):
        return y * c + pltpu.roll(y, LANES - d, 1) * a + pltpu.roll(y, d, 1) * b

    def chunks(p):
        return [p[:, i:i + LANES] for i in range(0, p.shape[1], LANES)]

    qscale = HEAD64 ** -0.5 * LOG2E
    kscale = LANES ** -0.5
    cd, ad, bd = cd_ref[...], ad_ref[...], bd_ref[...]
    cg, ag, bg = cg_ref[...], ag_ref[...], bg_ref[...]
    cr, sr = cr_ref[...], sr_ref[...]

    def diff_q(p):
        for hd, y in enumerate(qk_norm(p, nrm_ref[0:1, :])):
            y = rope(y, cd, ad, bd, 8) * qscale
            dqT_ref[hd, :, 0:TM] = jnp.where(lo, y, 0.0).T.astype(BF16)
            dqT_ref[hd, :, TM:2 * TM] = jnp.where(lo, 0.0, y).T.astype(BF16)

    def diff_k(p):
        for hd, y in enumerate(qk_norm(p, nrm_ref[1:2, :])):
            dk_ref[:, hd * LANES:(hd + 1) * LANES] = rope(y, cd, ad, bd, 8).astype(BF16)

    def diff_v(p):
        for hd, y in enumerate(chunks(p)):
            dvT_ref[hd, 0:LANES, :] = y.T.astype(BF16)
            dvT_ref[hd, LANES:VROWS, :] = ones_rows

    def ret_q(p):
        for hd, y in enumerate(chunks(p)):
            rq_ref[:, hd * LANES:(hd + 1) * LANES] = (y * cr + pltpu.roll(y, 64, 1) * sr).astype(BF16)

    def ret_k(p):
        for hd, y in enumerate(chunks(p)):
            rkT_ref[hd] = ((y * cr + pltpu.roll(y, 64, 1) * sr) * kscale).T.astype(BF16)

    def ret_v(p):
        rv_ref[...] = p.astype(BF16)

    def ret_g(p):
        rg_ref[...] = p.astype(BF16)

    def gqa_q(p):
        for c, y in enumerate(qk_norm(p, nrm_ref[2:3, :])):
            y = rope(y, cg, ag, bg, 16) * qscale
            ysw = pltpu.roll(y, 64, 1)
            if c < 2:
                z_even, z_odd = jnp.where(lo, y, 0.0), jnp.where(lo, ysw, 0.0)
            else:
                z_even, z_odd = jnp.where(lo, 0.0, ysw), jnp.where(lo, 0.0, y)
            for hh, z in ((2 * c, z_even), (2 * c + 1, z_odd)):
                zt = z.T.astype(BF16)
                for qb in range(TM // LANES):
                    gqT_ref[qb, :, hh * LANES:(hh + 1) * LANES] = zt[:, qb * LANES:(qb + 1) * LANES]

    def gqa_kv(p):
        (y,) = qk_norm(p[:, 0:LANES], nrm_ref[3:4, :])
        gk_ref[...] = rope(y, cg, ag, bg, 16).astype(BF16)
        gvT_ref[0:LANES, :] = p[:, LANES:2 * LANES].T.astype(BF16)
        gvT_ref[LANES:VROWS, :] = ones_rows

    sections = [(3584, 512, gqa_q), (0, 512, diff_q), (2048, 512, ret_k), (1024, 512, diff_v),
                (512, 512, diff_k), (4096, 256, gqa_kv), (1536, 512, ret_q), (2560, 512, ret_v),
                (3072, 512, ret_g)]
    p_next = proj(*sections[0][:2])
    for i, (_, _, epilogue) in enumerate(sections):
        p_cur = p_next
        if i + 1 < len(sections):
            p_next = proj(*sections[i + 1][:2])
        epilogue(p_cur)


def _proj_call(h, attn_norm, w_attn, gmat, nrm, tabs):
    T = h.shape[0]
    S = tabs[0].shape[0]
    nt = T // TM
    tab_spec = pl.BlockSpec((TM, LANES), lambda i: (i % (S // TM), 0))
    row = lambda w: pl.BlockSpec((TM, w), lambda i: (i, 0))
    out_shape = (
        jax.ShapeDtypeStruct((T, D_MODEL), BF16),
        jax.ShapeDtypeStruct((nt, 4, LANES, 2 * TM), BF16),
        jax.ShapeDtypeStruct((T, 512), BF16),
        jax.ShapeDtypeStruct((nt, 4, VROWS, TM), BF16),
        jax.ShapeDtypeStruct((T, 512), BF16),
        jax.ShapeDtypeStruct((nt, 4, LANES, TM), BF16),
        jax.ShapeDtypeStruct((T, 512), BF16),
        jax.ShapeDtypeStruct((T, 512), BF16),
        jax.ShapeDtypeStruct((T // LANES, LANES, NQ), BF16),
        jax.ShapeDtypeStruct((T, LANES), BF16),
        jax.ShapeDtypeStruct((nt, VROWS, TM), BF16),
    )
    out_specs = (
        row(D_MODEL),
        pl.BlockSpec((None, 4, LANES, 2 * TM), lambda i: (i, 0, 0, 0)),
        row(512),
        pl.BlockSpec((None, 4, VROWS, TM), lambda i: (i, 0, 0, 0)),
        row(512), pl.BlockSpec((None, 4, LANES, TM), lambda i: (i, 0, 0, 0)), row(512), row(512),
        pl.BlockSpec((TM // LANES, LANES, NQ), lambda i: (i, 0, 0)),
        row(LANES),
        pl.BlockSpec((None, VROWS, TM), lambda i: (i, 0, 0)),
    )
    in_specs = [row(D_MODEL), _resident((1, D_MODEL)), _resident(w_attn.shape),
                _resident(gmat.shape), _resident(nrm.shape)] + [tab_spec] * 8
    return pl.pallas_call(
        _proj_kernel, grid=(nt,), in_specs=in_specs, out_specs=out_specs,
        out_shape=out_shape, compiler_params=_cparams(1), name="proj",
    )(h, attn_norm, w_attn, gmat, nrm, *tabs)


def _attn_core(fast_ref, qT_ref, k_ref, vT_ref, m_ref, acc_ref):
    n_steps = k_ref.shape[0] // TK
    n_trips = n_steps // KV_UNROLL
    qT = jnp.concatenate([qT_ref[i] for i in range(Q_BLOCKS)], axis=-1)

    def kv(j):
        k = k_ref[pl.ds(pl.multiple_of(j * TK, TK), TK), :]
        per = TK // TM
        vT = jnp.concatenate([vT_ref[j * per + i] for i in range(per)], axis=-1)
        return k, vT

    @pl.when(fast_ref[0] == 1)
    def _():
        def body(jj, carry):
            pv = lsum = None
            for u in range(KV_UNROLL):
                k, vT = kv(jj * KV_UNROLL + u)
                s = jnp.dot(k, qT, preferred_element_type=F32)
                pf = jnp.exp2(s)
                ls = jnp.sum(pf, axis=0, keepdims=True)
                t = jnp.dot(vT[0:LANES], pf.astype(BF16), preferred_element_type=F32)
                pv, lsum = (t, ls) if pv is None else (pv + t, lsum + ls)
            if n_trips == 1:
                acc_ref[0:LANES, :] = pv
                acc_ref[LANES:LANES + 1, :] = lsum
            else:
                acc_ref[0:LANES, :] += pv
                acc_ref[LANES:LANES + 1, :] += lsum
            return carry

        if n_trips > 1:
            acc_ref[...] = jnp.zeros(acc_ref.shape, F32)
        lax.fori_loop(0, n_trips, body, 0)

    @pl.when(fast_ref[0] == 0)
    def _():
        acc_ref[...] = jnp.zeros(acc_ref.shape, F32)
        m_ref[...] = jnp.full(m_ref.shape, -1e30, F32)

        def body(j, carry):
            k, vT = kv(j)
            for c in range(NCOLS // NC):
                cs = slice(c * NC, (c + 1) * NC)
                s = jnp.dot(k, qT[:, cs], preferred_element_type=F32)
                m_old = m_ref[:, cs]
                m_new = jnp.maximum(m_old, jnp.max(s, axis=0, keepdims=True))
                alpha = jnp.exp2(m_old - m_new)
                p = jnp.exp2(s - m_new).astype(BF16)
                pv = jnp.dot(vT, p, preferred_element_type=F32)
                acc_ref[:, cs] = acc_ref[:, cs] * alpha + pv
                m_ref[:, cs] = m_new
            return carry

        lax.fori_loop(0, n_steps, body, 0)


def _diff_kernel(fast_ref, lam_ref, sub_ref, qT_ref, k_ref, vT_ref, o_ref, m_ref, acc_ref, *,
                 lambda_init):
    lv = lam_ref[...]
    lam = (jnp.exp(jnp.sum(lv[0:1] * lv[1:2], axis=-1, keepdims=True))
           - jnp.exp(jnp.sum(lv[2:3] * lv[3:4], axis=-1, keepdims=True)) + lambda_init)
    _attn_core(fast_ref, qT_ref, k_ref, vT_ref, m_ref, acc_ref)

    tq = NQ // 2
    for qb in range(Q_BLOCKS):
        c1 = slice(qb * NQ, qb * NQ + tq)
        c2 = slice(qb * NQ + tq, (qb + 1) * NQ)
        o1 = acc_ref[0:LANES, c1] / acc_ref[LANES:LANES + 1, c1]
        o2 = acc_ref[0:LANES, c2] / acc_ref[LANES:LANES + 1, c2]
        d = o1 - lam * o2
        ms = jnp.mean(d * d, axis=0, keepdims=True)
        dn = d * lax.rsqrt(ms + 1e-5)
        o_ref[qb * tq:(qb + 1) * tq, :] = (
            dn.T * (sub_ref[...] * (1.0 - lambda_init))).astype(o_ref.dtype)


_SMEM_SPEC = pl.BlockSpec(memory_space=pltpu.SMEM)
_ATTN_SCRATCH = [pltpu.VMEM((1, NCOLS), F32), pltpu.VMEM((VROWS, NCOLS), F32)]


def _diff_call(fast, lamv, subln, dqT, dk, dvT, B, S, lambda_init):
    T = dk.shape[0]
    tq = Q_BLOCKS * NQ // 2
    nq = S // tq
    kern = functools.partial(_diff_kernel, lambda_init=lambda_init)
    return pl.pallas_call(
        kern, grid=(B, DIFF_HEADS, nq),
        in_specs=[
            _SMEM_SPEC, _resident(lamv.shape), _resident(subln.shape),
            pl.BlockSpec((Q_BLOCKS, None, LANES, NQ), lambda b, h, i: (b * nq + i, h, 0, 0)),
            pl.BlockSpec((S, LANES), lambda b, h, i: (b, h)),
            pl.BlockSpec((S // TM, None, VROWS, TM), lambda b, h, i: (b, h, 0, 0)),
        ],
        out_specs=pl.BlockSpec((tq, LANES), lambda b, h, i: (b * nq + i, h)),
        out_shape=jax.ShapeDtypeStruct((T, 512), BF16),
        scratch_shapes=_ATTN_SCRATCH,
        compiler_params=_cparams(3), name="diff_attn",
    )(fast, lamv, subln, dqT, dk, dvT)


def _gqa_kernel(fast_ref, qT_ref, k_ref, vT_ref, o_ref, m_ref, acc_ref):
    _attn_core(fast_ref, qT_ref, k_ref, vT_ref, m_ref, acc_ref)
    for qb, c in [(qb, c) for qb in range(Q_BLOCKS) for c in range(GQA_Q_HEADS // 2)]:
        g = (2 * c) // GQA_GROUP
        rows = slice(g * HEAD64, (g + 1) * HEAD64)
        parts = []
        for hh in (2 * c, 2 * c + 1):
            cols = slice(qb * NQ + hh * LANES, qb * NQ + (hh + 1) * LANES)
            parts.append(acc_ref[rows, cols] / acc_ref[LANES:LANES + 1, cols])
        o_ref[qb * LANES:(qb + 1) * LANES, c * LANES:(c + 1) * LANES] = (
            jnp.concatenate(parts, axis=0).T.astype(o_ref.dtype))


def _gqa_call(fast, gqT, gk, gvT, B, S):
    T = gk.shape[0]
    tq = Q_BLOCKS * NQ // GQA_Q_HEADS
    nq = S // tq
    return pl.pallas_call(
        _gqa_kernel, grid=(B, nq),
        in_specs=[
            _SMEM_SPEC,
            pl.BlockSpec((Q_BLOCKS, LANES, NQ), lambda b, i: (b * nq + i, 0, 0)),
            pl.BlockSpec((S, LANES), lambda b, i: (b, 0)),
            pl.BlockSpec((S // TM, VROWS, TM), lambda b, i: (b, 0, 0)),
        ],
        out_specs=pl.BlockSpec((tq, 512), lambda b, i: (b * nq + i, 0)),
        out_shape=jax.ShapeDtypeStruct((T, 512), BF16),
        scratch_shapes=_ATTN_SCRATCH,
        compiler_params=_cparams(2), name="gqa_attn",
    )(fast, gqT, gk, gvT)


def _ret_kernel(dec_ref, gain_ref, q_ref, kT_ref, v_ref, g_ref, o_ref, sf_ref, sb_ref):
    C = RET_CHUNK
    hd = pl.program_id(1)
    lf = -jnp.exp(jnp.full((C, C), dec_ref[0, hd], F32))
    lb = -jnp.exp(jnp.full((C, C), dec_ref[1, hd], F32))
    ii = lax.broadcasted_iota(jnp.int32, (C, C), 0).astype(F32)
    jj = lax.broadcasted_iota(jnp.int32, (C, C), 1).astype(F32)
    diff = ii - jj
    decay = jnp.where(diff >= 0, jnp.exp(jnp.maximum(diff, 0.0) * lf),
                      jnp.exp(jnp.maximum(-diff, 0.0) * lb))
    xi_f, zeta_f = jnp.exp((ii + 1.0) * lf), jnp.exp((C - 1.0 - ii) * lf)
    xi_b, zeta_b = jnp.exp((C - ii) * lb), jnp.exp(ii * lb)
    dec_f, dec_b = jnp.exp(C * lf), jnp.exp(C * lb)
    n_tiles = kT_ref.shape[0]
    per_tile = TM // C

    def rows(t, c):
        return pl.ds(pl.multiple_of(t * TM + c * C, C), C)

    def chunk_kv(t, c, zeta):
        vz = (v_ref[rows(t, c), :].astype(F32) * zeta).astype(BF16)
        return jnp.dot(kT_ref[t, :, c * C:(c + 1) * C], vz, preferred_element_type=F32)

    def scans(i, states):
        st_f, st_b = states
        tf, tb = i, n_tiles - 1 - i
        kv_f = [chunk_kv(tf, c, zeta_f) for c in range(per_tile)]
        kv_b = [chunk_kv(tb, c, zeta_b) for c in range(per_tile)]
        for c in range(per_tile):
            sf_ref[tf * per_tile + c] = st_f.astype(BF16)
            st_f = st_f * dec_f + kv_f[c]
            cb = per_tile - 1 - c
            sb_ref[tb * per_tile + cb] = st_b.astype(BF16)
            st_b = st_b * dec_b + kv_b[cb]
        return st_f, st_b

    zero = jnp.zeros((C, C), F32)
    lax.fori_loop(0, n_tiles, scans, (zero, zero))

    def outputs(tt, carry):
        chunks = [(tt * RET_OUT_TILES + i, c) for i in range(RET_OUT_TILES) for c in range(per_tile)]
        rs = [rows(t, c) for t, c in chunks]
        qs = [q_ref[r, :] for r in rs]
        a = [jnp.dot(q, kT_ref[t, :, c * C:(c + 1) * C], preferred_element_type=F32)
             for q, (t, c) in zip(qs, chunks)]
        cross = [jnp.dot(q, sf_ref[t * per_tile + c], preferred_element_type=F32) * xi_f
                 + jnp.dot(q, sb_ref[t * per_tile + c], preferred_element_type=F32) * xi_b
                 for q, (t, c) in zip(qs, chunks)]
        a = [(x * decay).astype(BF16) for x in a]
        o = [jnp.dot(x, v_ref[r, :], preferred_element_type=F32) + y
             for x, r, y in zip(a, rs, cross)]
        mu = [jnp.mean(x, axis=-1, keepdims=True) for x in o]
        oc = [x - m for x, m in zip(o, mu)]
        var = [jnp.mean(x * x, axis=-1, keepdims=True) for x in oc]
        for x, s2, r in zip(oc, var, rs):
            y = x * lax.rsqrt(s2 + 1e-5) * gain_ref[...]
            g = g_ref[r, :].astype(F32)
            o_ref[r, :] = (g / (1.0 + jnp.exp(-g)) * y).astype(o_ref.dtype)
        return carry

    lax.fori_loop(0, n_tiles // RET_OUT_TILES, outputs, 0)


def _ret_call(dec, gain, rq, rkT, rv, rg, B, S):
    T = rq.shape[0]
    blk = pl.BlockSpec((S, LANES), lambda b, h: (b, h))
    n_chunks = S // RET_CHUNK
    return pl.pallas_call(
        _ret_kernel, grid=(B, RET_HEADS),
        in_specs=[pl.BlockSpec(memory_space=pltpu.SMEM),
                  pl.BlockSpec((1, LANES), lambda b, h: (0, h)), blk,
                  pl.BlockSpec((S // TM, None, LANES, TM), lambda b, h: (b, h, 0, 0)), blk, blk],
        out_specs=blk,
        out_shape=jax.ShapeDtypeStruct((T, 512), BF16),
        scratch_shapes=[pltpu.VMEM((n_chunks, RET_CHUNK, LANES), BF16),
                        pltpu.VMEM((n_chunks, RET_CHUNK, LANES), BF16)],
        compiler_params=_cparams(2), name="retention",
    )(dec, gain, rq, rkT, rv, rg)


def _merge_kernel(h_ref, u_ref, a_ref, r_ref, c_ref, wg_ref, wb_ref, wo_ref, mn_ref,
                  hn_ref, u2_ref):
    u = u_ref[...]
    merged = None
    for n, br in enumerate((a_ref, r_ref, c_ref)):
        logits = jnp.dot(u, wg_ref[:, n * D_MODEL:(n + 1) * D_MODEL], preferred_element_type=F32)
        gate = 1.0 / (1.0 + jnp.exp(-logits))
        term = gate * jnp.dot(br[...], wb_ref[n], preferred_element_type=F32)
        merged = term if merged is None else merged + term
    hn = h_ref[...] + jnp.dot(merged.astype(BF16), wo_ref[...], preferred_element_type=F32)
    hn_ref[...] = hn
    ms = jnp.mean(hn * hn, axis=-1, keepdims=True)
    u2_ref[...] = (hn * lax.rsqrt(ms + NORM_EPS) * mn_ref[...]).astype(BF16)


def _merge_call(h, u, a, r, c, wg, wb, wo, mlp_norm):
    T = h.shape[0]
    row = lambda w: pl.BlockSpec((TM, w), lambda i: (i, 0))
    return pl.pallas_call(
        _merge_kernel, grid=(T // TM,),
        in_specs=[row(D_MODEL), row(D_MODEL), row(512), row(512), row(512),
                  _resident(wg.shape), _resident(wb.shape), _resident(wo.shape),
                  _resident((1, D_MODEL))],
        out_specs=(row(D_MODEL), row(D_MODEL)),
        out_shape=(jax.ShapeDtypeStruct((T, D_MODEL), F32),
                   jax.ShapeDtypeStruct((T, D_MODEL), BF16)),
        compiler_params=_cparams(1), name="merge",
    )(h, u, a, r, c, wg, wb, wo, mlp_norm)


def _mlp_kernel(h_ref, u_ref, w1_ref, w2_ref, o_ref):
    u = u_ref[...]
    acc = h_ref[...]
    for c in range(D_FF // D_MODEL):
        cs = slice(c * D_MODEL, (c + 1) * D_MODEL)
        m = jnp.maximum(jnp.dot(u, w1_ref[:, cs], preferred_element_type=F32), 0.0)
        acc = acc + jnp.dot((m * m).astype(BF16), w2_ref[cs, :], preferred_element_type=F32)
    o_ref[...] = acc


def _mlp_call(h, u2, w1, w2):
    T = h.shape[0]
    row = pl.BlockSpec((TM, D_MODEL), lambda i: (i, 0))
    return pl.pallas_call(
        _mlp_kernel, grid=(T // TM,),
        in_specs=[row, row, _resident(w1.shape), _resident(w2.shape)],
        out_specs=row,
        out_shape=jax.ShapeDtypeStruct((T, D_MODEL), F32),
        compiler_params=_cparams(1), name="mlp",
    )(h, u2, w1, w2)


def _rope_tables(S):
    pos = jnp.arange(S, dtype=F32)
    lane = jnp.arange(LANES)
    l64 = lane % 64

    inv = 500000.0 ** (-jnp.arange(0, 16, 2, dtype=F32) / 16)
    ang = pos[:, None] * inv[None, :]
    cos, sin = jnp.cos(ang)[:, l64 % 8], jnp.sin(ang)[:, l64 % 8]
    cd = jnp.where(l64 < 16, cos, 1.0)
    ad = jnp.where(l64 < 8, -sin, 0.0)
    bd = jnp.where((l64 >= 8) & (l64 < 16), sin, 0.0)

    inv = 10000.0 ** (-jnp.arange(0, 32, 2, dtype=F32) / 32)
    row_pos = jnp.floor(pos / GRID_W)
    col_pos = pos - row_pos * GRID_W
    p2 = jnp.where((l64 < 32)[None, :], row_pos[:, None], col_pos[:, None])
    ang = p2 * inv[l64 % 16][None, :]
    cg = jnp.cos(ang)
    ag = jnp.where(l64 % 32 < 16, -jnp.sin(ang), 0.0)
    bg = jnp.where(l64 % 32 >= 16, jnp.sin(ang), 0.0)

    inv = 10000.0 ** (-jnp.arange(0, 128, 2, dtype=F32) / 128)
    ang = pos[:, None] * inv[l64][None, :]
    cr = jnp.cos(ang)
    sr = jnp.where(lane < 64, -jnp.sin(ang), jnp.sin(ang))
    return (cd, ad, bd, cg, ag, bg, cr, sr)


def _fast_flag(q_gain, k_gain):
    bound = (HEAD64 ** 0.5 * LOG2E * 1.02) * jnp.max(jnp.abs(q_gain)) * jnp.max(jnp.abs(k_gain))
    return (bound <= FAST_SOFTMAX_BOUND).astype(jnp.int32).reshape(1)


def kernel(x, attn_norm, w_in, diff_q_norm, diff_k_norm, diff_lam_q1, diff_lam_k1, diff_lam_q2, diff_lam_k2, diff_subln, ret_decay_fwd, ret_decay_bwd, ret_group_norm, gqa_q_norm, gqa_k_norm, w_branch, w_out, mlp_norm, w_mlp_in, w_mlp_out):
    B, S, D = x.shape
    T = B * S
    tabs = _rope_tables(S)
    blk = jnp.arange(2 * LANES) // HEAD64
    gmat = jnp.where(blk[:, None] == blk[None, :], 1.0 / HEAD64, 0.0).astype(BF16)
    tile2 = lambda v: jnp.concatenate([v, v]).astype(F32)

    h = x.reshape(T, D)
    for l in range(DEPTH):
        lambda_init = 0.8 - 0.6 * math.exp(-0.3 * l)
        w_attn = w_in[l, :, :ATTN_COLS].astype(BF16)
        w_gate = w_in[l, :, ATTN_COLS:].astype(BF16)
        nrm = jnp.stack([tile2(diff_q_norm[l]), tile2(diff_k_norm[l]),
                         tile2(gqa_q_norm[l]), tile2(gqa_k_norm[l])])
        (u, dqT, dk, dvT, rq, rkT, rv, rg, gqT, gk, gvT) = _proj_call(
            h, attn_norm[l].reshape(1, D).astype(F32), w_attn, gmat, nrm, tabs)

        lamv = jnp.stack([diff_lam_q1[l], diff_lam_k1[l], diff_lam_q2[l], diff_lam_k2[l]]).astype(F32)
        a = _diff_call(_fast_flag(diff_q_norm[l], diff_k_norm[l]), lamv,
                       diff_subln[l].reshape(1, LANES).astype(F32), dqT, dk, dvT, B, S, lambda_init)
        dec = jnp.stack([ret_decay_fwd[l], ret_decay_bwd[l]]).astype(F32)
        r = _ret_call(dec, ret_group_norm[l].reshape(1, 512).astype(F32), rq, rkT, rv, rg, B, S)
        c = _gqa_call(_fast_flag(gqa_q_norm[l], gqa_k_norm[l]), gqT, gk, gvT, B, S)

        h, u2 = _merge_call(h, u, a, r, c, w_gate, w_branch[l].astype(BF16),
                            w_out[l].astype(BF16), mlp_norm[l].reshape(1, D).astype(F32))
        h = _mlp_call(h, u2, w_mlp_in[l].astype(BF16), w_mlp_out[l].astype(BF16))
    return h.reshape(B, S, D)
```

```python
import functools
import math

import jax
import jax.numpy as jnp
from jax import lax
from jax.experimental import pallas as pl
from jax.experimental.pallas import tpu as pltpu

F32 = jnp.float32
BF16 = jnp.bfloat16

D_MODEL = 1024
DEPTH = 2
GRID_W = 64
NORM_EPS = 1e-6
HEAD64 = 64
DIFF_HEADS = 4
RET_HEADS = 4
RET_CHUNK = 128
GQA_Q_HEADS = 8
GQA_GROUP = 4
N_BRANCHES = 3
BRANCH_WIDTH = 512
D_FF = 4 * D_MODEL
LANES = 128
ATTN_COLS = 8 * BRANCH_WIDTH + 2 * LANES

TM = 512
TK = 2048
NQ = 1024
Q_BLOCKS = 2
NCOLS = Q_BLOCKS * NQ
NC = 256
KV_UNROLL = 4
RET_OUT_TILES = 4
ONES_ROWS = 16
VROWS = LANES + ONES_ROWS
FAST_SOFTMAX_BOUND = 64.0
VMEM_LIMIT = 56 * 1024 * 1024
LOG2E = 1.4426950408889634


def _cparams(n_axes):
    return pltpu.CompilerParams(dimension_semantics=("arbitrary",) * n_axes,
                                vmem_limit_bytes=VMEM_LIMIT)


def _resident(shape):
    nd = len(shape)
    return pl.BlockSpec(shape, lambda *_: (0,) * nd, pipeline_mode=pl.Buffered(1))


def _proj_kernel(h_ref, an_ref, w_ref, g_ref, nrm_ref,
                 cd_ref, ad_ref, bd_ref, cg_ref, ag_ref, bg_ref, cr_ref, sr_ref,
                 u_ref, dqT_ref, dk_ref, dvT_ref, rq_ref, rkT_ref, rv_ref, rg_ref,
                 gqT_ref, gk_ref, gvT_ref):
    x = h_ref[...]
    ms = jnp.mean(x * x, axis=-1, keepdims=True)
    u = (x * lax.rsqrt(ms + NORM_EPS) * an_ref[...]).astype(BF16)
    u_ref[...] = u

    def proj(c0, n):
        return jnp.dot(u, w_ref[:, c0:c0 + n], preferred_element_type=F32)

    gmat = g_ref[...]
    lo = lax.broadcasted_iota(jnp.int32, (1, LANES), 1) < HEAD64
    first_row = lax.broadcasted_iota(jnp.int32, (ONES_ROWS, TM), 0) == 0
    ones_rows = jnp.where(first_row, 1.0, 0.0).astype(BF16)

    def qk_norm(p, gain):
        out = []
        for c0 in range(0, p.shape[1], 2 * LANES):
            y = p[:, c0:c0 + 2 * LANES]
            w = y.shape[1]
            msq = jnp.dot((y * y).astype(BF16), gmat[0:w, 0:w], preferred_element_type=F32)
            y = y * lax.rsqrt(msq + NORM_EPS)
            out += [y[:, i:i + LANES] * gain for i in range(0, w, LANES)]
        return out

    def rope(y, c, a, b, d):
        return y * c + pltpu.roll(y, LANES - d, 1) * a + pltpu.roll(y, d, 1) * b

    def chunks(p):
        return [p[:, i:i + LANES] for i in range(0, p.shape[1], LANES)]

    qscale = HEAD64 ** -0.5 * LOG2E
    kscale = LANES ** -0.5
    cd, ad, bd = cd_ref[...], ad_ref[...], bd_ref[...]
    cg, ag, bg = cg_ref[...], ag_ref[...], bg_ref[...]
    cr, sr = cr_ref[...], sr_ref[...]

    def diff_q(p):
        for hd, y in enumerate(qk_norm(p, nrm_ref[0:1, :])):
            y = rope(y, cd, ad, bd, 8) * qscale
            dqT_ref[hd, :, 0:TM] = jnp.where(lo, y, 0.0).T.astype(BF16)
            dqT_ref[hd, :, TM:2 * TM] = jnp.where(lo, 0.0, y).T.astype(BF16)

    def diff_k(p):
        for hd, y in enumerate(qk_norm(p, nrm_ref[1:2, :])):
            dk_ref[:, hd * LANES:(hd + 1) * LANES] = rope(y, cd, ad, bd, 8).astype(BF16)

    def diff_v(p):
        for hd, y in enumerate(chunks(p)):
            dvT_ref[hd, 0:LANES, :] = y.T.astype(BF16)
            dvT_ref[hd, LANES:VROWS, :] = ones_rows

    def ret_q(p):
        for hd, y in enumerate(chunks(p)):
            rq_ref[:, hd * LANES:(hd + 1) * LANES] = (y * cr + pltpu.roll(y, LANES // 2, 1) * sr).astype(BF16)

    def ret_k(p):
        for hd, y in enumerate(chunks(p)):
            rkT_ref[hd] = ((y * cr + pltpu.roll(y, LANES // 2, 1) * sr) * kscale).T.astype(BF16)

    def ret_v(p):
        rv_ref[...] = p.astype(BF16)

    def ret_g(p):
        rg_ref[...] = p.astype(BF16)

    def gqa_q(p):
        for c, y in enumerate(qk_norm(p, nrm_ref[2:3, :])):
            y = rope(y, cg, ag, bg, 16) * qscale
            ysw = pltpu.roll(y, LANES // 2, 1)
            if c < 2:
                z_even, z_odd = jnp.where(lo, y, 0.0), jnp.where(lo, ysw, 0.0)
            else:
                z_even, z_odd = jnp.where(lo, 0.0, ysw), jnp.where(lo, 0.0, y)
            for hh, z in ((2 * c, z_even), (2 * c + 1, z_odd)):
                zt = z.T.astype(BF16)
                for qb in range(TM // LANES):
                    gqT_ref[qb, :, hh * LANES:(hh + 1) * LANES] = zt[:, qb * LANES:(qb + 1) * LANES]

    def gqa_kv(p):
        (y,) = qk_norm(p[:, 0:LANES], nrm_ref[3:4, :])
        gk_ref[...] = rope(y, cg, ag, bg, 16).astype(BF16)
        gvT_ref[0:LANES, :] = p[:, LANES:2 * LANES].T.astype(BF16)
        gvT_ref[LANES:VROWS, :] = ones_rows

    order = [gqa_q, diff_q, ret_k, diff_v, diff_k, gqa_kv, ret_q, ret_v, ret_g]
    in_w_in = [diff_q, diff_k, diff_v, ret_q, ret_k, ret_v, ret_g, gqa_q, gqa_kv]
    width = {f: BRANCH_WIDTH for f in in_w_in}
    width[gqa_kv] = 2 * LANES
    start = {f: sum(width[g] for g in in_w_in[:i]) for i, f in enumerate(in_w_in)}
    p_next = proj(start[order[0]], width[order[0]])
    for i, epilogue in enumerate(order):
        p_cur = p_next
        if i + 1 < len(order):
            p_next = proj(start[order[i + 1]], width[order[i + 1]])
        epilogue(p_cur)


def _proj_call(h, attn_norm, w_attn, gmat, nrm, tabs):
    T = h.shape[0]
    S = tabs[0].shape[0]
    nt = T // TM
    tab_spec = pl.BlockSpec((TM, LANES), lambda i: (i % (S // TM), 0))
    row = lambda w: pl.BlockSpec((TM, w), lambda i: (i, 0))
    out_shape = (
        jax.ShapeDtypeStruct((T, D_MODEL), BF16),
        jax.ShapeDtypeStruct((nt, DIFF_HEADS, LANES, 2 * TM), BF16),
        jax.ShapeDtypeStruct((T, BRANCH_WIDTH), BF16),
        jax.ShapeDtypeStruct((nt, DIFF_HEADS, VROWS, TM), BF16),
        jax.ShapeDtypeStruct((T, BRANCH_WIDTH), BF16),
        jax.ShapeDtypeStruct((nt, RET_HEADS, LANES, TM), BF16),
        jax.ShapeDtypeStruct((T, BRANCH_WIDTH), BF16),
        jax.ShapeDtypeStruct((T, BRANCH_WIDTH), BF16),
        jax.ShapeDtypeStruct((T // LANES, LANES, NQ), BF16),
        jax.ShapeDtypeStruct((T, LANES), BF16),
        jax.ShapeDtypeStruct((nt, VROWS, TM), BF16),
    )
    out_specs = (
        row(D_MODEL),
        pl.BlockSpec((None, DIFF_HEADS, LANES, 2 * TM), lambda i: (i, 0, 0, 0)),
        row(BRANCH_WIDTH),
        pl.BlockSpec((None, DIFF_HEADS, VROWS, TM), lambda i: (i, 0, 0, 0)),
        row(BRANCH_WIDTH),
        pl.BlockSpec((None, RET_HEADS, LANES, TM), lambda i: (i, 0, 0, 0)),
        row(BRANCH_WIDTH), row(BRANCH_WIDTH),
        pl.BlockSpec((TM // LANES, LANES, NQ), lambda i: (i, 0, 0)),
        row(LANES),
        pl.BlockSpec((None, VROWS, TM), lambda i: (i, 0, 0)),
    )
    in_specs = [row(D_MODEL), _resident((1, D_MODEL)), _resident(w_attn.shape),
                _resident(gmat.shape), _resident(nrm.shape)] + [tab_spec] * 8
    return pl.pallas_call(
        _proj_kernel, grid=(nt,), in_specs=in_specs, out_specs=out_specs,
        out_shape=out_shape, compiler_params=_cparams(1), name="proj",
    )(h, attn_norm, w_attn, gmat, nrm, *tabs)


def _attn_core(fast_ref, qT_ref, k_ref, vT_ref, m_ref, acc_ref):
    n_steps = k_ref.shape[0] // TK
    n_trips = n_steps // KV_UNROLL
    qT = jnp.concatenate([qT_ref[i] for i in range(Q_BLOCKS)], axis=-1)

    def kv(j):
        k = k_ref[pl.ds(pl.multiple_of(j * TK, TK), TK), :]
        per = TK // TM
        vT = jnp.concatenate([vT_ref[j * per + i] for i in range(per)], axis=-1)
        return k, vT

    @pl.when(fast_ref[0] == 1)
    def _():
        def body(jj, carry):
            pv = lsum = None
            for u in range(KV_UNROLL):
                k, vT = kv(jj * KV_UNROLL + u)
                s = jnp.dot(k, qT, preferred_element_type=F32)
                pf = jnp.exp2(s)
                ls = jnp.sum(pf, axis=0, keepdims=True)
                t = jnp.dot(vT[0:LANES], pf.astype(BF16), preferred_element_type=F32)
                pv, lsum = (t, ls) if pv is None else (pv + t, lsum + ls)
            if n_trips == 1:
                acc_ref[0:LANES, :] = pv
                acc_ref[LANES:LANES + 1, :] = lsum
            else:
                acc_ref[0:LANES, :] += pv
                acc_ref[LANES:LANES + 1, :] += lsum
            return carry

        if n_trips > 1:
            acc_ref[...] = jnp.zeros(acc_ref.shape, F32)
        lax.fori_loop(0, n_trips, body, 0)

    @pl.when(fast_ref[0] == 0)
    def _():
        acc_ref[...] = jnp.zeros(acc_ref.shape, F32)
        m_ref[...] = jnp.full(m_ref.shape, -1e30, F32)

        def body(j, carry):
            k, vT = kv(j)
            for c in range(NCOLS // NC):
                cs = slice(c * NC, (c + 1) * NC)
                s = jnp.dot(k, qT[:, cs], preferred_element_type=F32)
                m_old = m_ref[:, cs]
                m_new = jnp.maximum(m_old, jnp.max(s, axis=0, keepdims=True))
                alpha = jnp.exp2(m_old - m_new)
                p = jnp.exp2(s - m_new).astype(BF16)
                pv = jnp.dot(vT, p, preferred_element_type=F32)
                acc_ref[:, cs] = acc_ref[:, cs] * alpha + pv
                m_ref[:, cs] = m_new
            return carry

        lax.fori_loop(0, n_steps, body, 0)


def _diff_kernel(fast_ref, lam_ref, sub_ref, qT_ref, k_ref, vT_ref, o_ref, m_ref, acc_ref, *,
                 lambda_init):
    lv = lam_ref[...]
    lam = (jnp.exp(jnp.sum(lv[0:1] * lv[1:2], axis=-1, keepdims=True))
           - jnp.exp(jnp.sum(lv[2:3] * lv[3:4], axis=-1, keepdims=True)) + lambda_init)
    _attn_core(fast_ref, qT_ref, k_ref, vT_ref, m_ref, acc_ref)

    tq = NQ // 2
    for qb in range(Q_BLOCKS):
        c1 = slice(qb * NQ, qb * NQ + tq)
        c2 = slice(qb * NQ + tq, (qb + 1) * NQ)
        o1 = acc_ref[0:LANES, c1] / acc_ref[LANES:LANES + 1, c1]
        o2 = acc_ref[0:LANES, c2] / acc_ref[LANES:LANES + 1, c2]
        d = o1 - lam * o2
        ms = jnp.mean(d * d, axis=0, keepdims=True)
        dn = d * lax.rsqrt(ms + 1e-5)
        o_ref[qb * tq:(qb + 1) * tq, :] = (
            dn.T * (sub_ref[...] * (1.0 - lambda_init))).astype(o_ref.dtype)


_SMEM_SPEC = pl.BlockSpec(memory_space=pltpu.SMEM)
_ATTN_SCRATCH = [pltpu.VMEM((1, NCOLS), F32), pltpu.VMEM((VROWS, NCOLS), F32)]


def _diff_call(fast, lamv, subln, dqT, dk, dvT, B, S, lambda_init):
    T = dk.shape[0]
    tq = Q_BLOCKS * NQ // 2
    nq = S // tq
    kern = functools.partial(_diff_kernel, lambda_init=lambda_init)
    return pl.pallas_call(
        kern, grid=(B, DIFF_HEADS, nq),
        in_specs=[
            _SMEM_SPEC, _resident(lamv.shape), _resident(subln.shape),
            pl.BlockSpec((Q_BLOCKS, None, LANES, NQ), lambda b, h, i: (b * nq + i, h, 0, 0)),
            pl.BlockSpec((S, LANES), lambda b, h, i: (b, h)),
            pl.BlockSpec((S // TM, None, VROWS, TM), lambda b, h, i: (b, h, 0, 0)),
        ],
        out_specs=pl.BlockSpec((tq, LANES), lambda b, h, i: (b * nq + i, h)),
        out_shape=jax.ShapeDtypeStruct((T, BRANCH_WIDTH), BF16),
        scratch_shapes=_ATTN_SCRATCH,
        compiler_params=_cparams(3), name="diff_attn",
    )(fast, lamv, subln, dqT, dk, dvT)


def _gqa_kernel(fast_ref, qT_ref, k_ref, vT_ref, o_ref, m_ref, acc_ref):
    _attn_core(fast_ref, qT_ref, k_ref, vT_ref, m_ref, acc_ref)
    for qb, c in [(qb, c) for qb in range(Q_BLOCKS) for c in range(GQA_Q_HEADS // 2)]:
        g = (2 * c) // GQA_GROUP
        rows = slice(g * HEAD64, (g + 1) * HEAD64)
        parts = []
        for hh in (2 * c, 2 * c + 1):
            cols = slice(qb * NQ + hh * LANES, qb * NQ + (hh + 1) * LANES)
            parts.append(acc_ref[rows, cols] / acc_ref[LANES:LANES + 1, cols])
        o_ref[qb * LANES:(qb + 1) * LANES, c * LANES:(c + 1) * LANES] = (
            jnp.concatenate(parts, axis=0).T.astype(o_ref.dtype))


def _gqa_call(fast, gqT, gk, gvT, B, S):
    T = gk.shape[0]
    tq = Q_BLOCKS * NQ // GQA_Q_HEADS
    nq = S // tq
    return pl.pallas_call(
        _gqa_kernel, grid=(B, nq),
        in_specs=[
            _SMEM_SPEC,
            pl.BlockSpec((Q_BLOCKS, LANES, NQ), lambda b, i: (b * nq + i, 0, 0)),
            pl.BlockSpec((S, LANES), lambda b, i: (b, 0)),
            pl.BlockSpec((S // TM, VROWS, TM), lambda b, i: (b, 0, 0)),
        ],
        out_specs=pl.BlockSpec((tq, BRANCH_WIDTH), lambda b, i: (b * nq + i, 0)),
        out_shape=jax.ShapeDtypeStruct((T, BRANCH_WIDTH), BF16),
        scratch_shapes=_ATTN_SCRATCH,
        compiler_params=_cparams(2), name="gqa_attn",
    )(fast, gqT, gk, gvT)


def _ret_kernel(dec_ref, gain_ref, q_ref, kT_ref, v_ref, g_ref, o_ref, sf_ref, sb_ref):
    C = RET_CHUNK
    hd = pl.program_id(1)
    lf = -jnp.exp(jnp.full((C, C), dec_ref[0, hd], F32))
    lb = -jnp.exp(jnp.full((C, C), dec_ref[1, hd], F32))
    ii = lax.broadcasted_iota(jnp.int32, (C, C), 0).astype(F32)
    jj = lax.broadcasted_iota(jnp.int32, (C, C), 1).astype(F32)
    diff = ii - jj
    decay = jnp.where(diff >= 0, jnp.exp(jnp.maximum(diff, 0.0) * lf),
                      jnp.exp(jnp.maximum(-diff, 0.0) * lb))
    xi_f, zeta_f = jnp.exp((ii + 1.0) * lf), jnp.exp((C - 1.0 - ii) * lf)
    xi_b, zeta_b = jnp.exp((C - ii) * lb), jnp.exp(ii * lb)
    dec_f, dec_b = jnp.exp(C * lf), jnp.exp(C * lb)
    n_tiles = kT_ref.shape[0]
    per_tile = TM // C

    def rows(t, c):
        return pl.ds(pl.multiple_of(t * TM + c * C, C), C)

    def chunk_kv(t, c, zeta):
        vz = (v_ref[rows(t, c), :].astype(F32) * zeta).astype(BF16)
        return jnp.dot(kT_ref[t, :, c * C:(c + 1) * C], vz, preferred_element_type=F32)

    def scans(i, states):
        st_f, st_b = states
        tf, tb = i, n_tiles - 1 - i
        kv_f = [chunk_kv(tf, c, zeta_f) for c in range(per_tile)]
        kv_b = [chunk_kv(tb, c, zeta_b) for c in range(per_tile)]
        for c in range(per_tile):
            sf_ref[tf * per_tile + c] = st_f.astype(BF16)
            st_f = st_f * dec_f + kv_f[c]
            cb = per_tile - 1 - c
            sb_ref[tb * per_tile + cb] = st_b.astype(BF16)
            st_b = st_b * dec_b + kv_b[cb]
        return st_f, st_b

    zero = jnp.zeros((C, C), F32)
    lax.fori_loop(0, n_tiles, scans, (zero, zero))

    def outputs(tt, carry):
        chunks = [(tt * RET_OUT_TILES + i, c) for i in range(RET_OUT_TILES) for c in range(per_tile)]
        rs = [rows(t, c) for t, c in chunks]
        qs = [q_ref[r, :] for r in rs]
        a = [jnp.dot(q, kT_ref[t, :, c * C:(c + 1) * C], preferred_element_type=F32)
             for q, (t, c) in zip(qs, chunks)]
        cross = [jnp.dot(q, sf_ref[t * per_tile + c], preferred_element_type=F32) * xi_f
                 + jnp.dot(q, sb_ref[t * per_tile + c], preferred_element_type=F32) * xi_b
                 for q, (t, c) in zip(qs, chunks)]
        a = [(x * decay).astype(BF16) for x in a]
        o = [jnp.dot(x, v_ref[r, :], preferred_element_type=F32) + y
             for x, r, y in zip(a, rs, cross)]
        mu = [jnp.mean(x, axis=-1, keepdims=True) for x in o]
        oc = [x - m for x, m in zip(o, mu)]
        var = [jnp.mean(x * x, axis=-1, keepdims=True) for x in oc]
        for x, s2, r in zip(oc, var, rs):
            y = x * lax.rsqrt(s2 + 1e-5) * gain_ref[...]
            g = g_ref[r, :].astype(F32)
            o_ref[r, :] = (g / (1.0 + jnp.exp(-g)) * y).astype(o_ref.dtype)
        return carry

    lax.fori_loop(0, n_tiles // RET_OUT_TILES, outputs, 0)


def _ret_call(dec, gain, rq, rkT, rv, rg, B, S):
    T = rq.shape[0]
    blk = pl.BlockSpec((S, LANES), lambda b, h: (b, h))
    n_chunks = S // RET_CHUNK
    return pl.pallas_call(
        _ret_kernel, grid=(B, RET_HEADS),
        in_specs=[pl.BlockSpec(memory_space=pltpu.SMEM),
                  pl.BlockSpec((1, LANES), lambda b, h: (0, h)), blk,
                  pl.BlockSpec((S // TM, None, LANES, TM), lambda b, h: (b, h, 0, 0)), blk, blk],
        out_specs=blk,
        out_shape=jax.ShapeDtypeStruct((T, BRANCH_WIDTH), BF16),
        scratch_shapes=[pltpu.VMEM((n_chunks, RET_CHUNK, LANES), BF16),
                        pltpu.VMEM((n_chunks, RET_CHUNK, LANES), BF16)],
        compiler_params=_cparams(2), name="retention",
    )(dec, gain, rq, rkT, rv, rg)


def _merge_kernel(h_ref, u_ref, a_ref, r_ref, c_ref, wg_ref, wb_ref, wo_ref, mn_ref,
                  hn_ref, u2_ref):
    u = u_ref[...]
    merged = None
    for n, br in enumerate((a_ref, r_ref, c_ref)):
        logits = jnp.dot(u, wg_ref[:, n * D_MODEL:(n + 1) * D_MODEL], preferred_element_type=F32)
        gate = 1.0 / (1.0 + jnp.exp(-logits))
        term = gate * jnp.dot(br[...], wb_ref[n], preferred_element_type=F32)
        merged = term if merged is None else merged + term
    merged = merged.astype(BF16)
    halves = [slice(0, TM // 2), slice(TM // 2, TM)]
    hns = [h_ref[r, :] + jnp.dot(merged[r], wo_ref[...], preferred_element_type=F32) for r in halves]
    for r, hn in zip(halves, hns):
        hn_ref[r, :] = hn
        ms = jnp.mean(hn * hn, axis=-1, keepdims=True)
        u2_ref[r, :] = (hn * lax.rsqrt(ms + NORM_EPS) * mn_ref[...]).astype(BF16)


def _merge_call(h, u, a, r, c, wg, wb, wo, mlp_norm):
    T = h.shape[0]
    row = lambda w: pl.BlockSpec((TM, w), lambda i: (i, 0))
    return pl.pallas_call(
        _merge_kernel, grid=(T // TM,),
        in_specs=[row(D_MODEL), row(D_MODEL), row(BRANCH_WIDTH), row(BRANCH_WIDTH), row(BRANCH_WIDTH),
                  _resident(wg.shape), _resident(wb.shape), _resident(wo.shape),
                  _resident((1, D_MODEL))],
        out_specs=(row(D_MODEL), row(D_MODEL)),
        out_shape=(jax.ShapeDtypeStruct((T, D_MODEL), F32),
                   jax.ShapeDtypeStruct((T, D_MODEL), BF16)),
        compiler_params=_cparams(1), name="merge",
    )(h, u, a, r, c, wg, wb, wo, mlp_norm)


def _mlp_kernel(h_ref, u_ref, w1_ref, w2_ref, o_ref):
    u = u_ref[...]
    acc = h_ref[...]
    for c in range(D_FF // D_MODEL):
        cs = slice(c * D_MODEL, (c + 1) * D_MODEL)
        m = jnp.maximum(jnp.dot(u, w1_ref[:, cs], preferred_element_type=F32), 0.0)
        acc = acc + jnp.dot((m * m).astype(BF16), w2_ref[cs, :], preferred_element_type=F32)
    o_ref[...] = acc


def _mlp_call(h, u2, w1, w2):
    T = h.shape[0]
    row = pl.BlockSpec((TM, D_MODEL), lambda i: (i, 0))
    return pl.pallas_call(
        _mlp_kernel, grid=(T // TM,),
        in_specs=[row, row, _resident(w1.shape), _resident(w2.shape)],
        out_specs=row,
        out_shape=jax.ShapeDtypeStruct((T, D_MODEL), F32),
        compiler_params=_cparams(1), name="mlp",
    )(h, u2, w1, w2)


def _rope_tables(S):
    pos = jnp.arange(S, dtype=F32)
    lane = jnp.arange(LANES)
    l64 = lane % HEAD64

    inv = 500000.0 ** (-jnp.arange(0, 16, 2, dtype=F32) / 16)
    ang = pos[:, None] * inv[None, :]
    cos, sin = jnp.cos(ang)[:, l64 % 8], jnp.sin(ang)[:, l64 % 8]
    cd = jnp.where(l64 < 16, cos, 1.0)
    ad = jnp.where(l64 < 8, -sin, 0.0)
    bd = jnp.where((l64 >= 8) & (l64 < 16), sin, 0.0)

    inv = 10000.0 ** (-jnp.arange(0, 32, 2, dtype=F32) / 32)
    row_pos = jnp.floor(pos / GRID_W)
    col_pos = pos - row_pos * GRID_W
    p2 = jnp.where((l64 < 32)[None, :], row_pos[:, None], col_pos[:, None])
    ang = p2 * inv[l64 % 16][None, :]
    cg = jnp.cos(ang)
    ag = jnp.where(l64 % 32 < 16, -jnp.sin(ang), 0.0)
    bg = jnp.where(l64 % 32 >= 16, jnp.sin(ang), 0.0)

    inv = 10000.0 ** (-jnp.arange(0, 128, 2, dtype=F32) / 128)
    ang = pos[:, None] * inv[l64][None, :]
    cr = jnp.cos(ang)
    sr = jnp.where(lane < LANES // 2, -jnp.sin(ang), jnp.sin(ang))
    return (cd, ad, bd, cg, ag, bg, cr, sr)


def _fast_flag(q_gain, k_gain):
    bound = (HEAD64 ** 0.5 * LOG2E * 1.02) * jnp.max(jnp.abs(q_gain)) * jnp.max(jnp.abs(k_gain))
    return (bound <= FAST_SOFTMAX_BOUND).astype(jnp.int32).reshape(1)


def kernel(x, attn_norm, w_in, diff_q_norm, diff_k_norm, diff_lam_q1, diff_lam_k1, diff_lam_q2, diff_lam_k2, diff_subln, ret_decay_fwd, ret_decay_bwd, ret_group_norm, gqa_q_norm, gqa_k_norm, w_branch, w_out, mlp_norm, w_mlp_in, w_mlp_out):
    B, S, D = x.shape
    T = B * S
    tabs = _rope_tables(S)
    blk = jnp.arange(2 * LANES) // HEAD64
    gmat = jnp.where(blk[:, None] == blk[None, :], 1.0 / HEAD64, 0.0).astype(BF16)
    tile2 = lambda v: jnp.concatenate([v, v]).astype(F32)

    h = x.reshape(T, D)
    for l in range(DEPTH):
        lambda_init = 0.8 - 0.6 * math.exp(-0.3 * l)
        w_attn = w_in[l, :, :ATTN_COLS].astype(BF16)
        w_gate = w_in[l, :, ATTN_COLS:].astype(BF16)
        nrm = jnp.stack([tile2(diff_q_norm[l]), tile2(diff_k_norm[l]),
                         tile2(gqa_q_norm[l]), tile2(gqa_k_norm[l])])
        (u, dqT, dk, dvT, rq, rkT, rv, rg, gqT, gk, gvT) = _proj_call(
            h, attn_norm[l].reshape(1, D).astype(F32), w_attn, gmat, nrm, tabs)

        lamv = jnp.stack([diff_lam_q1[l], diff_lam_k1[l], diff_lam_q2[l], diff_lam_k2[l]]).astype(F32)
        a = _diff_call(_fast_flag(diff_q_norm[l], diff_k_norm[l]), lamv,
                       diff_subln[l].reshape(1, LANES).astype(F32), dqT, dk, dvT, B, S, lambda_init)
        dec = jnp.stack([ret_decay_fwd[l], ret_decay_bwd[l]]).astype(F32)
        r = _ret_call(dec, ret_group_norm[l].reshape(1, BRANCH_WIDTH).astype(F32), rq, rkT, rv, rg, B, S)
        c = _gqa_call(_fast_flag(gqa_q_norm[l], gqa_k_norm[l]), gqT, gk, gvT, B, S)

        h, u2 = _merge_call(h, u, a, r, c, w_gate, w_branch[l].astype(BF16),
                            w_out[l].astype(BF16), mlp_norm[l].reshape(1, D).astype(F32))
        h = _mlp_call(h, u2, w_mlp_in[l].astype(BF16), w_mlp_out[l].astype(BF16))
    return h.reshape(B, S, D)
```

```python
import functools
import math

import jax
import jax.numpy as jnp
from jax import lax
from jax.experimental import pallas as pl
from jax.experimental.pallas import tpu as pltpu

F32 = jnp.float32
BF16 = jnp.bfloat16

D_MODEL = 1024
DEPTH = 2
GRID_W = 64
NORM_EPS = 1e-6
HEAD64 = 64
DIFF_HEADS = 4
RET_HEADS = 4
RET_CHUNK = 128
GQA_Q_HEADS = 8
GQA_GROUP = 4
N_BRANCHES = 3
BRANCH_WIDTH = 512
D_FF = 4 * D_MODEL
LANES = 128
ATTN_COLS = 8 * BRANCH_WIDTH + 2 * LANES

TM = 512
TK = 2048
NQ = 1024
Q_BLOCKS = 2
NCOLS = Q_BLOCKS * NQ
NC = 256
KV_UNROLL = 4
RET_SCAN_TILES = 4
RET_OUT_TILES = 4
ONES_ROWS = 16
VROWS = LANES + ONES_ROWS
FAST_SOFTMAX_BOUND = 64.0
VMEM_LIMIT = 56 * 1024 * 1024
LOG2E = 1.4426950408889634


def _cparams(n_axes):
    return pltpu.CompilerParams(dimension_semantics=("arbitrary",) * n_axes,
                                vmem_limit_bytes=VMEM_LIMIT)


def _resident(shape):
    nd = len(shape)
    return pl.BlockSpec(shape, lambda *_: (0,) * nd, pipeline_mode=pl.Buffered(1))


def _proj_kernel(h_ref, an_ref, w_ref, g_ref, nrm_ref,
                 cd_ref, ad_ref, bd_ref, cg_ref, ag_ref, bg_ref, cr_ref, sr_ref,
                 u_ref, dqT_ref, dk_ref, dvT_ref, rq_ref, rkT_ref, rv_ref, rg_ref,
                 gqT_ref, gk_ref, gvT_ref):
    x = h_ref[...]
    ms = jnp.mean(x * x, axis=-1, keepdims=True)
    u = (x * lax.rsqrt(ms + NORM_EPS) * an_ref[...]).astype(BF16)
    u_ref[...] = u

    def proj(c0, n):
        return jnp.dot(u, w_ref[:, c0:c0 + n], preferred_element_type=F32)

    gmat = g_ref[...]
    lo = lax.broadcasted_iota(jnp.int32, (1, LANES), 1) < HEAD64
    first_row = lax.broadcasted_iota(jnp.int32, (ONES_ROWS, TM), 0) == 0
    ones_rows = jnp.where(first_row, 1.0, 0.0).astype(BF16)

    def qk_norm(p, gain):
        out = []
        for c0 in range(0, p.shape[1], 2 * LANES):
            y = p[:, c0:c0 + 2 * LANES]
            w = y.shape[1]
            msq = jnp.dot((y * y).astype(BF16), gmat[0:w, 0:w], preferred_element_type=F32)
            y = y * lax.rsqrt(msq + NORM_EPS)
            out += [y[:, i:i + LANES] * gain for i in range(0, w, LANES)]
        return out

    def rope(y, c, a, b, d):
        return y * c + pltpu.roll(y, LANES - d, 1) * a + pltpu.roll(y, d, 1) * b

    def chunks(p):
        return [p[:, i:i + LANES] for i in range(0, p.shape[1], LANES)]

    qscale = HEAD64 ** -0.5 * LOG2E
    kscale = LANES ** -0.5
    cd, ad, bd = cd_ref[...], ad_ref[...], bd_ref[...]
    cg, ag, bg = cg_ref[...], ag_ref[...], bg_ref[...]
    cr, sr = cr_ref[...], sr_ref[...]

    def diff_q(p):
        for hd, y in enumerate(qk_norm(p, nrm_ref[0:1, :])):
            y = rope(y, cd, ad, bd, 8) * qscale
            dqT_ref[hd, :, 0:TM] = jnp.where(lo, y, 0.0).T.astype(BF16)
            dqT_ref[hd, :, TM:2 * TM] = jnp.where(lo, 0.0, y).T.astype(BF16)

    def diff_k(p):
        for hd, y in enumerate(qk_norm(p, nrm_ref[1:2, :])):
            dk_ref[:, hd * LANES:(hd + 1) * LANES] = rope(y, cd, ad, bd, 8).astype(BF16)

    def diff_v(p):
        for hd, y in enumerate(chunks(p)):
            dvT_ref[hd, 0:LANES, :] = y.T.astype(BF16)
            dvT_ref[hd, LANES:VROWS, :] = ones_rows

    def ret_q(p):
        for hd, y in enumerate(chunks(p)):
            rq_ref[:, hd * LANES:(hd + 1) * LANES] = (y * cr + pltpu.roll(y, LANES // 2, 1) * sr).astype(BF16)

    def ret_k(p):
        for hd, y in enumerate(chunks(p)):
            rkT_ref[hd] = ((y * cr + pltpu.roll(y, LANES // 2, 1) * sr) * kscale).T.astype(BF16)

    def ret_v(p):
        rv_ref[...] = p.astype(BF16)

    def ret_g(p):
        rg_ref[...] = p.astype(BF16)

    def gqa_q(p):
        for c, y in enumerate(qk_norm(p, nrm_ref[2:3, :])):
            y = rope(y, cg, ag, bg, 16) * qscale
            ysw = pltpu.roll(y, LANES // 2, 1)
            if c < 2:
                z_even, z_odd = jnp.where(lo, y, 0.0), jnp.where(lo, ysw, 0.0)
            else:
                z_even, z_odd = jnp.where(lo, 0.0, ysw), jnp.where(lo, 0.0, y)
            for hh, z in ((2 * c, z_even), (2 * c + 1, z_odd)):
                zt = z.T.astype(BF16)
                for qb in range(TM // LANES):
                    gqT_ref[qb, :, hh * LANES:(hh + 1) * LANES] = zt[:, qb * LANES:(qb + 1) * LANES]

    def gqa_kv(p):
        (y,) = qk_norm(p[:, 0:LANES], nrm_ref[3:4, :])
        gk_ref[...] = rope(y, cg, ag, bg, 16).astype(BF16)
        gvT_ref[0:LANES, :] = p[:, LANES:2 * LANES].T.astype(BF16)
        gvT_ref[LANES:VROWS, :] = ones_rows

    order = [gqa_q, diff_q, ret_k, diff_v, diff_k, gqa_kv, ret_q, ret_v, ret_g]
    in_w_in = [diff_q, diff_k, diff_v, ret_q, ret_k, ret_v, ret_g, gqa_q, gqa_kv]
    width = {f: BRANCH_WIDTH for f in in_w_in}
    width[gqa_kv] = 2 * LANES
    start = {f: sum(width[g] for g in in_w_in[:i]) for i, f in enumerate(in_w_in)}
    p_next = proj(start[order[0]], width[order[0]])
    for i, epilogue in enumerate(order):
        p_cur = p_next
        if i + 1 < len(order):
            p_next = proj(start[order[i + 1]], width[order[i + 1]])
        epilogue(p_cur)


def _proj_call(h, attn_norm, w_attn, gmat, nrm, tabs):
    T = h.shape[0]
    S = tabs[0].shape[0]
    nt = T // TM
    tab_spec = pl.BlockSpec((TM, LANES), lambda i: (i % (S // TM), 0))
    row = lambda w: pl.BlockSpec((TM, w), lambda i: (i, 0))
    out_shape = (
        jax.ShapeDtypeStruct((T, D_MODEL), BF16),
        jax.ShapeDtypeStruct((nt, DIFF_HEADS, LANES, 2 * TM), BF16),
        jax.ShapeDtypeStruct((T, BRANCH_WIDTH), BF16),
        jax.ShapeDtypeStruct((nt, DIFF_HEADS, VROWS, TM), BF16),
        jax.ShapeDtypeStruct((T, BRANCH_WIDTH), BF16),
        jax.ShapeDtypeStruct((nt, RET_HEADS, LANES, TM), BF16),
        jax.ShapeDtypeStruct((T, BRANCH_WIDTH), BF16),
        jax.ShapeDtypeStruct((T, BRANCH_WIDTH), BF16),
        jax.ShapeDtypeStruct((T // LANES, LANES, NQ), BF16),
        jax.ShapeDtypeStruct((T, LANES), BF16),
        jax.ShapeDtypeStruct((nt, VROWS, TM), BF16),
    )
    out_specs = (
        row(D_MODEL),
        pl.BlockSpec((None, DIFF_HEADS, LANES, 2 * TM), lambda i: (i, 0, 0, 0)),
        row(BRANCH_WIDTH),
        pl.BlockSpec((None, DIFF_HEADS, VROWS, TM), lambda i: (i, 0, 0, 0)),
        row(BRANCH_WIDTH),
        pl.BlockSpec((None, RET_HEADS, LANES, TM), lambda i: (i, 0, 0, 0)),
        row(BRANCH_WIDTH), row(BRANCH_WIDTH),
        pl.BlockSpec((TM // LANES, LANES, NQ), lambda i: (i, 0, 0)),
        row(LANES),
        pl.BlockSpec((None, VROWS, TM), lambda i: (i, 0, 0)),
    )
    in_specs = [row(D_MODEL), _resident((1, D_MODEL)), _resident(w_attn.shape),
                _resident(gmat.shape), _resident(nrm.shape)] + [tab_spec] * 8
    return pl.pallas_call(
        _proj_kernel, grid=(nt,), in_specs=in_specs, out_specs=out_specs,
        out_shape=out_shape, compiler_params=_cparams(1), name="proj",
    )(h, attn_norm, w_attn, gmat, nrm, *tabs)


def _attn_core(fast_ref, qT_ref, k_ref, vT_ref, m_ref, acc_ref):
    n_steps = k_ref.shape[0] // TK
    n_trips = n_steps // KV_UNROLL
    qT = jnp.concatenate([qT_ref[i] for i in range(Q_BLOCKS)], axis=-1)

    def kv(j):
        k = k_ref[pl.ds(pl.multiple_of(j * TK, TK), TK), :]
        per = TK // TM
        vT = jnp.concatenate([vT_ref[j * per + i] for i in range(per)], axis=-1)
        return k, vT

    @pl.when(fast_ref[0] == 1)
    def _():
        def body(jj, carry):
            pv = lsum = None
            for u in range(KV_UNROLL):
                k, vT = kv(jj * KV_UNROLL + u)
                s = jnp.dot(k, qT, preferred_element_type=F32)
                pf = jnp.exp2(s)
                ls = jnp.sum(pf, axis=0, keepdims=True)
                t = jnp.dot(vT[0:LANES], pf.astype(BF16), preferred_element_type=F32)
                pv, lsum = (t, ls) if pv is None else (pv + t, lsum + ls)
            if n_trips == 1:
                acc_ref[0:LANES, :] = pv
                acc_ref[LANES:LANES + 1, :] = lsum
            else:
                acc_ref[0:LANES, :] += pv
                acc_ref[LANES:LANES + 1, :] += lsum
            return carry

        if n_trips > 1:
            acc_ref[...] = jnp.zeros(acc_ref.shape, F32)
        lax.fori_loop(0, n_trips, body, 0)

    @pl.when(fast_ref[0] == 0)
    def _():
        acc_ref[...] = jnp.zeros(acc_ref.shape, F32)
        m_ref[...] = jnp.full(m_ref.shape, -1e30, F32)

        def body(j, carry):
            k, vT = kv(j)
            for c in range(NCOLS // NC):
                cs = slice(c * NC, (c + 1) * NC)
                s = jnp.dot(k, qT[:, cs], preferred_element_type=F32)
                m_old = m_ref[:, cs]
                m_new = jnp.maximum(m_old, jnp.max(s, axis=0, keepdims=True))
                alpha = jnp.exp2(m_old - m_new)
                p = jnp.exp2(s - m_new).astype(BF16)
                pv = jnp.dot(vT, p, preferred_element_type=F32)
                acc_ref[:, cs] = acc_ref[:, cs] * alpha + pv
                m_ref[:, cs] = m_new
            return carry

        lax.fori_loop(0, n_steps, body, 0)


def _diff_kernel(fast_ref, lam_ref, sub_ref, qT_ref, k_ref, vT_ref, o_ref, m_ref, acc_ref, *,
                 lambda_init):
    lv = lam_ref[...]
    lam = (jnp.exp(jnp.sum(lv[0:1] * lv[1:2], axis=-1, keepdims=True))
           - jnp.exp(jnp.sum(lv[2:3] * lv[3:4], axis=-1, keepdims=True)) + lambda_init)
    _attn_core(fast_ref, qT_ref, k_ref, vT_ref, m_ref, acc_ref)

    tq = NQ // 2
    for qb in range(Q_BLOCKS):
        c1 = slice(qb * NQ, qb * NQ + tq)
        c2 = slice(qb * NQ + tq, (qb + 1) * NQ)
        o1 = acc_ref[0:LANES, c1] / acc_ref[LANES:LANES + 1, c1]
        o2 = acc_ref[0:LANES, c2] / acc_ref[LANES:LANES + 1, c2]
        d = o1 - lam * o2
        ms = jnp.mean(d * d, axis=0, keepdims=True)
        dn = d * lax.rsqrt(ms + 1e-5)
        o_ref[qb * tq:(qb + 1) * tq, :] = (
            dn.T * (sub_ref[...] * (1.0 - lambda_init))).astype(o_ref.dtype)


_SMEM_SPEC = pl.BlockSpec(memory_space=pltpu.SMEM)
_ATTN_SCRATCH = [pltpu.VMEM((1, NCOLS), F32), pltpu.VMEM((VROWS, NCOLS), F32)]


def _diff_call(fast, lamv, subln, dqT, dk, dvT, B, S, lambda_init):
    T = dk.shape[0]
    tq = Q_BLOCKS * NQ // 2
    nq = S // tq
    kern = functools.partial(_diff_kernel, lambda_init=lambda_init)
    return pl.pallas_call(
        kern, grid=(B, DIFF_HEADS, nq),
        in_specs=[
            _SMEM_SPEC, _resident(lamv.shape), _resident(subln.shape),
            pl.BlockSpec((Q_BLOCKS, None, LANES, NQ), lambda b, h, i: (b * nq + i, h, 0, 0)),
            pl.BlockSpec((S, LANES), lambda b, h, i: (b, h)),
            pl.BlockSpec((S // TM, None, VROWS, TM), lambda b, h, i: (b, h, 0, 0)),
        ],
        out_specs=pl.BlockSpec((tq, LANES), lambda b, h, i: (b * nq + i, h)),
        out_shape=jax.ShapeDtypeStruct((T, BRANCH_WIDTH), BF16),
        scratch_shapes=_ATTN_SCRATCH,
        compiler_params=_cparams(3), name="diff_attn",
    )(fast, lamv, subln, dqT, dk, dvT)


def _gqa_kernel(fast_ref, qT_ref, k_ref, vT_ref, o_ref, m_ref, acc_ref):
    _attn_core(fast_ref, qT_ref, k_ref, vT_ref, m_ref, acc_ref)
    for qb, c in [(qb, c) for qb in range(Q_BLOCKS) for c in range(GQA_Q_HEADS // 2)]:
        g = (2 * c) // GQA_GROUP
        rows = slice(g * HEAD64, (g + 1) * HEAD64)
        parts = []
        for hh in (2 * c, 2 * c + 1):
            cols = slice(qb * NQ + hh * LANES, qb * NQ + (hh + 1) * LANES)
            parts.append(acc_ref[rows, cols] / acc_ref[LANES:LANES + 1, cols])
        o_ref[qb * LANES:(qb + 1) * LANES, c * LANES:(c + 1) * LANES] = (
            jnp.concatenate(parts, axis=0).T.astype(o_ref.dtype))


def _gqa_call(fast, gqT, gk, gvT, B, S):
    T = gk.shape[0]
    tq = Q_BLOCKS * NQ // GQA_Q_HEADS
    nq = S // tq
    return pl.pallas_call(
        _gqa_kernel, grid=(B, nq),
        in_specs=[
            _SMEM_SPEC,
            pl.BlockSpec((Q_BLOCKS, LANES, NQ), lambda b, i: (b * nq + i, 0, 0)),
            pl.BlockSpec((S, LANES), lambda b, i: (b, 0)),
            pl.BlockSpec((S // TM, VROWS, TM), lambda b, i: (b, 0, 0)),
        ],
        out_specs=pl.BlockSpec((tq, BRANCH_WIDTH), lambda b, i: (b * nq + i, 0)),
        out_shape=jax.ShapeDtypeStruct((T, BRANCH_WIDTH), BF16),
        scratch_shapes=_ATTN_SCRATCH,
        compiler_params=_cparams(2), name="gqa_attn",
    )(fast, gqT, gk, gvT)


def _ret_kernel(dec_ref, gain_ref, q_ref, kT_ref, v_ref, g_ref, o_ref, sf_ref, sb_ref):
    C = RET_CHUNK
    hd = pl.program_id(1)
    lf = -jnp.exp(jnp.full((C, C), dec_ref[0, hd], F32))
    lb = -jnp.exp(jnp.full((C, C), dec_ref[1, hd], F32))
    ii = lax.broadcasted_iota(jnp.int32, (C, C), 0).astype(F32)
    jj = lax.broadcasted_iota(jnp.int32, (C, C), 1).astype(F32)
    diff = ii - jj
    decay = jnp.where(diff >= 0, jnp.exp(jnp.maximum(diff, 0.0) * lf),
                      jnp.exp(jnp.maximum(-diff, 0.0) * lb))
    xi_f, zeta_f = jnp.exp((ii + 1.0) * lf), jnp.exp((C - 1.0 - ii) * lf)
    xi_b, zeta_b = jnp.exp((C - ii) * lb), jnp.exp(ii * lb)
    dec_f, dec_b = jnp.exp(C * lf), jnp.exp(C * lb)
    n_tiles = kT_ref.shape[0]
    per_tile = TM // C

    def rows(t, c):
        return pl.ds(pl.multiple_of(t * TM + c * C, C), C)

    def chunk_kv(t, c, zeta):
        vz = (v_ref[rows(t, c), :].astype(F32) * zeta).astype(BF16)
        return jnp.dot(kT_ref[t, :, c * C:(c + 1) * C], vz, preferred_element_type=F32)

    def scans(i, states):
        st_f, st_b = states
        fwd = [(i * RET_SCAN_TILES + t, c) for t in range(RET_SCAN_TILES) for c in range(per_tile)]
        bwd = [(n_tiles - 1 - t, per_tile - 1 - c) for t, c in fwd]
        kv_f = [chunk_kv(t, c, zeta_f) for t, c in fwd]
        kv_b = [chunk_kv(t, c, zeta_b) for t, c in bwd]
        for (tf, cf), kf, (tb, cb), kb in zip(fwd, kv_f, bwd, kv_b):
            sf_ref[tf * per_tile + cf] = st_f.astype(BF16)
            st_f = st_f * dec_f + kf
            sb_ref[tb * per_tile + cb] = st_b.astype(BF16)
            st_b = st_b * dec_b + kb
        return st_f, st_b

    zero = jnp.zeros((C, C), F32)
    lax.fori_loop(0, n_tiles // RET_SCAN_TILES, scans, (zero, zero))

    def outputs(tt, carry):
        chunks = [(tt * RET_OUT_TILES + i, c) for i in range(RET_OUT_TILES) for c in range(per_tile)]
        rs = [rows(t, c) for t, c in chunks]
        qs = [q_ref[r, :] for r in rs]
        a = [jnp.dot(q, kT_ref[t, :, c * C:(c + 1) * C], preferred_element_type=F32)
             for q, (t, c) in zip(qs, chunks)]
        cross = [jnp.dot(q, sf_ref[t * per_tile + c], preferred_element_type=F32) * xi_f
                 + jnp.dot(q, sb_ref[t * per_tile + c], preferred_element_type=F32) * xi_b
                 for q, (t, c) in zip(qs, chunks)]
        a = [(x * decay).astype(BF16) for x in a]
        o = [jnp.dot(x, v_ref[r, :], preferred_element_type=F32) + y
             for x, r, y in zip(a, rs, cross)]
        mu = [jnp.mean(x, axis=-1, keepdims=True) for x in o]
        oc = [x - m for x, m in zip(o, mu)]
        var = [jnp.mean(x * x, axis=-1, keepdims=True) for x in oc]
        for x, s2, r in zip(oc, var, rs):
            y = x * lax.rsqrt(s2 + 1e-5) * gain_ref[...]
            g = g_ref[r, :].astype(F32)
            o_ref[r, :] = (g / (1.0 + jnp.exp(-g)) * y).astype(o_ref.dtype)
        return carry

    lax.fori_loop(0, n_tiles // RET_OUT_TILES, outputs, 0)


def _ret_call(dec, gain, rq, rkT, rv, rg, B, S):
    T = rq.shape[0]
    blk = pl.BlockSpec((S, LANES), lambda b, h: (b, h))
    n_chunks = S // RET_CHUNK
    return pl.pallas_call(
        _ret_kernel, grid=(B, RET_HEADS),
        in_specs=[pl.BlockSpec(memory_space=pltpu.SMEM),
                  pl.BlockSpec((1, LANES), lambda b, h: (0, h)), blk,
                  pl.BlockSpec((S // TM, None, LANES, TM), lambda b, h: (b, h, 0, 0)), blk, blk],
        out_specs=blk,
        out_shape=jax.ShapeDtypeStruct((T, BRANCH_WIDTH), BF16),
        scratch_shapes=[pltpu.VMEM((n_chunks, RET_CHUNK, LANES), BF16),
                        pltpu.VMEM((n_chunks, RET_CHUNK, LANES), BF16)],
        compiler_params=_cparams(2), name="retention",
    )(dec, gain, rq, rkT, rv, rg)


def _merge_kernel(h_ref, u_ref, a_ref, r_ref, c_ref, wg_ref, wb_ref, wo_ref, mn_ref,
                  hn_ref, u2_ref):
    u = u_ref[...]
    merged = None
    for n, br in enumerate((a_ref, r_ref, c_ref)):
        logits = jnp.dot(u, wg_ref[:, n * D_MODEL:(n + 1) * D_MODEL], preferred_element_type=F32)
        gate = 1.0 / (1.0 + jnp.exp(-logits))
        term = gate * jnp.dot(br[...], wb_ref[n], preferred_element_type=F32)
        merged = term if merged is None else merged + term
    merged = merged.astype(BF16)
    halves = [slice(0, TM // 2), slice(TM // 2, TM)]
    hns = [h_ref[r, :] + jnp.dot(merged[r], wo_ref[...], preferred_element_type=F32) for r in halves]
    for r, hn in zip(halves, hns):
        hn_ref[r, :] = hn
        ms = jnp.mean(hn * hn, axis=-1, keepdims=True)
        u2_ref[r, :] = (hn * lax.rsqrt(ms + NORM_EPS) * mn_ref[...]).astype(BF16)


def _merge_call(h, u, a, r, c, wg, wb, wo, mlp_norm):
    T = h.shape[0]
    row = lambda w: pl.BlockSpec((TM, w), lambda i: (i, 0))
    return pl.pallas_call(
        _merge_kernel, grid=(T // TM,),
        in_specs=[row(D_MODEL), row(D_MODEL), row(BRANCH_WIDTH), row(BRANCH_WIDTH), row(BRANCH_WIDTH),
                  _resident(wg.shape), _resident(wb.shape), _resident(wo.shape),
                  _resident((1, D_MODEL))],
        out_specs=(row(D_MODEL), row(D_MODEL)),
        out_shape=(jax.ShapeDtypeStruct((T, D_MODEL), F32),
                   jax.ShapeDtypeStruct((T, D_MODEL), BF16)),
        compiler_params=_cparams(1), name="merge",
    )(h, u, a, r, c, wg, wb, wo, mlp_norm)


def _mlp_kernel(h_ref, u_ref, w1_ref, w2_ref, o_ref):
    u = u_ref[...]
    acc = h_ref[...]
    for c in range(D_FF // D_MODEL):
        cs = slice(c * D_MODEL, (c + 1) * D_MODEL)
        m = jnp.maximum(jnp.dot(u, w1_ref[:, cs], preferred_element_type=F32), 0.0)
        acc = acc + jnp.dot((m * m).astype(BF16), w2_ref[cs, :], preferred_element_type=F32)
    o_ref[...] = acc


def _mlp_call(h, u2, w1, w2):
    T = h.shape[0]
    row = pl.BlockSpec((TM, D_MODEL), lambda i: (i, 0))
    return pl.pallas_call(
        _mlp_kernel, grid=(T // TM,),
        in_specs=[row, row, _resident(w1.shape), _resident(w2.shape)],
        out_specs=row,
        out_shape=jax.ShapeDtypeStruct((T, D_MODEL), F32),
        compiler_params=_cparams(1), name="mlp",
    )(h, u2, w1, w2)


def _rope_tables(S):
    pos = jnp.arange(S, dtype=F32)
    lane = jnp.arange(LANES)
    l64 = lane % HEAD64

    inv = 500000.0 ** (-jnp.arange(0, 16, 2, dtype=F32) / 16)
    ang = pos[:, None] * inv[None, :]
    cos, sin = jnp.cos(ang)[:, l64 % 8], jnp.sin(ang)[:, l64 % 8]
    cd = jnp.where(l64 < 16, cos, 1.0)
    ad = jnp.where(l64 < 8, -sin, 0.0)
    bd = jnp.where((l64 >= 8) & (l64 < 16), sin, 0.0)

    inv = 10000.0 ** (-jnp.arange(0, 32, 2, dtype=F32) / 32)
    row_pos = jnp.floor(pos / GRID_W)
    col_pos = pos - row_pos * GRID_W
    p2 = jnp.where((l64 < 32)[None, :], row_pos[:, None], col_pos[:, None])
    ang = p2 * inv[l64 % 16][None, :]
    cg = jnp.cos(ang)
    ag = jnp.where(l64 % 32 < 16, -jnp.sin(ang), 0.0)
    bg = jnp.where(l64 % 32 >= 16, jnp.sin(ang), 0.0)

    inv = 10000.0 ** (-jnp.arange(0, 128, 2, dtype=F32) / 128)
    ang = pos[:, None] * inv[l64][None, :]
    cr = jnp.cos(ang)
    sr = jnp.where(lane < LANES // 2, -jnp.sin(ang), jnp.sin(ang))
    return (cd, ad, bd, cg, ag, bg, cr, sr)


def _fast_flag(q_gain, k_gain):
    bound = (HEAD64 ** 0.5 * LOG2E * 1.02) * jnp.max(jnp.abs(q_gain)) * jnp.max(jnp.abs(k_gain))
    return (bound <= FAST_SOFTMAX_BOUND).astype(jnp.int32).reshape(1)


def kernel(x, attn_norm, w_in, diff_q_norm, diff_k_norm, diff_lam_q1, diff_lam_k1, diff_lam_q2, diff_lam_k2, diff_subln, ret_decay_fwd, ret_decay_bwd, ret_group_norm, gqa_q_norm, gqa_k_norm, w_branch, w_out, mlp_norm, w_mlp_in, w_mlp_out):
    B, S, D = x.shape
    T = B * S
    tabs = _rope_tables(S)
    blk = jnp.arange(2 * LANES) // HEAD64
    gmat = jnp.where(blk[:, None] == blk[None, :], 1.0 / HEAD64, 0.0).astype(BF16)
    tile2 = lambda v: jnp.concatenate([v, v]).astype(F32)

    h = x.reshape(T, D)
    for l in range(DEPTH):
        lambda_init = 0.8 - 0.6 * math.exp(-0.3 * l)
        w_attn = w_in[l, :, :ATTN_COLS].astype(BF16)
        w_gate = w_in[l, :, ATTN_COLS:].astype(BF16)
        nrm = jnp.stack([tile2(diff_q_norm[l]), tile2(diff_k_norm[l]),
                         tile2(gqa_q_norm[l]), tile2(gqa_k_norm[l])])
        (u, dqT, dk, dvT, rq, rkT, rv, rg, gqT, gk, gvT) = _proj_call(
            h, attn_norm[l].reshape(1, D).astype(F32), w_attn, gmat, nrm, tabs)

        lamv = jnp.stack([diff_lam_q1[l], diff_lam_k1[l], diff_lam_q2[l], diff_lam_k2[l]]).astype(F32)
        a = _diff_call(_fast_flag(diff_q_norm[l], diff_k_norm[l]), lamv,
                       diff_subln[l].reshape(1, LANES).astype(F32), dqT, dk, dvT, B, S, lambda_init)
        dec = jnp.stack([ret_decay_fwd[l], ret_decay_bwd[l]]).astype(F32)
        r = _ret_call(dec, ret_group_norm[l].reshape(1, BRANCH_WIDTH).astype(F32), rq, rkT, rv, rg, B, S)
        c = _gqa_call(_fast_flag(gqa_q_norm[l], gqa_k_norm[l]), gqT, gk, gvT, B, S)

        h, u2 = _merge_call(h, u, a, r, c, w_gate, w_branch[l].astype(BF16),
                            w_out[l].astype(BF16), mlp_norm[l].reshape(1, D).astype(F32))
        h = _mlp_call(h, u2, w_mlp_in[l].astype(BF16), w_mlp_out[l].astype(BF16))
    return h.reshape(B, S, D)
```

```python
import functools
import math

import jax
import jax.numpy as jnp
from jax import lax
from jax.experimental import pallas as pl
from jax.experimental.pallas import tpu as pltpu

F32 = jnp.float32
BF16 = jnp.bfloat16

D_MODEL = 1024
DEPTH = 2
GRID_W = 64
NORM_EPS = 1e-6
HEAD64 = 64
DIFF_HEADS = 4
RET_HEADS = 4
RET_CHUNK = 128
GQA_Q_HEADS = 8
GQA_GROUP = 4
N_BRANCHES = 3
BRANCH_WIDTH = 512
D_FF = 4 * D_MODEL
LANES = 128
MXU_DIM = 256
ATTN_COLS = 8 * BRANCH_WIDTH + 2 * LANES

TM = 512
TK = 2048
NQ = 1024
Q_BLOCKS = 2
NCOLS = Q_BLOCKS * NQ
NC = 256
KV_UNROLL = 4
RET_SCAN_TILES = 4
RET_OUT_TILES = 4
ONES_ROWS = 16
VROWS = LANES + ONES_ROWS
FAST_SOFTMAX_BOUND = 64.0
VMEM_LIMIT = 56 * 1024 * 1024
LOG2E = 1.4426950408889634


def _cparams(n_axes):
    return pltpu.CompilerParams(dimension_semantics=("arbitrary",) * n_axes,
                                vmem_limit_bytes=VMEM_LIMIT)


def _resident(shape):
    nd = len(shape)
    return pl.BlockSpec(shape, lambda *_: (0,) * nd, pipeline_mode=pl.Buffered(1))


def _proj_kernel(h_ref, an_ref, w_ref, g_ref, nrm_ref,
                 cd_ref, ad_ref, bd_ref, cg_ref, ag_ref, bg_ref, cr_ref, sr_ref,
                 u_ref, dqT_ref, dk_ref, dvT_ref, rq_ref, rkT_ref, rv_ref, rg_ref,
                 gqT_ref, gk_ref, gvT_ref, gvT2_ref):
    x = h_ref[...]
    ms = jnp.mean(x * x, axis=-1, keepdims=True)
    u = (x * lax.rsqrt(ms + NORM_EPS) * an_ref[...]).astype(BF16)
    u_ref[...] = u

    def proj(c0, n):
        return jnp.dot(u, w_ref[:, c0:c0 + n], preferred_element_type=F32)

    gmat = g_ref[...]
    lo = lax.broadcasted_iota(jnp.int32, (1, LANES), 1) < HEAD64
    first_row = lax.broadcasted_iota(jnp.int32, (ONES_ROWS, TM), 0) == 0
    ones_rows = jnp.where(first_row, 1.0, 0.0).astype(BF16)

    def qk_norm(p, gain):
        out = []
        for c0 in range(0, p.shape[1], 2 * LANES):
            y = p[:, c0:c0 + 2 * LANES]
            w = y.shape[1]
            msq = jnp.dot((y * y).astype(BF16), gmat[0:w, 0:w], preferred_element_type=F32)
            y = y * lax.rsqrt(msq + NORM_EPS)
            out += [y[:, i:i + LANES] * gain for i in range(0, w, LANES)]
        return out

    def rope(y, c, a, b, d):
        return y * c + pltpu.roll(y, LANES - d, 1) * a + pltpu.roll(y, d, 1) * b

    def chunks(p):
        return [p[:, i:i + LANES] for i in range(0, p.shape[1], LANES)]

    qscale = HEAD64 ** -0.5 * LOG2E
    kscale = LANES ** -0.5
    cd, ad, bd = cd_ref[...], ad_ref[...], bd_ref[...]
    cg, ag, bg = cg_ref[...], ag_ref[...], bg_ref[...]
    cr, sr = cr_ref[...], sr_ref[...]

    def diff_q(p):
        for hd, y in enumerate(qk_norm(p, nrm_ref[0:1, :])):
            y = rope(y, cd, ad, bd, 8) * qscale
            dqT_ref[hd, :, 0:TM] = jnp.where(lo, y, 0.0).T.astype(BF16)
            dqT_ref[hd, :, TM:2 * TM] = jnp.where(lo, 0.0, y).T.astype(BF16)

    def diff_k(p):
        for hd, y in enumerate(qk_norm(p, nrm_ref[1:2, :])):
            dk_ref[:, hd * LANES:(hd + 1) * LANES] = rope(y, cd, ad, bd, 8).astype(BF16)

    def diff_v(p):
        for hd, y in enumerate(chunks(p)):
            dvT_ref[hd, 0:LANES, :] = y.T.astype(BF16)
            dvT_ref[hd, LANES:VROWS, :] = ones_rows

    def ret_q(p):
        for hd, y in enumerate(chunks(p)):
            rq_ref[:, hd * LANES:(hd + 1) * LANES] = (y * cr + pltpu.roll(y, LANES // 2, 1) * sr).astype(BF16)

    def ret_k(p):
        for hd, y in enumerate(chunks(p)):
            rkT_ref[hd] = ((y * cr + pltpu.roll(y, LANES // 2, 1) * sr) * kscale).T.astype(BF16)

    def ret_v(p):
        rv_ref[...] = p.astype(BF16)

    def ret_g(p):
        rg_ref[...] = p.astype(BF16)

    def gqa_q(p):
        for c, y in enumerate(qk_norm(p, nrm_ref[2:3, :])):
            y = rope(y, cg, ag, bg, 16) * qscale
            ysw = pltpu.roll(y, LANES // 2, 1)
            if c < 2:
                z_even, z_odd = jnp.where(lo, y, 0.0), jnp.where(lo, ysw, 0.0)
            else:
                z_even, z_odd = jnp.where(lo, 0.0, ysw), jnp.where(lo, 0.0, y)
            for hh, z in ((2 * c, z_even), (2 * c + 1, z_odd)):
                zt = z.T.astype(BF16)
                for qb in range(TM // LANES):
                    gqT_ref[qb, 0:LANES, hh * LANES:(hh + 1) * LANES] = zt[:, qb * LANES:(qb + 1) * LANES]
        gqT_ref[:, LANES:MXU_DIM, :] = jnp.zeros((TM // LANES, MXU_DIM - LANES, NQ), BF16)

    def gqa_kv(p):
        (y,) = qk_norm(p[:, 0:LANES], nrm_ref[3:4, :])
        gk_ref[:, 0:LANES] = rope(y, cg, ag, bg, 16).astype(BF16)
        gk_ref[:, LANES:MXU_DIM] = jnp.zeros((TM, MXU_DIM - LANES), BF16)
        vt = p[:, LANES:2 * LANES].T.astype(BF16)
        gvT_ref[0:LANES, :] = vt
        gvT_ref[LANES:VROWS, :] = ones_rows
        for ku in range(TM // MXU_DIM):
            for g in range(2):
                gvT2_ref[ku, g] = vt[g * HEAD64:(g + 1) * HEAD64, ku * MXU_DIM:(ku + 1) * MXU_DIM]

    order = [gqa_q, diff_q, ret_k, diff_v, diff_k, gqa_kv, ret_q, ret_v, ret_g]
    in_w_in = [diff_q, diff_k, diff_v, ret_q, ret_k, ret_v, ret_g, gqa_q, gqa_kv]
    width = {f: BRANCH_WIDTH for f in in_w_in}
    width[gqa_kv] = 2 * LANES
    start = {f: sum(width[g] for g in in_w_in[:i]) for i, f in enumerate(in_w_in)}
    p_next = proj(start[order[0]], width[order[0]])
    for i, epilogue in enumerate(order):
        p_cur = p_next
        if i + 1 < len(order):
            p_next = proj(start[order[i + 1]], width[order[i + 1]])
        epilogue(p_cur)


def _proj_call(h, attn_norm, w_attn, gmat, nrm, tabs):
    T = h.shape[0]
    S = tabs[0].shape[0]
    nt = T // TM
    tab_spec = pl.BlockSpec((TM, LANES), lambda i: (i % (S // TM), 0))
    row = lambda w: pl.BlockSpec((TM, w), lambda i: (i, 0))
    out_shape = (
        jax.ShapeDtypeStruct((T, D_MODEL), BF16),
        jax.ShapeDtypeStruct((nt, DIFF_HEADS, LANES, 2 * TM), BF16),
        jax.ShapeDtypeStruct((T, BRANCH_WIDTH), BF16),
        jax.ShapeDtypeStruct((nt, DIFF_HEADS, VROWS, TM), BF16),
        jax.ShapeDtypeStruct((T, BRANCH_WIDTH), BF16),
        jax.ShapeDtypeStruct((nt, RET_HEADS, LANES, TM), BF16),
        jax.ShapeDtypeStruct((T, BRANCH_WIDTH), BF16),
        jax.ShapeDtypeStruct((T, BRANCH_WIDTH), BF16),
        jax.ShapeDtypeStruct((T // LANES, MXU_DIM, NQ), BF16),
        jax.ShapeDtypeStruct((T, MXU_DIM), BF16),
        jax.ShapeDtypeStruct((nt, VROWS, TM), BF16),
        jax.ShapeDtypeStruct((T // MXU_DIM, 2, HEAD64, MXU_DIM), BF16),
    )
    out_specs = (
        row(D_MODEL),
        pl.BlockSpec((None, DIFF_HEADS, LANES, 2 * TM), lambda i: (i, 0, 0, 0)),
        row(BRANCH_WIDTH),
        pl.BlockSpec((None, DIFF_HEADS, VROWS, TM), lambda i: (i, 0, 0, 0)),
        row(BRANCH_WIDTH),
        pl.BlockSpec((None, RET_HEADS, LANES, TM), lambda i: (i, 0, 0, 0)),
        row(BRANCH_WIDTH), row(BRANCH_WIDTH),
        pl.BlockSpec((TM // LANES, MXU_DIM, NQ), lambda i: (i, 0, 0)),
        row(MXU_DIM),
        pl.BlockSpec((None, VROWS, TM), lambda i: (i, 0, 0)),
        pl.BlockSpec((TM // MXU_DIM, 2, HEAD64, MXU_DIM), lambda i: (i, 0, 0, 0)),
    )
    in_specs = [row(D_MODEL), _resident((1, D_MODEL)), _resident(w_attn.shape),
                _resident(gmat.shape), _resident(nrm.shape)] + [tab_spec] * 8
    return pl.pallas_call(
        _proj_kernel, grid=(nt,), in_specs=in_specs, out_specs=out_specs,
        out_shape=out_shape, compiler_params=_cparams(1), name="proj",
    )(h, attn_norm, w_attn, gmat, nrm, *tabs)


def _attn_core(fast_ref, qT_ref, k_ref, vT_ref, m_ref, acc_ref):
    n_steps = k_ref.shape[0] // TK
    n_trips = n_steps // KV_UNROLL
    qT = jnp.concatenate([qT_ref[i] for i in range(Q_BLOCKS)], axis=-1)

    def kv(j):
        k = k_ref[pl.ds(pl.multiple_of(j * TK, TK), TK), :]
        per = TK // TM
        vT = jnp.concatenate([vT_ref[j * per + i] for i in range(per)], axis=-1)
        return k, vT

    @pl.when(fast_ref[0] == 1)
    def _():
        def body(jj, carry):
            pv = lsum = None
            for u in range(KV_UNROLL):
                k, vT = kv(jj * KV_UNROLL + u)
                s = jnp.dot(k, qT, preferred_element_type=F32)
                pf = jnp.exp2(s)
                ls = jnp.sum(pf, axis=0, keepdims=True)
                t = jnp.dot(vT[0:LANES], pf.astype(BF16), preferred_element_type=F32)
                pv, lsum = (t, ls) if pv is None else (pv + t, lsum + ls)
            if n_trips == 1:
                acc_ref[0:LANES, :] = pv
                acc_ref[LANES:LANES + 1, :] = lsum
            else:
                acc_ref[0:LANES, :] += pv
                acc_ref[LANES:LANES + 1, :] += lsum
            return carry

        if n_trips > 1:
            acc_ref[...] = jnp.zeros(acc_ref.shape, F32)
        lax.fori_loop(0, n_trips, body, 0)

    @pl.when(fast_ref[0] == 0)
    def _():
        acc_ref[...] = jnp.zeros(acc_ref.shape, F32)
        m_ref[...] = jnp.full(m_ref.shape, -1e30, F32)

        def body(j, carry):
            k, vT = kv(j)
            for c in range(NCOLS // NC):
                cs = slice(c * NC, (c + 1) * NC)
                s = jnp.dot(k, qT[:, cs], preferred_element_type=F32)
                m_old = m_ref[:, cs]
                m_new = jnp.maximum(m_old, jnp.max(s, axis=0, keepdims=True))
                alpha = jnp.exp2(m_old - m_new)
                p = jnp.exp2(s - m_new).astype(BF16)
                pv = jnp.dot(vT, p, preferred_element_type=F32)
                acc_ref[:, cs] = acc_ref[:, cs] * alpha + pv
                m_ref[:, cs] = m_new
            return carry

        lax.fori_loop(0, n_steps, body, 0)


def _diff_kernel(fast_ref, lam_ref, sub_ref, qT_ref, k_ref, vT_ref, o_ref, m_ref, acc_ref, *,
                 lambda_init):
    lv = lam_ref[...]
    lam = (jnp.exp(jnp.sum(lv[0:1] * lv[1:2], axis=-1, keepdims=True))
           - jnp.exp(jnp.sum(lv[2:3] * lv[3:4], axis=-1, keepdims=True)) + lambda_init)
    _attn_core(fast_ref, qT_ref, k_ref, vT_ref, m_ref, acc_ref)

    tq = NQ // 2
    for qb in range(Q_BLOCKS):
        c1 = slice(qb * NQ, qb * NQ + tq)
        c2 = slice(qb * NQ + tq, (qb + 1) * NQ)
        o1 = acc_ref[0:LANES, c1] / acc_ref[LANES:LANES + 1, c1]
        o2 = acc_ref[0:LANES, c2] / acc_ref[LANES:LANES + 1, c2]
        d = o1 - lam * o2
        ms = jnp.mean(d * d, axis=0, keepdims=True)
        dn = d * lax.rsqrt(ms + 1e-5)
        o_ref[qb * tq:(qb + 1) * tq, :] = (
            dn.T * (sub_ref[...] * (1.0 - lambda_init))).astype(o_ref.dtype)


_SMEM_SPEC = pl.BlockSpec(memory_space=pltpu.SMEM)
_ATTN_SCRATCH = [pltpu.VMEM((1, NCOLS), F32), pltpu.VMEM((VROWS, NCOLS), F32)]


def _diff_call(fast, lamv, subln, dqT, dk, dvT, B, S, lambda_init):
    T = dk.shape[0]
    tq = Q_BLOCKS * NQ // 2
    nq = S // tq
    kern = functools.partial(_diff_kernel, lambda_init=lambda_init)
    return pl.pallas_call(
        kern, grid=(B, DIFF_HEADS, nq),
        in_specs=[
            _SMEM_SPEC, _resident(lamv.shape), _resident(subln.shape),
            pl.BlockSpec((Q_BLOCKS, None, LANES, NQ), lambda b, h, i: (b * nq + i, h, 0, 0)),
            pl.BlockSpec((S, LANES), lambda b, h, i: (b, h)),
            pl.BlockSpec((S // TM, None, VROWS, TM), lambda b, h, i: (b, h, 0, 0)),
        ],
        out_specs=pl.BlockSpec((tq, LANES), lambda b, h, i: (b * nq + i, h)),
        out_shape=jax.ShapeDtypeStruct((T, BRANCH_WIDTH), BF16),
        scratch_shapes=_ATTN_SCRATCH,
        compiler_params=_cparams(3), name="diff_attn",
    )(fast, lamv, subln, dqT, dk, dvT)


def _gqa_kernel(fast_ref, qT_ref, k_ref, vT_ref, o_ref, m_ref, acc_ref):
    _attn_core(fast_ref, qT_ref, k_ref, vT_ref, m_ref, acc_ref)
    for qb, c in [(qb, c) for qb in range(Q_BLOCKS) for c in range(GQA_Q_HEADS // 2)]:
        g = (2 * c) // GQA_GROUP
        rows = slice(g * HEAD64, (g + 1) * HEAD64)
        parts = []
        for hh in (2 * c, 2 * c + 1):
            cols = slice(qb * NQ + hh * LANES, qb * NQ + (hh + 1) * LANES)
            parts.append(acc_ref[rows, cols] / acc_ref[LANES:LANES + 1, cols])
        o_ref[qb * LANES:(qb + 1) * LANES, c * LANES:(c + 1) * LANES] = (
            jnp.concatenate(parts, axis=0).T.astype(o_ref.dtype))


def _gqa_call(fast, gqT, gk, gvT, B, S):
    T = gk.shape[0]
    tq = Q_BLOCKS * NQ // GQA_Q_HEADS
    nq = S // tq
    return pl.pallas_call(
        _gqa_kernel, grid=(B, nq),
        in_specs=[
            _SMEM_SPEC,
            pl.BlockSpec((Q_BLOCKS, LANES, NQ), lambda b, i: (b * nq + i, 0, 0)),
            pl.BlockSpec((S, LANES), lambda b, i: (b, 0)),
            pl.BlockSpec((S // TM, VROWS, TM), lambda b, i: (b, 0, 0)),
        ],
        out_specs=pl.BlockSpec((tq, BRANCH_WIDTH), lambda b, i: (b * nq + i, 0)),
        out_shape=jax.ShapeDtypeStruct((T, BRANCH_WIDTH), BF16),
        scratch_shapes=_ATTN_SCRATCH,
        compiler_params=_cparams(2), name="gqa_attn",
    )(fast, gqT, gk, gvT)


S_REGIONS = (0, 64, 128)
O_REGIONS = (192, 208)
POP_LAG = 2
PV_LAG = 3
P_RING = 6


def _gqa_mxu_kernel(qT_ref, k_ref, vT_ref, o_ref, p_ref, l_ref):
    n_slots = 2 * (k_ref.shape[0] // MXU_DIM)
    trips = (n_slots - PV_LAG - 1) // P_RING
    first_loop_slot = n_slots - trips * P_RING
    assert first_loop_slot >= PV_LAG and first_loop_slot % P_RING == (PV_LAG + 1) % P_RING

    @pl.when((pl.program_id(0) == 0) & (pl.program_id(1) == 0))
    def _():
        for mxu in range(2):
            for region in S_REGIONS:
                pltpu.matmul_pop(region, (MXU_DIM, MXU_DIM), F32, mxu)
            for region in O_REGIONS:
                pltpu.matmul_pop(region, (HEAD64, MXU_DIM), F32, mxu)

    l_ref[...] = jnp.zeros(l_ref.shape, F32)

    def q_tile(mxu, n):
        c = 2 * mxu + n % 2
        return qT_ref[:, c * MXU_DIM:(c + 1) * MXU_DIM]

    def slot(n, unit, unit_v, stage_next=True):
        start = unit * MXU_DIM
        if not isinstance(unit, int):
            start = pl.multiple_of(start, MXU_DIM)
        k = k_ref[pl.ds(start, MXU_DIM), :]
        for mxu in range(2):
            if n >= PV_LAG:
                pltpu.matmul_push_rhs(p_ref[mxu, (n - PV_LAG) % P_RING], staging_register=1,
                                      mxu_index=mxu)
            pltpu.matmul_acc_lhs(S_REGIONS[n % 3], k, mxu_index=mxu, load_staged_rhs=0)
            if stage_next:
                pltpu.matmul_push_rhs(q_tile(mxu, n + 1), staging_register=0, mxu_index=mxu)
            if n >= PV_LAG:
                pltpu.matmul_acc_lhs(O_REGIONS[(n - PV_LAG) % 2], vT_ref[unit_v, mxu],
                                     mxu_index=mxu, load_staged_rhs=1)
        for mxu in range(2):
            if n >= POP_LAG:
                pop_scores(mxu, n - POP_LAG)

    def pop_scores(mxu, n):
        s = pltpu.matmul_pop(S_REGIONS[n % 3], (MXU_DIM, MXU_DIM), F32, mxu)
        pf = jnp.exp2(s)
        c = 2 * mxu + n % 2
        l_ref[:, c * MXU_DIM:(c + 1) * MXU_DIM] += jnp.sum(pf, axis=0, keepdims=True)
        p_ref[mxu, n % P_RING] = pf.astype(BF16)

    def values_only(mxu, n, unit_v):
        pltpu.matmul_push_rhs(p_ref[mxu, n % P_RING], staging_register=1, mxu_index=mxu)
        pltpu.matmul_acc_lhs(O_REGIONS[n % 2], vT_ref[unit_v, mxu], mxu_index=mxu,
                             load_staged_rhs=1)

    for mxu in range(2):
        pltpu.matmul_push_rhs(q_tile(mxu, 0), staging_register=0, mxu_index=mxu)
    for n in range(first_loop_slot):
        slot(n, n // 2, max(n - PV_LAG, 0) // 2)

    def trip(t, carry):
        for i in range(P_RING):
            n = first_loop_slot + i
            base = first_loop_slot + P_RING * t + i
            slot(n, base // 2, (base - PV_LAG) // 2)
        return carry

    lax.fori_loop(0, trips, trip, 0)

    for mxu in range(2):
        pltpu.matmul_acc_lhs(S_REGIONS[n_slots % 3], jnp.zeros((16, MXU_DIM), BF16),
                             mxu_index=mxu, load_staged_rhs=0)
    for n in range(n_slots - POP_LAG, n_slots):
        for mxu in range(2):
            pop_scores(mxu, n)
    for mxu in range(2):
        pltpu.matmul_pop(S_REGIONS[n_slots % 3], (16, MXU_DIM), F32, mxu)
    for n in range(n_slots - PV_LAG, n_slots):
        for mxu in range(2):
            values_only(mxu, n, n // 2)

    for mxu in range(2):
        for cc in range(2):
            c = 2 * mxu + cc
            o = pltpu.matmul_pop(O_REGIONS[cc], (HEAD64, MXU_DIM), F32, mxu)
            o = o / l_ref[:, c * MXU_DIM:(c + 1) * MXU_DIM]
            pair = jnp.concatenate([o[:, 0:LANES], o[:, LANES:2 * LANES]], axis=0)
            o_ref[:, c * LANES:(c + 1) * LANES] = pair.T.astype(o_ref.dtype)


def _gqa_mxu_call(gqT, gk, gvT2, B, S):
    T = gk.shape[0]
    tq = NQ // GQA_Q_HEADS
    nq = S // tq
    return pl.pallas_call(
        _gqa_mxu_kernel, grid=(B, nq),
        in_specs=[
            pl.BlockSpec((None, MXU_DIM, NQ), lambda b, i: (b * nq + i, 0, 0)),
            pl.BlockSpec((S, MXU_DIM), lambda b, i: (b, 0)),
            pl.BlockSpec((S // MXU_DIM, 2, HEAD64, MXU_DIM), lambda b, i: (b, 0, 0, 0)),
        ],
        out_specs=pl.BlockSpec((tq, BRANCH_WIDTH), lambda b, i: (b * nq + i, 0)),
        out_shape=jax.ShapeDtypeStruct((T, BRANCH_WIDTH), BF16),
        scratch_shapes=[pltpu.VMEM((2, P_RING, MXU_DIM, MXU_DIM), BF16),
                        pltpu.VMEM((1, NQ), F32)],
        compiler_params=_cparams(2), name="gqa_attn_mxu",
    )(gqT, gk, gvT2)


def _ret_kernel(dec_ref, gain_ref, q_ref, kT_ref, v_ref, g_ref, o_ref, sf_ref, sb_ref):
    C = RET_CHUNK
    hd = pl.program_id(1)
    lf = -jnp.exp(jnp.full((C, C), dec_ref[0, hd], F32))
    lb = -jnp.exp(jnp.full((C, C), dec_ref[1, hd], F32))
    ii = lax.broadcasted_iota(jnp.int32, (C, C), 0).astype(F32)
    jj = lax.broadcasted_iota(jnp.int32, (C, C), 1).astype(F32)
    diff = ii - jj
    decay = jnp.where(diff >= 0, jnp.exp(jnp.maximum(diff, 0.0) * lf),
                      jnp.exp(jnp.maximum(-diff, 0.0) * lb))
    xi_f, zeta_f = jnp.exp((ii + 1.0) * lf), jnp.exp((C - 1.0 - ii) * lf)
    xi_b, zeta_b = jnp.exp((C - ii) * lb), jnp.exp(ii * lb)
    dec_f, dec_b = jnp.exp(C * lf), jnp.exp(C * lb)
    n_tiles = kT_ref.shape[0]
    per_tile = TM // C

    def rows(t, c):
        return pl.ds(pl.multiple_of(t * TM + c * C, C), C)

    def chunk_kv(t, c, zeta):
        vz = (v_ref[rows(t, c), :].astype(F32) * zeta).astype(BF16)
        return jnp.dot(kT_ref[t, :, c * C:(c + 1) * C], vz, preferred_element_type=F32)

    def scans(i, states):
        st_f, st_b = states
        fwd = [(i * RET_SCAN_TILES + t, c) for t in range(RET_SCAN_TILES) for c in range(per_tile)]
        bwd = [(n_tiles - 1 - t, per_tile - 1 - c) for t, c in fwd]
        kv_f = [chunk_kv(t, c, zeta_f) for t, c in fwd]
        kv_b = [chunk_kv(t, c, zeta_b) for t, c in bwd]
        for (tf, cf), kf, (tb, cb), kb in zip(fwd, kv_f, bwd, kv_b):
            sf_ref[tf * per_tile + cf] = st_f.astype(BF16)
            st_f = st_f * dec_f + kf
            sb_ref[tb * per_tile + cb] = st_b.astype(BF16)
            st_b = st_b * dec_b + kb
        return st_f, st_b

    zero = jnp.zeros((C, C), F32)
    lax.fori_loop(0, n_tiles // RET_SCAN_TILES, scans, (zero, zero))

    def outputs(tt, carry):
        chunks = [(tt * RET_OUT_TILES + i, c) for i in range(RET_OUT_TILES) for c in range(per_tile)]
        rs = [rows(t, c) for t, c in chunks]
        qs = [q_ref[r, :] for r in rs]
        a = [jnp.dot(q, kT_ref[t, :, c * C:(c + 1) * C], preferred_element_type=F32)
             for q, (t, c) in zip(qs, chunks)]
        cross = [jnp.dot(q, sf_ref[t * per_tile + c], preferred_element_type=F32) * xi_f
                 + jnp.dot(q, sb_ref[t * per_tile + c], preferred_element_type=F32) * xi_b
                 for q, (t, c) in zip(qs, chunks)]
        a = [(x * decay).astype(BF16) for x in a]
        o = [jnp.dot(x, v_ref[r, :], preferred_element_type=F32) + y
             for x, r, y in zip(a, rs, cross)]
        mu = [jnp.mean(x, axis=-1, keepdims=True) for x in o]
        oc = [x - m for x, m in zip(o, mu)]
        var = [jnp.mean(x * x, axis=-1, keepdims=True) for x in oc]
        for x, s2, r in zip(oc, var, rs):
            y = x * lax.rsqrt(s2 + 1e-5) * gain_ref[...]
            g = g_ref[r, :].astype(F32)
            o_ref[r, :] = (g / (1.0 + jnp.exp(-g)) * y).astype(o_ref.dtype)
        return carry

    lax.fori_loop(0, n_tiles // RET_OUT_TILES, outputs, 0)


def _ret_call(dec, gain, rq, rkT, rv, rg, B, S):
    T = rq.shape[0]
    blk = pl.BlockSpec((S, LANES), lambda b, h: (b, h))
    n_chunks = S // RET_CHUNK
    return pl.pallas_call(
        _ret_kernel, grid=(B, RET_HEADS),
        in_specs=[pl.BlockSpec(memory_space=pltpu.SMEM),
                  pl.BlockSpec((1, LANES), lambda b, h: (0, h)), blk,
                  pl.BlockSpec((S // TM, None, LANES, TM), lambda b, h: (b, h, 0, 0)), blk, blk],
        out_specs=blk,
        out_shape=jax.ShapeDtypeStruct((T, BRANCH_WIDTH), BF16),
        scratch_shapes=[pltpu.VMEM((n_chunks, RET_CHUNK, LANES), BF16),
                        pltpu.VMEM((n_chunks, RET_CHUNK, LANES), BF16)],
        compiler_params=_cparams(2), name="retention",
    )(dec, gain, rq, rkT, rv, rg)


def _merge_kernel(h_ref, u_ref, a_ref, r_ref, c_ref, wg_ref, wb_ref, wo_ref, mn_ref,
                  hn_ref, u2_ref):
    u = u_ref[...]
    merged = None
    for n, br in enumerate((a_ref, r_ref, c_ref)):
        logits = jnp.dot(u, wg_ref[:, n * D_MODEL:(n + 1) * D_MODEL], preferred_element_type=F32)
        gate = 1.0 / (1.0 + jnp.exp(-logits))
        term = gate * jnp.dot(br[...], wb_ref[n], preferred_element_type=F32)
        merged = term if merged is None else merged + term
    merged = merged.astype(BF16)
    halves = [slice(0, TM // 2), slice(TM // 2, TM)]
    hns = [h_ref[r, :] + jnp.dot(merged[r], wo_ref[...], preferred_element_type=F32) for r in halves]
    for r, hn in zip(halves, hns):
        hn_ref[r, :] = hn
        ms = jnp.mean(hn * hn, axis=-1, keepdims=True)
        u2_ref[r, :] = (hn * lax.rsqrt(ms + NORM_EPS) * mn_ref[...]).astype(BF16)


def _merge_call(h, u, a, r, c, wg, wb, wo, mlp_norm):
    T = h.shape[0]
    row = lambda w: pl.BlockSpec((TM, w), lambda i: (i, 0))
    return pl.pallas_call(
        _merge_kernel, grid=(T // TM,),
        in_specs=[row(D_MODEL), row(D_MODEL), row(BRANCH_WIDTH), row(BRANCH_WIDTH), row(BRANCH_WIDTH),
                  _resident(wg.shape), _resident(wb.shape), _resident(wo.shape),
                  _resident((1, D_MODEL))],
        out_specs=(row(D_MODEL), row(D_MODEL)),
        out_shape=(jax.ShapeDtypeStruct((T, D_MODEL), F32),
                   jax.ShapeDtypeStruct((T, D_MODEL), BF16)),
        compiler_params=_cparams(1), name="merge",
    )(h, u, a, r, c, wg, wb, wo, mlp_norm)


def _mlp_kernel(h_ref, u_ref, w1_ref, w2_ref, o_ref):
    u = u_ref[...]
    acc = h_ref[...]
    for c in range(D_FF // D_MODEL):
        cs = slice(c * D_MODEL, (c + 1) * D_MODEL)
        m = jnp.maximum(jnp.dot(u, w1_ref[:, cs], preferred_element_type=F32), 0.0)
        acc = acc + jnp.dot((m * m).astype(BF16), w2_ref[cs, :], preferred_element_type=F32)
    o_ref[...] = acc


def _mlp_call(h, u2, w1, w2):
    T = h.shape[0]
    row = pl.BlockSpec((TM, D_MODEL), lambda i: (i, 0))
    return pl.pallas_call(
        _mlp_kernel, grid=(T // TM,),
        in_specs=[row, row, _resident(w1.shape), _resident(w2.shape)],
        out_specs=row,
        out_shape=jax.ShapeDtypeStruct((T, D_MODEL), F32),
        compiler_params=_cparams(1), name="mlp",
    )(h, u2, w1, w2)


def _rope_tables(S):
    pos = jnp.arange(S, dtype=F32)
    lane = jnp.arange(LANES)
    l64 = lane % HEAD64

    inv = 500000.0 ** (-jnp.arange(0, 16, 2, dtype=F32) / 16)
    ang = pos[:, None] * inv[None, :]
    cos, sin = jnp.cos(ang)[:, l64 % 8], jnp.sin(ang)[:, l64 % 8]
    cd = jnp.where(l64 < 16, cos, 1.0)
    ad = jnp.where(l64 < 8, -sin, 0.0)
    bd = jnp.where((l64 >= 8) & (l64 < 16), sin, 0.0)

    inv = 10000.0 ** (-jnp.arange(0, 32, 2, dtype=F32) / 32)
    row_pos = jnp.floor(pos / GRID_W)
    col_pos = pos - row_pos * GRID_W
    p2 = jnp.where((l64 < 32)[None, :], row_pos[:, None], col_pos[:, None])
    ang = p2 * inv[l64 % 16][None, :]
    cg = jnp.cos(ang)
    ag = jnp.where(l64 % 32 < 16, -jnp.sin(ang), 0.0)
    bg = jnp.where(l64 % 32 >= 16, jnp.sin(ang), 0.0)

    inv = 10000.0 ** (-jnp.arange(0, 128, 2, dtype=F32) / 128)
    ang = pos[:, None] * inv[l64][None, :]
    cr = jnp.cos(ang)
    sr = jnp.where(lane < LANES // 2, -jnp.sin(ang), jnp.sin(ang))
    return (cd, ad, bd, cg, ag, bg, cr, sr)


def _fast_flag(q_gain, k_gain):
    bound = (HEAD64 ** 0.5 * LOG2E * 1.02) * jnp.max(jnp.abs(q_gain)) * jnp.max(jnp.abs(k_gain))
    return (bound <= FAST_SOFTMAX_BOUND).astype(jnp.int32).reshape(1)


def kernel(x, attn_norm, w_in, diff_q_norm, diff_k_norm, diff_lam_q1, diff_lam_k1, diff_lam_q2, diff_lam_k2, diff_subln, ret_decay_fwd, ret_decay_bwd, ret_group_norm, gqa_q_norm, gqa_k_norm, w_branch, w_out, mlp_norm, w_mlp_in, w_mlp_out):
    B, S, D = x.shape
    T = B * S
    tabs = _rope_tables(S)
    blk = jnp.arange(2 * LANES) // HEAD64
    gmat = jnp.where(blk[:, None] == blk[None, :], 1.0 / HEAD64, 0.0).astype(BF16)
    tile2 = lambda v: jnp.concatenate([v, v]).astype(F32)

    h = x.reshape(T, D)
    for l in range(DEPTH):
        lambda_init = 0.8 - 0.6 * math.exp(-0.3 * l)
        w_attn = w_in[l, :, :ATTN_COLS].astype(BF16)
        w_gate = w_in[l, :, ATTN_COLS:].astype(BF16)
        nrm = jnp.stack([tile2(diff_q_norm[l]), tile2(diff_k_norm[l]),
                         tile2(gqa_q_norm[l]), tile2(gqa_k_norm[l])])
        (u, dqT, dk, dvT, rq, rkT, rv, rg, gqT, gk, gvT, gvT2) = _proj_call(
            h, attn_norm[l].reshape(1, D).astype(F32), w_attn, gmat, nrm, tabs)

        lamv = jnp.stack([diff_lam_q1[l], diff_lam_k1[l], diff_lam_q2[l], diff_lam_k2[l]]).astype(F32)
        a = _diff_call(_fast_flag(diff_q_norm[l], diff_k_norm[l]), lamv,
                       diff_subln[l].reshape(1, LANES).astype(F32), dqT, dk, dvT, B, S, lambda_init)
        dec = jnp.stack([ret_decay_fwd[l], ret_decay_bwd[l]]).astype(F32)
        r = _ret_call(dec, ret_group_norm[l].reshape(1, BRANCH_WIDTH).astype(F32), rq, rkT, rv, rg, B, S)
        gqa_fast = _fast_flag(gqa_q_norm[l], gqa_k_norm[l])
        c = lax.cond(gqa_fast[0] == 1,
                     lambda ops: _gqa_mxu_call(ops[0], ops[1], ops[3], B, S),
                     lambda ops: _gqa_call(gqa_fast, ops[0], ops[1], ops[2], B, S),
                     (gqT, gk, gvT, gvT2))

        h, u2 = _merge_call(h, u, a, r, c, w_gate, w_branch[l].astype(BF16),
                            w_out[l].astype(BF16), mlp_norm[l].reshape(1, D).astype(F32))
        h = _mlp_call(h, u2, w_mlp_in[l].astype(BF16), w_mlp_out[l].astype(BF16))
    return h.reshape(B, S, D)
```

```python
import functools
import math

import jax
import jax.numpy as jnp
from jax import lax
from jax.experimental import pallas as pl
from jax.experimental.pallas import tpu as pltpu

F32 = jnp.float32
BF16 = jnp.bfloat16

D_MODEL = 1024
DEPTH = 2
GRID_W = 64
NORM_EPS = 1e-6
HEAD64 = 64
DIFF_HEADS = 4
RET_HEADS = 4
RET_CHUNK = 128
GQA_Q_HEADS = 8
GQA_GROUP = 4
N_BRANCHES = 3
BRANCH_WIDTH = 512
D_FF = 4 * D_MODEL
LANES = 128
MXU_DIM = 256
ATTN_COLS = 8 * BRANCH_WIDTH + 2 * LANES

TM = 512
TK = 2048
NQ = 1024
Q_BLOCKS = 2
NCOLS = Q_BLOCKS * NQ
NC = 256
KV_UNROLL = 4
RET_SCAN_TILES = 4
RET_OUT_TILES = 4
ONES_ROWS = 16
VROWS = LANES + ONES_ROWS
FAST_SOFTMAX_BOUND = 64.0
VMEM_LIMIT = 56 * 1024 * 1024
LOG2E = 1.4426950408889634


def _cparams(n_axes):
    return pltpu.CompilerParams(dimension_semantics=("arbitrary",) * n_axes,
                                vmem_limit_bytes=VMEM_LIMIT)


def _resident(shape):
    nd = len(shape)
    return pl.BlockSpec(shape, lambda *_: (0,) * nd, pipeline_mode=pl.Buffered(1))


def _proj_kernel(h_ref, an_ref, w_ref, g_ref, nrm_ref,
                 cd_ref, ad_ref, bd_ref, cg_ref, ag_ref, bg_ref, cr_ref, sr_ref,
                 u_ref, dqT_ref, dk_ref, dvT_ref, rq_ref, rkT_ref, rv_ref, rg_ref,
                 gqT_ref, gk_ref, gvT_ref, gvT2_ref):
    x = h_ref[...]
    ms = jnp.mean(x * x, axis=-1, keepdims=True)
    u = (x * lax.rsqrt(ms + NORM_EPS) * an_ref[...]).astype(BF16)
    u_ref[...] = u

    def proj(c0, n):
        return jnp.dot(u, w_ref[:, c0:c0 + n], preferred_element_type=F32)

    gmat = g_ref[...]
    lo = lax.broadcasted_iota(jnp.int32, (1, LANES), 1) < HEAD64
    first_row = lax.broadcasted_iota(jnp.int32, (ONES_ROWS, TM), 0) == 0
    ones_rows = jnp.where(first_row, 1.0, 0.0).astype(BF16)

    def qk_norm(p, gain):
        out = []
        for c0 in range(0, p.shape[1], 2 * LANES):
            y = p[:, c0:c0 + 2 * LANES]
            w = y.shape[1]
            msq = jnp.dot((y * y).astype(BF16), gmat[0:w, 0:w], preferred_element_type=F32)
            y = y * lax.rsqrt(msq + NORM_EPS)
            out += [y[:, i:i + LANES] * gain for i in range(0, w, LANES)]
        return out

    def rope(y, c, a, b, d):
        return y * c + pltpu.roll(y, LANES - d, 1) * a + pltpu.roll(y, d, 1) * b

    def chunks(p):
        return [p[:, i:i + LANES] for i in range(0, p.shape[1], LANES)]

    qscale = HEAD64 ** -0.5 * LOG2E
    kscale = LANES ** -0.5
    cd, ad, bd = cd_ref[...], ad_ref[...], bd_ref[...]
    cg, ag, bg = cg_ref[...], ag_ref[...], bg_ref[...]
    cr, sr = cr_ref[...], sr_ref[...]

    def diff_q(p):
        for hd, y in enumerate(qk_norm(p, nrm_ref[0:1, :])):
            y = rope(y, cd, ad, bd, 8) * qscale
            dqT_ref[hd, :, 0:TM] = jnp.where(lo, y, 0.0).T.astype(BF16)
            dqT_ref[hd, :, TM:2 * TM] = jnp.where(lo, 0.0, y).T.astype(BF16)

    def diff_k(p):
        for hd, y in enumerate(qk_norm(p, nrm_ref[1:2, :])):
            dk_ref[:, hd * LANES:(hd + 1) * LANES] = rope(y, cd, ad, bd, 8).astype(BF16)

    def diff_v(p):
        for hd, y in enumerate(chunks(p)):
            dvT_ref[hd, 0:LANES, :] = y.T.astype(BF16)
            dvT_ref[hd, LANES:VROWS, :] = ones_rows

    def ret_q(p):
        for hd, y in enumerate(chunks(p)):
            rq_ref[:, hd * LANES:(hd + 1) * LANES] = (y * cr + pltpu.roll(y, LANES // 2, 1) * sr).astype(BF16)

    def ret_k(p):
        for hd, y in enumerate(chunks(p)):
            rkT_ref[hd] = ((y * cr + pltpu.roll(y, LANES // 2, 1) * sr) * kscale).T.astype(BF16)

    def ret_v(p):
        rv_ref[...] = p.astype(BF16)

    def ret_g(p):
        rg_ref[...] = p.astype(BF16)

    def gqa_q(p):
        for c, y in enumerate(qk_norm(p, nrm_ref[2:3, :])):
            y = rope(y, cg, ag, bg, 16) * qscale
            ysw = pltpu.roll(y, LANES // 2, 1)
            if c < 2:
                z_even, z_odd = jnp.where(lo, y, 0.0), jnp.where(lo, ysw, 0.0)
            else:
                z_even, z_odd = jnp.where(lo, 0.0, ysw), jnp.where(lo, 0.0, y)
            for hh, z in ((2 * c, z_even), (2 * c + 1, z_odd)):
                zt = z.T.astype(BF16)
                for qb in range(TM // LANES):
                    gqT_ref[qb, 0:LANES, hh * LANES:(hh + 1) * LANES] = zt[:, qb * LANES:(qb + 1) * LANES]
        gqT_ref[:, LANES:MXU_DIM, :] = jnp.zeros((TM // LANES, MXU_DIM - LANES, NQ), BF16)

    def gqa_kv(p):
        (y,) = qk_norm(p[:, 0:LANES], nrm_ref[3:4, :])
        gk_ref[:, 0:LANES] = rope(y, cg, ag, bg, 16).astype(BF16)
        gk_ref[:, LANES:MXU_DIM] = jnp.zeros((TM, MXU_DIM - LANES), BF16)
        vt = p[:, LANES:2 * LANES].T.astype(BF16)
        gvT_ref[0:LANES, :] = vt
        gvT_ref[LANES:VROWS, :] = ones_rows
        for ku in range(TM // MXU_DIM):
            for g in range(2):
                gvT2_ref[ku, g] = vt[g * HEAD64:(g + 1) * HEAD64, ku * MXU_DIM:(ku + 1) * MXU_DIM]

    order = [gqa_q, diff_q, ret_k, diff_v, diff_k, gqa_kv, ret_q, ret_v, ret_g]
    in_w_in = [diff_q, diff_k, diff_v, ret_q, ret_k, ret_v, ret_g, gqa_q, gqa_kv]
    width = {f: BRANCH_WIDTH for f in in_w_in}
    width[gqa_kv] = 2 * LANES
    start = {f: sum(width[g] for g in in_w_in[:i]) for i, f in enumerate(in_w_in)}
    p_next = proj(start[order[0]], width[order[0]])
    for i, epilogue in enumerate(order):
        p_cur = p_next
        if i + 1 < len(order):
            p_next = proj(start[order[i + 1]], width[order[i + 1]])
        epilogue(p_cur)


def _proj_call(h, attn_norm, w_attn, gmat, nrm, tabs):
    T = h.shape[0]
    S = tabs[0].shape[0]
    nt = T // TM
    tab_spec = pl.BlockSpec((TM, LANES), lambda i: (i % (S // TM), 0))
    row = lambda w: pl.BlockSpec((TM, w), lambda i: (i, 0))
    out_shape = (
        jax.ShapeDtypeStruct((T, D_MODEL), BF16),
        jax.ShapeDtypeStruct((nt, DIFF_HEADS, LANES, 2 * TM), BF16),
        jax.ShapeDtypeStruct((T, BRANCH_WIDTH), BF16),
        jax.ShapeDtypeStruct((nt, DIFF_HEADS, VROWS, TM), BF16),
        jax.ShapeDtypeStruct((T, BRANCH_WIDTH), BF16),
        jax.ShapeDtypeStruct((nt, RET_HEADS, LANES, TM), BF16),
        jax.ShapeDtypeStruct((T, BRANCH_WIDTH), BF16),
        jax.ShapeDtypeStruct((T, BRANCH_WIDTH), BF16),
        jax.ShapeDtypeStruct((T // LANES, MXU_DIM, NQ), BF16),
        jax.ShapeDtypeStruct((T, MXU_DIM), BF16),
        jax.ShapeDtypeStruct((nt, VROWS, TM), BF16),
        jax.ShapeDtypeStruct((T // MXU_DIM, 2, HEAD64, MXU_DIM), BF16),
    )
    out_specs = (
        row(D_MODEL),
        pl.BlockSpec((None, DIFF_HEADS, LANES, 2 * TM), lambda i: (i, 0, 0, 0)),
        row(BRANCH_WIDTH),
        pl.BlockSpec((None, DIFF_HEADS, VROWS, TM), lambda i: (i, 0, 0, 0)),
        row(BRANCH_WIDTH),
        pl.BlockSpec((None, RET_HEADS, LANES, TM), lambda i: (i, 0, 0, 0)),
        row(BRANCH_WIDTH), row(BRANCH_WIDTH),
        pl.BlockSpec((TM // LANES, MXU_DIM, NQ), lambda i: (i, 0, 0)),
        row(MXU_DIM),
        pl.BlockSpec((None, VROWS, TM), lambda i: (i, 0, 0)),
        pl.BlockSpec((TM // MXU_DIM, 2, HEAD64, MXU_DIM), lambda i: (i, 0, 0, 0)),
    )
    in_specs = [row(D_MODEL), _resident((1, D_MODEL)), _resident(w_attn.shape),
                _resident(gmat.shape), _resident(nrm.shape)] + [tab_spec] * 8
    return pl.pallas_call(
        _proj_kernel, grid=(nt,), in_specs=in_specs, out_specs=out_specs,
        out_shape=out_shape, compiler_params=_cparams(1), name="proj",
    )(h, attn_norm, w_attn, gmat, nrm, *tabs)


def _attn_core(fast_ref, qT_ref, k_ref, vT_ref, m_ref, acc_ref):
    n_steps = k_ref.shape[0] // TK
    n_trips = n_steps // KV_UNROLL
    qT = jnp.concatenate([qT_ref[i] for i in range(Q_BLOCKS)], axis=-1)

    def kv(j):
        k = k_ref[pl.ds(pl.multiple_of(j * TK, TK), TK), :]
        per = TK // TM
        vT = jnp.concatenate([vT_ref[j * per + i] for i in range(per)], axis=-1)
        return k, vT

    @pl.when(fast_ref[0] == 1)
    def _():
        def body(jj, carry):
            pv = lsum = None
            for u in range(KV_UNROLL):
                k, vT = kv(jj * KV_UNROLL + u)
                s = jnp.dot(k, qT, preferred_element_type=F32)
                pf = jnp.exp2(s)
                ls = jnp.sum(pf, axis=0, keepdims=True)
                t = jnp.dot(vT[0:LANES], pf.astype(BF16), preferred_element_type=F32)
                pv, lsum = (t, ls) if pv is None else (pv + t, lsum + ls)
            if n_trips == 1:
                acc_ref[0:LANES, :] = pv
                acc_ref[LANES:LANES + 1, :] = lsum
            else:
                acc_ref[0:LANES, :] += pv
                acc_ref[LANES:LANES + 1, :] += lsum
            return carry

        if n_trips > 1:
            acc_ref[...] = jnp.zeros(acc_ref.shape, F32)
        lax.fori_loop(0, n_trips, body, 0)

    @pl.when(fast_ref[0] == 0)
    def _():
        acc_ref[...] = jnp.zeros(acc_ref.shape, F32)
        m_ref[...] = jnp.full(m_ref.shape, -1e30, F32)

        def body(j, carry):
            k, vT = kv(j)
            for c in range(NCOLS // NC):
                cs = slice(c * NC, (c + 1) * NC)
                s = jnp.dot(k, qT[:, cs], preferred_element_type=F32)
                m_old = m_ref[:, cs]
                m_new = jnp.maximum(m_old, jnp.max(s, axis=0, keepdims=True))
                alpha = jnp.exp2(m_old - m_new)
                p = jnp.exp2(s - m_new).astype(BF16)
                pv = jnp.dot(vT, p, preferred_element_type=F32)
                acc_ref[:, cs] = acc_ref[:, cs] * alpha + pv
                m_ref[:, cs] = m_new
            return carry

        lax.fori_loop(0, n_steps, body, 0)


def _diff_kernel(fast_ref, lam_ref, sub_ref, qT_ref, k_ref, vT_ref, o_ref, m_ref, acc_ref, *,
                 lambda_init):
    lv = lam_ref[...]
    lam = (jnp.exp(jnp.sum(lv[0:1] * lv[1:2], axis=-1, keepdims=True))
           - jnp.exp(jnp.sum(lv[2:3] * lv[3:4], axis=-1, keepdims=True)) + lambda_init)
    _attn_core(fast_ref, qT_ref, k_ref, vT_ref, m_ref, acc_ref)

    tq = NQ // 2
    for qb in range(Q_BLOCKS):
        c1 = slice(qb * NQ, qb * NQ + tq)
        c2 = slice(qb * NQ + tq, (qb + 1) * NQ)
        o1 = acc_ref[0:LANES, c1] / acc_ref[LANES:LANES + 1, c1]
        o2 = acc_ref[0:LANES, c2] / acc_ref[LANES:LANES + 1, c2]
        d = o1 - lam * o2
        ms = jnp.mean(d * d, axis=0, keepdims=True)
        dn = d * lax.rsqrt(ms + 1e-5)
        o_ref[qb * tq:(qb + 1) * tq, :] = (
            dn.T * (sub_ref[...] * (1.0 - lambda_init))).astype(o_ref.dtype)


_SMEM_SPEC = pl.BlockSpec(memory_space=pltpu.SMEM)
_ATTN_SCRATCH = [pltpu.VMEM((1, NCOLS), F32), pltpu.VMEM((VROWS, NCOLS), F32)]


def _diff_call(fast, lamv, subln, dqT, dk, dvT, B, S, lambda_init):
    T = dk.shape[0]
    tq = Q_BLOCKS * NQ // 2
    nq = S // tq
    kern = functools.partial(_diff_kernel, lambda_init=lambda_init)
    return pl.pallas_call(
        kern, grid=(B, DIFF_HEADS, nq),
        in_specs=[
            _SMEM_SPEC, _resident(lamv.shape), _resident(subln.shape),
            pl.BlockSpec((Q_BLOCKS, None, LANES, NQ), lambda b, h, i: (b * nq + i, h, 0, 0)),
            pl.BlockSpec((S, LANES), lambda b, h, i: (b, h)),
            pl.BlockSpec((S // TM, None, VROWS, TM), lambda b, h, i: (b, h, 0, 0)),
        ],
        out_specs=pl.BlockSpec((tq, LANES), lambda b, h, i: (b * nq + i, h)),
        out_shape=jax.ShapeDtypeStruct((T, BRANCH_WIDTH), BF16),
        scratch_shapes=_ATTN_SCRATCH,
        compiler_params=_cparams(3), name="diff_attn",
    )(fast, lamv, subln, dqT, dk, dvT)


def _gqa_kernel(fast_ref, qT_ref, k_ref, vT_ref, o_ref, m_ref, acc_ref):
    _attn_core(fast_ref, qT_ref, k_ref, vT_ref, m_ref, acc_ref)
    for qb, c in [(qb, c) for qb in range(Q_BLOCKS) for c in range(GQA_Q_HEADS // 2)]:
        g = (2 * c) // GQA_GROUP
        rows = slice(g * HEAD64, (g + 1) * HEAD64)
        parts = []
        for hh in (2 * c, 2 * c + 1):
            cols = slice(qb * NQ + hh * LANES, qb * NQ + (hh + 1) * LANES)
            parts.append(acc_ref[rows, cols] / acc_ref[LANES:LANES + 1, cols])
        o_ref[qb * LANES:(qb + 1) * LANES, c * LANES:(c + 1) * LANES] = (
            jnp.concatenate(parts, axis=0).T.astype(o_ref.dtype))


def _gqa_call(fast, gqT, gk, gvT, B, S):
    T = gk.shape[0]
    tq = Q_BLOCKS * NQ // GQA_Q_HEADS
    nq = S // tq
    return pl.pallas_call(
        _gqa_kernel, grid=(B, nq),
        in_specs=[
            _SMEM_SPEC,
            pl.BlockSpec((Q_BLOCKS, LANES, NQ), lambda b, i: (b * nq + i, 0, 0)),
            pl.BlockSpec((S, LANES), lambda b, i: (b, 0)),
            pl.BlockSpec((S // TM, VROWS, TM), lambda b, i: (b, 0, 0)),
        ],
        out_specs=pl.BlockSpec((tq, BRANCH_WIDTH), lambda b, i: (b * nq + i, 0)),
        out_shape=jax.ShapeDtypeStruct((T, BRANCH_WIDTH), BF16),
        scratch_shapes=_ATTN_SCRATCH,
        compiler_params=_cparams(2), name="gqa_attn",
    )(fast, gqT, gk, gvT)


S_REGIONS = (0, 64, 128)
O_REGIONS = (192, 208)
POP_LAG = 2
PV_LAG = 3
P_RING = 6
TRIP_SLOTS = 30


def _gqa_mxu_kernel(qT_ref, k_ref, vT_ref, o_ref, p_ref, l_ref):
    n_slots = 2 * (k_ref.shape[0] // MXU_DIM)
    trips = (n_slots - PV_LAG - 1) // TRIP_SLOTS
    first_loop_slot = n_slots - trips * TRIP_SLOTS
    assert first_loop_slot >= PV_LAG and TRIP_SLOTS % P_RING == 0

    @pl.when((pl.program_id(0) == 0) & (pl.program_id(1) == 0))
    def _():
        for mxu in range(2):
            for region in S_REGIONS:
                pltpu.matmul_pop(region, (MXU_DIM, MXU_DIM), F32, mxu)
            for region in O_REGIONS:
                pltpu.matmul_pop(region, (HEAD64, MXU_DIM), F32, mxu)

    l_ref[...] = jnp.zeros(l_ref.shape, F32)

    def q_tile(mxu, n):
        c = 2 * mxu + n % 2
        return qT_ref[:, c * MXU_DIM:(c + 1) * MXU_DIM]

    def slot(n, unit, unit_v, stage_next=True):
        start = unit * MXU_DIM
        if not isinstance(unit, int):
            start = pl.multiple_of(start, MXU_DIM)
        k = k_ref[pl.ds(start, MXU_DIM), :]
        for mxu in range(2):
            if n >= PV_LAG:
                pltpu.matmul_push_rhs(p_ref[mxu, (n - PV_LAG) % P_RING], staging_register=1,
                                      mxu_index=mxu)
            pltpu.matmul_acc_lhs(S_REGIONS[n % 3], k, mxu_index=mxu, load_staged_rhs=0)
            if stage_next:
                pltpu.matmul_push_rhs(q_tile(mxu, n + 1), staging_register=0, mxu_index=mxu)
            if n >= PV_LAG:
                pltpu.matmul_acc_lhs(O_REGIONS[(n - PV_LAG) % 2], vT_ref[unit_v, mxu],
                                     mxu_index=mxu, load_staged_rhs=1)
        for mxu in range(2):
            if n >= POP_LAG:
                pop_scores(mxu, n - POP_LAG)

    def pop_scores(mxu, n):
        s = pltpu.matmul_pop(S_REGIONS[n % 3], (MXU_DIM, MXU_DIM), F32, mxu)
        pf = jnp.exp2(s)
        c = 2 * mxu + n % 2
        l_ref[:, c * MXU_DIM:(c + 1) * MXU_DIM] += jnp.sum(pf, axis=0, keepdims=True)
        p_ref[mxu, n % P_RING] = pf.astype(BF16)

    def values_only(mxu, n, unit_v):
        pltpu.matmul_push_rhs(p_ref[mxu, n % P_RING], staging_register=1, mxu_index=mxu)
        pltpu.matmul_acc_lhs(O_REGIONS[n % 2], vT_ref[unit_v, mxu], mxu_index=mxu,
                             load_staged_rhs=1)

    for mxu in range(2):
        pltpu.matmul_push_rhs(q_tile(mxu, 0), staging_register=0, mxu_index=mxu)
    for n in range(first_loop_slot):
        slot(n, n // 2, max(n - PV_LAG, 0) // 2)

    def trip(t, carry):
        for i in range(TRIP_SLOTS):
            n = first_loop_slot + i
            base = first_loop_slot + TRIP_SLOTS * t + i
            slot(n, base // 2, (base - PV_LAG) // 2)
        return carry

    lax.fori_loop(0, trips, trip, 0)

    for mxu in range(2):
        pltpu.matmul_acc_lhs(S_REGIONS[n_slots % 3], jnp.zeros((16, MXU_DIM), BF16),
                             mxu_index=mxu, load_staged_rhs=0)
    for n in range(n_slots - POP_LAG, n_slots):
        for mxu in range(2):
            pop_scores(mxu, n)
    for mxu in range(2):
        pltpu.matmul_pop(S_REGIONS[n_slots % 3], (16, MXU_DIM), F32, mxu)
    for n in range(n_slots - PV_LAG, n_slots):
        for mxu in range(2):
            values_only(mxu, n, n // 2)

    for mxu in range(2):
        for cc in range(2):
            c = 2 * mxu + cc
            o = pltpu.matmul_pop(O_REGIONS[cc], (HEAD64, MXU_DIM), F32, mxu)
            o = o / l_ref[:, c * MXU_DIM:(c + 1) * MXU_DIM]
            pair = jnp.concatenate([o[:, 0:LANES], o[:, LANES:2 * LANES]], axis=0)
            o_ref[:, c * LANES:(c + 1) * LANES] = pair.T.astype(o_ref.dtype)


def _gqa_mxu_call(gqT, gk, gvT2, B, S):
    T = gk.shape[0]
    tq = NQ // GQA_Q_HEADS
    nq = S // tq
    return pl.pallas_call(
        _gqa_mxu_kernel, grid=(B, nq),
        in_specs=[
            pl.BlockSpec((None, MXU_DIM, NQ), lambda b, i: (b * nq + i, 0, 0)),
            pl.BlockSpec((S, MXU_DIM), lambda b, i: (b, 0)),
            pl.BlockSpec((S // MXU_DIM, 2, HEAD64, MXU_DIM), lambda b, i: (b, 0, 0, 0)),
        ],
        out_specs=pl.BlockSpec((tq, BRANCH_WIDTH), lambda b, i: (b * nq + i, 0)),
        out_shape=jax.ShapeDtypeStruct((T, BRANCH_WIDTH), BF16),
        scratch_shapes=[pltpu.VMEM((2, P_RING, MXU_DIM, MXU_DIM), BF16),
                        pltpu.VMEM((1, NQ), F32)],
        compiler_params=_cparams(2), name="gqa_attn_mxu",
    )(gqT, gk, gvT2)


def _ret_kernel(dec_ref, gain_ref, q_ref, kT_ref, v_ref, g_ref, o_ref, sf_ref, sb_ref):
    C = RET_CHUNK
    hd = pl.program_id(1)
    lf = -jnp.exp(jnp.full((C, C), dec_ref[0, hd], F32))
    lb = -jnp.exp(jnp.full((C, C), dec_ref[1, hd], F32))
    ii = lax.broadcasted_iota(jnp.int32, (C, C), 0).astype(F32)
    jj = lax.broadcasted_iota(jnp.int32, (C, C), 1).astype(F32)
    diff = ii - jj
    decay = jnp.where(diff >= 0, jnp.exp(jnp.maximum(diff, 0.0) * lf),
                      jnp.exp(jnp.maximum(-diff, 0.0) * lb))
    xi_f, zeta_f = jnp.exp((ii + 1.0) * lf), jnp.exp((C - 1.0 - ii) * lf)
    xi_b, zeta_b = jnp.exp((C - ii) * lb), jnp.exp(ii * lb)
    dec_f, dec_b = jnp.exp(C * lf), jnp.exp(C * lb)
    n_tiles = kT_ref.shape[0]
    per_tile = TM // C

    def rows(t, c):
        return pl.ds(pl.multiple_of(t * TM + c * C, C), C)

    def chunk_kv(t, c, zeta):
        vz = (v_ref[rows(t, c), :].astype(F32) * zeta).astype(BF16)
        return jnp.dot(kT_ref[t, :, c * C:(c + 1) * C], vz, preferred_element_type=F32)

    def scans(i, states):
        st_f, st_b = states
        fwd = [(i * RET_SCAN_TILES + t, c) for t in range(RET_SCAN_TILES) for c in range(per_tile)]
        bwd = [(n_tiles - 1 - t, per_tile - 1 - c) for t, c in fwd]
        kv_f = [chunk_kv(t, c, zeta_f) for t, c in fwd]
        kv_b = [chunk_kv(t, c, zeta_b) for t, c in bwd]
        for (tf, cf), kf, (tb, cb), kb in zip(fwd, kv_f, bwd, kv_b):
            sf_ref[tf * per_tile + cf] = st_f.astype(BF16)
            st_f = st_f * dec_f + kf
            sb_ref[tb * per_tile + cb] = st_b.astype(BF16)
            st_b = st_b * dec_b + kb
        return st_f, st_b

    zero = jnp.zeros((C, C), F32)
    lax.fori_loop(0, n_tiles // RET_SCAN_TILES, scans, (zero, zero))

    def outputs(tt, carry):
        chunks = [(tt * RET_OUT_TILES + i, c) for i in range(RET_OUT_TILES) for c in range(per_tile)]
        rs = [rows(t, c) for t, c in chunks]
        qs = [q_ref[r, :] for r in rs]
        a = [jnp.dot(q, kT_ref[t, :, c * C:(c + 1) * C], preferred_element_type=F32)
             for q, (t, c) in zip(qs, chunks)]
        cross = [jnp.dot(q, sf_ref[t * per_tile + c], preferred_element_type=F32) * xi_f
                 + jnp.dot(q, sb_ref[t * per_tile + c], preferred_element_type=F32) * xi_b
                 for q, (t, c) in zip(qs, chunks)]
        a = [(x * decay).astype(BF16) for x in a]
        o = [jnp.dot(x, v_ref[r, :], preferred_element_type=F32) + y
             for x, r, y in zip(a, rs, cross)]
        mu = [jnp.mean(x, axis=-1, keepdims=True) for x in o]
        oc = [x - m for x, m in zip(o, mu)]
        var = [jnp.mean(x * x, axis=-1, keepdims=True) for x in oc]
        for x, s2, r in zip(oc, var, rs):
            y = x * lax.rsqrt(s2 + 1e-5) * gain_ref[...]
            g = g_ref[r, :].astype(F32)
            o_ref[r, :] = (g / (1.0 + jnp.exp(-g)) * y).astype(o_ref.dtype)
        return carry

    lax.fori_loop(0, n_tiles // RET_OUT_TILES, outputs, 0)


def _ret_call(dec, gain, rq, rkT, rv, rg, B, S):
    T = rq.shape[0]
    blk = pl.BlockSpec((S, LANES), lambda b, h: (b, h))
    n_chunks = S // RET_CHUNK
    return pl.pallas_call(
        _ret_kernel, grid=(B, RET_HEADS),
        in_specs=[pl.BlockSpec(memory_space=pltpu.SMEM),
                  pl.BlockSpec((1, LANES), lambda b, h: (0, h)), blk,
                  pl.BlockSpec((S // TM, None, LANES, TM), lambda b, h: (b, h, 0, 0)), blk, blk],
        out_specs=blk,
        out_shape=jax.ShapeDtypeStruct((T, BRANCH_WIDTH), BF16),
        scratch_shapes=[pltpu.VMEM((n_chunks, RET_CHUNK, LANES), BF16),
                        pltpu.VMEM((n_chunks, RET_CHUNK, LANES), BF16)],
        compiler_params=_cparams(2), name="retention",
    )(dec, gain, rq, rkT, rv, rg)


def _merge_kernel(h_ref, u_ref, a_ref, r_ref, c_ref, wg_ref, wb_ref, wo_ref, mn_ref,
                  hn_ref, u2_ref):
    u = u_ref[...]
    merged = None
    for n, br in enumerate((a_ref, r_ref, c_ref)):
        logits = jnp.dot(u, wg_ref[:, n * D_MODEL:(n + 1) * D_MODEL], preferred_element_type=F32)
        gate = 1.0 / (1.0 + jnp.exp(-logits))
        term = gate * jnp.dot(br[...], wb_ref[n], preferred_element_type=F32)
        merged = term if merged is None else merged + term
    merged = merged.astype(BF16)
    halves = [slice(0, TM // 2), slice(TM // 2, TM)]
    hns = [h_ref[r, :] + jnp.dot(merged[r], wo_ref[...], preferred_element_type=F32) for r in halves]
    for r, hn in zip(halves, hns):
        hn_ref[r, :] = hn
        ms = jnp.mean(hn * hn, axis=-1, keepdims=True)
        u2_ref[r, :] = (hn * lax.rsqrt(ms + NORM_EPS) * mn_ref[...]).astype(BF16)


def _merge_call(h, u, a, r, c, wg, wb, wo, mlp_norm):
    T = h.shape[0]
    row = lambda w: pl.BlockSpec((TM, w), lambda i: (i, 0))
    return pl.pallas_call(
        _merge_kernel, grid=(T // TM,),
        in_specs=[row(D_MODEL), row(D_MODEL), row(BRANCH_WIDTH), row(BRANCH_WIDTH), row(BRANCH_WIDTH),
                  _resident(wg.shape), _resident(wb.shape), _resident(wo.shape),
                  _resident((1, D_MODEL))],
        out_specs=(row(D_MODEL), row(D_MODEL)),
        out_shape=(jax.ShapeDtypeStruct((T, D_MODEL), F32),
                   jax.ShapeDtypeStruct((T, D_MODEL), BF16)),
        compiler_params=_cparams(1), name="merge",
    )(h, u, a, r, c, wg, wb, wo, mlp_norm)


def _mlp_kernel(h_ref, u_ref, w1_ref, w2_ref, o_ref):
    u = u_ref[...]
    acc = h_ref[...]
    for c in range(D_FF // D_MODEL):
        cs = slice(c * D_MODEL, (c + 1) * D_MODEL)
        m = jnp.maximum(jnp.dot(u, w1_ref[:, cs], preferred_element_type=F32), 0.0)
        acc = acc + jnp.dot((m * m).astype(BF16), w2_ref[cs, :], preferred_element_type=F32)
    o_ref[...] = acc


def _mlp_call(h, u2, w1, w2):
    T = h.shape[0]
    row = pl.BlockSpec((TM, D_MODEL), lambda i: (i, 0))
    return pl.pallas_call(
        _mlp_kernel, grid=(T // TM,),
        in_specs=[row, row, _resident(w1.shape), _resident(w2.shape)],
        out_specs=row,
        out_shape=jax.ShapeDtypeStruct((T, D_MODEL), F32),
        compiler_params=_cparams(1), name="mlp",
    )(h, u2, w1, w2)


def _rope_tables(S):
    pos = jnp.arange(S, dtype=F32)
    lane = jnp.arange(LANES)
    l64 = lane % HEAD64

    inv = 500000.0 ** (-jnp.arange(0, 16, 2, dtype=F32) / 16)
    ang = pos[:, None] * inv[None, :]
    cos, sin = jnp.cos(ang)[:, l64 % 8], jnp.sin(ang)[:, l64 % 8]
    cd = jnp.where(l64 < 16, cos, 1.0)
    ad = jnp.where(l64 < 8, -sin, 0.0)
    bd = jnp.where((l64 >= 8) & (l64 < 16), sin, 0.0)

    inv = 10000.0 ** (-jnp.arange(0, 32, 2, dtype=F32) / 32)
    row_pos = jnp.floor(pos / GRID_W)
    col_pos = pos - row_pos * GRID_W
    p2 = jnp.where((l64 < 32)[None, :], row_pos[:, None], col_pos[:, None])
    ang = p2 * inv[l64 % 16][None, :]
    cg = jnp.cos(ang)
    ag = jnp.where(l64 % 32 < 16, -jnp.sin(ang), 0.0)
    bg = jnp.where(l64 % 32 >= 16, jnp.sin(ang), 0.0)

    inv = 10000.0 ** (-jnp.arange(0, 128, 2, dtype=F32) / 128)
    ang = pos[:, None] * inv[l64][None, :]
    cr = jnp.cos(ang)
    sr = jnp.where(lane < LANES // 2, -jnp.sin(ang), jnp.sin(ang))
    return (cd, ad, bd, cg, ag, bg, cr, sr)


def _fast_flag(q_gain, k_gain):
    bound = (HEAD64 ** 0.5 * LOG2E * 1.02) * jnp.max(jnp.abs(q_gain)) * jnp.max(jnp.abs(k_gain))
    return (bound <= FAST_SOFTMAX_BOUND).astype(jnp.int32).reshape(1)


def kernel(x, attn_norm, w_in, diff_q_norm, diff_k_norm, diff_lam_q1, diff_lam_k1, diff_lam_q2, diff_lam_k2, diff_subln, ret_decay_fwd, ret_decay_bwd, ret_group_norm, gqa_q_norm, gqa_k_norm, w_branch, w_out, mlp_norm, w_mlp_in, w_mlp_out):
    B, S, D = x.shape
    T = B * S
    tabs = _rope_tables(S)
    blk = jnp.arange(2 * LANES) // HEAD64
    gmat = jnp.where(blk[:, None] == blk[None, :], 1.0 / HEAD64, 0.0).astype(BF16)
    tile2 = lambda v: jnp.concatenate([v, v]).astype(F32)

    h = x.reshape(T, D)
    for l in range(DEPTH):
        lambda_init = 0.8 - 0.6 * math.exp(-0.3 * l)
        w_attn = w_in[l, :, :ATTN_COLS].astype(BF16)
        w_gate = w_in[l, :, ATTN_COLS:].astype(BF16)
        nrm = jnp.stack([tile2(diff_q_norm[l]), tile2(diff_k_norm[l]),
                         tile2(gqa_q_norm[l]), tile2(gqa_k_norm[l])])
        (u, dqT, dk, dvT, rq, rkT, rv, rg, gqT, gk, gvT, gvT2) = _proj_call(
            h, attn_norm[l].reshape(1, D).astype(F32), w_attn, gmat, nrm, tabs)

        lamv = jnp.stack([diff_lam_q1[l], diff_lam_k1[l], diff_lam_q2[l], diff_lam_k2[l]]).astype(F32)
        a = _diff_call(_fast_flag(diff_q_norm[l], diff_k_norm[l]), lamv,
                       diff_subln[l].reshape(1, LANES).astype(F32), dqT, dk, dvT, B, S, lambda_init)
        dec = jnp.stack([ret_decay_fwd[l], ret_decay_bwd[l]]).astype(F32)
        r = _ret_call(dec, ret_group_norm[l].reshape(1, BRANCH_WIDTH).astype(F32), rq, rkT, rv, rg, B, S)
        gqa_fast = _fast_flag(gqa_q_norm[l], gqa_k_norm[l])
        c = lax.cond(gqa_fast[0] == 1,
                     lambda ops: _gqa_mxu_call(ops[0], ops[1], ops[3], B, S),
                     lambda ops: _gqa_call(gqa_fast, ops[0], ops[1], ops[2], B, S),
                     (gqT, gk, gvT, gvT2))

        h, u2 = _merge_call(h, u, a, r, c, w_gate, w_branch[l].astype(BF16),
                            w_out[l].astype(BF16), mlp_norm[l].reshape(1, D).astype(F32))
        h = _mlp_call(h, u2, w_mlp_in[l].astype(BF16), w_mlp_out[l].astype(BF16))
    return h.reshape(B, S, D)
```

```python
import functools
import math

import jax
import jax.numpy as jnp
from jax import lax
from jax.experimental import pallas as pl
from jax.experimental.pallas import tpu as pltpu

F32 = jnp.float32
BF16 = jnp.bfloat16

D_MODEL = 1024
DEPTH = 2
GRID_W = 64
NORM_EPS = 1e-6
HEAD64 = 64
DIFF_HEADS = 4
RET_HEADS = 4
RET_CHUNK = 128
GQA_Q_HEADS = 8
GQA_GROUP = 4
N_BRANCHES = 3
BRANCH_WIDTH = 512
D_FF = 4 * D_MODEL
LANES = 128
MXU_DIM = 256
ATTN_COLS = 8 * BRANCH_WIDTH + 2 * LANES

TM = 512
TK = 2048
NQ = 1024
Q_BLOCKS = 2
NCOLS = Q_BLOCKS * NQ
NC = 256
KV_UNROLL = 4
RET_SCAN_TILES = 4
RET_OUT_TILES = 4
ONES_ROWS = 16
VROWS = LANES + ONES_ROWS
FAST_SOFTMAX_BOUND = 64.0
VMEM_LIMIT = 56 * 1024 * 1024
LOG2E = 1.4426950408889634


def _cparams(n_axes):
    return pltpu.CompilerParams(dimension_semantics=("arbitrary",) * n_axes,
                                vmem_limit_bytes=VMEM_LIMIT)


def _resident(shape):
    nd = len(shape)
    return pl.BlockSpec(shape, lambda *_: (0,) * nd, pipeline_mode=pl.Buffered(1))


def _proj_kernel(h_ref, an_ref, w_ref, g_ref, nrm_ref,
                 cd_ref, ad_ref, bd_ref, cg_ref, ag_ref, bg_ref, cr_ref, sr_ref,
                 u_ref, dqT_ref, dk_ref, dvT_ref, rq_ref, rkT_ref, rv_ref, rg_ref,
                 gqT_ref, gk_ref, gvT_ref, gvT2_ref):
    x = h_ref[...]
    ms = jnp.mean(x * x, axis=-1, keepdims=True)
    u = (x * lax.rsqrt(ms + NORM_EPS) * an_ref[...]).astype(BF16)
    u_ref[...] = u

    def proj(c0, n):
        return jnp.dot(u, w_ref[:, c0:c0 + n], preferred_element_type=F32)

    gmat = g_ref[...]
    lo = lax.broadcasted_iota(jnp.int32, (1, LANES), 1) < HEAD64
    first_row = lax.broadcasted_iota(jnp.int32, (ONES_ROWS, TM), 0) == 0
    ones_rows = jnp.where(first_row, 1.0, 0.0).astype(BF16)

    def qk_norm(p, gain):
        out = []
        for c0 in range(0, p.shape[1], 2 * LANES):
            y = p[:, c0:c0 + 2 * LANES]
            w = y.shape[1]
            msq = jnp.dot((y * y).astype(BF16), gmat[0:w, 0:w], preferred_element_type=F32)
            y = y * lax.rsqrt(msq + NORM_EPS)
            out += [y[:, i:i + LANES] * gain for i in range(0, w, LANES)]
        return out

    def rope(y, c, a, b, d):
        return y * c + pltpu.roll(y, LANES - d, 1) * a + pltpu.roll(y, d, 1) * b

    def chunks(p):
        return [p[:, i:i + LANES] for i in range(0, p.shape[1], LANES)]

    qscale = HEAD64 ** -0.5 * LOG2E
    kscale = LANES ** -0.5
    cd, ad, bd = cd_ref[...], ad_ref[...], bd_ref[...]
    cg, ag, bg = cg_ref[...], ag_ref[...], bg_ref[...]
    cr, sr = cr_ref[...], sr_ref[...]

    def diff_q(p):
        for hd, y in enumerate(qk_norm(p, nrm_ref[0:1, :])):
            y = rope(y, cd, ad, bd, 8) * qscale
            dqT_ref[hd, :, 0:TM] = jnp.where(lo, y, 0.0).T.astype(BF16)
            dqT_ref[hd, :, TM:2 * TM] = jnp.where(lo, 0.0, y).T.astype(BF16)

    def diff_k(p):
        for hd, y in enumerate(qk_norm(p, nrm_ref[1:2, :])):
            dk_ref[:, hd * LANES:(hd + 1) * LANES] = rope(y, cd, ad, bd, 8).astype(BF16)

    def diff_v(p):
        for hd, y in enumerate(chunks(p)):
            dvT_ref[hd, 0:LANES, :] = y.T.astype(BF16)
            dvT_ref[hd, LANES:VROWS, :] = ones_rows

    def ret_q(p):
        for hd, y in enumerate(chunks(p)):
            rq_ref[:, hd * LANES:(hd + 1) * LANES] = (y * cr + pltpu.roll(y, LANES // 2, 1) * sr).astype(BF16)

    def ret_k(p):
        for hd, y in enumerate(chunks(p)):
            rkT_ref[hd] = ((y * cr + pltpu.roll(y, LANES // 2, 1) * sr) * kscale).T.astype(BF16)

    def ret_v(p):
        rv_ref[...] = p.astype(BF16)

    def ret_g(p):
        rg_ref[...] = p.astype(BF16)

    def gqa_q(p):
        for c, y in enumerate(qk_norm(p, nrm_ref[2:3, :])):
            y = rope(y, cg, ag, bg, 16) * qscale
            ysw = pltpu.roll(y, LANES // 2, 1)
            if c < 2:
                z_even, z_odd = jnp.where(lo, y, 0.0), jnp.where(lo, ysw, 0.0)
            else:
                z_even, z_odd = jnp.where(lo, 0.0, ysw), jnp.where(lo, 0.0, y)
            for hh, z in ((2 * c, z_even), (2 * c + 1, z_odd)):
                zt = z.T.astype(BF16)
                for qb in range(TM // LANES):
                    gqT_ref[qb, 0:LANES, hh * LANES:(hh + 1) * LANES] = zt[:, qb * LANES:(qb + 1) * LANES]
        gqT_ref[:, LANES:MXU_DIM, :] = jnp.zeros((TM // LANES, MXU_DIM - LANES, NQ), BF16)

    def gqa_kv(p):
        (y,) = qk_norm(p[:, 0:LANES], nrm_ref[3:4, :])
        gk_ref[:, 0:LANES] = rope(y, cg, ag, bg, 16).astype(BF16)
        gk_ref[:, LANES:MXU_DIM] = jnp.zeros((TM, MXU_DIM - LANES), BF16)
        vt = p[:, LANES:2 * LANES].T.astype(BF16)
        gvT_ref[0:LANES, :] = vt
        gvT_ref[LANES:VROWS, :] = ones_rows
        for ku in range(TM // MXU_DIM):
            for g in range(2):
                gvT2_ref[ku, g] = vt[g * HEAD64:(g + 1) * HEAD64, ku * MXU_DIM:(ku + 1) * MXU_DIM]

    order = [gqa_q, diff_q, ret_k, diff_v, diff_k, gqa_kv, ret_q, ret_v, ret_g]
    in_w_in = [diff_q, diff_k, diff_v, ret_q, ret_k, ret_v, ret_g, gqa_q, gqa_kv]
    width = {f: BRANCH_WIDTH for f in in_w_in}
    width[gqa_kv] = 2 * LANES
    start = {f: sum(width[g] for g in in_w_in[:i]) for i, f in enumerate(in_w_in)}
    p_next = proj(start[order[0]], width[order[0]])
    for i, epilogue in enumerate(order):
        p_cur = p_next
        if i + 1 < len(order):
            p_next = proj(start[order[i + 1]], width[order[i + 1]])
        epilogue(p_cur)


def _proj_call(h, attn_norm, w_attn, gmat, nrm, tabs):
    T = h.shape[0]
    S = tabs[0].shape[0]
    nt = T // TM
    tab_spec = pl.BlockSpec((TM, LANES), lambda i: (i % (S // TM), 0))
    row = lambda w: pl.BlockSpec((TM, w), lambda i: (i, 0))
    out_shape = (
        jax.ShapeDtypeStruct((T, D_MODEL), BF16),
        jax.ShapeDtypeStruct((nt, DIFF_HEADS, LANES, 2 * TM), BF16),
        jax.ShapeDtypeStruct((T, BRANCH_WIDTH), BF16),
        jax.ShapeDtypeStruct((nt, DIFF_HEADS, VROWS, TM), BF16),
        jax.ShapeDtypeStruct((T, BRANCH_WIDTH), BF16),
        jax.ShapeDtypeStruct((nt, RET_HEADS, LANES, TM), BF16),
        jax.ShapeDtypeStruct((T, BRANCH_WIDTH), BF16),
        jax.ShapeDtypeStruct((T, BRANCH_WIDTH), BF16),
        jax.ShapeDtypeStruct((T // LANES, MXU_DIM, NQ), BF16),
        jax.ShapeDtypeStruct((T, MXU_DIM), BF16),
        jax.ShapeDtypeStruct((nt, VROWS, TM), BF16),
        jax.ShapeDtypeStruct((T // MXU_DIM, 2, HEAD64, MXU_DIM), BF16),
    )
    out_specs = (
        row(D_MODEL),
        pl.BlockSpec((None, DIFF_HEADS, LANES, 2 * TM), lambda i: (i, 0, 0, 0)),
        row(BRANCH_WIDTH),
        pl.BlockSpec((None, DIFF_HEADS, VROWS, TM), lambda i: (i, 0, 0, 0)),
        row(BRANCH_WIDTH),
        pl.BlockSpec((None, RET_HEADS, LANES, TM), lambda i: (i, 0, 0, 0)),
        row(BRANCH_WIDTH), row(BRANCH_WIDTH),
        pl.BlockSpec((TM // LANES, MXU_DIM, NQ), lambda i: (i, 0, 0)),
        row(MXU_DIM),
        pl.BlockSpec((None, VROWS, TM), lambda i: (i, 0, 0)),
        pl.BlockSpec((TM // MXU_DIM, 2, HEAD64, MXU_DIM), lambda i: (i, 0, 0, 0)),
    )
    in_specs = [row(D_MODEL), _resident((1, D_MODEL)), _resident(w_attn.shape),
                _resident(gmat.shape), _resident(nrm.shape)] + [tab_spec] * 8
    return pl.pallas_call(
        _proj_kernel, grid=(nt,), in_specs=in_specs, out_specs=out_specs,
        out_shape=out_shape, compiler_params=_cparams(1), name="proj",
    )(h, attn_norm, w_attn, gmat, nrm, *tabs)


def _attn_core(fast_ref, qT_ref, k_ref, vT_ref, m_ref, acc_ref):
    n_steps = k_ref.shape[0] // TK
    n_trips = n_steps // KV_UNROLL
    qT = jnp.concatenate([qT_ref[i] for i in range(Q_BLOCKS)], axis=-1)

    def kv(j):
        k = k_ref[pl.ds(pl.multiple_of(j * TK, TK), TK), :]
        per = TK // TM
        vT = jnp.concatenate([vT_ref[j * per + i] for i in range(per)], axis=-1)
        return k, vT

    @pl.when(fast_ref[0] == 1)
    def _():
        def body(jj, carry):
            pv = lsum = None
            for u in range(KV_UNROLL):
                k, vT = kv(jj * KV_UNROLL + u)
                s = jnp.dot(k, qT, preferred_element_type=F32)
                pf = jnp.exp2(s)
                ls = jnp.sum(pf, axis=0, keepdims=True)
                t = jnp.dot(vT[0:LANES], pf.astype(BF16), preferred_element_type=F32)
                pv, lsum = (t, ls) if pv is None else (pv + t, lsum + ls)
            if n_trips == 1:
                acc_ref[0:LANES, :] = pv
                acc_ref[LANES:LANES + 1, :] = lsum
            else:
                acc_ref[0:LANES, :] += pv
                acc_ref[LANES:LANES + 1, :] += lsum
            return carry

        if n_trips > 1:
            acc_ref[...] = jnp.zeros(acc_ref.shape, F32)
        lax.fori_loop(0, n_trips, body, 0)

    @pl.when(fast_ref[0] == 0)
    def _():
        acc_ref[...] = jnp.zeros(acc_ref.shape, F32)
        m_ref[...] = jnp.full(m_ref.shape, -1e30, F32)

        def body(j, carry):
            k, vT = kv(j)
            for c in range(NCOLS // NC):
                cs = slice(c * NC, (c + 1) * NC)
                s = jnp.dot(k, qT[:, cs], preferred_element_type=F32)
                m_old = m_ref[:, cs]
                m_new = jnp.maximum(m_old, jnp.max(s, axis=0, keepdims=True))
                alpha = jnp.exp2(m_old - m_new)
                p = jnp.exp2(s - m_new).astype(BF16)
                pv = jnp.dot(vT, p, preferred_element_type=F32)
                acc_ref[:, cs] = acc_ref[:, cs] * alpha + pv
                m_ref[:, cs] = m_new
            return carry

        lax.fori_loop(0, n_steps, body, 0)


def _diff_kernel(fast_ref, lam_ref, sub_ref, qT_ref, k_ref, vT_ref, o_ref, m_ref, acc_ref, *,
                 lambda_init):
    lv = lam_ref[...]
    lam = (jnp.exp(jnp.sum(lv[0:1] * lv[1:2], axis=-1, keepdims=True))
           - jnp.exp(jnp.sum(lv[2:3] * lv[3:4], axis=-1, keepdims=True)) + lambda_init)
    _attn_core(fast_ref, qT_ref, k_ref, vT_ref, m_ref, acc_ref)

    tq = NQ // 2
    for qb in range(Q_BLOCKS):
        c1 = slice(qb * NQ, qb * NQ + tq)
        c2 = slice(qb * NQ + tq, (qb + 1) * NQ)
        o1 = acc_ref[0:LANES, c1] / acc_ref[LANES:LANES + 1, c1]
        o2 = acc_ref[0:LANES, c2] / acc_ref[LANES:LANES + 1, c2]
        d = o1 - lam * o2
        ms = jnp.mean(d * d, axis=0, keepdims=True)
        dn = d * lax.rsqrt(ms + 1e-5)
        o_ref[qb * tq:(qb + 1) * tq, :] = (
            dn.T * (sub_ref[...] * (1.0 - lambda_init))).astype(o_ref.dtype)


_SMEM_SPEC = pl.BlockSpec(memory_space=pltpu.SMEM)
_ATTN_SCRATCH = [pltpu.VMEM((1, NCOLS), F32), pltpu.VMEM((VROWS, NCOLS), F32)]


def _diff_call(fast, lamv, subln, dqT, dk, dvT, B, S, lambda_init):
    T = dk.shape[0]
    tq = Q_BLOCKS * NQ // 2
    nq = S // tq
    kern = functools.partial(_diff_kernel, lambda_init=lambda_init)
    return pl.pallas_call(
        kern, grid=(B, DIFF_HEADS, nq),
        in_specs=[
            _SMEM_SPEC, _resident(lamv.shape), _resident(subln.shape),
            pl.BlockSpec((Q_BLOCKS, None, LANES, NQ), lambda b, h, i: (b * nq + i, h, 0, 0)),
            pl.BlockSpec((S, LANES), lambda b, h, i: (b, h)),
            pl.BlockSpec((S // TM, None, VROWS, TM), lambda b, h, i: (b, h, 0, 0)),
        ],
        out_specs=pl.BlockSpec((tq, LANES), lambda b, h, i: (b * nq + i, h)),
        out_shape=jax.ShapeDtypeStruct((T, BRANCH_WIDTH), BF16),
        scratch_shapes=_ATTN_SCRATCH,
        compiler_params=_cparams(3), name="diff_attn",
    )(fast, lamv, subln, dqT, dk, dvT)


def _gqa_kernel(fast_ref, qT_ref, k_ref, vT_ref, o_ref, m_ref, acc_ref):
    _attn_core(fast_ref, qT_ref, k_ref, vT_ref, m_ref, acc_ref)
    for qb, c in [(qb, c) for qb in range(Q_BLOCKS) for c in range(GQA_Q_HEADS // 2)]:
        g = (2 * c) // GQA_GROUP
        rows = slice(g * HEAD64, (g + 1) * HEAD64)
        parts = []
        for hh in (2 * c, 2 * c + 1):
            cols = slice(qb * NQ + hh * LANES, qb * NQ + (hh + 1) * LANES)
            parts.append(acc_ref[rows, cols] / acc_ref[LANES:LANES + 1, cols])
        o_ref[qb * LANES:(qb + 1) * LANES, c * LANES:(c + 1) * LANES] = (
            jnp.concatenate(parts, axis=0).T.astype(o_ref.dtype))


def _gqa_call(fast, gqT, gk, gvT, B, S):
    T = gk.shape[0]
    tq = Q_BLOCKS * NQ // GQA_Q_HEADS
    nq = S // tq
    return pl.pallas_call(
        _gqa_kernel, grid=(B, nq),
        in_specs=[
            _SMEM_SPEC,
            pl.BlockSpec((Q_BLOCKS, LANES, NQ), lambda b, i: (b * nq + i, 0, 0)),
            pl.BlockSpec((S, LANES), lambda b, i: (b, 0)),
            pl.BlockSpec((S // TM, VROWS, TM), lambda b, i: (b, 0, 0)),
        ],
        out_specs=pl.BlockSpec((tq, BRANCH_WIDTH), lambda b, i: (b * nq + i, 0)),
        out_shape=jax.ShapeDtypeStruct((T, BRANCH_WIDTH), BF16),
        scratch_shapes=_ATTN_SCRATCH,
        compiler_params=_cparams(2), name="gqa_attn",
    )(fast, gqT, gk, gvT)


S_REGIONS = (0, 64, 128)
O_REGIONS = ((192, 208), (224, 240))
MXU_Q_BLOCKS = len(O_REGIONS)
POP_LAG = 2
PV_LAG = 3
P_RING = 6
TRIP_SLOTS = 30


def _gqa_mxu_kernel(qT_ref, k_ref, vT_ref, o_ref, p_ref, l_ref):
    per_block = 2 * (k_ref.shape[0] // MXU_DIM)
    n_slots = MXU_Q_BLOCKS * per_block
    head = per_block % TRIP_SLOTS
    assert head > PV_LAG and TRIP_SLOTS % P_RING == 0

    @pl.when((pl.program_id(0) == 0) & (pl.program_id(1) == 0))
    def _():
        for mxu in range(2):
            for region in S_REGIONS:
                pltpu.matmul_pop(region, (MXU_DIM, MXU_DIM), F32, mxu)
            for region in sum(O_REGIONS, ()):
                pltpu.matmul_pop(region, (HEAD64, MXU_DIM), F32, mxu)

    l_ref[...] = jnp.zeros(l_ref.shape, F32)

    def q_tile(mxu, n):
        c = 2 * mxu + n % 2
        return qT_ref[n // per_block, :, c * MXU_DIM:(c + 1) * MXU_DIM]

    def o_region(n):
        return O_REGIONS[n // per_block][n % 2]

    def l_cols(mxu, n):
        c0 = (n // per_block) * NQ + (2 * mxu + n % 2) * MXU_DIM
        return slice(c0, c0 + MXU_DIM)

    def slot(n, unit, unit_v, stage_next=True):
        start = unit * MXU_DIM
        if not isinstance(unit, int):
            start = pl.multiple_of(start, MXU_DIM)
        k = k_ref[pl.ds(start, MXU_DIM), :]
        for mxu in range(2):
            if n >= PV_LAG:
                pltpu.matmul_push_rhs(p_ref[mxu, (n - PV_LAG) % P_RING], staging_register=1,
                                      mxu_index=mxu)
            pltpu.matmul_acc_lhs(S_REGIONS[n % 3], k, mxu_index=mxu, load_staged_rhs=0)
            if stage_next:
                pltpu.matmul_push_rhs(q_tile(mxu, n + 1), staging_register=0, mxu_index=mxu)
            if n >= PV_LAG:
                pltpu.matmul_acc_lhs(o_region(n - PV_LAG), vT_ref[unit_v, mxu],
                                     mxu_index=mxu, load_staged_rhs=1)
        for mxu in range(2):
            if n >= POP_LAG:
                pop_scores(mxu, n - POP_LAG)

    def pop_scores(mxu, n):
        s = pltpu.matmul_pop(S_REGIONS[n % 3], (MXU_DIM, MXU_DIM), F32, mxu)
        pf = jnp.exp2(s)
        l_ref[:, l_cols(mxu, n)] += jnp.sum(pf, axis=0, keepdims=True)
        p_ref[mxu, n % P_RING] = pf.astype(BF16)

    def values_only(mxu, n, unit_v):
        pltpu.matmul_push_rhs(p_ref[mxu, n % P_RING], staging_register=1, mxu_index=mxu)
        pltpu.matmul_acc_lhs(o_region(n), vT_ref[unit_v, mxu], mxu_index=mxu, load_staged_rhs=1)

    def unit_of(n):
        return (n % per_block) // 2

    for mxu in range(2):
        pltpu.matmul_push_rhs(q_tile(mxu, 0), staging_register=0, mxu_index=mxu)
    for blk in range(MXU_Q_BLOCKS):
        first = blk * per_block
        for n in range(first, first + head):
            slot(n, unit_of(n), unit_of(max(n - PV_LAG, 0)))

        def trip(t, carry, first=first + head):
            for i in range(TRIP_SLOTS):
                base = first + TRIP_SLOTS * t + i
                slot(first + i, unit_of(base), unit_of(base - PV_LAG))
            return carry

        lax.fori_loop(0, per_block // TRIP_SLOTS, trip, 0)

    for mxu in range(2):
        pltpu.matmul_acc_lhs(S_REGIONS[n_slots % 3], jnp.zeros((16, MXU_DIM), BF16),
                             mxu_index=mxu, load_staged_rhs=0)
    for n in range(n_slots - POP_LAG, n_slots):
        for mxu in range(2):
            pop_scores(mxu, n)
    for mxu in range(2):
        pltpu.matmul_pop(S_REGIONS[n_slots % 3], (16, MXU_DIM), F32, mxu)
    for n in range(n_slots - PV_LAG, n_slots):
        for mxu in range(2):
            values_only(mxu, n, unit_of(n))

    tq = NQ // GQA_Q_HEADS
    for blk in range(MXU_Q_BLOCKS):
        for mxu in range(2):
            for cc in range(2):
                n = blk * per_block + cc
                c = 2 * mxu + cc
                o = pltpu.matmul_pop(o_region(n), (HEAD64, MXU_DIM), F32, mxu)
                o = o / l_ref[:, l_cols(mxu, n)]
                pair = jnp.concatenate([o[:, 0:LANES], o[:, LANES:2 * LANES]], axis=0)
                o_ref[blk * tq:(blk + 1) * tq, c * LANES:(c + 1) * LANES] = pair.T.astype(o_ref.dtype)


def _gqa_mxu_call(gqT, gk, gvT2, B, S):
    T = gk.shape[0]
    tq = MXU_Q_BLOCKS * NQ // GQA_Q_HEADS
    nq = S // tq
    return pl.pallas_call(
        _gqa_mxu_kernel, grid=(B, nq),
        in_specs=[
            pl.BlockSpec((MXU_Q_BLOCKS, MXU_DIM, NQ), lambda b, i: (b * nq + i, 0, 0)),
            pl.BlockSpec((S, MXU_DIM), lambda b, i: (b, 0)),
            pl.BlockSpec((S // MXU_DIM, 2, HEAD64, MXU_DIM), lambda b, i: (b, 0, 0, 0)),
        ],
        out_specs=pl.BlockSpec((tq, BRANCH_WIDTH), lambda b, i: (b * nq + i, 0)),
        out_shape=jax.ShapeDtypeStruct((T, BRANCH_WIDTH), BF16),
        scratch_shapes=[pltpu.VMEM((2, P_RING, MXU_DIM, MXU_DIM), BF16),
                        pltpu.VMEM((1, MXU_Q_BLOCKS * NQ), F32)],
        compiler_params=_cparams(2), name="gqa_attn_mxu",
    )(gqT, gk, gvT2)


def _ret_kernel(dec_ref, gain_ref, q_ref, kT_ref, v_ref, g_ref, o_ref, sf_ref, sb_ref):
    C = RET_CHUNK
    hd = pl.program_id(1)
    lf = -jnp.exp(jnp.full((C, C), dec_ref[0, hd], F32))
    lb = -jnp.exp(jnp.full((C, C), dec_ref[1, hd], F32))
    ii = lax.broadcasted_iota(jnp.int32, (C, C), 0).astype(F32)
    jj = lax.broadcasted_iota(jnp.int32, (C, C), 1).astype(F32)
    diff = ii - jj
    decay = jnp.where(diff >= 0, jnp.exp(jnp.maximum(diff, 0.0) * lf),
                      jnp.exp(jnp.maximum(-diff, 0.0) * lb))
    xi_f, zeta_f = jnp.exp((ii + 1.0) * lf), jnp.exp((C - 1.0 - ii) * lf)
    xi_b, zeta_b = jnp.exp((C - ii) * lb), jnp.exp(ii * lb)
    dec_f, dec_b = jnp.exp(C * lf), jnp.exp(C * lb)
    n_tiles = kT_ref.shape[0]
    per_tile = TM // C

    def rows(t, c):
        return pl.ds(pl.multiple_of(t * TM + c * C, C), C)

    def chunk_kv(t, c, zeta):
        vz = (v_ref[rows(t, c), :].astype(F32) * zeta).astype(BF16)
        return jnp.dot(kT_ref[t, :, c * C:(c + 1) * C], vz, preferred_element_type=F32)

    def scans(i, states):
        st_f, st_b = states
        fwd = [(i * RET_SCAN_TILES + t, c) for t in range(RET_SCAN_TILES) for c in range(per_tile)]
        bwd = [(n_tiles - 1 - t, per_tile - 1 - c) for t, c in fwd]
        kv_f = [chunk_kv(t, c, zeta_f) for t, c in fwd]
        kv_b = [chunk_kv(t, c, zeta_b) for t, c in bwd]
        for (tf, cf), kf, (tb, cb), kb in zip(fwd, kv_f, bwd, kv_b):
            sf_ref[tf * per_tile + cf] = st_f.astype(BF16)
            st_f = st_f * dec_f + kf
            sb_ref[tb * per_tile + cb] = st_b.astype(BF16)
            st_b = st_b * dec_b + kb
        return st_f, st_b

    zero = jnp.zeros((C, C), F32)
    lax.fori_loop(0, n_tiles // RET_SCAN_TILES, scans, (zero, zero))

    def outputs(tt, carry):
        chunks = [(tt * RET_OUT_TILES + i, c) for i in range(RET_OUT_TILES) for c in range(per_tile)]
        rs = [rows(t, c) for t, c in chunks]
        qs = [q_ref[r, :] for r in rs]
        a = [jnp.dot(q, kT_ref[t, :, c * C:(c + 1) * C], preferred_element_type=F32)
             for q, (t, c) in zip(qs, chunks)]
        cross = [jnp.dot(q, sf_ref[t * per_tile + c], preferred_element_type=F32) * xi_f
                 + jnp.dot(q, sb_ref[t * per_tile + c], preferred_element_type=F32) * xi_b
                 for q, (t, c) in zip(qs, chunks)]
        a = [(x * decay).astype(BF16) for x in a]
        o = [jnp.dot(x, v_ref[r, :], preferred_element_type=F32) + y
             for x, r, y in zip(a, rs, cross)]
        mu = [jnp.mean(x, axis=-1, keepdims=True) for x in o]
        oc = [x - m for x, m in zip(o, mu)]
        var = [jnp.mean(x * x, axis=-1, keepdims=True) for x in oc]
        for x, s2, r in zip(oc, var, rs):
            y = x * lax.rsqrt(s2 + 1e-5) * gain_ref[...]
            g = g_ref[r, :].astype(F32)
            o_ref[r, :] = (g / (1.0 + jnp.exp(-g)) * y).astype(o_ref.dtype)
        return carry

    lax.fori_loop(0, n_tiles // RET_OUT_TILES, outputs, 0)


def _ret_call(dec, gain, rq, rkT, rv, rg, B, S):
    T = rq.shape[0]
    blk = pl.BlockSpec((S, LANES), lambda b, h: (b, h))
    n_chunks = S // RET_CHUNK
    return pl.pallas_call(
        _ret_kernel, grid=(B, RET_HEADS),
        in_specs=[pl.BlockSpec(memory_space=pltpu.SMEM),
                  pl.BlockSpec((1, LANES), lambda b, h: (0, h)), blk,
                  pl.BlockSpec((S // TM, None, LANES, TM), lambda b, h: (b, h, 0, 0)), blk, blk],
        out_specs=blk,
        out_shape=jax.ShapeDtypeStruct((T, BRANCH_WIDTH), BF16),
        scratch_shapes=[pltpu.VMEM((n_chunks, RET_CHUNK, LANES), BF16),
                        pltpu.VMEM((n_chunks, RET_CHUNK, LANES), BF16)],
        compiler_params=_cparams(2), name="retention",
    )(dec, gain, rq, rkT, rv, rg)


def _merge_kernel(h_ref, u_ref, a_ref, r_ref, c_ref, wg_ref, wb_ref, wo_ref, mn_ref,
                  hn_ref, u2_ref):
    u = u_ref[...]
    merged = None
    for n, br in enumerate((a_ref, r_ref, c_ref)):
        logits = jnp.dot(u, wg_ref[:, n * D_MODEL:(n + 1) * D_MODEL], preferred_element_type=F32)
        gate = 1.0 / (1.0 + jnp.exp(-logits))
        term = gate * jnp.dot(br[...], wb_ref[n], preferred_element_type=F32)
        merged = term if merged is None else merged + term
    merged = merged.astype(BF16)
    halves = [slice(0, TM // 2), slice(TM // 2, TM)]
    hns = [h_ref[r, :] + jnp.dot(merged[r], wo_ref[...], preferred_element_type=F32) for r in halves]
    for r, hn in zip(halves, hns):
        hn_ref[r, :] = hn
        ms = jnp.mean(hn * hn, axis=-1, keepdims=True)
        u2_ref[r, :] = (hn * lax.rsqrt(ms + NORM_EPS) * mn_ref[...]).astype(BF16)


def _merge_call(h, u, a, r, c, wg, wb, wo, mlp_norm):
    T = h.shape[0]
    row = lambda w: pl.BlockSpec((TM, w), lambda i: (i, 0))
    return pl.pallas_call(
        _merge_kernel, grid=(T // TM,),
        in_specs=[row(D_MODEL), row(D_MODEL), row(BRANCH_WIDTH), row(BRANCH_WIDTH), row(BRANCH_WIDTH),
                  _resident(wg.shape), _resident(wb.shape), _resident(wo.shape),
                  _resident((1, D_MODEL))],
        out_specs=(row(D_MODEL), row(D_MODEL)),
        out_shape=(jax.ShapeDtypeStruct((T, D_MODEL), F32),
                   jax.ShapeDtypeStruct((T, D_MODEL), BF16)),
        compiler_params=_cparams(1), name="merge",
    )(h, u, a, r, c, wg, wb, wo, mlp_norm)


def _mlp_kernel(h_ref, u_ref, w1_ref, w2_ref, o_ref):
    u = u_ref[...]
    acc = h_ref[...]
    for c in range(D_FF // D_MODEL):
        cs = slice(c * D_MODEL, (c + 1) * D_MODEL)
        m = jnp.maximum(jnp.dot(u, w1_ref[:, cs], preferred_element_type=F32), 0.0)
        acc = acc + jnp.dot((m * m).astype(BF16), w2_ref[cs, :], preferred_element_type=F32)
    o_ref[...] = acc


def _mlp_call(h, u2, w1, w2):
    T = h.shape[0]
    row = pl.BlockSpec((TM, D_MODEL), lambda i: (i, 0))
    return pl.pallas_call(
        _mlp_kernel, grid=(T // TM,),
        in_specs=[row, row, _resident(w1.shape), _resident(w2.shape)],
        out_specs=row,
        out_shape=jax.ShapeDtypeStruct((T, D_MODEL), F32),
        compiler_params=_cparams(1), name="mlp",
    )(h, u2, w1, w2)


def _rope_tables(S):
    pos = jnp.arange(S, dtype=F32)
    lane = jnp.arange(LANES)
    l64 = lane % HEAD64

    inv = 500000.0 ** (-jnp.arange(0, 16, 2, dtype=F32) / 16)
    ang = pos[:, None] * inv[None, :]
    cos, sin = jnp.cos(ang)[:, l64 % 8], jnp.sin(ang)[:, l64 % 8]
    cd = jnp.where(l64 < 16, cos, 1.0)
    ad = jnp.where(l64 < 8, -sin, 0.0)
    bd = jnp.where((l64 >= 8) & (l64 < 16), sin, 0.0)

    inv = 10000.0 ** (-jnp.arange(0, 32, 2, dtype=F32) / 32)
    row_pos = jnp.floor(pos / GRID_W)
    col_pos = pos - row_pos * GRID_W
    p2 = jnp.where((l64 < 32)[None, :], row_pos[:, None], col_pos[:, None])
    ang = p2 * inv[l64 % 16][None, :]
    cg = jnp.cos(ang)
    ag = jnp.where(l64 % 32 < 16, -jnp.sin(ang), 0.0)
    bg = jnp.where(l64 % 32 >= 16, jnp.sin(ang), 0.0)

    inv = 10000.0 ** (-jnp.arange(0, 128, 2, dtype=F32) / 128)
    ang = pos[:, None] * inv[l64][None, :]
    cr = jnp.cos(ang)
    sr = jnp.where(lane < LANES // 2, -jnp.sin(ang), jnp.sin(ang))
    return (cd, ad, bd, cg, ag, bg, cr, sr)


def _fast_flag(q_gain, k_gain):
    bound = (HEAD64 ** 0.5 * LOG2E * 1.02) * jnp.max(jnp.abs(q_gain)) * jnp.max(jnp.abs(k_gain))
    return (bound <= FAST_SOFTMAX_BOUND).astype(jnp.int32).reshape(1)


def kernel(x, attn_norm, w_in, diff_q_norm, diff_k_norm, diff_lam_q1, diff_lam_k1, diff_lam_q2, diff_lam_k2, diff_subln, ret_decay_fwd, ret_decay_bwd, ret_group_norm, gqa_q_norm, gqa_k_norm, w_branch, w_out, mlp_norm, w_mlp_in, w_mlp_out):
    B, S, D = x.shape
    T = B * S
    tabs = _rope_tables(S)
    blk = jnp.arange(2 * LANES) // HEAD64
    gmat = jnp.where(blk[:, None] == blk[None, :], 1.0 / HEAD64, 0.0).astype(BF16)
    tile2 = lambda v: jnp.concatenate([v, v]).astype(F32)

    h = x.reshape(T, D)
    for l in range(DEPTH):
        lambda_init = 0.8 - 0.6 * math.exp(-0.3 * l)
        w_attn = w_in[l, :, :ATTN_COLS].astype(BF16)
        w_gate = w_in[l, :, ATTN_COLS:].astype(BF16)
        nrm = jnp.stack([tile2(diff_q_norm[l]), tile2(diff_k_norm[l]),
                         tile2(gqa_q_norm[l]), tile2(gqa_k_norm[l])])
        (u, dqT, dk, dvT, rq, rkT, rv, rg, gqT, gk, gvT, gvT2) = _proj_call(
            h, attn_norm[l].reshape(1, D).astype(F32), w_attn, gmat, nrm, tabs)

        lamv = jnp.stack([diff_lam_q1[l], diff_lam_k1[l], diff_lam_q2[l], diff_lam_k2[l]]).astype(F32)
        a = _diff_call(_fast_flag(diff_q_norm[l], diff_k_norm[l]), lamv,
                       diff_subln[l].reshape(1, LANES).astype(F32), dqT, dk, dvT, B, S, lambda_init)
        dec = jnp.stack([ret_decay_fwd[l], ret_decay_bwd[l]]).astype(F32)
        r = _ret_call(dec, ret_group_norm[l].reshape(1, BRANCH_WIDTH).astype(F32), rq, rkT, rv, rg, B, S)
        gqa_fast = _fast_flag(gqa_q_norm[l], gqa_k_norm[l])
        c = lax.cond(gqa_fast[0] == 1,
                     lambda ops: _gqa_mxu_call(ops[0], ops[1], ops[3], B, S),
                     lambda ops: _gqa_call(gqa_fast, ops[0], ops[1], ops[2], B, S),
                     (gqT, gk, gvT, gvT2))

        h, u2 = _merge_call(h, u, a, r, c, w_gate, w_branch[l].astype(BF16),
                            w_out[l].astype(BF16), mlp_norm[l].reshape(1, D).astype(F32))
        h = _mlp_call(h, u2, w_mlp_in[l].astype(BF16), w_mlp_out[l].astype(BF16))
    return h.reshape(B, S, D)
```

```python
import functools
import math

import jax
import jax.numpy as jnp
from jax import lax
from jax.experimental import pallas as pl
from jax.experimental.pallas import tpu as pltpu

F32 = jnp.float32
BF16 = jnp.bfloat16

D_MODEL = 1024
DEPTH = 2
GRID_W = 64
NORM_EPS = 1e-6
HEAD64 = 64
DIFF_HEADS = 4
RET_HEADS = 4
RET_CHUNK = 128
GQA_Q_HEADS = 8
GQA_GROUP = 4
N_BRANCHES = 3
BRANCH_WIDTH = 512
D_FF = 4 * D_MODEL
LANES = 128
MXU_DIM = 256
ATTN_COLS = 8 * BRANCH_WIDTH + 2 * LANES

TM = 512
TK = 2048
NQ = 1024
Q_BLOCKS = 2
NCOLS = Q_BLOCKS * NQ
NC = 256
KV_UNROLL = 4
RET_SCAN_TILES = 4
RET_OUT_TILES = 8
ONES_ROWS = 16
VROWS = LANES + ONES_ROWS
FAST_SOFTMAX_BOUND = 64.0
VMEM_LIMIT = 56 * 1024 * 1024
LOG2E = 1.4426950408889634


def _cparams(n_axes):
    return pltpu.CompilerParams(dimension_semantics=("arbitrary",) * n_axes,
                                vmem_limit_bytes=VMEM_LIMIT)


def _resident(shape):
    nd = len(shape)
    return pl.BlockSpec(shape, lambda *_: (0,) * nd, pipeline_mode=pl.Buffered(1))


def _proj_kernel(h_ref, an_ref, w_ref, g_ref, nrm_ref,
                 cd_ref, ad_ref, bd_ref, cg_ref, ag_ref, bg_ref, cr_ref, sr_ref,
                 u_ref, dqT_ref, dk_ref, dvT_ref, rq_ref, rkT_ref, rv_ref, rg_ref,
                 gqT_ref, gk_ref, gvT_ref, gvT2_ref):
    x = h_ref[...]
    ms = jnp.mean(x * x, axis=-1, keepdims=True)
    u = (x * lax.rsqrt(ms + NORM_EPS) * an_ref[...]).astype(BF16)
    u_ref[...] = u

    def proj(c0, n):
        return jnp.dot(u, w_ref[:, c0:c0 + n], preferred_element_type=F32)

    gmat = g_ref[...]
    lo = lax.broadcasted_iota(jnp.int32, (1, LANES), 1) < HEAD64
    first_row = lax.broadcasted_iota(jnp.int32, (ONES_ROWS, TM), 0) == 0
    ones_rows = jnp.where(first_row, 1.0, 0.0).astype(BF16)

    def qk_norm(p, gain):
        out = []
        for c0 in range(0, p.shape[1], 2 * LANES):
            y = p[:, c0:c0 + 2 * LANES]
            w = y.shape[1]
            msq = jnp.dot((y * y).astype(BF16), gmat[0:w, 0:w], preferred_element_type=F32)
            y = y * lax.rsqrt(msq + NORM_EPS)
            out += [y[:, i:i + LANES] * gain for i in range(0, w, LANES)]
        return out

    def rope(y, c, a, b, d):
        return y * c + pltpu.roll(y, LANES - d, 1) * a + pltpu.roll(y, d, 1) * b

    def chunks(p):
        return [p[:, i:i + LANES] for i in range(0, p.shape[1], LANES)]

    qscale = HEAD64 ** -0.5 * LOG2E
    kscale = LANES ** -0.5
    cd, ad, bd = cd_ref[...], ad_ref[...], bd_ref[...]
    cg, ag, bg = cg_ref[...], ag_ref[...], bg_ref[...]
    cr, sr = cr_ref[...], sr_ref[...]

    def diff_q(p):
        for hd, y in enumerate(qk_norm(p, nrm_ref[0:1, :])):
            y = rope(y, cd, ad, bd, 8) * qscale
            dqT_ref[hd, :, 0:TM] = jnp.where(lo, y, 0.0).T.astype(BF16)
            dqT_ref[hd, :, TM:2 * TM] = jnp.where(lo, 0.0, y).T.astype(BF16)

    def diff_k(p):
        for hd, y in enumerate(qk_norm(p, nrm_ref[1:2, :])):
            dk_ref[:, hd * LANES:(hd + 1) * LANES] = rope(y, cd, ad, bd, 8).astype(BF16)

    def diff_v(p):
        for hd, y in enumerate(chunks(p)):
            dvT_ref[hd, 0:LANES, :] = y.T.astype(BF16)
            dvT_ref[hd, LANES:VROWS, :] = ones_rows

    def ret_q(p):
        for hd, y in enumerate(chunks(p)):
            rq_ref[:, hd * LANES:(hd + 1) * LANES] = (y * cr + pltpu.roll(y, LANES // 2, 1) * sr).astype(BF16)

    def ret_k(p):
        for hd, y in enumerate(chunks(p)):
            rkT_ref[hd] = ((y * cr + pltpu.roll(y, LANES // 2, 1) * sr) * kscale).T.astype(BF16)

    def ret_v(p):
        rv_ref[...] = p.astype(BF16)

    def ret_g(p):
        rg_ref[...] = p.astype(BF16)

    def gqa_q(p):
        for c, y in enumerate(qk_norm(p, nrm_ref[2:3, :])):
            y = rope(y, cg, ag, bg, 16) * qscale
            ysw = pltpu.roll(y, LANES // 2, 1)
            if c < 2:
                z_even, z_odd = jnp.where(lo, y, 0.0), jnp.where(lo, ysw, 0.0)
            else:
                z_even, z_odd = jnp.where(lo, 0.0, ysw), jnp.where(lo, 0.0, y)
            for hh, z in ((2 * c, z_even), (2 * c + 1, z_odd)):
                zt = z.T.astype(BF16)
                for qb in range(TM // LANES):
                    gqT_ref[qb, 0:LANES, hh * LANES:(hh + 1) * LANES] = zt[:, qb * LANES:(qb + 1) * LANES]
        gqT_ref[:, LANES:MXU_DIM, :] = jnp.zeros((TM // LANES, MXU_DIM - LANES, NQ), BF16)

    def gqa_kv(p):
        (y,) = qk_norm(p[:, 0:LANES], nrm_ref[3:4, :])
        gk_ref[:, 0:LANES] = rope(y, cg, ag, bg, 16).astype(BF16)
        gk_ref[:, LANES:MXU_DIM] = jnp.zeros((TM, MXU_DIM - LANES), BF16)
        vt = p[:, LANES:2 * LANES].T.astype(BF16)
        gvT_ref[0:LANES, :] = vt
        gvT_ref[LANES:VROWS, :] = ones_rows
        for ku in range(TM // MXU_DIM):
            for g in range(2):
                gvT2_ref[ku, g] = vt[g * HEAD64:(g + 1) * HEAD64, ku * MXU_DIM:(ku + 1) * MXU_DIM]

    order = [gqa_q, diff_q, ret_k, diff_v, diff_k, gqa_kv, ret_q, ret_v, ret_g]
    in_w_in = [diff_q, diff_k, diff_v, ret_q, ret_k, ret_v, ret_g, gqa_q, gqa_kv]
    width = {f: BRANCH_WIDTH for f in in_w_in}
    width[gqa_kv] = 2 * LANES
    start = {f: sum(width[g] for g in in_w_in[:i]) for i, f in enumerate(in_w_in)}
    p_next = proj(start[order[0]], width[order[0]])
    for i, epilogue in enumerate(order):
        p_cur = p_next
        if i + 1 < len(order):
            p_next = proj(start[order[i + 1]], width[order[i + 1]])
        epilogue(p_cur)


def _proj_call(h, attn_norm, w_attn, gmat, nrm, tabs):
    T = h.shape[0]
    S = tabs[0].shape[0]
    nt = T // TM
    tab_spec = pl.BlockSpec((TM, LANES), lambda i: (i % (S // TM), 0))
    row = lambda w: pl.BlockSpec((TM, w), lambda i: (i, 0))
    out_shape = (
        jax.ShapeDtypeStruct((T, D_MODEL), BF16),
        jax.ShapeDtypeStruct((nt, DIFF_HEADS, LANES, 2 * TM), BF16),
        jax.ShapeDtypeStruct((T, BRANCH_WIDTH), BF16),
        jax.ShapeDtypeStruct((nt, DIFF_HEADS, VROWS, TM), BF16),
        jax.ShapeDtypeStruct((T, BRANCH_WIDTH), BF16),
        jax.ShapeDtypeStruct((nt, RET_HEADS, LANES, TM), BF16),
        jax.ShapeDtypeStruct((T, BRANCH_WIDTH), BF16),
        jax.ShapeDtypeStruct((T, BRANCH_WIDTH), BF16),
        jax.ShapeDtypeStruct((T // LANES, MXU_DIM, NQ), BF16),
        jax.ShapeDtypeStruct((T, MXU_DIM), BF16),
        jax.ShapeDtypeStruct((nt, VROWS, TM), BF16),
        jax.ShapeDtypeStruct((T // MXU_DIM, 2, HEAD64, MXU_DIM), BF16),
    )
    out_specs = (
        row(D_MODEL),
        pl.BlockSpec((None, DIFF_HEADS, LANES, 2 * TM), lambda i: (i, 0, 0, 0)),
        row(BRANCH_WIDTH),
        pl.BlockSpec((None, DIFF_HEADS, VROWS, TM), lambda i: (i, 0, 0, 0)),
        row(BRANCH_WIDTH),
        pl.BlockSpec((None, RET_HEADS, LANES, TM), lambda i: (i, 0, 0, 0)),
        row(BRANCH_WIDTH), row(BRANCH_WIDTH),
        pl.BlockSpec((TM // LANES, MXU_DIM, NQ), lambda i: (i, 0, 0)),
        row(MXU_DIM),
        pl.BlockSpec((None, VROWS, TM), lambda i: (i, 0, 0)),
        pl.BlockSpec((TM // MXU_DIM, 2, HEAD64, MXU_DIM), lambda i: (i, 0, 0, 0)),
    )
    in_specs = [row(D_MODEL), _resident((1, D_MODEL)), _resident(w_attn.shape),
                _resident(gmat.shape), _resident(nrm.shape)] + [tab_spec] * 8
    return pl.pallas_call(
        _proj_kernel, grid=(nt,), in_specs=in_specs, out_specs=out_specs,
        out_shape=out_shape, compiler_params=_cparams(1), name="proj",
    )(h, attn_norm, w_attn, gmat, nrm, *tabs)


def _attn_core(fast_ref, qT_ref, k_ref, vT_ref, m_ref, acc_ref):
    n_steps = k_ref.shape[0] // TK
    n_trips = n_steps // KV_UNROLL
    qT = jnp.concatenate([qT_ref[i] for i in range(Q_BLOCKS)], axis=-1)

    def kv(j):
        k = k_ref[pl.ds(pl.multiple_of(j * TK, TK), TK), :]
        per = TK // TM
        vT = jnp.concatenate([vT_ref[j * per + i] for i in range(per)], axis=-1)
        return k, vT

    @pl.when(fast_ref[0] == 1)
    def _():
        def body(jj, carry):
            pv = lsum = None
            for u in range(KV_UNROLL):
                k, vT = kv(jj * KV_UNROLL + u)
                s = jnp.dot(k, qT, preferred_element_type=F32)
                pf = jnp.exp2(s)
                ls = jnp.sum(pf, axis=0, keepdims=True)
                t = jnp.dot(vT[0:LANES], pf.astype(BF16), preferred_element_type=F32)
                pv, lsum = (t, ls) if pv is None else (pv + t, lsum + ls)
            if n_trips == 1:
                acc_ref[0:LANES, :] = pv
                acc_ref[LANES:LANES + 1, :] = lsum
            else:
                acc_ref[0:LANES, :] += pv
                acc_ref[LANES:LANES + 1, :] += lsum
            return carry

        if n_trips > 1:
            acc_ref[...] = jnp.zeros(acc_ref.shape, F32)
        lax.fori_loop(0, n_trips, body, 0)

    @pl.when(fast_ref[0] == 0)
    def _():
        acc_ref[...] = jnp.zeros(acc_ref.shape, F32)
        m_ref[...] = jnp.full(m_ref.shape, -1e30, F32)

        def body(j, carry):
            k, vT = kv(j)
            for c in range(NCOLS // NC):
                cs = slice(c * NC, (c + 1) * NC)
                s = jnp.dot(k, qT[:, cs], preferred_element_type=F32)
                m_old = m_ref[:, cs]
                m_new = jnp.maximum(m_old, jnp.max(s, axis=0, keepdims=True))
                alpha = jnp.exp2(m_old - m_new)
                p = jnp.exp2(s - m_new).astype(BF16)
                pv = jnp.dot(vT, p, preferred_element_type=F32)
                acc_ref[:, cs] = acc_ref[:, cs] * alpha + pv
                m_ref[:, cs] = m_new
            return carry

        lax.fori_loop(0, n_steps, body, 0)


def _diff_kernel(fast_ref, lam_ref, sub_ref, qT_ref, k_ref, vT_ref, o_ref, m_ref, acc_ref, *,
                 lambda_init):
    lv = lam_ref[...]
    lam = (jnp.exp(jnp.sum(lv[0:1] * lv[1:2], axis=-1, keepdims=True))
           - jnp.exp(jnp.sum(lv[2:3] * lv[3:4], axis=-1, keepdims=True)) + lambda_init)
    _attn_core(fast_ref, qT_ref, k_ref, vT_ref, m_ref, acc_ref)

    tq = NQ // 2
    for qb in range(Q_BLOCKS):
        c1 = slice(qb * NQ, qb * NQ + tq)
        c2 = slice(qb * NQ + tq, (qb + 1) * NQ)
        o1 = acc_ref[0:LANES, c1] / acc_ref[LANES:LANES + 1, c1]
        o2 = acc_ref[0:LANES, c2] / acc_ref[LANES:LANES + 1, c2]
        d = o1 - lam * o2
        ms = jnp.mean(d * d, axis=0, keepdims=True)
        dn = d * lax.rsqrt(ms + 1e-5)
        o_ref[qb * tq:(qb + 1) * tq, :] = (
            dn.T * (sub_ref[...] * (1.0 - lambda_init))).astype(o_ref.dtype)


_SMEM_SPEC = pl.BlockSpec(memory_space=pltpu.SMEM)
_ATTN_SCRATCH = [pltpu.VMEM((1, NCOLS), F32), pltpu.VMEM((VROWS, NCOLS), F32)]


def _diff_call(fast, lamv, subln, dqT, dk, dvT, B, S, lambda_init):
    T = dk.shape[0]
    tq = Q_BLOCKS * NQ // 2
    nq = S // tq
    kern = functools.partial(_diff_kernel, lambda_init=lambda_init)
    return pl.pallas_call(
        kern, grid=(B, DIFF_HEADS, nq),
        in_specs=[
            _SMEM_SPEC, _resident(lamv.shape), _resident(subln.shape),
            pl.BlockSpec((Q_BLOCKS, None, LANES, NQ), lambda b, h, i: (b * nq + i, h, 0, 0)),
            pl.BlockSpec((S, LANES), lambda b, h, i: (b, h)),
            pl.BlockSpec((S // TM, None, VROWS, TM), lambda b, h, i: (b, h, 0, 0)),
        ],
        out_specs=pl.BlockSpec((tq, LANES), lambda b, h, i: (b * nq + i, h)),
        out_shape=jax.ShapeDtypeStruct((T, BRANCH_WIDTH), BF16),
        scratch_shapes=_ATTN_SCRATCH,
        compiler_params=_cparams(3), name="diff_attn",
    )(fast, lamv, subln, dqT, dk, dvT)


def _gqa_kernel(fast_ref, qT_ref, k_ref, vT_ref, o_ref, m_ref, acc_ref):
    _attn_core(fast_ref, qT_ref, k_ref, vT_ref, m_ref, acc_ref)
    for qb, c in [(qb, c) for qb in range(Q_BLOCKS) for c in range(GQA_Q_HEADS // 2)]:
        g = (2 * c) // GQA_GROUP
        rows = slice(g * HEAD64, (g + 1) * HEAD64)
        parts = []
        for hh in (2 * c, 2 * c + 1):
            cols = slice(qb * NQ + hh * LANES, qb * NQ + (hh + 1) * LANES)
            parts.append(acc_ref[rows, cols] / acc_ref[LANES:LANES + 1, cols])
        o_ref[qb * LANES:(qb + 1) * LANES, c * LANES:(c + 1) * LANES] = (
            jnp.concatenate(parts, axis=0).T.astype(o_ref.dtype))


def _gqa_call(fast, gqT, gk, gvT, B, S):
    T = gk.shape[0]
    tq = Q_BLOCKS * NQ // GQA_Q_HEADS
    nq = S // tq
    return pl.pallas_call(
        _gqa_kernel, grid=(B, nq),
        in_specs=[
            _SMEM_SPEC,
            pl.BlockSpec((Q_BLOCKS, LANES, NQ), lambda b, i: (b * nq + i, 0, 0)),
            pl.BlockSpec((S, LANES), lambda b, i: (b, 0)),
            pl.BlockSpec((S // TM, VROWS, TM), lambda b, i: (b, 0, 0)),
        ],
        out_specs=pl.BlockSpec((tq, BRANCH_WIDTH), lambda b, i: (b * nq + i, 0)),
        out_shape=jax.ShapeDtypeStruct((T, BRANCH_WIDTH), BF16),
        scratch_shapes=_ATTN_SCRATCH,
        compiler_params=_cparams(2), name="gqa_attn",
    )(fast, gqT, gk, gvT)


S_REGIONS = (0, 64, 128)
O_REGIONS = ((192, 208), (224, 240))
MXU_Q_BLOCKS = len(O_REGIONS)
POP_LAG = 2
PV_LAG = 3
P_RING = 6
TRIP_SLOTS = 30


def _gqa_mxu_kernel(qT_ref, k_ref, vT_ref, o_ref, p_ref, l_ref):
    per_block = 2 * (k_ref.shape[0] // MXU_DIM)
    n_slots = MXU_Q_BLOCKS * per_block
    head = per_block % TRIP_SLOTS
    assert head > PV_LAG and TRIP_SLOTS % P_RING == 0

    @pl.when((pl.program_id(0) == 0) & (pl.program_id(1) == 0))
    def _():
        for mxu in range(2):
            for region in S_REGIONS:
                pltpu.matmul_pop(region, (MXU_DIM, MXU_DIM), F32, mxu)
            for region in sum(O_REGIONS, ()):
                pltpu.matmul_pop(region, (HEAD64, MXU_DIM), F32, mxu)

    l_ref[...] = jnp.zeros(l_ref.shape, F32)

    def q_tile(mxu, n):
        c = 2 * mxu + n % 2
        return qT_ref[n // per_block, :, c * MXU_DIM:(c + 1) * MXU_DIM]

    def o_region(n):
        return O_REGIONS[n // per_block][n % 2]

    def l_cols(mxu, n):
        c0 = (n // per_block) * NQ + (2 * mxu + n % 2) * MXU_DIM
        return slice(c0, c0 + MXU_DIM)

    def slot(n, unit, unit_v, stage_next=True):
        start = unit * MXU_DIM
        if not isinstance(unit, int):
            start = pl.multiple_of(start, MXU_DIM)
        k = k_ref[pl.ds(start, MXU_DIM), :]
        for mxu in range(2):
            if n >= PV_LAG:
                pltpu.matmul_push_rhs(p_ref[mxu, (n - PV_LAG) % P_RING], staging_register=1,
                                      mxu_index=mxu)
            pltpu.matmul_acc_lhs(S_REGIONS[n % 3], k, mxu_index=mxu, load_staged_rhs=0)
            if stage_next:
                pltpu.matmul_push_rhs(q_tile(mxu, n + 1), staging_register=0, mxu_index=mxu)
            if n >= PV_LAG:
                pltpu.matmul_acc_lhs(o_region(n - PV_LAG), vT_ref[unit_v, mxu],
                                     mxu_index=mxu, load_staged_rhs=1)
        for mxu in range(2):
            if n >= POP_LAG:
                pop_scores(mxu, n - POP_LAG)

    def pop_scores(mxu, n):
        s = pltpu.matmul_pop(S_REGIONS[n % 3], (MXU_DIM, MXU_DIM), F32, mxu)
        pf = jnp.exp2(s)
        l_ref[:, l_cols(mxu, n)] += jnp.sum(pf, axis=0, keepdims=True)
        p_ref[mxu, n % P_RING] = pf.astype(BF16)

    def values_only(mxu, n, unit_v):
        pltpu.matmul_push_rhs(p_ref[mxu, n % P_RING], staging_register=1, mxu_index=mxu)
        pltpu.matmul_acc_lhs(o_region(n), vT_ref[unit_v, mxu], mxu_index=mxu, load_staged_rhs=1)

    def unit_of(n):
        return (n % per_block) // 2

    for mxu in range(2):
        pltpu.matmul_push_rhs(q_tile(mxu, 0), staging_register=0, mxu_index=mxu)
    for blk in range(MXU_Q_BLOCKS):
        first = blk * per_block
        for n in range(first, first + head):
            slot(n, unit_of(n), unit_of(max(n - PV_LAG, 0)))

        def trip(t, carry, first=first + head):
            for i in range(TRIP_SLOTS):
                base = first + TRIP_SLOTS * t + i
                slot(first + i, unit_of(base), unit_of(base - PV_LAG))
            return carry

        lax.fori_loop(0, per_block // TRIP_SLOTS, trip, 0)

    for mxu in range(2):
        pltpu.matmul_acc_lhs(S_REGIONS[n_slots % 3], jnp.zeros((16, MXU_DIM), BF16),
                             mxu_index=mxu, load_staged_rhs=0)
    for n in range(n_slots - POP_LAG, n_slots):
        for mxu in range(2):
            pop_scores(mxu, n)
    for mxu in range(2):
        pltpu.matmul_pop(S_REGIONS[n_slots % 3], (16, MXU_DIM), F32, mxu)
    for n in range(n_slots - PV_LAG, n_slots):
        for mxu in range(2):
            values_only(mxu, n, unit_of(n))

    tq = NQ // GQA_Q_HEADS
    for blk in range(MXU_Q_BLOCKS):
        for mxu in range(2):
            for cc in range(2):
                n = blk * per_block + cc
                c = 2 * mxu + cc
                o = pltpu.matmul_pop(o_region(n), (HEAD64, MXU_DIM), F32, mxu)
                o = o / l_ref[:, l_cols(mxu, n)]
                pair = jnp.concatenate([o[:, 0:LANES], o[:, LANES:2 * LANES]], axis=0)
                o_ref[blk * tq:(blk + 1) * tq, c * LANES:(c + 1) * LANES] = pair.T.astype(o_ref.dtype)


def _gqa_mxu_call(gqT, gk, gvT2, B, S):
    T = gk.shape[0]
    tq = MXU_Q_BLOCKS * NQ // GQA_Q_HEADS
    nq = S // tq
    return pl.pallas_call(
        _gqa_mxu_kernel, grid=(B, nq),
        in_specs=[
            pl.BlockSpec((MXU_Q_BLOCKS, MXU_DIM, NQ), lambda b, i: (b * nq + i, 0, 0)),
            pl.BlockSpec((S, MXU_DIM), lambda b, i: (b, 0)),
            pl.BlockSpec((S // MXU_DIM, 2, HEAD64, MXU_DIM), lambda b, i: (b, 0, 0, 0)),
        ],
        out_specs=pl.BlockSpec((tq, BRANCH_WIDTH), lambda b, i: (b * nq + i, 0)),
        out_shape=jax.ShapeDtypeStruct((T, BRANCH_WIDTH), BF16),
        scratch_shapes=[pltpu.VMEM((2, P_RING, MXU_DIM, MXU_DIM), BF16),
                        pltpu.VMEM((1, MXU_Q_BLOCKS * NQ), F32)],
        compiler_params=_cparams(2), name="gqa_attn_mxu",
    )(gqT, gk, gvT2)


def _ret_kernel(dec_ref, gain_ref, q_ref, kT_ref, v_ref, g_ref, o_ref, sf_ref, sb_ref):
    C = RET_CHUNK
    hd = pl.program_id(1)
    lf = -jnp.exp(jnp.full((C, C), dec_ref[0, hd], F32))
    lb = -jnp.exp(jnp.full((C, C), dec_ref[1, hd], F32))
    ii = lax.broadcasted_iota(jnp.int32, (C, C), 0).astype(F32)
    jj = lax.broadcasted_iota(jnp.int32, (C, C), 1).astype(F32)
    diff = ii - jj
    decay = jnp.where(diff >= 0, jnp.exp(jnp.maximum(diff, 0.0) * lf),
                      jnp.exp(jnp.maximum(-diff, 0.0) * lb))
    xi_f, zeta_f = jnp.exp((ii + 1.0) * lf), jnp.exp((C - 1.0 - ii) * lf)
    xi_b, zeta_b = jnp.exp((C - ii) * lb), jnp.exp(ii * lb)
    dec_f, dec_b = jnp.exp(C * lf), jnp.exp(C * lb)
    n_tiles = kT_ref.shape[0]
    per_tile = TM // C

    def rows(t, c):
        return pl.ds(pl.multiple_of(t * TM + c * C, C), C)

    def chunk_kv(t, c, zeta):
        vz = (v_ref[rows(t, c), :].astype(F32) * zeta).astype(BF16)
        return jnp.dot(kT_ref[t, :, c * C:(c + 1) * C], vz, preferred_element_type=F32)

    def scans(i, states):
        st_f, st_b = states
        fwd = [(i * RET_SCAN_TILES + t, c) for t in range(RET_SCAN_TILES) for c in range(per_tile)]
        bwd = [(n_tiles - 1 - t, per_tile - 1 - c) for t, c in fwd]
        kv_f = [chunk_kv(t, c, zeta_f) for t, c in fwd]
        kv_b = [chunk_kv(t, c, zeta_b) for t, c in bwd]
        for (tf, cf), kf, (tb, cb), kb in zip(fwd, kv_f, bwd, kv_b):
            sf_ref[tf * per_tile + cf] = st_f.astype(BF16)
            st_f = st_f * dec_f + kf
            sb_ref[tb * per_tile + cb] = st_b.astype(BF16)
            st_b = st_b * dec_b + kb
        return st_f, st_b

    zero = jnp.zeros((C, C), F32)
    lax.fori_loop(0, n_tiles // RET_SCAN_TILES, scans, (zero, zero))

    def outputs(tt, carry):
        chunks = [(tt * RET_OUT_TILES + i, c) for i in range(RET_OUT_TILES) for c in range(per_tile)]
        rs = [rows(t, c) for t, c in chunks]
        qs = [q_ref[r, :] for r in rs]
        a = [jnp.dot(q, kT_ref[t, :, c * C:(c + 1) * C], preferred_element_type=F32)
             for q, (t, c) in zip(qs, chunks)]
        cross = [jnp.dot(q, sf_ref[t * per_tile + c], preferred_element_type=F32) * xi_f
                 + jnp.dot(q, sb_ref[t * per_tile + c], preferred_element_type=F32) * xi_b
                 for q, (t, c) in zip(qs, chunks)]
        a = [(x * decay).astype(BF16) for x in a]
        o = [jnp.dot(x, v_ref[r, :], preferred_element_type=F32) + y
             for x, r, y in zip(a, rs, cross)]
        mu = [jnp.mean(x, axis=-1, keepdims=True) for x in o]
        oc = [x - m for x, m in zip(o, mu)]
        var = [jnp.mean(x * x, axis=-1, keepdims=True) for x in oc]
        for x, s2, r in zip(oc, var, rs):
            y = x * lax.rsqrt(s2 + 1e-5) * gain_ref[...]
            g = g_ref[r, :].astype(F32)
            o_ref[r, :] = (g / (1.0 + jnp.exp(-g)) * y).astype(o_ref.dtype)
        return carry

    lax.fori_loop(0, n_tiles // RET_OUT_TILES, outputs, 0)


def _ret_call(dec, gain, rq, rkT, rv, rg, B, S):
    T = rq.shape[0]
    blk = pl.BlockSpec((S, LANES), lambda b, h: (b, h))
    n_chunks = S // RET_CHUNK
    return pl.pallas_call(
        _ret_kernel, grid=(B, RET_HEADS),
        in_specs=[pl.BlockSpec(memory_space=pltpu.SMEM),
                  pl.BlockSpec((1, LANES), lambda b, h: (0, h)), blk,
                  pl.BlockSpec((S // TM, None, LANES, TM), lambda b, h: (b, h, 0, 0)), blk, blk],
        out_specs=blk,
        out_shape=jax.ShapeDtypeStruct((T, BRANCH_WIDTH), BF16),
        scratch_shapes=[pltpu.VMEM((n_chunks, RET_CHUNK, LANES), BF16),
                        pltpu.VMEM((n_chunks, RET_CHUNK, LANES), BF16)],
        compiler_params=_cparams(2), name="retention",
    )(dec, gain, rq, rkT, rv, rg)


def _merge_kernel(h_ref, u_ref, a_ref, r_ref, c_ref, wg_ref, wb_ref, wo_ref, mn_ref,
                  hn_ref, u2_ref):
    u = u_ref[...]
    merged = None
    for n, br in enumerate((a_ref, r_ref, c_ref)):
        logits = jnp.dot(u, wg_ref[:, n * D_MODEL:(n + 1) * D_MODEL], preferred_element_type=F32)
        gate = 1.0 / (1.0 + jnp.exp(-logits))
        term = gate * jnp.dot(br[...], wb_ref[n], preferred_element_type=F32)
        merged = term if merged is None else merged + term
    merged = merged.astype(BF16)
    halves = [slice(0, TM // 2), slice(TM // 2, TM)]
    hns = [h_ref[r, :] + jnp.dot(merged[r], wo_ref[...], preferred_element_type=F32) for r in halves]
    for r, hn in zip(halves, hns):
        hn_ref[r, :] = hn
        ms = jnp.mean(hn * hn, axis=-1, keepdims=True)
        u2_ref[r, :] = (hn * lax.rsqrt(ms + NORM_EPS) * mn_ref[...]).astype(BF16)


def _merge_call(h, u, a, r, c, wg, wb, wo, mlp_norm):
    T = h.shape[0]
    row = lambda w: pl.BlockSpec((TM, w), lambda i: (i, 0))
    return pl.pallas_call(
        _merge_kernel, grid=(T // TM,),
        in_specs=[row(D_MODEL), row(D_MODEL), row(BRANCH_WIDTH), row(BRANCH_WIDTH), row(BRANCH_WIDTH),
                  _resident(wg.shape), _resident(wb.shape), _resident(wo.shape),
                  _resident((1, D_MODEL))],
        out_specs=(row(D_MODEL), row(D_MODEL)),
        out_shape=(jax.ShapeDtypeStruct((T, D_MODEL), F32),
                   jax.ShapeDtypeStruct((T, D_MODEL), BF16)),
        compiler_params=_cparams(1), name="merge",
    )(h, u, a, r, c, wg, wb, wo, mlp_norm)


def _mlp_kernel(h_ref, u_ref, w1_ref, w2_ref, o_ref):
    u = u_ref[...]
    acc = h_ref[...]
    for c in range(D_FF // D_MODEL):
        cs = slice(c * D_MODEL, (c + 1) * D_MODEL)
        m = jnp.maximum(jnp.dot(u, w1_ref[:, cs], preferred_element_type=F32), 0.0)
        acc = acc + jnp.dot((m * m).astype(BF16), w2_ref[cs, :], preferred_element_type=F32)
    o_ref[...] = acc


def _mlp_call(h, u2, w1, w2):
    T = h.shape[0]
    row = pl.BlockSpec((TM, D_MODEL), lambda i: (i, 0))
    return pl.pallas_call(
        _mlp_kernel, grid=(T // TM,),
        in_specs=[row, row, _resident(w1.shape), _resident(w2.shape)],
        out_specs=row,
        out_shape=jax.ShapeDtypeStruct((T, D_MODEL), F32),
        compiler_params=_cparams(1), name="mlp",
    )(h, u2, w1, w2)


def _rope_tables(S):
    pos = jnp.arange(S, dtype=F32)
    lane = jnp.arange(LANES)
    l64 = lane % HEAD64

    inv = 500000.0 ** (-jnp.arange(0, 16, 2, dtype=F32) / 16)
    ang = pos[:, None] * inv[None, :]
    cos, sin = jnp.cos(ang)[:, l64 % 8], jnp.sin(ang)[:, l64 % 8]
    cd = jnp.where(l64 < 16, cos, 1.0)
    ad = jnp.where(l64 < 8, -sin, 0.0)
    bd = jnp.where((l64 >= 8) & (l64 < 16), sin, 0.0)

    inv = 10000.0 ** (-jnp.arange(0, 32, 2, dtype=F32) / 32)
    row_pos = jnp.floor(pos / GRID_W)
    col_pos = pos - row_pos * GRID_W
    p2 = jnp.where((l64 < 32)[None, :], row_pos[:, None], col_pos[:, None])
    ang = p2 * inv[l64 % 16][None, :]
    cg = jnp.cos(ang)
    ag = jnp.where(l64 % 32 < 16, -jnp.sin(ang), 0.0)
    bg = jnp.where(l64 % 32 >= 16, jnp.sin(ang), 0.0)

    inv = 10000.0 ** (-jnp.arange(0, 128, 2, dtype=F32) / 128)
    ang = pos[:, None] * inv[l64][None, :]
    cr = jnp.cos(ang)
    sr = jnp.where(lane < LANES // 2, -jnp.sin(ang), jnp.sin(ang))
    return (cd, ad, bd, cg, ag, bg, cr, sr)


def _fast_flag(q_gain, k_gain):
    bound = (HEAD64 ** 0.5 * LOG2E * 1.02) * jnp.max(jnp.abs(q_gain)) * jnp.max(jnp.abs(k_gain))
    return (bound <= FAST_SOFTMAX_BOUND).astype(jnp.int32).reshape(1)


def kernel(x, attn_norm, w_in, diff_q_norm, diff_k_norm, diff_lam_q1, diff_lam_k1, diff_lam_q2, diff_lam_k2, diff_subln, ret_decay_fwd, ret_decay_bwd, ret_group_norm, gqa_q_norm, gqa_k_norm, w_branch, w_out, mlp_norm, w_mlp_in, w_mlp_out):
    B, S, D = x.shape
    T = B * S
    tabs = _rope_tables(S)
    blk = jnp.arange(2 * LANES) // HEAD64
    gmat = jnp.where(blk[:, None] == blk[None, :], 1.0 / HEAD64, 0.0).astype(BF16)
    tile2 = lambda v: jnp.concatenate([v, v]).astype(F32)

    h = x.reshape(T, D)
    for l in range(DEPTH):
        lambda_init = 0.8 - 0.6 * math.exp(-0.3 * l)
        w_attn = w_in[l, :, :ATTN_COLS].astype(BF16)
        w_gate = w_in[l, :, ATTN_COLS:].astype(BF16)
        nrm = jnp.stack([tile2(diff_q_norm[l]), tile2(diff_k_norm[l]),
                         tile2(gqa_q_norm[l]), tile2(gqa_k_norm[l])])
        (u, dqT, dk, dvT, rq, rkT, rv, rg, gqT, gk, gvT, gvT2) = _proj_call(
            h, attn_norm[l].reshape(1, D).astype(F32), w_attn, gmat, nrm, tabs)

        lamv = jnp.stack([diff_lam_q1[l], diff_lam_k1[l], diff_lam_q2[l], diff_lam_k2[l]]).astype(F32)
        a = _diff_call(_fast_flag(diff_q_norm[l], diff_k_norm[l]), lamv,
                       diff_subln[l].reshape(1, LANES).astype(F32), dqT, dk, dvT, B, S, lambda_init)
        dec = jnp.stack([ret_decay_fwd[l], ret_decay_bwd[l]]).astype(F32)
        r = _ret_call(dec, ret_group_norm[l].reshape(1, BRANCH_WIDTH).astype(F32), rq, rkT, rv, rg, B, S)
        gqa_fast = _fast_flag(gqa_q_norm[l], gqa_k_norm[l])
        c = lax.cond(gqa_fast[0] == 1,
                     lambda ops: _gqa_mxu_call(ops[0], ops[1], ops[3], B, S),
                     lambda ops: _gqa_call(gqa_fast, ops[0], ops[1], ops[2], B, S),
                     (gqT, gk, gvT, gvT2))

        h, u2 = _merge_call(h, u, a, r, c, w_gate, w_branch[l].astype(BF16),
                            w_out[l].astype(BF16), mlp_norm[l].reshape(1, D).astype(F32))
        h = _mlp_call(h, u2, w_mlp_in[l].astype(BF16), w_mlp_out[l].astype(BF16))
    return h.reshape(B, S, D)
```

```python
import functools
import math

import jax
import jax.numpy as jnp
from jax import lax
from jax.experimental import pallas as pl
from jax.experimental.pallas import tpu as pltpu

F32 = jnp.float32
BF16 = jnp.bfloat16

D_MODEL = 1024
DEPTH = 2
GRID_W = 64
NORM_EPS = 1e-6
HEAD64 = 64
DIFF_HEADS = 4
RET_HEADS = 4
RET_CHUNK = 128
GQA_Q_HEADS = 8
GQA_GROUP = 4
N_BRANCHES = 3
BRANCH_WIDTH = 512
D_FF = 4 * D_MODEL
LANES = 128
MXU_DIM = 256
ATTN_COLS = 8 * BRANCH_WIDTH + 2 * LANES

TM = 512
TK = 2048
NQ = 1024
Q_BLOCKS = 2
NCOLS = Q_BLOCKS * NQ
NC = 256
KV_UNROLL = 4
RET_SCAN_TILES = 4
RET_OUT_TILES = 8
ONES_ROWS = 16
VROWS = LANES + ONES_ROWS
FAST_SOFTMAX_BOUND = 64.0
VMEM_LIMIT = 56 * 1024 * 1024
LOG2E = 1.4426950408889634


def _cparams(n_axes):
    return pltpu.CompilerParams(dimension_semantics=("arbitrary",) * n_axes,
                                vmem_limit_bytes=VMEM_LIMIT)


def _resident(shape):
    nd = len(shape)
    return pl.BlockSpec(shape, lambda *_: (0,) * nd, pipeline_mode=pl.Buffered(1))


def _proj_kernel(h_ref, an_ref, w_ref, g_ref, nrm_ref,
                 cd_ref, ad_ref, bd_ref, cg_ref, ag_ref, bg_ref, cr_ref, sr_ref,
                 u_ref, dqT_ref, dk_ref, dvT_ref, rq_ref, rkT_ref, rv_ref, rg_ref,
                 gqT_ref, gk_ref, gvT_ref, gvT2_ref):
    x = h_ref[...]
    ms = jnp.mean(x * x, axis=-1, keepdims=True)
    u = (x * lax.rsqrt(ms + NORM_EPS) * an_ref[...]).astype(BF16)
    u_ref[...] = u

    def proj(c0, n):
        return jnp.dot(u, w_ref[:, c0:c0 + n], preferred_element_type=F32)

    gmat = g_ref[...]
    lo = lax.broadcasted_iota(jnp.int32, (1, LANES), 1) < HEAD64
    first_row = lax.broadcasted_iota(jnp.int32, (ONES_ROWS, TM), 0) == 0
    ones_rows = jnp.where(first_row, 1.0, 0.0).astype(BF16)

    def qk_norm(p, gain):
        out = []
        for c0 in range(0, p.shape[1], 2 * LANES):
            y = p[:, c0:c0 + 2 * LANES]
            w = y.shape[1]
            msq = jnp.dot((y * y).astype(BF16), gmat[0:w, 0:w], preferred_element_type=F32)
            y = y * lax.rsqrt(msq + NORM_EPS)
            out += [y[:, i:i + LANES] * gain for i in range(0, w, LANES)]
        return out

    def rope(y, c, a, b, d):
        return y * c + pltpu.roll(y, LANES - d, 1) * a + pltpu.roll(y, d, 1) * b

    def chunks(p):
        return [p[:, i:i + LANES] for i in range(0, p.shape[1], LANES)]

    qscale = HEAD64 ** -0.5 * LOG2E
    kscale = LANES ** -0.5
    cd, ad, bd = cd_ref[...], ad_ref[...], bd_ref[...]
    cg, ag, bg = cg_ref[...], ag_ref[...], bg_ref[...]
    cr, sr = cr_ref[...], sr_ref[...]

    def diff_q(p):
        for hd, y in enumerate(qk_norm(p, nrm_ref[0:1, :])):
            y = rope(y, cd, ad, bd, 8) * qscale
            dqT_ref[hd, :, 0:TM] = jnp.where(lo, y, 0.0).T.astype(BF16)
            dqT_ref[hd, :, TM:2 * TM] = jnp.where(lo, 0.0, y).T.astype(BF16)

    def diff_k(p):
        for hd, y in enumerate(qk_norm(p, nrm_ref[1:2, :])):
            dk_ref[:, hd * LANES:(hd + 1) * LANES] = rope(y, cd, ad, bd, 8).astype(BF16)

    def diff_v(p):
        for hd, y in enumerate(chunks(p)):
            dvT_ref[hd, 0:LANES, :] = y.T.astype(BF16)
            dvT_ref[hd, LANES:VROWS, :] = ones_rows

    def ret_q(p):
        for hd, y in enumerate(chunks(p)):
            rq_ref[:, hd * LANES:(hd + 1) * LANES] = (y * cr + pltpu.roll(y, LANES // 2, 1) * sr).astype(BF16)

    def ret_k(p):
        for hd, y in enumerate(chunks(p)):
            rkT_ref[hd] = ((y * cr + pltpu.roll(y, LANES // 2, 1) * sr) * kscale).T.astype(BF16)

    def ret_v(p):
        rv_ref[...] = p.astype(BF16)

    def ret_g(p):
        rg_ref[...] = p.astype(BF16)

    def gqa_q(p):
        for c, y in enumerate(qk_norm(p, nrm_ref[2:3, :])):
            y = rope(y, cg, ag, bg, 16) * qscale
            ysw = pltpu.roll(y, LANES // 2, 1)
            if c < 2:
                z_even, z_odd = jnp.where(lo, y, 0.0), jnp.where(lo, ysw, 0.0)
            else:
                z_even, z_odd = jnp.where(lo, 0.0, ysw), jnp.where(lo, 0.0, y)
            for hh, z in ((2 * c, z_even), (2 * c + 1, z_odd)):
                zt = z.T.astype(BF16)
                for qb in range(TM // LANES):
                    gqT_ref[qb, 0:LANES, hh * LANES:(hh + 1) * LANES] = zt[:, qb * LANES:(qb + 1) * LANES]
        gqT_ref[:, LANES:MXU_DIM, :] = jnp.zeros((TM // LANES, MXU_DIM - LANES, NQ), BF16)

    def gqa_kv(p):
        (y,) = qk_norm(p[:, 0:LANES], nrm_ref[3:4, :])
        gk_ref[:, 0:LANES] = rope(y, cg, ag, bg, 16).astype(BF16)
        gk_ref[:, LANES:MXU_DIM] = jnp.zeros((TM, MXU_DIM - LANES), BF16)
        vt = p[:, LANES:2 * LANES].T.astype(BF16)
        gvT_ref[0:LANES, :] = vt
        gvT_ref[LANES:VROWS, :] = ones_rows
        for ku in range(TM // MXU_DIM):
            for g in range(2):
                gvT2_ref[ku, g] = vt[g * HEAD64:(g + 1) * HEAD64, ku * MXU_DIM:(ku + 1) * MXU_DIM]

    order = [gqa_q, diff_q, ret_k, diff_v, diff_k, gqa_kv, ret_q, ret_v, ret_g]
    in_w_in = [diff_q, diff_k, diff_v, ret_q, ret_k, ret_v, ret_g, gqa_q, gqa_kv]
    width = {f: BRANCH_WIDTH for f in in_w_in}
    width[gqa_kv] = 2 * LANES
    start = {f: sum(width[g] for g in in_w_in[:i]) for i, f in enumerate(in_w_in)}
    p_next = proj(start[order[0]], width[order[0]])
    for i, epilogue in enumerate(order):
        p_cur = p_next
        if i + 1 < len(order):
            p_next = proj(start[order[i + 1]], width[order[i + 1]])
        epilogue(p_cur)


def _proj_call(h, attn_norm, w_attn, gmat, nrm, tabs):
    T = h.shape[0]
    S = tabs[0].shape[0]
    nt = T // TM
    tab_spec = pl.BlockSpec((TM, LANES), lambda i: (i % (S // TM), 0))
    row = lambda w: pl.BlockSpec((TM, w), lambda i: (i, 0))
    out_shape = (
        jax.ShapeDtypeStruct((T, D_MODEL), BF16),
        jax.ShapeDtypeStruct((nt, DIFF_HEADS, LANES, 2 * TM), BF16),
        jax.ShapeDtypeStruct((T, BRANCH_WIDTH), BF16),
        jax.ShapeDtypeStruct((nt, DIFF_HEADS, VROWS, TM), BF16),
        jax.ShapeDtypeStruct((T, BRANCH_WIDTH), BF16),
        jax.ShapeDtypeStruct((nt, RET_HEADS, LANES, TM), BF16),
        jax.ShapeDtypeStruct((T, BRANCH_WIDTH), BF16),
        jax.ShapeDtypeStruct((T, BRANCH_WIDTH), BF16),
        jax.ShapeDtypeStruct((T // LANES, MXU_DIM, NQ), BF16),
        jax.ShapeDtypeStruct((T, MXU_DIM), BF16),
        jax.ShapeDtypeStruct((nt, VROWS, TM), BF16),
        jax.ShapeDtypeStruct((T // MXU_DIM, 2, HEAD64, MXU_DIM), BF16),
    )
    out_specs = (
        row(D_MODEL),
        pl.BlockSpec((None, DIFF_HEADS, LANES, 2 * TM), lambda i: (i, 0, 0, 0)),
        row(BRANCH_WIDTH),
        pl.BlockSpec((None, DIFF_HEADS, VROWS, TM), lambda i: (i, 0, 0, 0)),
        row(BRANCH_WIDTH),
        pl.BlockSpec((None, RET_HEADS, LANES, TM), lambda i: (i, 0, 0, 0)),
        row(BRANCH_WIDTH), row(BRANCH_WIDTH),
        pl.BlockSpec((TM // LANES, MXU_DIM, NQ), lambda i: (i, 0, 0)),
        row(MXU_DIM),
        pl.BlockSpec((None, VROWS, TM), lambda i: (i, 0, 0)),
        pl.BlockSpec((TM // MXU_DIM, 2, HEAD64, MXU_DIM), lambda i: (i, 0, 0, 0)),
    )
    in_specs = [row(D_MODEL), _resident((1, D_MODEL)), _resident(w_attn.shape),
                _resident(gmat.shape), _resident(nrm.shape)] + [tab_spec] * 8
    return pl.pallas_call(
        _proj_kernel, grid=(nt,), in_specs=in_specs, out_specs=out_specs,
        out_shape=out_shape, compiler_params=_cparams(1), name="proj",
    )(h, attn_norm, w_attn, gmat, nrm, *tabs)


def _attn_core(fast_ref, qT_ref, k_ref, vT_ref, m_ref, acc_ref):
    n_steps = k_ref.shape[0] // TK
    n_trips = n_steps // KV_UNROLL
    qT = jnp.concatenate([qT_ref[i] for i in range(Q_BLOCKS)], axis=-1)

    def kv(j):
        k = k_ref[pl.ds(pl.multiple_of(j * TK, TK), TK), :]
        per = TK // TM
        vT = jnp.concatenate([vT_ref[j * per + i] for i in range(per)], axis=-1)
        return k, vT

    @pl.when(fast_ref[0] == 1)
    def _():
        def body(jj, carry):
            pv = lsum = None
            for u in range(KV_UNROLL):
                k, vT = kv(jj * KV_UNROLL + u)
                s = jnp.dot(k, qT, preferred_element_type=F32)
                pf = jnp.exp2(s)
                ls = jnp.sum(pf, axis=0, keepdims=True)
                t = jnp.dot(vT[0:LANES], pf.astype(BF16), preferred_element_type=F32)
                pv, lsum = (t, ls) if pv is None else (pv + t, lsum + ls)
            if n_trips == 1:
                acc_ref[0:LANES, :] = pv
                acc_ref[LANES:LANES + 1, :] = lsum
            else:
                acc_ref[0:LANES, :] += pv
                acc_ref[LANES:LANES + 1, :] += lsum
            return carry

        if n_trips > 1:
            acc_ref[...] = jnp.zeros(acc_ref.shape, F32)
        lax.fori_loop(0, n_trips, body, 0)

    @pl.when(fast_ref[0] == 0)
    def _():
        acc_ref[...] = jnp.zeros(acc_ref.shape, F32)
        m_ref[...] = jnp.full(m_ref.shape, -1e30, F32)

        def body(j, carry):
            k, vT = kv(j)
            for c in range(NCOLS // NC):
                cs = slice(c * NC, (c + 1) * NC)
                s = jnp.dot(k, qT[:, cs], preferred_element_type=F32)
                m_old = m_ref[:, cs]
                m_new = jnp.maximum(m_old, jnp.max(s, axis=0, keepdims=True))
                alpha = jnp.exp2(m_old - m_new)
                p = jnp.exp2(s - m_new).astype(BF16)
                pv = jnp.dot(vT, p, preferred_element_type=F32)
                acc_ref[:, cs] = acc_ref[:, cs] * alpha + pv
                m_ref[:, cs] = m_new
            return carry

        lax.fori_loop(0, n_steps, body, 0)


def _diff_kernel(fast_ref, lam_ref, sub_ref, qT_ref, k_ref, vT_ref, o_ref, m_ref, acc_ref, *,
                 lambda_init):
    lv = lam_ref[...]
    lam = (jnp.exp(jnp.sum(lv[0:1] * lv[1:2], axis=-1, keepdims=True))
           - jnp.exp(jnp.sum(lv[2:3] * lv[3:4], axis=-1, keepdims=True)) + lambda_init)
    _attn_core(fast_ref, qT_ref, k_ref, vT_ref, m_ref, acc_ref)

    tq = NQ // 2
    for qb in range(Q_BLOCKS):
        c1 = slice(qb * NQ, qb * NQ + tq)
        c2 = slice(qb * NQ + tq, (qb + 1) * NQ)
        o1 = acc_ref[0:LANES, c1] / acc_ref[LANES:LANES + 1, c1]
        o2 = acc_ref[0:LANES, c2] / acc_ref[LANES:LANES + 1, c2]
        d = o1 - lam * o2
        ms = jnp.mean(d * d, axis=0, keepdims=True)
        dn = d * lax.rsqrt(ms + 1e-5)
        o_ref[qb * tq:(qb + 1) * tq, :] = (
            dn.T * (sub_ref[...] * (1.0 - lambda_init))).astype(o_ref.dtype)


_SMEM_SPEC = pl.BlockSpec(memory_space=pltpu.SMEM)
_ATTN_SCRATCH = [pltpu.VMEM((1, NCOLS), F32), pltpu.VMEM((VROWS, NCOLS), F32)]


def _diff_call(fast, lamv, subln, dqT, dk, dvT, B, S, lambda_init):
    T = dk.shape[0]
    tq = Q_BLOCKS * NQ // 2
    nq = S // tq
    kern = functools.partial(_diff_kernel, lambda_init=lambda_init)
    return pl.pallas_call(
        kern, grid=(B, DIFF_HEADS, nq),
        in_specs=[
            _SMEM_SPEC, _resident(lamv.shape), _resident(subln.shape),
            pl.BlockSpec((Q_BLOCKS, None, LANES, NQ), lambda b, h, i: (b * nq + i, h, 0, 0)),
            pl.BlockSpec((S, LANES), lambda b, h, i: (b, h)),
            pl.BlockSpec((S // TM, None, VROWS, TM), lambda b, h, i: (b, h, 0, 0)),
        ],
        out_specs=pl.BlockSpec((tq, LANES), lambda b, h, i: (b * nq + i, h)),
        out_shape=jax.ShapeDtypeStruct((T, BRANCH_WIDTH), BF16),
        scratch_shapes=_ATTN_SCRATCH,
        compiler_params=_cparams(3), name="diff_attn",
    )(fast, lamv, subln, dqT, dk, dvT)


def _gqa_kernel(fast_ref, qT_ref, k_ref, vT_ref, o_ref, m_ref, acc_ref):
    _attn_core(fast_ref, qT_ref, k_ref, vT_ref, m_ref, acc_ref)
    for qb, c in [(qb, c) for qb in range(Q_BLOCKS) for c in range(GQA_Q_HEADS // 2)]:
        g = (2 * c) // GQA_GROUP
        rows = slice(g * HEAD64, (g + 1) * HEAD64)
        parts = []
        for hh in (2 * c, 2 * c + 1):
            cols = slice(qb * NQ + hh * LANES, qb * NQ + (hh + 1) * LANES)
            parts.append(acc_ref[rows, cols] / acc_ref[LANES:LANES + 1, cols])
        o_ref[qb * LANES:(qb + 1) * LANES, c * LANES:(c + 1) * LANES] = (
            jnp.concatenate(parts, axis=0).T.astype(o_ref.dtype))


def _gqa_call(fast, gqT, gk, gvT, B, S):
    T = gk.shape[0]
    tq = Q_BLOCKS * NQ // GQA_Q_HEADS
    nq = S // tq
    return pl.pallas_call(
        _gqa_kernel, grid=(B, nq),
        in_specs=[
            _SMEM_SPEC,
            pl.BlockSpec((Q_BLOCKS, LANES, NQ), lambda b, i: (b * nq + i, 0, 0)),
            pl.BlockSpec((S, LANES), lambda b, i: (b, 0)),
            pl.BlockSpec((S // TM, VROWS, TM), lambda b, i: (b, 0, 0)),
        ],
        out_specs=pl.BlockSpec((tq, BRANCH_WIDTH), lambda b, i: (b * nq + i, 0)),
        out_shape=jax.ShapeDtypeStruct((T, BRANCH_WIDTH), BF16),
        scratch_shapes=_ATTN_SCRATCH,
        compiler_params=_cparams(2), name="gqa_attn",
    )(fast, gqT, gk, gvT)


S_REGIONS = (0, 64, 128)
O_REGIONS = ((192, 208), (224, 240))
MXU_Q_BLOCKS = len(O_REGIONS)
POP_LAG = 2
PV_LAG = 3
P_RING = 6
TRIP_SLOTS = 30


def _gqa_mxu_kernel(qT_ref, k_ref, vT_ref, o_ref, p_ref, l_ref):
    per_block = 2 * (k_ref.shape[0] // MXU_DIM)
    n_slots = MXU_Q_BLOCKS * per_block
    head = per_block % TRIP_SLOTS
    assert head > PV_LAG and TRIP_SLOTS % P_RING == 0

    @pl.when((pl.program_id(0) == 0) & (pl.program_id(1) == 0))
    def _():
        for mxu in range(2):
            for region in S_REGIONS:
                pltpu.matmul_pop(region, (MXU_DIM, MXU_DIM), F32, mxu)
            for region in sum(O_REGIONS, ()):
                pltpu.matmul_pop(region, (HEAD64, MXU_DIM), F32, mxu)

    l_ref[...] = jnp.zeros(l_ref.shape, F32)

    def q_tile(mxu, n, blk):
        c = 2 * mxu + n % 2
        return qT_ref[blk, :, c * MXU_DIM:(c + 1) * MXU_DIM]

    def o_region(n):
        return O_REGIONS[n // per_block][n % 2]

    def l_cols(mxu, n):
        c0 = (n // per_block) * NQ + (2 * mxu + n % 2) * MXU_DIM
        return slice(c0, c0 + MXU_DIM)

    def slot(n, unit, unit_v, next_blk):
        start = unit * MXU_DIM
        if not isinstance(unit, int):
            start = pl.multiple_of(start, MXU_DIM)
        k = k_ref[pl.ds(start, MXU_DIM), :]
        for mxu in range(2):
            if n >= PV_LAG:
                pltpu.matmul_push_rhs(p_ref[mxu, (n - PV_LAG) % P_RING], staging_register=1,
                                      mxu_index=mxu)
            pltpu.matmul_acc_lhs(S_REGIONS[n % 3], k, mxu_index=mxu, load_staged_rhs=0)
            pltpu.matmul_push_rhs(q_tile(mxu, n + 1, next_blk), staging_register=0, mxu_index=mxu)
            if n >= PV_LAG:
                pltpu.matmul_acc_lhs(o_region(n - PV_LAG), vT_ref[unit_v, mxu],
                                     mxu_index=mxu, load_staged_rhs=1)
        for mxu in range(2):
            if n >= POP_LAG:
                pop_scores(mxu, n - POP_LAG)

    def pop_scores(mxu, n):
        s = pltpu.matmul_pop(S_REGIONS[n % 3], (MXU_DIM, MXU_DIM), F32, mxu)
        pf = jnp.exp2(s)
        l_ref[:, l_cols(mxu, n)] += jnp.sum(pf, axis=0, keepdims=True)
        p_ref[mxu, n % P_RING] = pf.astype(BF16)

    def values_only(mxu, n, unit_v):
        pltpu.matmul_push_rhs(p_ref[mxu, n % P_RING], staging_register=1, mxu_index=mxu)
        pltpu.matmul_acc_lhs(o_region(n), vT_ref[unit_v, mxu], mxu_index=mxu, load_staged_rhs=1)

    def unit_of(n):
        return (n % per_block) // 2

    def block_of(n):
        return jnp.minimum(n // per_block, MXU_Q_BLOCKS - 1)

    for mxu in range(2):
        pltpu.matmul_push_rhs(q_tile(mxu, 0, 0), staging_register=0, mxu_index=mxu)
    for blk in range(MXU_Q_BLOCKS):
        first = blk * per_block
        for n in range(first, first + head):
            slot(n, unit_of(n), unit_of(max(n - PV_LAG, 0)), blk)

        def trip(t, carry, first=first + head):
            for i in range(TRIP_SLOTS):
                base = first + TRIP_SLOTS * t + i
                slot(first + i, unit_of(base), unit_of(base - PV_LAG), block_of(base + 1))
            return carry

        lax.fori_loop(0, per_block // TRIP_SLOTS, trip, 0)

    for mxu in range(2):
        pltpu.matmul_acc_lhs(S_REGIONS[n_slots % 3], jnp.zeros((16, MXU_DIM), BF16),
                             mxu_index=mxu, load_staged_rhs=0)
    for n in range(n_slots - POP_LAG, n_slots):
        for mxu in range(2):
            pop_scores(mxu, n)
    for mxu in range(2):
        pltpu.matmul_pop(S_REGIONS[n_slots % 3], (16, MXU_DIM), F32, mxu)
    for n in range(n_slots - PV_LAG, n_slots):
        for mxu in range(2):
            values_only(mxu, n, unit_of(n))

    tq = NQ // GQA_Q_HEADS
    for blk in range(MXU_Q_BLOCKS):
        for mxu in range(2):
            for cc in range(2):
                n = blk * per_block + cc
                c = 2 * mxu + cc
                o = pltpu.matmul_pop(o_region(n), (HEAD64, MXU_DIM), F32, mxu)
                o = o / l_ref[:, l_cols(mxu, n)]
                pair = jnp.concatenate([o[:, 0:LANES], o[:, LANES:2 * LANES]], axis=0)
                o_ref[blk * tq:(blk + 1) * tq, c * LANES:(c + 1) * LANES] = pair.T.astype(o_ref.dtype)


def _gqa_mxu_call(gqT, gk, gvT2, B, S):
    T = gk.shape[0]
    tq = MXU_Q_BLOCKS * NQ // GQA_Q_HEADS
    nq = S // tq
    return pl.pallas_call(
        _gqa_mxu_kernel, grid=(B, nq),
        in_specs=[
            pl.BlockSpec((MXU_Q_BLOCKS, MXU_DIM, NQ), lambda b, i: (b * nq + i, 0, 0)),
            pl.BlockSpec((S, MXU_DIM), lambda b, i: (b, 0)),
            pl.BlockSpec((S // MXU_DIM, 2, HEAD64, MXU_DIM), lambda b, i: (b, 0, 0, 0)),
        ],
        out_specs=pl.BlockSpec((tq, BRANCH_WIDTH), lambda b, i: (b * nq + i, 0)),
        out_shape=jax.ShapeDtypeStruct((T, BRANCH_WIDTH), BF16),
        scratch_shapes=[pltpu.VMEM((2, P_RING, MXU_DIM, MXU_DIM), BF16),
                        pltpu.VMEM((1, MXU_Q_BLOCKS * NQ), F32)],
        compiler_params=_cparams(2), name="gqa_attn_mxu",
    )(gqT, gk, gvT2)


def _ret_kernel(dec_ref, gain_ref, q_ref, kT_ref, v_ref, g_ref, o_ref, sf_ref, sb_ref):
    C = RET_CHUNK
    hd = pl.program_id(1)
    lf = -jnp.exp(jnp.full((C, C), dec_ref[0, hd], F32))
    lb = -jnp.exp(jnp.full((C, C), dec_ref[1, hd], F32))
    ii = lax.broadcasted_iota(jnp.int32, (C, C), 0).astype(F32)
    jj = lax.broadcasted_iota(jnp.int32, (C, C), 1).astype(F32)
    diff = ii - jj
    decay = jnp.where(diff >= 0, jnp.exp(jnp.maximum(diff, 0.0) * lf),
                      jnp.exp(jnp.maximum(-diff, 0.0) * lb))
    xi_f, zeta_f = jnp.exp((ii + 1.0) * lf), jnp.exp((C - 1.0 - ii) * lf)
    xi_b, zeta_b = jnp.exp((C - ii) * lb), jnp.exp(ii * lb)
    dec_f, dec_b = jnp.exp(C * lf), jnp.exp(C * lb)
    n_tiles = kT_ref.shape[0]
    per_tile = TM // C

    def rows(t, c):
        return pl.ds(pl.multiple_of(t * TM + c * C, C), C)

    def chunk_kv(t, c, zeta):
        vz = (v_ref[rows(t, c), :].astype(F32) * zeta).astype(BF16)
        return jnp.dot(kT_ref[t, :, c * C:(c + 1) * C], vz, preferred_element_type=F32)

    def scans(i, states):
        st_f, st_b = states
        fwd = [(i * RET_SCAN_TILES + t, c) for t in range(RET_SCAN_TILES) for c in range(per_tile)]
        bwd = [(n_tiles - 1 - t, per_tile - 1 - c) for t, c in fwd]
        kv_f = [chunk_kv(t, c, zeta_f) for t, c in fwd]
        kv_b = [chunk_kv(t, c, zeta_b) for t, c in bwd]
        for (tf, cf), kf, (tb, cb), kb in zip(fwd, kv_f, bwd, kv_b):
            sf_ref[tf * per_tile + cf] = st_f.astype(BF16)
            st_f = st_f * dec_f + kf
            sb_ref[tb * per_tile + cb] = st_b.astype(BF16)
            st_b = st_b * dec_b + kb
        return st_f, st_b

    zero = jnp.zeros((C, C), F32)
    lax.fori_loop(0, n_tiles // RET_SCAN_TILES, scans, (zero, zero))

    def outputs(tt, carry):
        chunks = [(tt * RET_OUT_TILES + i, c) for i in range(RET_OUT_TILES) for c in range(per_tile)]
        rs = [rows(t, c) for t, c in chunks]
        qs = [q_ref[r, :] for r in rs]
        a = [jnp.dot(q, kT_ref[t, :, c * C:(c + 1) * C], preferred_element_type=F32)
             for q, (t, c) in zip(qs, chunks)]
        cross = [jnp.dot(q, sf_ref[t * per_tile + c], preferred_element_type=F32) * xi_f
                 + jnp.dot(q, sb_ref[t * per_tile + c], preferred_element_type=F32) * xi_b
                 for q, (t, c) in zip(qs, chunks)]
        a = [(x * decay).astype(BF16) for x in a]
        o = [jnp.dot(x, v_ref[r, :], preferred_element_type=F32) + y
             for x, r, y in zip(a, rs, cross)]
        mu = [jnp.mean(x, axis=-1, keepdims=True) for x in o]
        oc = [x - m for x, m in zip(o, mu)]
        var = [jnp.mean(x * x, axis=-1, keepdims=True) for x in oc]
        for x, s2, r in zip(oc, var, rs):
            y = x * lax.rsqrt(s2 + 1e-5) * gain_ref[...]
            g = g_ref[r, :].astype(F32)
            o_ref[r, :] = (g / (1.0 + jnp.exp(-g)) * y).astype(o_ref.dtype)
        return carry

    lax.fori_loop(0, n_tiles // RET_OUT_TILES, outputs, 0)


def _ret_call(dec, gain, rq, rkT, rv, rg, B, S):
    T = rq.shape[0]
    blk = pl.BlockSpec((S, LANES), lambda b, h: (b, h))
    n_chunks = S // RET_CHUNK
    return pl.pallas_call(
        _ret_kernel, grid=(B, RET_HEADS),
        in_specs=[pl.BlockSpec(memory_space=pltpu.SMEM),
                  pl.BlockSpec((1, LANES), lambda b, h: (0, h)), blk,
                  pl.BlockSpec((S // TM, None, LANES, TM), lambda b, h: (b, h, 0, 0)), blk, blk],
        out_specs=blk,
        out_shape=jax.ShapeDtypeStruct((T, BRANCH_WIDTH), BF16),
        scratch_shapes=[pltpu.VMEM((n_chunks, RET_CHUNK, LANES), BF16),
                        pltpu.VMEM((n_chunks, RET_CHUNK, LANES), BF16)],
        compiler_params=_cparams(2), name="retention",
    )(dec, gain, rq, rkT, rv, rg)


def _merge_kernel(h_ref, u_ref, a_ref, r_ref, c_ref, wg_ref, wb_ref, wo_ref, mn_ref,
                  hn_ref, u2_ref):
    u = u_ref[...]
    merged = None
    for n, br in enumerate((a_ref, r_ref, c_ref)):
        logits = jnp.dot(u, wg_ref[:, n * D_MODEL:(n + 1) * D_MODEL], preferred_element_type=F32)
        gate = 1.0 / (1.0 + jnp.exp(-logits))
        term = gate * jnp.dot(br[...], wb_ref[n], preferred_element_type=F32)
        merged = term if merged is None else merged + term
    merged = merged.astype(BF16)
    halves = [slice(0, TM // 2), slice(TM // 2, TM)]
    hns = [h_ref[r, :] + jnp.dot(merged[r], wo_ref[...], preferred_element_type=F32) for r in halves]
    for r, hn in zip(halves, hns):
        hn_ref[r, :] = hn
        ms = jnp.mean(hn * hn, axis=-1, keepdims=True)
        u2_ref[r, :] = (hn * lax.rsqrt(ms + NORM_EPS) * mn_ref[...]).astype(BF16)


def _merge_call(h, u, a, r, c, wg, wb, wo, mlp_norm):
    T = h.shape[0]
    row = lambda w: pl.BlockSpec((TM, w), lambda i: (i, 0))
    return pl.pallas_call(
        _merge_kernel, grid=(T // TM,),
        in_specs=[row(D_MODEL), row(D_MODEL), row(BRANCH_WIDTH), row(BRANCH_WIDTH), row(BRANCH_WIDTH),
                  _resident(wg.shape), _resident(wb.shape), _resident(wo.shape),
                  _resident((1, D_MODEL))],
        out_specs=(row(D_MODEL), row(D_MODEL)),
        out_shape=(jax.ShapeDtypeStruct((T, D_MODEL), F32),
                   jax.ShapeDtypeStruct((T, D_MODEL), BF16)),
        compiler_params=_cparams(1), name="merge",
    )(h, u, a, r, c, wg, wb, wo, mlp_norm)


def _mlp_kernel(h_ref, u_ref, w1_ref, w2_ref, o_ref):
    u = u_ref[...]
    acc = h_ref[...]
    for c in range(D_FF // D_MODEL):
        cs = slice(c * D_MODEL, (c + 1) * D_MODEL)
        m = jnp.maximum(jnp.dot(u, w1_ref[:, cs], preferred_element_type=F32), 0.0)
        acc = acc + jnp.dot((m * m).astype(BF16), w2_ref[cs, :], preferred_element_type=F32)
    o_ref[...] = acc


def _mlp_call(h, u2, w1, w2):
    T = h.shape[0]
    row = pl.BlockSpec((TM, D_MODEL), lambda i: (i, 0))
    return pl.pallas_call(
        _mlp_kernel, grid=(T // TM,),
        in_specs=[row, row, _resident(w1.shape), _resident(w2.shape)],
        out_specs=row,
        out_shape=jax.ShapeDtypeStruct((T, D_MODEL), F32),
        compiler_params=_cparams(1), name="mlp",
    )(h, u2, w1, w2)


def _rope_tables(S):
    pos = jnp.arange(S, dtype=F32)
    lane = jnp.arange(LANES)
    l64 = lane % HEAD64

    inv = 500000.0 ** (-jnp.arange(0, 16, 2, dtype=F32) / 16)
    ang = pos[:, None] * inv[None, :]
    cos, sin = jnp.cos(ang)[:, l64 % 8], jnp.sin(ang)[:, l64 % 8]
    cd = jnp.where(l64 < 16, cos, 1.0)
    ad = jnp.where(l64 < 8, -sin, 0.0)
    bd = jnp.where((l64 >= 8) & (l64 < 16), sin, 0.0)

    inv = 10000.0 ** (-jnp.arange(0, 32, 2, dtype=F32) / 32)
    row_pos = jnp.floor(pos / GRID_W)
    col_pos = pos - row_pos * GRID_W
    p2 = jnp.where((l64 < 32)[None, :], row_pos[:, None], col_pos[:, None])
    ang = p2 * inv[l64 % 16][None, :]
    cg = jnp.cos(ang)
    ag = jnp.where(l64 % 32 < 16, -jnp.sin(ang), 0.0)
    bg = jnp.where(l64 % 32 >= 16, jnp.sin(ang), 0.0)

    inv = 10000.0 ** (-jnp.arange(0, 128, 2, dtype=F32) / 128)
    ang = pos[:, None] * inv[l64][None, :]
    cr = jnp.cos(ang)
    sr = jnp.where(lane < LANES // 2, -jnp.sin(ang), jnp.sin(ang))
    return (cd, ad, bd, cg, ag, bg, cr, sr)


def _fast_flag(q_gain, k_gain):
    bound = (HEAD64 ** 0.5 * LOG2E * 1.02) * jnp.max(jnp.abs(q_gain)) * jnp.max(jnp.abs(k_gain))
    return (bound <= FAST_SOFTMAX_BOUND).astype(jnp.int32).reshape(1)


def kernel(x, attn_norm, w_in, diff_q_norm, diff_k_norm, diff_lam_q1, diff_lam_k1, diff_lam_q2, diff_lam_k2, diff_subln, ret_decay_fwd, ret_decay_bwd, ret_group_norm, gqa_q_norm, gqa_k_norm, w_branch, w_out, mlp_norm, w_mlp_in, w_mlp_out):
    B, S, D = x.shape
    T = B * S
    tabs = _rope_tables(S)
    blk = jnp.arange(2 * LANES) // HEAD64
    gmat = jnp.where(blk[:, None] == blk[None, :], 1.0 / HEAD64, 0.0).astype(BF16)
    tile2 = lambda v: jnp.concatenate([v, v]).astype(F32)

    h = x.reshape(T, D)
    for l in range(DEPTH):
        lambda_init = 0.8 - 0.6 * math.exp(-0.3 * l)
        w_attn = w_in[l, :, :ATTN_COLS].astype(BF16)
        w_gate = w_in[l, :, ATTN_COLS:].astype(BF16)
        nrm = jnp.stack([tile2(diff_q_norm[l]), tile2(diff_k_norm[l]),
                         tile2(gqa_q_norm[l]), tile2(gqa_k_norm[l])])
        (u, dqT, dk, dvT, rq, rkT, rv, rg, gqT, gk, gvT, gvT2) = _proj_call(
            h, attn_norm[l].reshape(1, D).astype(F32), w_attn, gmat, nrm, tabs)

        lamv = jnp.stack([diff_lam_q1[l], diff_lam_k1[l], diff_lam_q2[l], diff_lam_k2[l]]).astype(F32)
        a = _diff_call(_fast_flag(diff_q_norm[l], diff_k_norm[l]), lamv,
                       diff_subln[l].reshape(1, LANES).astype(F32), dqT, dk, dvT, B, S, lambda_init)
        dec = jnp.stack([ret_decay_fwd[l], ret_decay_bwd[l]]).astype(F32)
        r = _ret_call(dec, ret_group_norm[l].reshape(1, BRANCH_WIDTH).astype(F32), rq, rkT, rv, rg, B, S)
        gqa_fast = _fast_flag(gqa_q_norm[l], gqa_k_norm[l])
        c = lax.cond(gqa_fast[0] == 1,
                     lambda ops: _gqa_mxu_call(ops[0], ops[1], ops[3], B, S),
                     lambda ops: _gqa_call(gqa_fast, ops[0], ops[1], ops[2], B, S),
                     (gqT, gk, gvT, gvT2))

        h, u2 = _merge_call(h, u, a, r, c, w_gate, w_branch[l].astype(BF16),
                            w_out[l].astype(BF16), mlp_norm[l].reshape(1, D).astype(F32))
        h = _mlp_call(h, u2, w_mlp_in[l].astype(BF16), w_mlp_out[l].astype(BF16))
    return h.reshape(B, S, D)
```

```python
import functools
import math

import jax
import jax.numpy as jnp
from jax import lax
from jax.experimental import pallas as pl
from jax.experimental.pallas import tpu as pltpu

F32 = jnp.float32
BF16 = jnp.bfloat16

D_MODEL = 1024
DEPTH = 2
GRID_W = 64
NORM_EPS = 1e-6
HEAD64 = 64
DIFF_HEADS = 4
RET_HEADS = 4
RET_CHUNK = 128
GQA_Q_HEADS = 8
GQA_GROUP = 4
N_BRANCHES = 3
BRANCH_WIDTH = 512
D_FF = 4 * D_MODEL
LANES = 128
MXU_DIM = 256
ATTN_COLS = 8 * BRANCH_WIDTH + 2 * LANES

TM = 512
TK = 2048
NQ = 1024
Q_BLOCKS = 2
NCOLS = Q_BLOCKS * NQ
NC = 256
KV_UNROLL = 4
RET_SCAN_TILES = 4
RET_OUT_TILES = 8
ONES_ROWS = 16
VROWS = LANES + ONES_ROWS
FAST_SOFTMAX_BOUND = 64.0
VMEM_LIMIT = 56 * 1024 * 1024
LOG2E = 1.4426950408889634


def _cparams(n_axes):
    return pltpu.CompilerParams(dimension_semantics=("arbitrary",) * n_axes,
                                vmem_limit_bytes=VMEM_LIMIT)


def _resident(shape):
    nd = len(shape)
    return pl.BlockSpec(shape, lambda *_: (0,) * nd, pipeline_mode=pl.Buffered(1))


def _proj_kernel(h_ref, an_ref, w_ref, g_ref, nrm_ref,
                 cd_ref, ad_ref, bd_ref, cg_ref, ag_ref, bg_ref, cr_ref, sr_ref,
                 u_ref, dqT_ref, dk_ref, dvT_ref, rq_ref, rkT_ref, rv_ref, rg_ref,
                 gqT_ref, gk_ref, gvT_ref, gvT2_ref):
    x = h_ref[...]
    ms = jnp.mean(x * x, axis=-1, keepdims=True)
    u = (x * lax.rsqrt(ms + NORM_EPS) * an_ref[...]).astype(BF16)
    u_ref[...] = u

    def proj(c0, n):
        return jnp.dot(u, w_ref[:, c0:c0 + n], preferred_element_type=F32)

    gmat = g_ref[...]
    lo = lax.broadcasted_iota(jnp.int32, (1, LANES), 1) < HEAD64
    first_row = lax.broadcasted_iota(jnp.int32, (ONES_ROWS, TM), 0) == 0
    ones_rows = jnp.where(first_row, 1.0, 0.0).astype(BF16)

    def qk_norm(p, gain):
        out = []
        for c0 in range(0, p.shape[1], 2 * LANES):
            y = p[:, c0:c0 + 2 * LANES]
            w = y.shape[1]
            msq = jnp.dot((y * y).astype(BF16), gmat[0:w, 0:w], preferred_element_type=F32)
            y = y * lax.rsqrt(msq + NORM_EPS)
            out += [y[:, i:i + LANES] * gain for i in range(0, w, LANES)]
        return out

    def rope(y, c, a, b, d):
        return y * c + pltpu.roll(y, LANES - d, 1) * a + pltpu.roll(y, d, 1) * b

    def chunks(p):
        return [p[:, i:i + LANES] for i in range(0, p.shape[1], LANES)]

    qscale = HEAD64 ** -0.5 * LOG2E
    kscale = LANES ** -0.5
    cd, ad, bd = cd_ref[...], ad_ref[...], bd_ref[...]
    cg, ag, bg = cg_ref[...], ag_ref[...], bg_ref[...]
    cr, sr = cr_ref[...], sr_ref[...]

    def diff_q(p):
        for hd, y in enumerate(qk_norm(p, nrm_ref[0:1, :])):
            y = rope(y, cd, ad, bd, 8) * qscale
            dqT_ref[hd, :, 0:TM] = jnp.where(lo, y, 0.0).T.astype(BF16)
            dqT_ref[hd, :, TM:2 * TM] = jnp.where(lo, 0.0, y).T.astype(BF16)

    def diff_k(p):
        for hd, y in enumerate(qk_norm(p, nrm_ref[1:2, :])):
            dk_ref[:, hd * LANES:(hd + 1) * LANES] = rope(y, cd, ad, bd, 8).astype(BF16)

    def diff_v(p):
        for hd, y in enumerate(chunks(p)):
            dvT_ref[hd, 0:LANES, :] = y.T.astype(BF16)
            dvT_ref[hd, LANES:VROWS, :] = ones_rows

    def ret_q(p):
        for hd, y in enumerate(chunks(p)):
            rq_ref[:, hd * LANES:(hd + 1) * LANES] = (y * cr + pltpu.roll(y, LANES // 2, 1) * sr).astype(BF16)

    def ret_k(p):
        for hd, y in enumerate(chunks(p)):
            rkT_ref[hd] = ((y * cr + pltpu.roll(y, LANES // 2, 1) * sr) * kscale).T.astype(BF16)

    def ret_v(p):
        rv_ref[...] = p.astype(BF16)

    def ret_g(p):
        rg_ref[...] = p.astype(BF16)

    def gqa_q(p):
        for c, y in enumerate(qk_norm(p, nrm_ref[2:3, :])):
            y = rope(y, cg, ag, bg, 16) * qscale
            ysw = pltpu.roll(y, LANES // 2, 1)
            if c < 2:
                z_even, z_odd = jnp.where(lo, y, 0.0), jnp.where(lo, ysw, 0.0)
            else:
                z_even, z_odd = jnp.where(lo, 0.0, ysw), jnp.where(lo, 0.0, y)
            for hh, z in ((2 * c, z_even), (2 * c + 1, z_odd)):
                zt = z.T.astype(BF16)
                for qb in range(TM // LANES):
                    gqT_ref[qb, 0:LANES, hh * LANES:(hh + 1) * LANES] = zt[:, qb * LANES:(qb + 1) * LANES]
        gqT_ref[:, LANES:MXU_DIM, :] = jnp.zeros((TM // LANES, MXU_DIM - LANES, NQ), BF16)

    def gqa_kv(p):
        (y,) = qk_norm(p[:, 0:LANES], nrm_ref[3:4, :])
        gk_ref[:, 0:LANES] = rope(y, cg, ag, bg, 16).astype(BF16)
        gk_ref[:, LANES:MXU_DIM] = jnp.zeros((TM, MXU_DIM - LANES), BF16)
        vt = p[:, LANES:2 * LANES].T.astype(BF16)
        gvT_ref[0:LANES, :] = vt
        gvT_ref[LANES:VROWS, :] = ones_rows
        for ku in range(TM // MXU_DIM):
            for g in range(2):
                gvT2_ref[ku, g] = vt[g * HEAD64:(g + 1) * HEAD64, ku * MXU_DIM:(ku + 1) * MXU_DIM]

    order = [gqa_q, diff_q, ret_k, diff_v, diff_k, gqa_kv, ret_q, ret_v, ret_g]
    in_w_in = [diff_q, diff_k, diff_v, ret_q, ret_k, ret_v, ret_g, gqa_q, gqa_kv]
    width = {f: BRANCH_WIDTH for f in in_w_in}
    width[gqa_kv] = 2 * LANES
    start = {f: sum(width[g] for g in in_w_in[:i]) for i, f in enumerate(in_w_in)}
    p_next = proj(start[order[0]], width[order[0]])
    for i, epilogue in enumerate(order):
        p_cur = p_next
        if i + 1 < len(order):
            p_next = proj(start[order[i + 1]], width[order[i + 1]])
        epilogue(p_cur)


def _proj_call(h, attn_norm, w_attn, gmat, nrm, tabs):
    T = h.shape[0]
    S = tabs[0].shape[0]
    nt = T // TM
    tab_spec = pl.BlockSpec((TM, LANES), lambda i: (i % (S // TM), 0))
    row = lambda w: pl.BlockSpec((TM, w), lambda i: (i, 0))
    out_shape = (
        jax.ShapeDtypeStruct((T, D_MODEL), BF16),
        jax.ShapeDtypeStruct((nt, DIFF_HEADS, LANES, 2 * TM), BF16),
        jax.ShapeDtypeStruct((T, BRANCH_WIDTH), BF16),
        jax.ShapeDtypeStruct((nt, DIFF_HEADS, VROWS, TM), BF16),
        jax.ShapeDtypeStruct((T, BRANCH_WIDTH), BF16),
        jax.ShapeDtypeStruct((nt, RET_HEADS, LANES, TM), BF16),
        jax.ShapeDtypeStruct((T, BRANCH_WIDTH), BF16),
        jax.ShapeDtypeStruct((T, BRANCH_WIDTH), BF16),
        jax.ShapeDtypeStruct((T // LANES, MXU_DIM, NQ), BF16),
        jax.ShapeDtypeStruct((T, MXU_DIM), BF16),
        jax.ShapeDtypeStruct((nt, VROWS, TM), BF16),
        jax.ShapeDtypeStruct((T // MXU_DIM, 2, HEAD64, MXU_DIM), BF16),
    )
    out_specs = (
        row(D_MODEL),
        pl.BlockSpec((None, DIFF_HEADS, LANES, 2 * TM), lambda i: (i, 0, 0, 0)),
        row(BRANCH_WIDTH),
        pl.BlockSpec((None, DIFF_HEADS, VROWS, TM), lambda i: (i, 0, 0, 0)),
        row(BRANCH_WIDTH),
        pl.BlockSpec((None, RET_HEADS, LANES, TM), lambda i: (i, 0, 0, 0)),
        row(BRANCH_WIDTH), row(BRANCH_WIDTH),
        pl.BlockSpec((TM // LANES, MXU_DIM, NQ), lambda i: (i, 0, 0)),
        row(MXU_DIM),
        pl.BlockSpec((None, VROWS, TM), lambda i: (i, 0, 0)),
        pl.BlockSpec((TM // MXU_DIM, 2, HEAD64, MXU_DIM), lambda i: (i, 0, 0, 0)),
    )
    in_specs = [row(D_MODEL), _resident((1, D_MODEL)), _resident(w_attn.shape),
                _resident(gmat.shape), _resident(nrm.shape)] + [tab_spec] * 8
    return pl.pallas_call(
        _proj_kernel, grid=(nt,), in_specs=in_specs, out_specs=out_specs,
        out_shape=out_shape, compiler_params=_cparams(1), name="proj",
    )(h, attn_norm, w_attn, gmat, nrm, *tabs)


def _attn_core(fast_ref, qT_ref, k_ref, vT_ref, m_ref, acc_ref):
    n_steps = k_ref.shape[0] // TK
    n_trips = n_steps // KV_UNROLL
    qT = jnp.concatenate([qT_ref[i] for i in range(Q_BLOCKS)], axis=-1)

    def kv(j):
        k = k_ref[pl.ds(pl.multiple_of(j * TK, TK), TK), :]
        per = TK // TM
        vT = jnp.concatenate([vT_ref[j * per + i] for i in range(per)], axis=-1)
        return k, vT

    @pl.when(fast_ref[0] == 1)
    def _():
        def body(jj, carry):
            pv = lsum = None
            for u in range(KV_UNROLL):
                k, vT = kv(jj * KV_UNROLL + u)
                s = jnp.dot(k, qT, preferred_element_type=F32)
                pf = jnp.exp2(s)
                ls = jnp.sum(pf, axis=0, keepdims=True)
                t = jnp.dot(vT[0:LANES], pf.astype(BF16), preferred_element_type=F32)
                pv, lsum = (t, ls) if pv is None else (pv + t, lsum + ls)
            if n_trips == 1:
                acc_ref[0:LANES, :] = pv
                acc_ref[LANES:LANES + 1, :] = lsum
            else:
                acc_ref[0:LANES, :] += pv
                acc_ref[LANES:LANES + 1, :] += lsum
            return carry

        if n_trips > 1:
            acc_ref[...] = jnp.zeros(acc_ref.shape, F32)
        lax.fori_loop(0, n_trips, body, 0)

    @pl.when(fast_ref[0] == 0)
    def _():
        acc_ref[...] = jnp.zeros(acc_ref.shape, F32)
        m_ref[...] = jnp.full(m_ref.shape, -1e30, F32)

        def body(j, carry):
            k, vT = kv(j)
            for c in range(NCOLS // NC):
                cs = slice(c * NC, (c + 1) * NC)
                s = jnp.dot(k, qT[:, cs], preferred_element_type=F32)
                m_old = m_ref[:, cs]
                m_new = jnp.maximum(m_old, jnp.max(s, axis=0, keepdims=True))
                alpha = jnp.exp2(m_old - m_new)
                p = jnp.exp2(s - m_new).astype(BF16)
                pv = jnp.dot(vT, p, preferred_element_type=F32)
                acc_ref[:, cs] = acc_ref[:, cs] * alpha + pv
                m_ref[:, cs] = m_new
            return carry

        lax.fori_loop(0, n_steps, body, 0)


def _diff_kernel(fast_ref, lam_ref, sub_ref, qT_ref, k_ref, vT_ref, o_ref, m_ref, acc_ref, *,
                 lambda_init):
    lv = lam_ref[...]
    lam = (jnp.exp(jnp.sum(lv[0:1] * lv[1:2], axis=-1, keepdims=True))
           - jnp.exp(jnp.sum(lv[2:3] * lv[3:4], axis=-1, keepdims=True)) + lambda_init)
    _attn_core(fast_ref, qT_ref, k_ref, vT_ref, m_ref, acc_ref)

    tq = NQ // 2
    for qb in range(Q_BLOCKS):
        c1 = slice(qb * NQ, qb * NQ + tq)
        c2 = slice(qb * NQ + tq, (qb + 1) * NQ)
        o1 = acc_ref[0:LANES, c1] / acc_ref[LANES:LANES + 1, c1]
        o2 = acc_ref[0:LANES, c2] / acc_ref[LANES:LANES + 1, c2]
        d = o1 - lam * o2
        ms = jnp.mean(d * d, axis=0, keepdims=True)
        dn = d * lax.rsqrt(ms + 1e-5)
        o_ref[qb * tq:(qb + 1) * tq, :] = (
            dn.T * (sub_ref[...] * (1.0 - lambda_init))).astype(o_ref.dtype)


_SMEM_SPEC = pl.BlockSpec(memory_space=pltpu.SMEM)
_ATTN_SCRATCH = [pltpu.VMEM((1, NCOLS), F32), pltpu.VMEM((VROWS, NCOLS), F32)]


def _diff_call(fast, lamv, subln, dqT, dk, dvT, B, S, lambda_init):
    T = dk.shape[0]
    tq = Q_BLOCKS * NQ // 2
    nq = S // tq
    kern = functools.partial(_diff_kernel, lambda_init=lambda_init)
    return pl.pallas_call(
        kern, grid=(B, DIFF_HEADS, nq),
        in_specs=[
            _SMEM_SPEC, _resident(lamv.shape), _resident(subln.shape),
            pl.BlockSpec((Q_BLOCKS, None, LANES, NQ), lambda b, h, i: (b * nq + i, h, 0, 0)),
            pl.BlockSpec((S, LANES), lambda b, h, i: (b, h)),
            pl.BlockSpec((S // TM, None, VROWS, TM), lambda b, h, i: (b, h, 0, 0)),
        ],
        out_specs=pl.BlockSpec((tq, LANES), lambda b, h, i: (b * nq + i, h)),
        out_shape=jax.ShapeDtypeStruct((T, BRANCH_WIDTH), BF16),
        scratch_shapes=_ATTN_SCRATCH,
        compiler_params=_cparams(3), name="diff_attn",
    )(fast, lamv, subln, dqT, dk, dvT)


def _gqa_kernel(fast_ref, qT_ref, k_ref, vT_ref, o_ref, m_ref, acc_ref):
    _attn_core(fast_ref, qT_ref, k_ref, vT_ref, m_ref, acc_ref)
    for qb, c in [(qb, c) for qb in range(Q_BLOCKS) for c in range(GQA_Q_HEADS // 2)]:
        g = (2 * c) // GQA_GROUP
        rows = slice(g * HEAD64, (g + 1) * HEAD64)
        parts = []
        for hh in (2 * c, 2 * c + 1):
            cols = slice(qb * NQ + hh * LANES, qb * NQ + (hh + 1) * LANES)
            parts.append(acc_ref[rows, cols] / acc_ref[LANES:LANES + 1, cols])
        o_ref[qb * LANES:(qb + 1) * LANES, c * LANES:(c + 1) * LANES] = (
            jnp.concatenate(parts, axis=0).T.astype(o_ref.dtype))


def _gqa_call(fast, gqT, gk, gvT, B, S):
    T = gk.shape[0]
    tq = Q_BLOCKS * NQ // GQA_Q_HEADS
    nq = S // tq
    return pl.pallas_call(
        _gqa_kernel, grid=(B, nq),
        in_specs=[
            _SMEM_SPEC,
            pl.BlockSpec((Q_BLOCKS, LANES, NQ), lambda b, i: (b * nq + i, 0, 0)),
            pl.BlockSpec((S, LANES), lambda b, i: (b, 0)),
            pl.BlockSpec((S // TM, VROWS, TM), lambda b, i: (b, 0, 0)),
        ],
        out_specs=pl.BlockSpec((tq, BRANCH_WIDTH), lambda b, i: (b * nq + i, 0)),
        out_shape=jax.ShapeDtypeStruct((T, BRANCH_WIDTH), BF16),
        scratch_shapes=_ATTN_SCRATCH,
        compiler_params=_cparams(2), name="gqa_attn",
    )(fast, gqT, gk, gvT)


S_REGIONS = (0, 64, 128)
O_REGIONS = ((192, 208), (224, 240))
MXU_Q_BLOCKS = len(O_REGIONS)
POP_LAG = 2
PV_LAG = 3
P_RING = 6
TRIP_SLOTS = 30


def _gqa_mxu_kernel(qT_ref, k_ref, vT_ref, o_ref, p_ref, l_ref):
    per_block = 2 * (k_ref.shape[0] // MXU_DIM)
    n_slots = MXU_Q_BLOCKS * per_block
    head = per_block % TRIP_SLOTS
    assert head > PV_LAG and TRIP_SLOTS % P_RING == 0

    @pl.when((pl.program_id(0) == 0) & (pl.program_id(1) == 0))
    def _():
        for mxu in range(2):
            for region in S_REGIONS:
                pltpu.matmul_pop(region, (MXU_DIM, MXU_DIM), F32, mxu)
            for region in sum(O_REGIONS, ()):
                pltpu.matmul_pop(region, (HEAD64, MXU_DIM), F32, mxu)

    l_ref[...] = jnp.zeros(l_ref.shape, F32)

    def q_tile(mxu, n, blk):
        c = 2 * mxu + n % 2
        return qT_ref[blk, :, c * MXU_DIM:(c + 1) * MXU_DIM]

    def o_region(n):
        return O_REGIONS[n // per_block][n % 2]

    def l_cols(mxu, n):
        c0 = (n // per_block) * NQ + (2 * mxu + n % 2) * MXU_DIM
        return slice(c0, c0 + MXU_DIM)

    def slot(n, unit, unit_v, next_blk):
        start = unit * MXU_DIM
        if not isinstance(unit, int):
            start = pl.multiple_of(start, MXU_DIM)
        k = k_ref[pl.ds(start, MXU_DIM), :]
        for mxu in range(2):
            if n >= PV_LAG:
                pltpu.matmul_push_rhs(p_ref[mxu, (n - PV_LAG) % P_RING], staging_register=1,
                                      mxu_index=mxu)
            pltpu.matmul_acc_lhs(S_REGIONS[n % 3], k, mxu_index=mxu, load_staged_rhs=0)
            pltpu.matmul_push_rhs(q_tile(mxu, n + 1, next_blk), staging_register=0, mxu_index=mxu)
            if n >= PV_LAG:
                pltpu.matmul_acc_lhs(o_region(n - PV_LAG), vT_ref[unit_v, mxu],
                                     mxu_index=mxu, load_staged_rhs=1)
        for mxu in range(2):
            if n >= POP_LAG:
                pop_scores(mxu, n - POP_LAG)

    def pop_scores(mxu, n):
        s = pltpu.matmul_pop(S_REGIONS[n % 3], (MXU_DIM, MXU_DIM), F32, mxu)
        pf = jnp.exp2(s)
        l_ref[:, l_cols(mxu, n)] += jnp.sum(pf, axis=0, keepdims=True)
        p_ref[mxu, n % P_RING] = pf.astype(BF16)

    def values_only(mxu, n, unit_v):
        pltpu.matmul_push_rhs(p_ref[mxu, n % P_RING], staging_register=1, mxu_index=mxu)
        pltpu.matmul_acc_lhs(o_region(n), vT_ref[unit_v, mxu], mxu_index=mxu, load_staged_rhs=1)

    def unit_of(n):
        return (n % per_block) // 2

    def block_of(n):
        return jnp.minimum(n // per_block, MXU_Q_BLOCKS - 1)

    for mxu in range(2):
        pltpu.matmul_push_rhs(q_tile(mxu, 0, 0), staging_register=0, mxu_index=mxu)
    for blk in range(MXU_Q_BLOCKS):
        first = blk * per_block
        for n in range(first, first + head):
            slot(n, unit_of(n), unit_of(max(n - PV_LAG, 0)), blk)

        def trip(t, carry, first=first + head):
            for i in range(TRIP_SLOTS):
                base = first + TRIP_SLOTS * t + i
                slot(first + i, unit_of(base), unit_of(base - PV_LAG), block_of(base + 1))
            return carry

        lax.fori_loop(0, per_block // TRIP_SLOTS, trip, 0)

    for mxu in range(2):
        pltpu.matmul_acc_lhs(S_REGIONS[n_slots % 3], jnp.zeros((16, MXU_DIM), BF16),
                             mxu_index=mxu, load_staged_rhs=0)
    for n in range(n_slots - POP_LAG, n_slots):
        for mxu in range(2):
            pop_scores(mxu, n)
    for mxu in range(2):
        pltpu.matmul_pop(S_REGIONS[n_slots % 3], (16, MXU_DIM), F32, mxu)
    for n in range(n_slots - PV_LAG, n_slots):
        for mxu in range(2):
            values_only(mxu, n, unit_of(n))

    tq = NQ // GQA_Q_HEADS
    for blk in range(MXU_Q_BLOCKS):
        for mxu in range(2):
            for cc in range(2):
                n = blk * per_block + cc
                c = 2 * mxu + cc
                o = pltpu.matmul_pop(o_region(n), (HEAD64, MXU_DIM), F32, mxu)
                o = o / l_ref[:, l_cols(mxu, n)]
                pair = jnp.concatenate([o[:, 0:LANES], o[:, LANES:2 * LANES]], axis=0)
                o_ref[blk * tq:(blk + 1) * tq, c * LANES:(c + 1) * LANES] = pair.T.astype(o_ref.dtype)


def _gqa_mxu_call(gqT, gk, gvT2, B, S):
    T = gk.shape[0]
    tq = MXU_Q_BLOCKS * NQ // GQA_Q_HEADS
    nq = S // tq
    return pl.pallas_call(
        _gqa_mxu_kernel, grid=(B, nq),
        in_specs=[
            pl.BlockSpec((MXU_Q_BLOCKS, MXU_DIM, NQ), lambda b, i: (b * nq + i, 0, 0)),
            pl.BlockSpec((S, MXU_DIM), lambda b, i: (b, 0)),
            pl.BlockSpec((S // MXU_DIM, 2, HEAD64, MXU_DIM), lambda b, i: (b, 0, 0, 0)),
        ],
        out_specs=pl.BlockSpec((tq, BRANCH_WIDTH), lambda b, i: (b * nq + i, 0)),
        out_shape=jax.ShapeDtypeStruct((T, BRANCH_WIDTH), BF16),
        scratch_shapes=[pltpu.VMEM((2, P_RING, MXU_DIM, MXU_DIM), BF16),
                        pltpu.VMEM((1, MXU_Q_BLOCKS * NQ), F32)],
        compiler_params=_cparams(2), name="gqa_attn_mxu",
    )(gqT, gk, gvT2)


def _ret_kernel(dec_ref, gain_ref, q_ref, kT_ref, v_ref, g_ref, o_ref, sf_ref, sb_ref):
    C = RET_CHUNK
    hd = pl.program_id(1)
    lf = -jnp.exp(jnp.full((C, C), dec_ref[0, hd], F32))
    lb = -jnp.exp(jnp.full((C, C), dec_ref[1, hd], F32))
    ii = lax.broadcasted_iota(jnp.int32, (C, C), 0).astype(F32)
    jj = lax.broadcasted_iota(jnp.int32, (C, C), 1).astype(F32)
    diff = ii - jj
    decay = jnp.where(diff >= 0, jnp.exp(jnp.maximum(diff, 0.0) * lf),
                      jnp.exp(jnp.maximum(-diff, 0.0) * lb))
    xi_f, zeta_f = jnp.exp((ii + 1.0) * lf), jnp.exp((C - 1.0 - ii) * lf)
    xi_b, zeta_b = jnp.exp((C - ii) * lb), jnp.exp(ii * lb)
    dec_f, dec_b = jnp.exp(C * lf), jnp.exp(C * lb)
    n_tiles = kT_ref.shape[0]
    per_tile = TM // C

    def rows(t, c):
        return pl.ds(pl.multiple_of(t * TM + c * C, C), C)

    def chunk_kv(t, c, zeta):
        vz = (v_ref[rows(t, c), :].astype(F32) * zeta).astype(BF16)
        return jnp.dot(kT_ref[t, :, c * C:(c + 1) * C], vz, preferred_element_type=F32)

    def scans(i, states):
        st_f, st_b = states
        fwd = [(i * RET_SCAN_TILES + t, c) for t in range(RET_SCAN_TILES) for c in range(per_tile)]
        bwd = [(n_tiles - 1 - t, per_tile - 1 - c) for t, c in fwd]
        kv_f = [chunk_kv(t, c, zeta_f) for t, c in fwd]
        kv_b = [chunk_kv(t, c, zeta_b) for t, c in bwd]
        for (tf, cf), kf, (tb, cb), kb in zip(fwd, kv_f, bwd, kv_b):
            sf_ref[tf * per_tile + cf] = st_f.astype(BF16)
            st_f = st_f * dec_f + kf
            sb_ref[tb * per_tile + cb] = st_b.astype(BF16)
            st_b = st_b * dec_b + kb
        return st_f, st_b

    zero = jnp.zeros((C, C), F32)
    lax.fori_loop(0, n_tiles // RET_SCAN_TILES, scans, (zero, zero))

    def outputs(tt, carry):
        chunks = [(tt * RET_OUT_TILES + i, c) for i in range(RET_OUT_TILES) for c in range(per_tile)]
        rs = [rows(t, c) for t, c in chunks]
        qs = [q_ref[r, :] for r in rs]
        a = [jnp.dot(q, kT_ref[t, :, c * C:(c + 1) * C], preferred_element_type=F32)
             for q, (t, c) in zip(qs, chunks)]
        cross = [jnp.dot(q, sf_ref[t * per_tile + c], preferred_element_type=F32) * xi_f
                 + jnp.dot(q, sb_ref[t * per_tile + c], preferred_element_type=F32) * xi_b
                 for q, (t, c) in zip(qs, chunks)]
        a = [(x * decay).astype(BF16) for x in a]
        o = [jnp.dot(x, v_ref[r, :], preferred_element_type=F32) + y
             for x, r, y in zip(a, rs, cross)]
        mu = [jnp.mean(x, axis=-1, keepdims=True) for x in o]
        oc = [x - m for x, m in zip(o, mu)]
        var = [jnp.mean(x * x, axis=-1, keepdims=True) for x in oc]
        for x, s2, r in zip(oc, var, rs):
            y = x * lax.rsqrt(s2 + 1e-5) * gain_ref[...]
            g = g_ref[r, :].astype(F32)
            o_ref[r, :] = (g / (1.0 + jnp.exp(-g)) * y).astype(o_ref.dtype)
        return carry

    lax.fori_loop(0, n_tiles // RET_OUT_TILES, outputs, 0)


def _ret_call(dec, gain, rq, rkT, rv, rg, B, S):
    T = rq.shape[0]
    blk = pl.BlockSpec((S, LANES), lambda b, h: (b, h))
    n_chunks = S // RET_CHUNK
    return pl.pallas_call(
        _ret_kernel, grid=(B, RET_HEADS),
        in_specs=[pl.BlockSpec(memory_space=pltpu.SMEM),
                  pl.BlockSpec((1, LANES), lambda b, h: (0, h)), blk,
                  pl.BlockSpec((S // TM, None, LANES, TM), lambda b, h: (b, h, 0, 0)), blk, blk],
        out_specs=blk,
        out_shape=jax.ShapeDtypeStruct((T, BRANCH_WIDTH), BF16),
        scratch_shapes=[pltpu.VMEM((n_chunks, RET_CHUNK, LANES), BF16),
                        pltpu.VMEM((n_chunks, RET_CHUNK, LANES), BF16)],
        compiler_params=_cparams(2), name="retention",
    )(dec, gain, rq, rkT, rv, rg)


def _merge_mlp_kernel(h_ref, u_ref, a_ref, r_ref, c_ref, wg_ref, wb_ref, wo_ref, mn_ref,
                      w1_ref, w2_ref, o_ref):
    u = u_ref[...]
    merged = None
    for n, br in enumerate((a_ref, r_ref, c_ref)):
        logits = jnp.dot(u, wg_ref[:, n * D_MODEL:(n + 1) * D_MODEL], preferred_element_type=F32)
        gate = 1.0 / (1.0 + jnp.exp(-logits))
        term = gate * jnp.dot(br[...], wb_ref[n], preferred_element_type=F32)
        merged = term if merged is None else merged + term
    merged = merged.astype(BF16)
    halves = [slice(0, TM // 2), slice(TM // 2, TM)]
    hns = [h_ref[r, :] + jnp.dot(merged[r], wo_ref[...], preferred_element_type=F32) for r in halves]
    u2s = []
    for hn in hns:
        ms = jnp.mean(hn * hn, axis=-1, keepdims=True)
        u2s.append((hn * lax.rsqrt(ms + NORM_EPS) * mn_ref[...]).astype(BF16))
    u2 = jnp.concatenate(u2s, axis=0)
    acc = jnp.concatenate(hns, axis=0)
    for c in range(D_FF // D_MODEL):
        cs = slice(c * D_MODEL, (c + 1) * D_MODEL)
        m = jnp.maximum(jnp.dot(u2, w1_ref[:, cs], preferred_element_type=F32), 0.0)
        acc = acc + jnp.dot((m * m).astype(BF16), w2_ref[cs, :], preferred_element_type=F32)
    o_ref[...] = acc


def _merge_mlp_call(h, u, a, r, c, wg, wb, wo, mlp_norm, w1, w2):
    T = h.shape[0]
    row = lambda w: pl.BlockSpec((TM, w), lambda i: (i, 0))
    return pl.pallas_call(
        _merge_mlp_kernel, grid=(T // TM,),
        in_specs=[row(D_MODEL), row(D_MODEL), row(BRANCH_WIDTH), row(BRANCH_WIDTH), row(BRANCH_WIDTH),
                  _resident(wg.shape), _resident(wb.shape), _resident(wo.shape),
                  _resident((1, D_MODEL)), _resident(w1.shape), _resident(w2.shape)],
        out_specs=row(D_MODEL),
        out_shape=jax.ShapeDtypeStruct((T, D_MODEL), F32),
        compiler_params=_cparams(1), name="merge_mlp",
    )(h, u, a, r, c, wg, wb, wo, mlp_norm, w1, w2)


def _rope_tables(S):
    pos = jnp.arange(S, dtype=F32)
    lane = jnp.arange(LANES)
    l64 = lane % HEAD64

    inv = 500000.0 ** (-jnp.arange(0, 16, 2, dtype=F32) / 16)
    ang = pos[:, None] * inv[None, :]
    cos, sin = jnp.cos(ang)[:, l64 % 8], jnp.sin(ang)[:, l64 % 8]
    cd = jnp.where(l64 < 16, cos, 1.0)
    ad = jnp.where(l64 < 8, -sin, 0.0)
    bd = jnp.where((l64 >= 8) & (l64 < 16), sin, 0.0)

    inv = 10000.0 ** (-jnp.arange(0, 32, 2, dtype=F32) / 32)
    row_pos = jnp.floor(pos / GRID_W)
    col_pos = pos - row_pos * GRID_W
    p2 = jnp.where((l64 < 32)[None, :], row_pos[:, None], col_pos[:, None])
    ang = p2 * inv[l64 % 16][None, :]
    cg = jnp.cos(ang)
    ag = jnp.where(l64 % 32 < 16, -jnp.sin(ang), 0.0)
    bg = jnp.where(l64 % 32 >= 16, jnp.sin(ang), 0.0)

    inv = 10000.0 ** (-jnp.arange(0, 128, 2, dtype=F32) / 128)
    ang = pos[:, None] * inv[l64][None, :]
    cr = jnp.cos(ang)
    sr = jnp.where(lane < LANES // 2, -jnp.sin(ang), jnp.sin(ang))
    return (cd, ad, bd, cg, ag, bg, cr, sr)


def _fast_flag(q_gain, k_gain):
    bound = (HEAD64 ** 0.5 * LOG2E * 1.02) * jnp.max(jnp.abs(q_gain)) * jnp.max(jnp.abs(k_gain))
    return (bound <= FAST_SOFTMAX_BOUND).astype(jnp.int32).reshape(1)


def kernel(x, attn_norm, w_in, diff_q_norm, diff_k_norm, diff_lam_q1, diff_lam_k1, diff_lam_q2, diff_lam_k2, diff_subln, ret_decay_fwd, ret_decay_bwd, ret_group_norm, gqa_q_norm, gqa_k_norm, w_branch, w_out, mlp_norm, w_mlp_in, w_mlp_out):
    B, S, D = x.shape
    T = B * S
    tabs = _rope_tables(S)
    blk = jnp.arange(2 * LANES) // HEAD64
    gmat = jnp.where(blk[:, None] == blk[None, :], 1.0 / HEAD64, 0.0).astype(BF16)
    tile2 = lambda v: jnp.concatenate([v, v]).astype(F32)

    h = x.reshape(T, D)
    for l in range(DEPTH):
        lambda_init = 0.8 - 0.6 * math.exp(-0.3 * l)
        w_attn = w_in[l, :, :ATTN_COLS].astype(BF16)
        w_gate = w_in[l, :, ATTN_COLS:].astype(BF16)
        nrm = jnp.stack([tile2(diff_q_norm[l]), tile2(diff_k_norm[l]),
                         tile2(gqa_q_norm[l]), tile2(gqa_k_norm[l])])
        (u, dqT, dk, dvT, rq, rkT, rv, rg, gqT, gk, gvT, gvT2) = _proj_call(
            h, attn_norm[l].reshape(1, D).astype(F32), w_attn, gmat, nrm, tabs)

        lamv = jnp.stack([diff_lam_q1[l], diff_lam_k1[l], diff_lam_q2[l], diff_lam_k2[l]]).astype(F32)
        a = _diff_call(_fast_flag(diff_q_norm[l], diff_k_norm[l]), lamv,
                       diff_subln[l].reshape(1, LANES).astype(F32), dqT, dk, dvT, B, S, lambda_init)
        dec = jnp.stack([ret_decay_fwd[l], ret_decay_bwd[l]]).astype(F32)
        r = _ret_call(dec, ret_group_norm[l].reshape(1, BRANCH_WIDTH).astype(F32), rq, rkT, rv, rg, B, S)
        gqa_fast = _fast_flag(gqa_q_norm[l], gqa_k_norm[l])
        c = lax.cond(gqa_fast[0] == 1,
                     lambda ops: _gqa_mxu_call(ops[0], ops[1], ops[3], B, S),
                     lambda ops: _gqa_call(gqa_fast, ops[0], ops[1], ops[2], B, S),
                     (gqT, gk, gvT, gvT2))

        h = _merge_mlp_call(h, u, a, r, c, w_gate, w_branch[l].astype(BF16), w_out[l].astype(BF16),
                            mlp_norm[l].reshape(1, D).astype(F32),
                            w_mlp_in[l].astype(BF16), w_mlp_out[l].astype(BF16))
    return h.reshape(B, S, D)
```

```python
import functools
import math

import jax
import jax.numpy as jnp
from jax import lax
from jax.experimental import pallas as pl
from jax.experimental.pallas import tpu as pltpu

F32 = jnp.float32
BF16 = jnp.bfloat16

D_MODEL = 1024
DEPTH = 2
GRID_W = 64
NORM_EPS = 1e-6
HEAD64 = 64
DIFF_HEADS = 4
RET_HEADS = 4
RET_CHUNK = 128
GQA_Q_HEADS = 8
GQA_GROUP = 4
N_BRANCHES = 3
BRANCH_WIDTH = 512
D_FF = 4 * D_MODEL
LANES = 128
MXU_DIM = 256
ATTN_COLS = 8 * BRANCH_WIDTH + 2 * LANES

TM = 512
TK = 1024
NQ = 1024
Q_BLOCKS = 4
NCOLS = Q_BLOCKS * NQ
NC = 256
KV_UNROLL = 8
RET_SCAN_TILES = 4
RET_OUT_TILES = 8
ONES_ROWS = 16
VROWS = LANES + ONES_ROWS
FAST_SOFTMAX_BOUND = 64.0
VMEM_LIMIT = 56 * 1024 * 1024
LOG2E = 1.4426950408889634


def _cparams(n_axes):
    return pltpu.CompilerParams(dimension_semantics=("arbitrary",) * n_axes,
                                vmem_limit_bytes=VMEM_LIMIT)


def _resident(shape):
    nd = len(shape)
    return pl.BlockSpec(shape, lambda *_: (0,) * nd, pipeline_mode=pl.Buffered(1))


def _proj_kernel(h_ref, an_ref, w_ref, g_ref, nrm_ref,
                 cd_ref, ad_ref, bd_ref, cg_ref, ag_ref, bg_ref, cr_ref, sr_ref,
                 u_ref, dqT_ref, dk_ref, dvT_ref, rq_ref, rkT_ref, rv_ref, rg_ref,
                 gqT_ref, gk_ref, gvT_ref, gvT2_ref):
    x = h_ref[...]
    ms = jnp.mean(x * x, axis=-1, keepdims=True)
    u = (x * lax.rsqrt(ms + NORM_EPS) * an_ref[...]).astype(BF16)
    u_ref[...] = u

    def proj(c0, n):
        return jnp.dot(u, w_ref[:, c0:c0 + n], preferred_element_type=F32)

    gmat = g_ref[...]
    lo = lax.broadcasted_iota(jnp.int32, (1, LANES), 1) < HEAD64
    first_row = lax.broadcasted_iota(jnp.int32, (ONES_ROWS, TM), 0) == 0
    ones_rows = jnp.where(first_row, 1.0, 0.0).astype(BF16)

    def qk_norm(p, gain):
        out = []
        for c0 in range(0, p.shape[1], 2 * LANES):
            y = p[:, c0:c0 + 2 * LANES]
            w = y.shape[1]
            msq = jnp.dot((y * y).astype(BF16), gmat[0:w, 0:w], preferred_element_type=F32)
            y = y * lax.rsqrt(msq + NORM_EPS)
            out += [y[:, i:i + LANES] * gain for i in range(0, w, LANES)]
        return out

    def rope(y, c, a, b, d):
        return y * c + pltpu.roll(y, LANES - d, 1) * a + pltpu.roll(y, d, 1) * b

    def chunks(p):
        return [p[:, i:i + LANES] for i in range(0, p.shape[1], LANES)]

    qscale = HEAD64 ** -0.5 * LOG2E
    kscale = LANES ** -0.5
    cd, ad, bd = cd_ref[...], ad_ref[...], bd_ref[...]
    cg, ag, bg = cg_ref[...], ag_ref[...], bg_ref[...]
    cr, sr = cr_ref[...], sr_ref[...]

    def diff_q(p):
        for hd, y in enumerate(qk_norm(p, nrm_ref[0:1, :])):
            y = rope(y, cd, ad, bd, 8) * qscale
            dqT_ref[hd, :, 0:TM] = jnp.where(lo, y, 0.0).T.astype(BF16)
            dqT_ref[hd, :, TM:2 * TM] = jnp.where(lo, 0.0, y).T.astype(BF16)

    def diff_k(p):
        for hd, y in enumerate(qk_norm(p, nrm_ref[1:2, :])):
            dk_ref[:, hd * LANES:(hd + 1) * LANES] = rope(y, cd, ad, bd, 8).astype(BF16)

    def diff_v(p):
        for hd, y in enumerate(chunks(p)):
            dvT_ref[hd, 0:LANES, :] = y.T.astype(BF16)
            dvT_ref[hd, LANES:VROWS, :] = ones_rows

    def ret_q(p):
        for hd, y in enumerate(chunks(p)):
            rq_ref[:, hd * LANES:(hd + 1) * LANES] = (y * cr + pltpu.roll(y, LANES // 2, 1) * sr).astype(BF16)

    def ret_k(p):
        for hd, y in enumerate(chunks(p)):
            rkT_ref[hd] = ((y * cr + pltpu.roll(y, LANES // 2, 1) * sr) * kscale).T.astype(BF16)

    def ret_v(p):
        rv_ref[...] = p.astype(BF16)

    def ret_g(p):
        rg_ref[...] = p.astype(BF16)

    def gqa_q(p):
        for c, y in enumerate(qk_norm(p, nrm_ref[2:3, :])):
            y = rope(y, cg, ag, bg, 16) * qscale
            ysw = pltpu.roll(y, LANES // 2, 1)
            if c < 2:
                z_even, z_odd = jnp.where(lo, y, 0.0), jnp.where(lo, ysw, 0.0)
            else:
                z_even, z_odd = jnp.where(lo, 0.0, ysw), jnp.where(lo, 0.0, y)
            for hh, z in ((2 * c, z_even), (2 * c + 1, z_odd)):
                zt = z.T.astype(BF16)
                for qb in range(TM // LANES):
                    gqT_ref[qb, 0:LANES, hh * LANES:(hh + 1) * LANES] = zt[:, qb * LANES:(qb + 1) * LANES]
        gqT_ref[:, LANES:MXU_DIM, :] = jnp.zeros((TM // LANES, MXU_DIM - LANES, NQ), BF16)

    def gqa_kv(p):
        (y,) = qk_norm(p[:, 0:LANES], nrm_ref[3:4, :])
        gk_ref[:, 0:LANES] = rope(y, cg, ag, bg, 16).astype(BF16)
        gk_ref[:, LANES:MXU_DIM] = jnp.zeros((TM, MXU_DIM - LANES), BF16)
        vt = p[:, LANES:2 * LANES].T.astype(BF16)
        gvT_ref[0:LANES, :] = vt
        gvT_ref[LANES:VROWS, :] = ones_rows
        for ku in range(TM // MXU_DIM):
            for g in range(2):
                gvT2_ref[ku, g] = vt[g * HEAD64:(g + 1) * HEAD64, ku * MXU_DIM:(ku + 1) * MXU_DIM]

    order = [gqa_q, diff_q, ret_k, diff_v, diff_k, gqa_kv, ret_q, ret_v, ret_g]
    in_w_in = [diff_q, diff_k, diff_v, ret_q, ret_k, ret_v, ret_g, gqa_q, gqa_kv]
    width = {f: BRANCH_WIDTH for f in in_w_in}
    width[gqa_kv] = 2 * LANES
    start = {f: sum(width[g] for g in in_w_in[:i]) for i, f in enumerate(in_w_in)}
    p_next = proj(start[order[0]], width[order[0]])
    for i, epilogue in enumerate(order):
        p_cur = p_next
        if i + 1 < len(order):
            p_next = proj(start[order[i + 1]], width[order[i + 1]])
        epilogue(p_cur)


def _proj_call(h, attn_norm, w_attn, gmat, nrm, tabs):
    T = h.shape[0]
    S = tabs[0].shape[0]
    nt = T // TM
    tab_spec = pl.BlockSpec((TM, LANES), lambda i: (i % (S // TM), 0))
    row = lambda w: pl.BlockSpec((TM, w), lambda i: (i, 0))
    out_shape = (
        jax.ShapeDtypeStruct((T, D_MODEL), BF16),
        jax.ShapeDtypeStruct((nt, DIFF_HEADS, LANES, 2 * TM), BF16),
        jax.ShapeDtypeStruct((T, BRANCH_WIDTH), BF16),
        jax.ShapeDtypeStruct((nt, DIFF_HEADS, VROWS, TM), BF16),
        jax.ShapeDtypeStruct((T, BRANCH_WIDTH), BF16),
        jax.ShapeDtypeStruct((nt, RET_HEADS, LANES, TM), BF16),
        jax.ShapeDtypeStruct((T, BRANCH_WIDTH), BF16),
        jax.ShapeDtypeStruct((T, BRANCH_WIDTH), BF16),
        jax.ShapeDtypeStruct((T // LANES, MXU_DIM, NQ), BF16),
        jax.ShapeDtypeStruct((T, MXU_DIM), BF16),
        jax.ShapeDtypeStruct((nt, VROWS, TM), BF16),
        jax.ShapeDtypeStruct((T // MXU_DIM, 2, HEAD64, MXU_DIM), BF16),
    )
    out_specs = (
        row(D_MODEL),
        pl.BlockSpec((None, DIFF_HEADS, LANES, 2 * TM), lambda i: (i, 0, 0, 0)),
        row(BRANCH_WIDTH),
        pl.BlockSpec((None, DIFF_HEADS, VROWS, TM), lambda i: (i, 0, 0, 0)),
        row(BRANCH_WIDTH),
        pl.BlockSpec((None, RET_HEADS, LANES, TM), lambda i: (i, 0, 0, 0)),
        row(BRANCH_WIDTH), row(BRANCH_WIDTH),
        pl.BlockSpec((TM // LANES, MXU_DIM, NQ), lambda i: (i, 0, 0)),
        row(MXU_DIM),
        pl.BlockSpec((None, VROWS, TM), lambda i: (i, 0, 0)),
        pl.BlockSpec((TM // MXU_DIM, 2, HEAD64, MXU_DIM), lambda i: (i, 0, 0, 0)),
    )
    in_specs = [row(D_MODEL), _resident((1, D_MODEL)), _resident(w_attn.shape),
                _resident(gmat.shape), _resident(nrm.shape)] + [tab_spec] * 8
    return pl.pallas_call(
        _proj_kernel, grid=(nt,), in_specs=in_specs, out_specs=out_specs,
        out_shape=out_shape, compiler_params=_cparams(1), name="proj",
    )(h, attn_norm, w_attn, gmat, nrm, *tabs)


def _attn_core(fast_ref, qT_ref, k_ref, vT_ref, m_ref, acc_ref):
    n_steps = k_ref.shape[0] // TK
    n_trips = n_steps // KV_UNROLL
    qT = jnp.concatenate([qT_ref[i] for i in range(Q_BLOCKS)], axis=-1)

    def kv(j):
        k = k_ref[pl.ds(pl.multiple_of(j * TK, TK), TK), :]
        per = TK // TM
        vT = jnp.concatenate([vT_ref[j * per + i] for i in range(per)], axis=-1)
        return k, vT

    @pl.when(fast_ref[0] == 1)
    def _():
        def body(jj, carry):
            pv = lsum = None
            for u in range(KV_UNROLL):
                k, vT = kv(jj * KV_UNROLL + u)
                s = jnp.dot(k, qT, preferred_element_type=F32)
                pf = jnp.exp2(s)
                ls = jnp.sum(pf, axis=0, keepdims=True)
                t = jnp.dot(vT[0:LANES], pf.astype(BF16), preferred_element_type=F32)
                pv, lsum = (t, ls) if pv is None else (pv + t, lsum + ls)
            if n_trips == 1:
                acc_ref[0:LANES, :] = pv
                acc_ref[LANES:LANES + 1, :] = lsum
            else:
                acc_ref[0:LANES, :] += pv
                acc_ref[LANES:LANES + 1, :] += lsum
            return carry

        if n_trips > 1:
            acc_ref[...] = jnp.zeros(acc_ref.shape, F32)
        lax.fori_loop(0, n_trips, body, 0)

    @pl.when(fast_ref[0] == 0)
    def _():
        acc_ref[...] = jnp.zeros(acc_ref.shape, F32)
        m_ref[...] = jnp.full(m_ref.shape, -1e30, F32)

        def body(j, carry):
            k, vT = kv(j)
            for c in range(NCOLS // NC):
                cs = slice(c * NC, (c + 1) * NC)
                s = jnp.dot(k, qT[:, cs], preferred_element_type=F32)
                m_old = m_ref[:, cs]
                m_new = jnp.maximum(m_old, jnp.max(s, axis=0, keepdims=True))
                alpha = jnp.exp2(m_old - m_new)
                p = jnp.exp2(s - m_new).astype(BF16)
                pv = jnp.dot(vT, p, preferred_element_type=F32)
                acc_ref[:, cs] = acc_ref[:, cs] * alpha + pv
                m_ref[:, cs] = m_new
            return carry

        lax.fori_loop(0, n_steps, body, 0)


def _diff_kernel(fast_ref, lam_ref, sub_ref, qT_ref, k_ref, vT_ref, o_ref, m_ref, acc_ref, *,
                 lambda_init):
    lv = lam_ref[...]
    lam = (jnp.exp(jnp.sum(lv[0:1] * lv[1:2], axis=-1, keepdims=True))
           - jnp.exp(jnp.sum(lv[2:3] * lv[3:4], axis=-1, keepdims=True)) + lambda_init)
    _attn_core(fast_ref, qT_ref, k_ref, vT_ref, m_ref, acc_ref)

    tq = NQ // 2
    for qb in range(Q_BLOCKS):
        c1 = slice(qb * NQ, qb * NQ + tq)
        c2 = slice(qb * NQ + tq, (qb + 1) * NQ)
        o1 = acc_ref[0:LANES, c1] / acc_ref[LANES:LANES + 1, c1]
        o2 = acc_ref[0:LANES, c2] / acc_ref[LANES:LANES + 1, c2]
        d = o1 - lam * o2
        ms = jnp.mean(d * d, axis=0, keepdims=True)
        dn = d * lax.rsqrt(ms + 1e-5)
        o_ref[qb * tq:(qb + 1) * tq, :] = (
            dn.T * (sub_ref[...] * (1.0 - lambda_init))).astype(o_ref.dtype)


_SMEM_SPEC = pl.BlockSpec(memory_space=pltpu.SMEM)
_ATTN_SCRATCH = [pltpu.VMEM((1, NCOLS), F32), pltpu.VMEM((VROWS, NCOLS), F32)]


def _diff_call(fast, lamv, subln, dqT, dk, dvT, B, S, lambda_init):
    T = dk.shape[0]
    tq = Q_BLOCKS * NQ // 2
    nq = S // tq
    kern = functools.partial(_diff_kernel, lambda_init=lambda_init)
    return pl.pallas_call(
        kern, grid=(B, DIFF_HEADS, nq),
        in_specs=[
            _SMEM_SPEC, _resident(lamv.shape), _resident(subln.shape),
            pl.BlockSpec((Q_BLOCKS, None, LANES, NQ), lambda b, h, i: (b * nq + i, h, 0, 0)),
            pl.BlockSpec((S, LANES), lambda b, h, i: (b, h)),
            pl.BlockSpec((S // TM, None, VROWS, TM), lambda b, h, i: (b, h, 0, 0)),
        ],
        out_specs=pl.BlockSpec((tq, LANES), lambda b, h, i: (b * nq + i, h)),
        out_shape=jax.ShapeDtypeStruct((T, BRANCH_WIDTH), BF16),
        scratch_shapes=_ATTN_SCRATCH,
        compiler_params=_cparams(3), name="diff_attn",
    )(fast, lamv, subln, dqT, dk, dvT)


def _gqa_kernel(fast_ref, qT_ref, k_ref, vT_ref, o_ref, m_ref, acc_ref):
    _attn_core(fast_ref, qT_ref, k_ref, vT_ref, m_ref, acc_ref)
    for qb, c in [(qb, c) for qb in range(Q_BLOCKS) for c in range(GQA_Q_HEADS // 2)]:
        g = (2 * c) // GQA_GROUP
        rows = slice(g * HEAD64, (g + 1) * HEAD64)
        parts = []
        for hh in (2 * c, 2 * c + 1):
            cols = slice(qb * NQ + hh * LANES, qb * NQ + (hh + 1) * LANES)
            parts.append(acc_ref[rows, cols] / acc_ref[LANES:LANES + 1, cols])
        o_ref[qb * LANES:(qb + 1) * LANES, c * LANES:(c + 1) * LANES] = (
            jnp.concatenate(parts, axis=0).T.astype(o_ref.dtype))


def _gqa_call(fast, gqT, gk, gvT, B, S):
    T = gk.shape[0]
    tq = Q_BLOCKS * NQ // GQA_Q_HEADS
    nq = S // tq
    return pl.pallas_call(
        _gqa_kernel, grid=(B, nq),
        in_specs=[
            _SMEM_SPEC,
            pl.BlockSpec((Q_BLOCKS, LANES, NQ), lambda b, i: (b * nq + i, 0, 0)),
            pl.BlockSpec((S, LANES), lambda b, i: (b, 0)),
            pl.BlockSpec((S // TM, VROWS, TM), lambda b, i: (b, 0, 0)),
        ],
        out_specs=pl.BlockSpec((tq, BRANCH_WIDTH), lambda b, i: (b * nq + i, 0)),
        out_shape=jax.ShapeDtypeStruct((T, BRANCH_WIDTH), BF16),
        scratch_shapes=_ATTN_SCRATCH,
        compiler_params=_cparams(2), name="gqa_attn",
    )(fast, gqT, gk, gvT)


S_REGIONS = (0, 64, 128)
O_REGIONS = ((192, 208), (224, 240))
MXU_Q_BLOCKS = len(O_REGIONS)
POP_LAG = 2
PV_LAG = 3
P_RING = 6
TRIP_SLOTS = 30


def _gqa_mxu_kernel(qT_ref, k_ref, vT_ref, o_ref, p_ref, l_ref):
    per_block = 2 * (k_ref.shape[0] // MXU_DIM)
    n_slots = MXU_Q_BLOCKS * per_block
    head = per_block % TRIP_SLOTS
    assert head > PV_LAG and TRIP_SLOTS % P_RING == 0

    @pl.when((pl.program_id(0) == 0) & (pl.program_id(1) == 0))
    def _():
        for mxu in range(2):
            for region in S_REGIONS:
                pltpu.matmul_pop(region, (MXU_DIM, MXU_DIM), F32, mxu)
            for region in sum(O_REGIONS, ()):
                pltpu.matmul_pop(region, (HEAD64, MXU_DIM), F32, mxu)

    l_ref[...] = jnp.zeros(l_ref.shape, F32)

    def q_tile(mxu, n, blk):
        c = 2 * mxu + n % 2
        return qT_ref[blk, :, c * MXU_DIM:(c + 1) * MXU_DIM]

    def o_region(n):
        return O_REGIONS[n // per_block][n % 2]

    def l_cols(mxu, n):
        c0 = (n // per_block) * NQ + (2 * mxu + n % 2) * MXU_DIM
        return slice(c0, c0 + MXU_DIM)

    def slot(n, unit, unit_v, next_blk):
        start = unit * MXU_DIM
        if not isinstance(unit, int):
            start = pl.multiple_of(start, MXU_DIM)
        k = k_ref[pl.ds(start, MXU_DIM), :]
        for mxu in range(2):
            if n >= PV_LAG:
                pltpu.matmul_push_rhs(p_ref[mxu, (n - PV_LAG) % P_RING], staging_register=1,
                                      mxu_index=mxu)
            pltpu.matmul_acc_lhs(S_REGIONS[n % 3], k, mxu_index=mxu, load_staged_rhs=0)
            pltpu.matmul_push_rhs(q_tile(mxu, n + 1, next_blk), staging_register=0, mxu_index=mxu)
            if n >= PV_LAG:
                pltpu.matmul_acc_lhs(o_region(n - PV_LAG), vT_ref[unit_v, mxu],
                                     mxu_index=mxu, load_staged_rhs=1)
        for mxu in range(2):
            if n >= POP_LAG:
                pop_scores(mxu, n - POP_LAG)

    def pop_scores(mxu, n):
        s = pltpu.matmul_pop(S_REGIONS[n % 3], (MXU_DIM, MXU_DIM), F32, mxu)
        pf = jnp.exp2(s)
        l_ref[:, l_cols(mxu, n)] += jnp.sum(pf, axis=0, keepdims=True)
        p_ref[mxu, n % P_RING] = pf.astype(BF16)

    def values_only(mxu, n, unit_v):
        pltpu.matmul_push_rhs(p_ref[mxu, n % P_RING], staging_register=1, mxu_index=mxu)
        pltpu.matmul_acc_lhs(o_region(n), vT_ref[unit_v, mxu], mxu_index=mxu, load_staged_rhs=1)

    def unit_of(n):
        return (n % per_block) // 2

    def block_of(n):
        return jnp.minimum(n // per_block, MXU_Q_BLOCKS - 1)

    for mxu in range(2):
        pltpu.matmul_push_rhs(q_tile(mxu, 0, 0), staging_register=0, mxu_index=mxu)
    for blk in range(MXU_Q_BLOCKS):
        first = blk * per_block
        for n in range(first, first + head):
            slot(n, unit_of(n), unit_of(max(n - PV_LAG, 0)), blk)

        def trip(t, carry, first=first + head):
            for i in range(TRIP_SLOTS):
                base = first + TRIP_SLOTS * t + i
                slot(first + i, unit_of(base), unit_of(base - PV_LAG), block_of(base + 1))
            return carry

        lax.fori_loop(0, per_block // TRIP_SLOTS, trip, 0)

    for mxu in range(2):
        pltpu.matmul_acc_lhs(S_REGIONS[n_slots % 3], jnp.zeros((16, MXU_DIM), BF16),
                             mxu_index=mxu, load_staged_rhs=0)
    for n in range(n_slots - POP_LAG, n_slots):
        for mxu in range(2):
            pop_scores(mxu, n)
    for mxu in range(2):
        pltpu.matmul_pop(S_REGIONS[n_slots % 3], (16, MXU_DIM), F32, mxu)
    for n in range(n_slots - PV_LAG, n_slots):
        for mxu in range(2):
            values_only(mxu, n, unit_of(n))

    tq = NQ // GQA_Q_HEADS
    for blk in range(MXU_Q_BLOCKS):
        for mxu in range(2):
            for cc in range(2):
                n = blk * per_block + cc
                c = 2 * mxu + cc
                o = pltpu.matmul_pop(o_region(n), (HEAD64, MXU_DIM), F32, mxu)
                o = o / l_ref[:, l_cols(mxu, n)]
                pair = jnp.concatenate([o[:, 0:LANES], o[:, LANES:2 * LANES]], axis=0)
                o_ref[blk * tq:(blk + 1) * tq, c * LANES:(c + 1) * LANES] = pair.T.astype(o_ref.dtype)


def _gqa_mxu_call(gqT, gk, gvT2, B, S):
    T = gk.shape[0]
    tq = MXU_Q_BLOCKS * NQ // GQA_Q_HEADS
    nq = S // tq
    return pl.pallas_call(
        _gqa_mxu_kernel, grid=(B, nq),
        in_specs=[
            pl.BlockSpec((MXU_Q_BLOCKS, MXU_DIM, NQ), lambda b, i: (b * nq + i, 0, 0)),
            pl.BlockSpec((S, MXU_DIM), lambda b, i: (b, 0)),
            pl.BlockSpec((S // MXU_DIM, 2, HEAD64, MXU_DIM), lambda b, i: (b, 0, 0, 0)),
        ],
        out_specs=pl.BlockSpec((tq, BRANCH_WIDTH), lambda b, i: (b * nq + i, 0)),
        out_shape=jax.ShapeDtypeStruct((T, BRANCH_WIDTH), BF16),
        scratch_shapes=[pltpu.VMEM((2, P_RING, MXU_DIM, MXU_DIM), BF16),
                        pltpu.VMEM((1, MXU_Q_BLOCKS * NQ), F32)],
        compiler_params=_cparams(2), name="gqa_attn_mxu",
    )(gqT, gk, gvT2)


def _ret_kernel(dec_ref, gain_ref, q_ref, kT_ref, v_ref, g_ref, o_ref, sf_ref, sb_ref):
    C = RET_CHUNK
    hd = pl.program_id(1)
    lf = -jnp.exp(jnp.full((C, C), dec_ref[0, hd], F32))
    lb = -jnp.exp(jnp.full((C, C), dec_ref[1, hd], F32))
    ii = lax.broadcasted_iota(jnp.int32, (C, C), 0).astype(F32)
    jj = lax.broadcasted_iota(jnp.int32, (C, C), 1).astype(F32)
    diff = ii - jj
    decay = jnp.where(diff >= 0, jnp.exp(jnp.maximum(diff, 0.0) * lf),
                      jnp.exp(jnp.maximum(-diff, 0.0) * lb))
    xi_f, zeta_f = jnp.exp((ii + 1.0) * lf), jnp.exp((C - 1.0 - ii) * lf)
    xi_b, zeta_b = jnp.exp((C - ii) * lb), jnp.exp(ii * lb)
    dec_f, dec_b = jnp.exp(C * lf), jnp.exp(C * lb)
    n_tiles = kT_ref.shape[0]
    per_tile = TM // C

    def rows(t, c):
        return pl.ds(pl.multiple_of(t * TM + c * C, C), C)

    def chunk_kv(t, c, zeta):
        vz = (v_ref[rows(t, c), :].astype(F32) * zeta).astype(BF16)
        return jnp.dot(kT_ref[t, :, c * C:(c + 1) * C], vz, preferred_element_type=F32)

    def scans(i, states):
        st_f, st_b = states
        fwd = [(i * RET_SCAN_TILES + t, c) for t in range(RET_SCAN_TILES) for c in range(per_tile)]
        bwd = [(n_tiles - 1 - t, per_tile - 1 - c) for t, c in fwd]
        kv_f = [chunk_kv(t, c, zeta_f) for t, c in fwd]
        kv_b = [chunk_kv(t, c, zeta_b) for t, c in bwd]
        for (tf, cf), kf, (tb, cb), kb in zip(fwd, kv_f, bwd, kv_b):
            sf_ref[tf * per_tile + cf] = st_f.astype(BF16)
            st_f = st_f * dec_f + kf
            sb_ref[tb * per_tile + cb] = st_b.astype(BF16)
            st_b = st_b * dec_b + kb
        return st_f, st_b

    zero = jnp.zeros((C, C), F32)
    lax.fori_loop(0, n_tiles // RET_SCAN_TILES, scans, (zero, zero))

    def outputs(tt, carry):
        chunks = [(tt * RET_OUT_TILES + i, c) for i in range(RET_OUT_TILES) for c in range(per_tile)]
        rs = [rows(t, c) for t, c in chunks]
        qs = [q_ref[r, :] for r in rs]
        a = [jnp.dot(q, kT_ref[t, :, c * C:(c + 1) * C], preferred_element_type=F32)
             for q, (t, c) in zip(qs, chunks)]
        cross = [jnp.dot(q, sf_ref[t * per_tile + c], preferred_element_type=F32) * xi_f
                 + jnp.dot(q, sb_ref[t * per_tile + c], preferred_element_type=F32) * xi_b
                 for q, (t, c) in zip(qs, chunks)]
        a = [(x * decay).astype(BF16) for x in a]
        o = [jnp.dot(x, v_ref[r, :], preferred_element_type=F32) + y
             for x, r, y in zip(a, rs, cross)]
        mu = [jnp.mean(x, axis=-1, keepdims=True) for x in o]
        oc = [x - m for x, m in zip(o, mu)]
        var = [jnp.mean(x * x, axis=-1, keepdims=True) for x in oc]
        for x, s2, r in zip(oc, var, rs):
            y = x * lax.rsqrt(s2 + 1e-5) * gain_ref[...]
            g = g_ref[r, :].astype(F32)
            o_ref[r, :] = (g / (1.0 + jnp.exp(-g)) * y).astype(o_ref.dtype)
        return carry

    lax.fori_loop(0, n_tiles // RET_OUT_TILES, outputs, 0)


def _ret_call(dec, gain, rq, rkT, rv, rg, B, S):
    T = rq.shape[0]
    blk = pl.BlockSpec((S, LANES), lambda b, h: (b, h))
    n_chunks = S // RET_CHUNK
    return pl.pallas_call(
        _ret_kernel, grid=(B, RET_HEADS),
        in_specs=[pl.BlockSpec(memory_space=pltpu.SMEM),
                  pl.BlockSpec((1, LANES), lambda b, h: (0, h)), blk,
                  pl.BlockSpec((S // TM, None, LANES, TM), lambda b, h: (b, h, 0, 0)), blk, blk],
        out_specs=blk,
        out_shape=jax.ShapeDtypeStruct((T, BRANCH_WIDTH), BF16),
        scratch_shapes=[pltpu.VMEM((n_chunks, RET_CHUNK, LANES), BF16),
                        pltpu.VMEM((n_chunks, RET_CHUNK, LANES), BF16)],
        compiler_params=_cparams(2), name="retention",
    )(dec, gain, rq, rkT, rv, rg)


def _merge_mlp_kernel(h_ref, u_ref, a_ref, r_ref, c_ref, wg_ref, wb_ref, wo_ref, mn_ref,
                      w1_ref, w2_ref, o_ref):
    u = u_ref[...]
    merged = None
    for n, br in enumerate((a_ref, r_ref, c_ref)):
        logits = jnp.dot(u, wg_ref[:, n * D_MODEL:(n + 1) * D_MODEL], preferred_element_type=F32)
        gate = 1.0 / (1.0 + jnp.exp(-logits))
        term = gate * jnp.dot(br[...], wb_ref[n], preferred_element_type=F32)
        merged = term if merged is None else merged + term
    merged = merged.astype(BF16)
    halves = [slice(0, TM // 2), slice(TM // 2, TM)]
    hns = [h_ref[r, :] + jnp.dot(merged[r], wo_ref[...], preferred_element_type=F32) for r in halves]
    u2s = []
    for hn in hns:
        ms = jnp.mean(hn * hn, axis=-1, keepdims=True)
        u2s.append((hn * lax.rsqrt(ms + NORM_EPS) * mn_ref[...]).astype(BF16))
    u2 = jnp.concatenate(u2s, axis=0)
    acc = jnp.concatenate(hns, axis=0)
    for c in range(D_FF // D_MODEL):
        cs = slice(c * D_MODEL, (c + 1) * D_MODEL)
        m = jnp.maximum(jnp.dot(u2, w1_ref[:, cs], preferred_element_type=F32), 0.0)
        acc = acc + jnp.dot((m * m).astype(BF16), w2_ref[cs, :], preferred_element_type=F32)
    o_ref[...] = acc


def _merge_mlp_call(h, u, a, r, c, wg, wb, wo, mlp_norm, w1, w2):
    T = h.shape[0]
    row = lambda w: pl.BlockSpec((TM, w), lambda i: (i, 0))
    return pl.pallas_call(
        _merge_mlp_kernel, grid=(T // TM,),
        in_specs=[row(D_MODEL), row(D_MODEL), row(BRANCH_WIDTH), row(BRANCH_WIDTH), row(BRANCH_WIDTH),
                  _resident(wg.shape), _resident(wb.shape), _resident(wo.shape),
                  _resident((1, D_MODEL)), _resident(w1.shape), _resident(w2.shape)],
        out_specs=row(D_MODEL),
        out_shape=jax.ShapeDtypeStruct((T, D_MODEL), F32),
        compiler_params=_cparams(1), name="merge_mlp",
    )(h, u, a, r, c, wg, wb, wo, mlp_norm, w1, w2)


def _rope_tables(S):
    pos = jnp.arange(S, dtype=F32)
    lane = jnp.arange(LANES)
    l64 = lane % HEAD64

    inv = 500000.0 ** (-jnp.arange(0, 16, 2, dtype=F32) / 16)
    ang = pos[:, None] * inv[None, :]
    cos, sin = jnp.cos(ang)[:, l64 % 8], jnp.sin(ang)[:, l64 % 8]
    cd = jnp.where(l64 < 16, cos, 1.0)
    ad = jnp.where(l64 < 8, -sin, 0.0)
    bd = jnp.where((l64 >= 8) & (l64 < 16), sin, 0.0)

    inv = 10000.0 ** (-jnp.arange(0, 32, 2, dtype=F32) / 32)
    row_pos = jnp.floor(pos / GRID_W)
    col_pos = pos - row_pos * GRID_W
    p2 = jnp.where((l64 < 32)[None, :], row_pos[:, None], col_pos[:, None])
    ang = p2 * inv[l64 % 16][None, :]
    cg = jnp.cos(ang)
    ag = jnp.where(l64 % 32 < 16, -jnp.sin(ang), 0.0)
    bg = jnp.where(l64 % 32 >= 16, jnp.sin(ang), 0.0)

    inv = 10000.0 ** (-jnp.arange(0, 128, 2, dtype=F32) / 128)
    ang = pos[:, None] * inv[l64][None, :]
    cr = jnp.cos(ang)
    sr = jnp.where(lane < LANES // 2, -jnp.sin(ang), jnp.sin(ang))
    return (cd, ad, bd, cg, ag, bg, cr, sr)


def _fast_flag(q_gain, k_gain):
    bound = (HEAD64 ** 0.5 * LOG2E * 1.02) * jnp.max(jnp.abs(q_gain)) * jnp.max(jnp.abs(k_gain))
    return (bound <= FAST_SOFTMAX_BOUND).astype(jnp.int32).reshape(1)


def kernel(x, attn_norm, w_in, diff_q_norm, diff_k_norm, diff_lam_q1, diff_lam_k1, diff_lam_q2, diff_lam_k2, diff_subln, ret_decay_fwd, ret_decay_bwd, ret_group_norm, gqa_q_norm, gqa_k_norm, w_branch, w_out, mlp_norm, w_mlp_in, w_mlp_out):
    B, S, D = x.shape
    T = B * S
    tabs = _rope_tables(S)
    blk = jnp.arange(2 * LANES) // HEAD64
    gmat = jnp.where(blk[:, None] == blk[None, :], 1.0 / HEAD64, 0.0).astype(BF16)
    tile2 = lambda v: jnp.concatenate([v, v]).astype(F32)

    h = x.reshape(T, D)
    for l in range(DEPTH):
        lambda_init = 0.8 - 0.6 * math.exp(-0.3 * l)
        w_attn = w_in[l, :, :ATTN_COLS].astype(BF16)
        w_gate = w_in[l, :, ATTN_COLS:].astype(BF16)
        nrm = jnp.stack([tile2(diff_q_norm[l]), tile2(diff_k_norm[l]),
                         tile2(gqa_q_norm[l]), tile2(gqa_k_norm[l])])
        (u, dqT, dk, dvT, rq, rkT, rv, rg, gqT, gk, gvT, gvT2) = _proj_call(
            h, attn_norm[l].reshape(1, D).astype(F32), w_attn, gmat, nrm, tabs)

        lamv = jnp.stack([diff_lam_q1[l], diff_lam_k1[l], diff_lam_q2[l], diff_lam_k2[l]]).astype(F32)
        a = _diff_call(_fast_flag(diff_q_norm[l], diff_k_norm[l]), lamv,
                       diff_subln[l].reshape(1, LANES).astype(F32), dqT, dk, dvT, B, S, lambda_init)
        dec = jnp.stack([ret_decay_fwd[l], ret_decay_bwd[l]]).astype(F32)
        r = _ret_call(dec, ret_group_norm[l].reshape(1, BRANCH_WIDTH).astype(F32), rq, rkT, rv, rg, B, S)
        gqa_fast = _fast_flag(gqa_q_norm[l], gqa_k_norm[l])
        c = lax.cond(gqa_fast[0] == 1,
                     lambda ops: _gqa_mxu_call(ops[0], ops[1], ops[3], B, S),
                     lambda ops: _gqa_call(gqa_fast, ops[0], ops[1], ops[2], B, S),
                     (gqT, gk, gvT, gvT2))

        h = _merge_mlp_call(h, u, a, r, c, w_gate, w_branch[l].astype(BF16), w_out[l].astype(BF16),
                            mlp_norm[l].reshape(1, D).astype(F32),
                            w_mlp_in[l].astype(BF16), w_mlp_out[l].astype(BF16))
    return h.reshape(B, S, D)
```

```python
import functools
import math

import jax
import jax.numpy as jnp
from jax import lax
from jax.experimental import pallas as pl
from jax.experimental.pallas import tpu as pltpu

F32 = jnp.float32
BF16 = jnp.bfloat16

D_MODEL = 1024
DEPTH = 2
GRID_W = 64
NORM_EPS = 1e-6
HEAD64 = 64
DIFF_HEADS = 4
RET_HEADS = 4
RET_CHUNK = 128
GQA_Q_HEADS = 8
GQA_GROUP = 4
N_BRANCHES = 3
BRANCH_WIDTH = 512
D_FF = 4 * D_MODEL
LANES = 128
MXU_DIM = 256
ATTN_COLS = 8 * BRANCH_WIDTH + 2 * LANES

TM = 512
TK = 2048
NQ = 1024
Q_BLOCKS = 2
NCOLS = Q_BLOCKS * NQ
NC = 256
KV_UNROLL = 4
RET_SCAN_TILES = 4
RET_OUT_TILES = 8
ONES_ROWS = 16
VROWS = LANES + ONES_ROWS
FAST_SOFTMAX_BOUND = 64.0
VMEM_LIMIT = 56 * 1024 * 1024
LOG2E = 1.4426950408889634


def _cparams(n_axes):
    return pltpu.CompilerParams(dimension_semantics=("arbitrary",) * n_axes,
                                vmem_limit_bytes=VMEM_LIMIT)


def _resident(shape):
    nd = len(shape)
    return pl.BlockSpec(shape, lambda *_: (0,) * nd, pipeline_mode=pl.Buffered(1))


def _proj_kernel(h_ref, an_ref, w_ref, g_ref, nrm_ref,
                 cd_ref, sd_ref, cg_ref, sg_ref, cr_ref, sr_ref,
                 u_ref, dqT_ref, dk_ref, dvT_ref, rq_ref, rkT_ref, rv_ref, rg_ref,
                 gqT_ref, gk_ref, gvT_ref, gvT2_ref):
    x = h_ref[...]
    ms = jnp.mean(x * x, axis=-1, keepdims=True)
    u = (x * lax.rsqrt(ms + NORM_EPS) * an_ref[...]).astype(BF16)
    u_ref[...] = u

    def proj(c0, n):
        return jnp.dot(u, w_ref[:, c0:c0 + n], preferred_element_type=F32)

    gmat = g_ref[...]
    lo = lax.broadcasted_iota(jnp.int32, (1, LANES), 1) < HEAD64
    first_row = lax.broadcasted_iota(jnp.int32, (ONES_ROWS, TM), 0) == 0
    ones_rows = jnp.where(first_row, 1.0, 0.0).astype(BF16)

    def qk_norm(p, gain):
        out = []
        for c0 in range(0, p.shape[1], 2 * LANES):
            y = p[:, c0:c0 + 2 * LANES]
            w = y.shape[1]
            msq = jnp.dot((y * y).astype(BF16), gmat[0:w, 0:w], preferred_element_type=F32)
            y = y * lax.rsqrt(msq + NORM_EPS)
            out += [y[:, i:i + LANES] * gain for i in range(0, w, LANES)]
        return out

    lane = lax.broadcasted_iota(jnp.int32, (1, LANES), 1)

    def rope(y, c, s, d):
        first = (lane % (2 * d)) < d
        partner = jnp.where(first, pltpu.roll(y, LANES - d, 1), pltpu.roll(y, d, 1))
        return y * c + partner * s

    def chunks(p):
        return [p[:, i:i + LANES] for i in range(0, p.shape[1], LANES)]

    qscale = HEAD64 ** -0.5 * LOG2E
    kscale = LANES ** -0.5
    cd, sd = cd_ref[...], sd_ref[...]
    cg, sg = cg_ref[...], sg_ref[...]
    cr, sr = cr_ref[...], sr_ref[...]

    def diff_q(p):
        for hd, y in enumerate(qk_norm(p, nrm_ref[0:1, :])):
            y = rope(y, cd, sd, 8) * qscale
            dqT_ref[hd, :, 0:TM] = jnp.where(lo, y, 0.0).T.astype(BF16)
            dqT_ref[hd, :, TM:2 * TM] = jnp.where(lo, 0.0, y).T.astype(BF16)

    def diff_k(p):
        for hd, y in enumerate(qk_norm(p, nrm_ref[1:2, :])):
            dk_ref[:, hd * LANES:(hd + 1) * LANES] = rope(y, cd, sd, 8).astype(BF16)

    def diff_v(p):
        for hd, y in enumerate(chunks(p)):
            dvT_ref[hd, 0:LANES, :] = y.T.astype(BF16)
            dvT_ref[hd, LANES:VROWS, :] = ones_rows

    def ret_q(p):
        for hd, y in enumerate(chunks(p)):
            rq_ref[:, hd * LANES:(hd + 1) * LANES] = (y * cr + pltpu.roll(y, LANES // 2, 1) * sr).astype(BF16)

    def ret_k(p):
        for hd, y in enumerate(chunks(p)):
            rkT_ref[hd] = ((y * cr + pltpu.roll(y, LANES // 2, 1) * sr) * kscale).T.astype(BF16)

    def ret_v(p):
        rv_ref[...] = p.astype(BF16)

    def ret_g(p):
        rg_ref[...] = p.astype(BF16)

    def gqa_q(p):
        for c, y in enumerate(qk_norm(p, nrm_ref[2:3, :])):
            y = rope(y, cg, sg, 16) * qscale
            ysw = pltpu.roll(y, LANES // 2, 1)
            if c < 2:
                z_even, z_odd = jnp.where(lo, y, 0.0), jnp.where(lo, ysw, 0.0)
            else:
                z_even, z_odd = jnp.where(lo, 0.0, ysw), jnp.where(lo, 0.0, y)
            for hh, z in ((2 * c, z_even), (2 * c + 1, z_odd)):
                zt = z.T.astype(BF16)
                for qb in range(TM // LANES):
                    gqT_ref[qb, 0:LANES, hh * LANES:(hh + 1) * LANES] = zt[:, qb * LANES:(qb + 1) * LANES]
        gqT_ref[:, LANES:MXU_DIM, :] = jnp.zeros((TM // LANES, MXU_DIM - LANES, NQ), BF16)

    def gqa_kv(p):
        (y,) = qk_norm(p[:, 0:LANES], nrm_ref[3:4, :])
        gk_ref[:, 0:LANES] = rope(y, cg, sg, 16).astype(BF16)
        gk_ref[:, LANES:MXU_DIM] = jnp.zeros((TM, MXU_DIM - LANES), BF16)
        vt = p[:, LANES:2 * LANES].T.astype(BF16)
        gvT_ref[0:LANES, :] = vt
        gvT_ref[LANES:VROWS, :] = ones_rows
        for ku in range(TM // MXU_DIM):
            for g in range(2):
                gvT2_ref[ku, g] = vt[g * HEAD64:(g + 1) * HEAD64, ku * MXU_DIM:(ku + 1) * MXU_DIM]

    order = [gqa_q, diff_q, ret_k, diff_v, diff_k, gqa_kv, ret_q, ret_v, ret_g]
    in_w_in = [diff_q, diff_k, diff_v, ret_q, ret_k, ret_v, ret_g, gqa_q, gqa_kv]
    width = {f: BRANCH_WIDTH for f in in_w_in}
    width[gqa_kv] = 2 * LANES
    start = {f: sum(width[g] for g in in_w_in[:i]) for i, f in enumerate(in_w_in)}
    p_next = proj(start[order[0]], width[order[0]])
    for i, epilogue in enumerate(order):
        p_cur = p_next
        if i + 1 < len(order):
            p_next = proj(start[order[i + 1]], width[order[i + 1]])
        epilogue(p_cur)


def _proj_call(h, attn_norm, w_attn, gmat, nrm, tabs):
    T = h.shape[0]
    S = tabs[0].shape[0]
    nt = T // TM
    tab_spec = pl.BlockSpec((TM, LANES), lambda i: (i % (S // TM), 0))
    row = lambda w: pl.BlockSpec((TM, w), lambda i: (i, 0))
    out_shape = (
        jax.ShapeDtypeStruct((T, D_MODEL), BF16),
        jax.ShapeDtypeStruct((nt, DIFF_HEADS, LANES, 2 * TM), BF16),
        jax.ShapeDtypeStruct((T, BRANCH_WIDTH), BF16),
        jax.ShapeDtypeStruct((nt, DIFF_HEADS, VROWS, TM), BF16),
        jax.ShapeDtypeStruct((T, BRANCH_WIDTH), BF16),
        jax.ShapeDtypeStruct((nt, RET_HEADS, LANES, TM), BF16),
        jax.ShapeDtypeStruct((T, BRANCH_WIDTH), BF16),
        jax.ShapeDtypeStruct((T, BRANCH_WIDTH), BF16),
        jax.ShapeDtypeStruct((T // LANES, MXU_DIM, NQ), BF16),
        jax.ShapeDtypeStruct((T, MXU_DIM), BF16),
        jax.ShapeDtypeStruct((nt, VROWS, TM), BF16),
        jax.ShapeDtypeStruct((T // MXU_DIM, 2, HEAD64, MXU_DIM), BF16),
    )
    out_specs = (
        row(D_MODEL),
        pl.BlockSpec((None, DIFF_HEADS, LANES, 2 * TM), lambda i: (i, 0, 0, 0)),
        row(BRANCH_WIDTH),
        pl.BlockSpec((None, DIFF_HEADS, VROWS, TM), lambda i: (i, 0, 0, 0)),
        row(BRANCH_WIDTH),
        pl.BlockSpec((None, RET_HEADS, LANES, TM), lambda i: (i, 0, 0, 0)),
        row(BRANCH_WIDTH), row(BRANCH_WIDTH),
        pl.BlockSpec((TM // LANES, MXU_DIM, NQ), lambda i: (i, 0, 0)),
        row(MXU_DIM),
        pl.BlockSpec((None, VROWS, TM), lambda i: (i, 0, 0)),
        pl.BlockSpec((TM // MXU_DIM, 2, HEAD64, MXU_DIM), lambda i: (i, 0, 0, 0)),
    )
    in_specs = [row(D_MODEL), _resident((1, D_MODEL)), _resident(w_attn.shape),
                _resident(gmat.shape), _resident(nrm.shape)] + [tab_spec] * len(tabs)
    return pl.pallas_call(
        _proj_kernel, grid=(nt,), in_specs=in_specs, out_specs=out_specs,
        out_shape=out_shape, compiler_params=_cparams(1), name="proj",
    )(h, attn_norm, w_attn, gmat, nrm, *tabs)


def _attn_core(fast_ref, qT_ref, k_ref, vT_ref, m_ref, acc_ref):
    n_steps = k_ref.shape[0] // TK
    n_trips = n_steps // KV_UNROLL
    qT = jnp.concatenate([qT_ref[i] for i in range(Q_BLOCKS)], axis=-1)

    def kv(j):
        k = k_ref[pl.ds(pl.multiple_of(j * TK, TK), TK), :]
        per = TK // TM
        vT = jnp.concatenate([vT_ref[j * per + i] for i in range(per)], axis=-1)
        return k, vT

    @pl.when(fast_ref[0] == 1)
    def _():
        def body(jj, carry):
            pv = lsum = None
            for u in range(KV_UNROLL):
                k, vT = kv(jj * KV_UNROLL + u)
                s = jnp.dot(k, qT, preferred_element_type=F32)
                pf = jnp.exp2(s)
                ls = jnp.sum(pf, axis=0, keepdims=True)
                t = jnp.dot(vT[0:LANES], pf.astype(BF16), preferred_element_type=F32)
                pv, lsum = (t, ls) if pv is None else (pv + t, lsum + ls)
            if n_trips == 1:
                acc_ref[0:LANES, :] = pv
                acc_ref[LANES:LANES + 1, :] = lsum
            else:
                acc_ref[0:LANES, :] += pv
                acc_ref[LANES:LANES + 1, :] += lsum
            return carry

        if n_trips > 1:
            acc_ref[...] = jnp.zeros(acc_ref.shape, F32)
        lax.fori_loop(0, n_trips, body, 0)

    @pl.when(fast_ref[0] == 0)
    def _():
        acc_ref[...] = jnp.zeros(acc_ref.shape, F32)
        m_ref[...] = jnp.full(m_ref.shape, -1e30, F32)

        def body(j, carry):
            k, vT = kv(j)
            for c in range(NCOLS // NC):
                cs = slice(c * NC, (c + 1) * NC)
                s = jnp.dot(k, qT[:, cs], preferred_element_type=F32)
                m_old = m_ref[:, cs]
                m_new = jnp.maximum(m_old, jnp.max(s, axis=0, keepdims=True))
                alpha = jnp.exp2(m_old - m_new)
                p = jnp.exp2(s - m_new).astype(BF16)
                pv = jnp.dot(vT, p, preferred_element_type=F32)
                acc_ref[:, cs] = acc_ref[:, cs] * alpha + pv
                m_ref[:, cs] = m_new
            return carry

        lax.fori_loop(0, n_steps, body, 0)


def _diff_kernel(fast_ref, lam_ref, sub_ref, qT_ref, k_ref, vT_ref, o_ref, m_ref, acc_ref, *,
                 lambda_init):
    lv = lam_ref[...]
    lam = (jnp.exp(jnp.sum(lv[0:1] * lv[1:2], axis=-1, keepdims=True))
           - jnp.exp(jnp.sum(lv[2:3] * lv[3:4], axis=-1, keepdims=True)) + lambda_init)
    _attn_core(fast_ref, qT_ref, k_ref, vT_ref, m_ref, acc_ref)

    tq = NQ // 2
    for qb in range(Q_BLOCKS):
        c1 = slice(qb * NQ, qb * NQ + tq)
        c2 = slice(qb * NQ + tq, (qb + 1) * NQ)
        o1 = acc_ref[0:LANES, c1] / acc_ref[LANES:LANES + 1, c1]
        o2 = acc_ref[0:LANES, c2] / acc_ref[LANES:LANES + 1, c2]
        d = o1 - lam * o2
        ms = jnp.mean(d * d, axis=0, keepdims=True)
        dn = d * lax.rsqrt(ms + 1e-5)
        o_ref[qb * tq:(qb + 1) * tq, :] = (
            dn.T * (sub_ref[...] * (1.0 - lambda_init))).astype(o_ref.dtype)


_SMEM_SPEC = pl.BlockSpec(memory_space=pltpu.SMEM)
_ATTN_SCRATCH = [pltpu.VMEM((1, NCOLS), F32), pltpu.VMEM((VROWS, NCOLS), F32)]


def _diff_call(fast, lamv, subln, dqT, dk, dvT, B, S, lambda_init):
    T = dk.shape[0]
    tq = Q_BLOCKS * NQ // 2
    nq = S // tq
    kern = functools.partial(_diff_kernel, lambda_init=lambda_init)
    return pl.pallas_call(
        kern, grid=(B, DIFF_HEADS, nq),
        in_specs=[
            _SMEM_SPEC, _resident(lamv.shape), _resident(subln.shape),
            pl.BlockSpec((Q_BLOCKS, None, LANES, NQ), lambda b, h, i: (b * nq + i, h, 0, 0)),
            pl.BlockSpec((S, LANES), lambda b, h, i: (b, h)),
            pl.BlockSpec((S // TM, None, VROWS, TM), lambda b, h, i: (b, h, 0, 0)),
        ],
        out_specs=pl.BlockSpec((tq, LANES), lambda b, h, i: (b * nq + i, h)),
        out_shape=jax.ShapeDtypeStruct((T, BRANCH_WIDTH), BF16),
        scratch_shapes=_ATTN_SCRATCH,
        compiler_params=_cparams(3), name="diff_attn",
    )(fast, lamv, subln, dqT, dk, dvT)


def _gqa_kernel(fast_ref, qT_ref, k_ref, vT_ref, o_ref, m_ref, acc_ref):
    _attn_core(fast_ref, qT_ref, k_ref, vT_ref, m_ref, acc_ref)
    for qb, c in [(qb, c) for qb in range(Q_BLOCKS) for c in range(GQA_Q_HEADS // 2)]:
        g = (2 * c) // GQA_GROUP
        rows = slice(g * HEAD64, (g + 1) * HEAD64)
        parts = []
        for hh in (2 * c, 2 * c + 1):
            cols = slice(qb * NQ + hh * LANES, qb * NQ + (hh + 1) * LANES)
            parts.append(acc_ref[rows, cols] / acc_ref[LANES:LANES + 1, cols])
        o_ref[qb * LANES:(qb + 1) * LANES, c * LANES:(c + 1) * LANES] = (
            jnp.concatenate(parts, axis=0).T.astype(o_ref.dtype))


def _gqa_call(fast, gqT, gk, gvT, B, S):
    T = gk.shape[0]
    tq = Q_BLOCKS * NQ // GQA_Q_HEADS
    nq = S // tq
    return pl.pallas_call(
        _gqa_kernel, grid=(B, nq),
        in_specs=[
            _SMEM_SPEC,
            pl.BlockSpec((Q_BLOCKS, LANES, NQ), lambda b, i: (b * nq + i, 0, 0)),
            pl.BlockSpec((S, LANES), lambda b, i: (b, 0)),
            pl.BlockSpec((S // TM, VROWS, TM), lambda b, i: (b, 0, 0)),
        ],
        out_specs=pl.BlockSpec((tq, BRANCH_WIDTH), lambda b, i: (b * nq + i, 0)),
        out_shape=jax.ShapeDtypeStruct((T, BRANCH_WIDTH), BF16),
        scratch_shapes=_ATTN_SCRATCH,
        compiler_params=_cparams(2), name="gqa_attn",
    )(fast, gqT, gk, gvT)


S_REGIONS = (0, 64, 128)
O_REGIONS = ((192, 208), (224, 240))
MXU_Q_BLOCKS = len(O_REGIONS)
POP_LAG = 2
PV_LAG = 3
P_RING = 6
TRIP_SLOTS = 30


def _gqa_mxu_kernel(qT_ref, k_ref, vT_ref, o_ref, p_ref, l_ref):
    per_block = 2 * (k_ref.shape[0] // MXU_DIM)
    n_slots = MXU_Q_BLOCKS * per_block
    head = per_block % TRIP_SLOTS
    assert head > PV_LAG and TRIP_SLOTS % P_RING == 0

    @pl.when((pl.program_id(0) == 0) & (pl.program_id(1) == 0))
    def _():
        for mxu in range(2):
            for region in S_REGIONS:
                pltpu.matmul_pop(region, (MXU_DIM, MXU_DIM), F32, mxu)
            for region in sum(O_REGIONS, ()):
                pltpu.matmul_pop(region, (HEAD64, MXU_DIM), F32, mxu)

    l_ref[...] = jnp.zeros(l_ref.shape, F32)

    def q_tile(mxu, n, blk):
        c = 2 * mxu + n % 2
        return qT_ref[blk, :, c * MXU_DIM:(c + 1) * MXU_DIM]

    def o_region(n):
        return O_REGIONS[n // per_block][n % 2]

    def l_cols(mxu, n):
        c0 = (n // per_block) * NQ + (2 * mxu + n % 2) * MXU_DIM
        return slice(c0, c0 + MXU_DIM)

    def slot(n, unit, unit_v, next_blk):
        start = unit * MXU_DIM
        if not isinstance(unit, int):
            start = pl.multiple_of(start, MXU_DIM)
        k = k_ref[pl.ds(start, MXU_DIM), :]
        for mxu in range(2):
            if n >= PV_LAG:
                pltpu.matmul_push_rhs(p_ref[mxu, (n - PV_LAG) % P_RING], staging_register=1,
                                      mxu_index=mxu)
            pltpu.matmul_acc_lhs(S_REGIONS[n % 3], k, mxu_index=mxu, load_staged_rhs=0)
            pltpu.matmul_push_rhs(q_tile(mxu, n + 1, next_blk), staging_register=0, mxu_index=mxu)
            if n >= PV_LAG:
                pltpu.matmul_acc_lhs(o_region(n - PV_LAG), vT_ref[unit_v, mxu],
                                     mxu_index=mxu, load_staged_rhs=1)
        for mxu in range(2):
            if n >= POP_LAG:
                pop_scores(mxu, n - POP_LAG)

    def pop_scores(mxu, n):
        s = pltpu.matmul_pop(S_REGIONS[n % 3], (MXU_DIM, MXU_DIM), F32, mxu)
        pf = jnp.exp2(s)
        l_ref[:, l_cols(mxu, n)] += jnp.sum(pf, axis=0, keepdims=True)
        p_ref[mxu, n % P_RING] = pf.astype(BF16)

    def values_only(mxu, n, unit_v):
        pltpu.matmul_push_rhs(p_ref[mxu, n % P_RING], staging_register=1, mxu_index=mxu)
        pltpu.matmul_acc_lhs(o_region(n), vT_ref[unit_v, mxu], mxu_index=mxu, load_staged_rhs=1)

    def unit_of(n):
        return (n % per_block) // 2

    def block_of(n):
        return jnp.minimum(n // per_block, MXU_Q_BLOCKS - 1)

    for mxu in range(2):
        pltpu.matmul_push_rhs(q_tile(mxu, 0, 0), staging_register=0, mxu_index=mxu)
    for blk in range(MXU_Q_BLOCKS):
        first = blk * per_block
        for n in range(first, first + head):
            slot(n, unit_of(n), unit_of(max(n - PV_LAG, 0)), blk)

        def trip(t, carry, first=first + head):
            for i in range(TRIP_SLOTS):
                base = first + TRIP_SLOTS * t + i
                slot(first + i, unit_of(base), unit_of(base - PV_LAG), block_of(base + 1))
            return carry

        lax.fori_loop(0, per_block // TRIP_SLOTS, trip, 0)

    for mxu in range(2):
        pltpu.matmul_acc_lhs(S_REGIONS[n_slots % 3], jnp.zeros((16, MXU_DIM), BF16),
                             mxu_index=mxu, load_staged_rhs=0)
    for n in range(n_slots - POP_LAG, n_slots):
        for mxu in range(2):
            pop_scores(mxu, n)
    for mxu in range(2):
        pltpu.matmul_pop(S_REGIONS[n_slots % 3], (16, MXU_DIM), F32, mxu)
    for n in range(n_slots - PV_LAG, n_slots):
        for mxu in range(2):
            values_only(mxu, n, unit_of(n))

    tq = NQ // GQA_Q_HEADS
    for blk in range(MXU_Q_BLOCKS):
        for mxu in range(2):
            for cc in range(2):
                n = blk * per_block + cc
                c = 2 * mxu + cc
                o = pltpu.matmul_pop(o_region(n), (HEAD64, MXU_DIM), F32, mxu)
                o = o / l_ref[:, l_cols(mxu, n)]
                pair = jnp.concatenate([o[:, 0:LANES], o[:, LANES:2 * LANES]], axis=0)
                o_ref[blk * tq:(blk + 1) * tq, c * LANES:(c + 1) * LANES] = pair.T.astype(o_ref.dtype)


def _gqa_mxu_call(gqT, gk, gvT2, B, S):
    T = gk.shape[0]
    tq = MXU_Q_BLOCKS * NQ // GQA_Q_HEADS
    nq = S // tq
    return pl.pallas_call(
        _gqa_mxu_kernel, grid=(B, nq),
        in_specs=[
            pl.BlockSpec((MXU_Q_BLOCKS, MXU_DIM, NQ), lambda b, i: (b * nq + i, 0, 0)),
            pl.BlockSpec((S, MXU_DIM), lambda b, i: (b, 0)),
            pl.BlockSpec((S // MXU_DIM, 2, HEAD64, MXU_DIM), lambda b, i: (b, 0, 0, 0)),
        ],
        out_specs=pl.BlockSpec((tq, BRANCH_WIDTH), lambda b, i: (b * nq + i, 0)),
        out_shape=jax.ShapeDtypeStruct((T, BRANCH_WIDTH), BF16),
        scratch_shapes=[pltpu.VMEM((2, P_RING, MXU_DIM, MXU_DIM), BF16),
                        pltpu.VMEM((1, MXU_Q_BLOCKS * NQ), F32)],
        compiler_params=_cparams(2), name="gqa_attn_mxu",
    )(gqT, gk, gvT2)


def _ret_kernel(dec_ref, gain_ref, q_ref, kT_ref, v_ref, g_ref, o_ref, sf_ref, sb_ref):
    C = RET_CHUNK
    hd = pl.program_id(1)
    lf = -jnp.exp(jnp.full((C, C), dec_ref[0, hd], F32))
    lb = -jnp.exp(jnp.full((C, C), dec_ref[1, hd], F32))
    ii = lax.broadcasted_iota(jnp.int32, (C, C), 0).astype(F32)
    jj = lax.broadcasted_iota(jnp.int32, (C, C), 1).astype(F32)
    diff = ii - jj
    decay = jnp.where(diff >= 0, jnp.exp(jnp.maximum(diff, 0.0) * lf),
                      jnp.exp(jnp.maximum(-diff, 0.0) * lb))
    xi_f, zeta_f = jnp.exp((ii + 1.0) * lf), jnp.exp((C - 1.0 - ii) * lf)
    xi_b, zeta_b = jnp.exp((C - ii) * lb), jnp.exp(ii * lb)
    dec_f, dec_b = jnp.exp(C * lf), jnp.exp(C * lb)
    n_tiles = kT_ref.shape[0]
    per_tile = TM // C

    def rows(t, c):
        return pl.ds(pl.multiple_of(t * TM + c * C, C), C)

    def chunk_kv(t, c, zeta):
        vz = (v_ref[rows(t, c), :].astype(F32) * zeta).astype(BF16)
        return jnp.dot(kT_ref[t, :, c * C:(c + 1) * C], vz, preferred_element_type=F32)

    def scans(i, states):
        st_f, st_b = states
        fwd = [(i * RET_SCAN_TILES + t, c) for t in range(RET_SCAN_TILES) for c in range(per_tile)]
        bwd = [(n_tiles - 1 - t, per_tile - 1 - c) for t, c in fwd]
        kv_f = [chunk_kv(t, c, zeta_f) for t, c in fwd]
        kv_b = [chunk_kv(t, c, zeta_b) for t, c in bwd]
        for (tf, cf), kf, (tb, cb), kb in zip(fwd, kv_f, bwd, kv_b):
            sf_ref[tf * per_tile + cf] = st_f.astype(BF16)
            st_f = st_f * dec_f + kf
            sb_ref[tb * per_tile + cb] = st_b.astype(BF16)
            st_b = st_b * dec_b + kb
        return st_f, st_b

    zero = jnp.zeros((C, C), F32)
    lax.fori_loop(0, n_tiles // RET_SCAN_TILES, scans, (zero, zero))

    def outputs(tt, carry):
        chunks = [(tt * RET_OUT_TILES + i, c) for i in range(RET_OUT_TILES) for c in range(per_tile)]
        rs = [rows(t, c) for t, c in chunks]
        qs = [q_ref[r, :] for r in rs]
        a = [jnp.dot(q, kT_ref[t, :, c * C:(c + 1) * C], preferred_element_type=F32)
             for q, (t, c) in zip(qs, chunks)]
        cross = [jnp.dot(q, sf_ref[t * per_tile + c], preferred_element_type=F32) * xi_f
                 + jnp.dot(q, sb_ref[t * per_tile + c], preferred_element_type=F32) * xi_b
                 for q, (t, c) in zip(qs, chunks)]
        a = [(x * decay).astype(BF16) for x in a]
        o = [jnp.dot(x, v_ref[r, :], preferred_element_type=F32) + y
             for x, r, y in zip(a, rs, cross)]
        mu = [jnp.mean(x, axis=-1, keepdims=True) for x in o]
        oc = [x - m for x, m in zip(o, mu)]
        var = [jnp.mean(x * x, axis=-1, keepdims=True) for x in oc]
        for x, s2, r in zip(oc, var, rs):
            y = x * lax.rsqrt(s2 + 1e-5) * gain_ref[...]
            g = g_ref[r, :].astype(F32)
            o_ref[r, :] = (g / (1.0 + jnp.exp(-g)) * y).astype(o_ref.dtype)
        return carry

    lax.fori_loop(0, n_tiles // RET_OUT_TILES, outputs, 0)


def _ret_call(dec, gain, rq, rkT, rv, rg, B, S):
    T = rq.shape[0]
    blk = pl.BlockSpec((S, LANES), lambda b, h: (b, h))
    n_chunks = S // RET_CHUNK
    return pl.pallas_call(
        _ret_kernel, grid=(B, RET_HEADS),
        in_specs=[pl.BlockSpec(memory_space=pltpu.SMEM),
                  pl.BlockSpec((1, LANES), lambda b, h: (0, h)), blk,
                  pl.BlockSpec((S // TM, None, LANES, TM), lambda b, h: (b, h, 0, 0)), blk, blk],
        out_specs=blk,
        out_shape=jax.ShapeDtypeStruct((T, BRANCH_WIDTH), BF16),
        scratch_shapes=[pltpu.VMEM((n_chunks, RET_CHUNK, LANES), BF16),
                        pltpu.VMEM((n_chunks, RET_CHUNK, LANES), BF16)],
        compiler_params=_cparams(2), name="retention",
    )(dec, gain, rq, rkT, rv, rg)


def _merge_mlp_kernel(h_ref, u_ref, a_ref, r_ref, c_ref, wg_ref, wb_ref, wo_ref, mn_ref,
                      w1_ref, w2_ref, o_ref):
    u = u_ref[...]
    merged = None
    for n, br in enumerate((a_ref, r_ref, c_ref)):
        logits = jnp.dot(u, wg_ref[:, n * D_MODEL:(n + 1) * D_MODEL], preferred_element_type=F32)
        gate = 1.0 / (1.0 + jnp.exp(-logits))
        term = gate * jnp.dot(br[...], wb_ref[n], preferred_element_type=F32)
        merged = term if merged is None else merged + term
    merged = merged.astype(BF16)
    halves = [slice(0, TM // 2), slice(TM // 2, TM)]
    hns = [h_ref[r, :] + jnp.dot(merged[r], wo_ref[...], preferred_element_type=F32) for r in halves]
    u2s = []
    for hn in hns:
        ms = jnp.mean(hn * hn, axis=-1, keepdims=True)
        u2s.append((hn * lax.rsqrt(ms + NORM_EPS) * mn_ref[...]).astype(BF16))
    u2 = jnp.concatenate(u2s, axis=0)
    acc = jnp.concatenate(hns, axis=0)
    for c in range(D_FF // D_MODEL):
        cs = slice(c * D_MODEL, (c + 1) * D_MODEL)
        m = jnp.maximum(jnp.dot(u2, w1_ref[:, cs], preferred_element_type=F32), 0.0)
        acc = acc + jnp.dot((m * m).astype(BF16), w2_ref[cs, :], preferred_element_type=F32)
    o_ref[...] = acc


def _merge_mlp_call(h, u, a, r, c, wg, wb, wo, mlp_norm, w1, w2):
    T = h.shape[0]
    row = lambda w: pl.BlockSpec((TM, w), lambda i: (i, 0))
    return pl.pallas_call(
        _merge_mlp_kernel, grid=(T // TM,),
        in_specs=[row(D_MODEL), row(D_MODEL), row(BRANCH_WIDTH), row(BRANCH_WIDTH), row(BRANCH_WIDTH),
                  _resident(wg.shape), _resident(wb.shape), _resident(wo.shape),
                  _resident((1, D_MODEL)), _resident(w1.shape), _resident(w2.shape)],
        out_specs=row(D_MODEL),
        out_shape=jax.ShapeDtypeStruct((T, D_MODEL), F32),
        compiler_params=_cparams(1), name="merge_mlp",
    )(h, u, a, r, c, wg, wb, wo, mlp_norm, w1, w2)


def _rope_tables(S):
    pos = jnp.arange(S, dtype=F32)
    lane = jnp.arange(LANES)
    l64 = lane % HEAD64

    inv = 500000.0 ** (-jnp.arange(0, 16, 2, dtype=F32) / 16)
    ang = pos[:, None] * inv[None, :]
    cos, sin = jnp.cos(ang)[:, l64 % 8], jnp.sin(ang)[:, l64 % 8]
    cd = jnp.where(l64 < 16, cos, 1.0)
    sd = jnp.where(l64 < 8, -sin, jnp.where(l64 < 16, sin, 0.0))

    inv = 10000.0 ** (-jnp.arange(0, 32, 2, dtype=F32) / 32)
    row_pos = jnp.floor(pos / GRID_W)
    col_pos = pos - row_pos * GRID_W
    p2 = jnp.where((l64 < 32)[None, :], row_pos[:, None], col_pos[:, None])
    ang = p2 * inv[l64 % 16][None, :]
    cg = jnp.cos(ang)
    sg = jnp.where(l64 % 32 < 16, -jnp.sin(ang), jnp.sin(ang))

    inv = 10000.0 ** (-jnp.arange(0, 128, 2, dtype=F32) / 128)
    ang = pos[:, None] * inv[l64][None, :]
    cr = jnp.cos(ang)
    sr = jnp.where(lane < LANES // 2, -jnp.sin(ang), jnp.sin(ang))
    return (cd, sd, cg, sg, cr, sr)


def _fast_flag(q_gain, k_gain):
    bound = (HEAD64 ** 0.5 * LOG2E * 1.02) * jnp.max(jnp.abs(q_gain)) * jnp.max(jnp.abs(k_gain))
    return (bound <= FAST_SOFTMAX_BOUND).astype(jnp.int32).reshape(1)


def kernel(x, attn_norm, w_in, diff_q_norm, diff_k_norm, diff_lam_q1, diff_lam_k1, diff_lam_q2, diff_lam_k2, diff_subln, ret_decay_fwd, ret_decay_bwd, ret_group_norm, gqa_q_norm, gqa_k_norm, w_branch, w_out, mlp_norm, w_mlp_in, w_mlp_out):
    B, S, D = x.shape
    T = B * S
    tabs = _rope_tables(S)
    blk = jnp.arange(2 * LANES) // HEAD64
    gmat = jnp.where(blk[:, None] == blk[None, :], 1.0 / HEAD64, 0.0).astype(BF16)
    tile2 = lambda v: jnp.concatenate([v, v]).astype(F32)

    h = x.reshape(T, D)
    for l in range(DEPTH):
        lambda_init = 0.8 - 0.6 * math.exp(-0.3 * l)
        w_attn = w_in[l, :, :ATTN_COLS].astype(BF16)
        w_gate = w_in[l, :, ATTN_COLS:].astype(BF16)
        nrm = jnp.stack([tile2(diff_q_norm[l]), tile2(diff_k_norm[l]),
                         tile2(gqa_q_norm[l]), tile2(gqa_k_norm[l])])
        (u, dqT, dk, dvT, rq, rkT, rv, rg, gqT, gk, gvT, gvT2) = _proj_call(
            h, attn_norm[l].reshape(1, D).astype(F32), w_attn, gmat, nrm, tabs)

        lamv = jnp.stack([diff_lam_q1[l], diff_lam_k1[l], diff_lam_q2[l], diff_lam_k2[l]]).astype(F32)
        a = _diff_call(_fast_flag(diff_q_norm[l], diff_k_norm[l]), lamv,
                       diff_subln[l].reshape(1, LANES).astype(F32), dqT, dk, dvT, B, S, lambda_init)
        dec = jnp.stack([ret_decay_fwd[l], ret_decay_bwd[l]]).astype(F32)
        r = _ret_call(dec, ret_group_norm[l].reshape(1, BRANCH_WIDTH).astype(F32), rq, rkT, rv, rg, B, S)
        gqa_fast = _fast_flag(gqa_q_norm[l], gqa_k_norm[l])
        c = lax.cond(gqa_fast[0] == 1,
                     lambda ops: _gqa_mxu_call(ops[0], ops[1], ops[3], B, S),
                     lambda ops: _gqa_call(gqa_fast, ops[0], ops[1], ops[2], B, S),
                     (gqT, gk, gvT, gvT2))

        h = _merge_mlp_call(h, u, a, r, c, w_gate, w_branch[l].astype(BF16), w_out[l].astype(BF16),
                            mlp_norm[l].reshape(1, D).astype(F32),
                            w_mlp_in[l].astype(BF16), w_mlp_out[l].astype(BF16))
    return h.reshape(B, S, D)
```

```python
import functools
import math

import jax
import jax.numpy as jnp
from jax import lax
from jax.experimental import pallas as pl
from jax.experimental.pallas import tpu as pltpu

F32 = jnp.float32
BF16 = jnp.bfloat16

D_MODEL = 1024
DEPTH = 2
GRID_W = 64
NORM_EPS = 1e-6
HEAD64 = 64
DIFF_HEADS = 4
RET_HEADS = 4
RET_CHUNK = 128
GQA_Q_HEADS = 8
GQA_GROUP = 4
N_BRANCHES = 3
BRANCH_WIDTH = 512
D_FF = 4 * D_MODEL
LANES = 128
MXU_DIM = 256
ATTN_COLS = 8 * BRANCH_WIDTH + 2 * LANES

TM = 512
TK = 2048
NQ = 1024
Q_BLOCKS = 2
NCOLS = Q_BLOCKS * NQ
NC = 256
KV_UNROLL = 4
RET_SCAN_TILES = 4
RET_OUT_TILES = 8
ONES_ROWS = 16
VROWS = LANES + ONES_ROWS
FAST_SOFTMAX_BOUND = 64.0
VMEM_LIMIT = 56 * 1024 * 1024
LOG2E = 1.4426950408889634


def _cparams(n_axes):
    return pltpu.CompilerParams(dimension_semantics=("arbitrary",) * n_axes,
                                vmem_limit_bytes=VMEM_LIMIT)


def _resident(shape):
    nd = len(shape)
    return pl.BlockSpec(shape, lambda *_: (0,) * nd, pipeline_mode=pl.Buffered(1))


def _proj_kernel(h_ref, an_ref, w_ref, g_ref, nrm_ref,
                 cd_ref, sd_ref, cg_ref, sg_ref, cr_ref, sr_ref,
                 u_ref, dqT_ref, dk_ref, dvT_ref, rq_ref, rkT_ref, rv_ref, rg_ref,
                 gqT_ref, gk_ref, gvT_ref, gvT2_ref):
    x = h_ref[...]
    ms = jnp.mean(x * x, axis=-1, keepdims=True)
    u = (x * lax.rsqrt(ms + NORM_EPS) * an_ref[...]).astype(BF16)
    u_ref[...] = u

    def proj(c0, n):
        return jnp.dot(u, w_ref[:, c0:c0 + n], preferred_element_type=F32)

    gmat = g_ref[...]
    lo = lax.broadcasted_iota(jnp.int32, (1, LANES), 1) < HEAD64
    first_row = lax.broadcasted_iota(jnp.int32, (ONES_ROWS, TM), 0) == 0
    ones_rows = jnp.where(first_row, 1.0, 0.0).astype(BF16)

    def qk_norm(p, gain):
        out = []
        for c0 in range(0, p.shape[1], 2 * LANES):
            y = p[:, c0:c0 + 2 * LANES]
            w = y.shape[1]
            msq = jnp.dot((y * y).astype(BF16), gmat[0:w, 0:w], preferred_element_type=F32)
            y = y * lax.rsqrt(msq + NORM_EPS)
            out += [y[:, i:i + LANES] * gain for i in range(0, w, LANES)]
        return out

    lane = lax.broadcasted_iota(jnp.int32, (1, LANES), 1)

    def rope(y, c, s, d):
        first = (lane % (2 * d)) < d
        partner = jnp.where(first, pltpu.roll(y, LANES - d, 1), pltpu.roll(y, d, 1))
        return y * c + partner * s

    def chunks(p):
        return [p[:, i:i + LANES] for i in range(0, p.shape[1], LANES)]

    qscale = HEAD64 ** -0.5 * LOG2E
    kscale = LANES ** -0.5

    def diff_q(p):
        for hd, y in enumerate(qk_norm(p, nrm_ref[0:1, :])):
            y = rope(y, cd_ref[...], sd_ref[...], 8) * qscale
            dqT_ref[hd, :, 0:TM] = jnp.where(lo, y, 0.0).T.astype(BF16)
            dqT_ref[hd, :, TM:2 * TM] = jnp.where(lo, 0.0, y).T.astype(BF16)

    def diff_k(p):
        for hd, y in enumerate(qk_norm(p, nrm_ref[1:2, :])):
            dk_ref[:, hd * LANES:(hd + 1) * LANES] = rope(y, cd_ref[...], sd_ref[...], 8).astype(BF16)

    def diff_v(p):
        for hd, y in enumerate(chunks(p)):
            dvT_ref[hd, 0:LANES, :] = y.T.astype(BF16)
            dvT_ref[hd, LANES:VROWS, :] = ones_rows

    def ret_q(p):
        for hd, y in enumerate(chunks(p)):
            rq_ref[:, hd * LANES:(hd + 1) * LANES] = (y * cr_ref[...] + pltpu.roll(y, LANES // 2, 1) * sr_ref[...]).astype(BF16)

    def ret_k(p):
        for hd, y in enumerate(chunks(p)):
            rkT_ref[hd] = ((y * cr_ref[...] + pltpu.roll(y, LANES // 2, 1) * sr_ref[...]) * kscale).T.astype(BF16)

    def ret_v(p):
        rv_ref[...] = p.astype(BF16)

    def ret_g(p):
        rg_ref[...] = p.astype(BF16)

    def gqa_q(p):
        for c, y in enumerate(qk_norm(p, nrm_ref[2:3, :])):
            y = rope(y, cg_ref[...], sg_ref[...], 16) * qscale
            ysw = pltpu.roll(y, LANES // 2, 1)
            if c < 2:
                z_even, z_odd = jnp.where(lo, y, 0.0), jnp.where(lo, ysw, 0.0)
            else:
                z_even, z_odd = jnp.where(lo, 0.0, ysw), jnp.where(lo, 0.0, y)
            for hh, z in ((2 * c, z_even), (2 * c + 1, z_odd)):
                zt = z.T.astype(BF16)
                for qb in range(TM // LANES):
                    gqT_ref[qb, 0:LANES, hh * LANES:(hh + 1) * LANES] = zt[:, qb * LANES:(qb + 1) * LANES]
        gqT_ref[:, LANES:MXU_DIM, :] = jnp.zeros((TM // LANES, MXU_DIM - LANES, NQ), BF16)

    def gqa_kv(p):
        (y,) = qk_norm(p[:, 0:LANES], nrm_ref[3:4, :])
        gk_ref[:, 0:LANES] = rope(y, cg_ref[...], sg_ref[...], 16).astype(BF16)
        gk_ref[:, LANES:MXU_DIM] = jnp.zeros((TM, MXU_DIM - LANES), BF16)
        vt = p[:, LANES:2 * LANES].T.astype(BF16)
        gvT_ref[0:LANES, :] = vt
        gvT_ref[LANES:VROWS, :] = ones_rows
        for ku in range(TM // MXU_DIM):
            for g in range(2):
                gvT2_ref[ku, g] = vt[g * HEAD64:(g + 1) * HEAD64, ku * MXU_DIM:(ku + 1) * MXU_DIM]

    order = [gqa_q, diff_q, ret_k, diff_v, diff_k, gqa_kv, ret_q, ret_v, ret_g]
    in_w_in = [diff_q, diff_k, diff_v, ret_q, ret_k, ret_v, ret_g, gqa_q, gqa_kv]
    width = {f: BRANCH_WIDTH for f in in_w_in}
    width[gqa_kv] = 2 * LANES
    start = {f: sum(width[g] for g in in_w_in[:i]) for i, f in enumerate(in_w_in)}
    p_next = proj(start[order[0]], width[order[0]])
    for i, epilogue in enumerate(order):
        p_cur = p_next
        if i + 1 < len(order):
            p_next = proj(start[order[i + 1]], width[order[i + 1]])
        epilogue(p_cur)


def _proj_call(h, attn_norm, w_attn, gmat, nrm, tabs):
    T = h.shape[0]
    S = tabs[0].shape[0]
    nt = T // TM
    tab_spec = pl.BlockSpec((TM, LANES), lambda i: (i % (S // TM), 0))
    row = lambda w: pl.BlockSpec((TM, w), lambda i: (i, 0))
    out_shape = (
        jax.ShapeDtypeStruct((T, D_MODEL), BF16),
        jax.ShapeDtypeStruct((nt, DIFF_HEADS, LANES, 2 * TM), BF16),
        jax.ShapeDtypeStruct((T, BRANCH_WIDTH), BF16),
        jax.ShapeDtypeStruct((nt, DIFF_HEADS, VROWS, TM), BF16),
        jax.ShapeDtypeStruct((T, BRANCH_WIDTH), BF16),
        jax.ShapeDtypeStruct((nt, RET_HEADS, LANES, TM), BF16),
        jax.ShapeDtypeStruct((T, BRANCH_WIDTH), BF16),
        jax.ShapeDtypeStruct((T, BRANCH_WIDTH), BF16),
        jax.ShapeDtypeStruct((T // LANES, MXU_DIM, NQ), BF16),
        jax.ShapeDtypeStruct((T, MXU_DIM), BF16),
        jax.ShapeDtypeStruct((nt, VROWS, TM), BF16),
        jax.ShapeDtypeStruct((T // MXU_DIM, 2, HEAD64, MXU_DIM), BF16),
    )
    out_specs = (
        row(D_MODEL),
        pl.BlockSpec((None, DIFF_HEADS, LANES, 2 * TM), lambda i: (i, 0, 0, 0)),
        row(BRANCH_WIDTH),
        pl.BlockSpec((None, DIFF_HEADS, VROWS, TM), lambda i: (i, 0, 0, 0)),
        row(BRANCH_WIDTH),
        pl.BlockSpec((None, RET_HEADS, LANES, TM), lambda i: (i, 0, 0, 0)),
        row(BRANCH_WIDTH), row(BRANCH_WIDTH),
        pl.BlockSpec((TM // LANES, MXU_DIM, NQ), lambda i: (i, 0, 0)),
        row(MXU_DIM),
        pl.BlockSpec((None, VROWS, TM), lambda i: (i, 0, 0)),
        pl.BlockSpec((TM // MXU_DIM, 2, HEAD64, MXU_DIM), lambda i: (i, 0, 0, 0)),
    )
    in_specs = [row(D_MODEL), _resident((1, D_MODEL)), _resident(w_attn.shape),
                _resident(gmat.shape), _resident(nrm.shape)] + [tab_spec] * len(tabs)
    return pl.pallas_call(
        _proj_kernel, grid=(nt,), in_specs=in_specs, out_specs=out_specs,
        out_shape=out_shape, compiler_params=_cparams(1), name="proj",
    )(h, attn_norm, w_attn, gmat, nrm, *tabs)


def _attn_core(fast_ref, qT_ref, k_ref, vT_ref, m_ref, acc_ref):
    n_steps = k_ref.shape[0] // TK
    n_trips = n_steps // KV_UNROLL
    qT = jnp.concatenate([qT_ref[i] for i in range(Q_BLOCKS)], axis=-1)

    def kv(j):
        k = k_ref[pl.ds(pl.multiple_of(j * TK, TK), TK), :]
        per = TK // TM
        vT = jnp.concatenate([vT_ref[j * per + i] for i in range(per)], axis=-1)
        return k, vT

    @pl.when(fast_ref[0] == 1)
    def _():
        def body(jj, carry):
            pv = lsum = None
            for u in range(KV_UNROLL):
                k, vT = kv(jj * KV_UNROLL + u)
                s = jnp.dot(k, qT, preferred_element_type=F32)
                pf = jnp.exp2(s)
                ls = jnp.sum(pf, axis=0, keepdims=True)
                t = jnp.dot(vT[0:LANES], pf.astype(BF16), preferred_element_type=F32)
                pv, lsum = (t, ls) if pv is None else (pv + t, lsum + ls)
            if n_trips == 1:
                acc_ref[0:LANES, :] = pv
                acc_ref[LANES:LANES + 1, :] = lsum
            else:
                acc_ref[0:LANES, :] += pv
                acc_ref[LANES:LANES + 1, :] += lsum
            return carry

        if n_trips > 1:
            acc_ref[...] = jnp.zeros(acc_ref.shape, F32)
        lax.fori_loop(0, n_trips, body, 0)

    @pl.when(fast_ref[0] == 0)
    def _():
        acc_ref[...] = jnp.zeros(acc_ref.shape, F32)
        m_ref[...] = jnp.full(m_ref.shape, -1e30, F32)

        def body(j, carry):
            k, vT = kv(j)
            for c in range(NCOLS // NC):
                cs = slice(c * NC, (c + 1) * NC)
                s = jnp.dot(k, qT[:, cs], preferred_element_type=F32)
                m_old = m_ref[:, cs]
                m_new = jnp.maximum(m_old, jnp.max(s, axis=0, keepdims=True))
                alpha = jnp.exp2(m_old - m_new)
                p = jnp.exp2(s - m_new).astype(BF16)
                pv = jnp.dot(vT, p, preferred_element_type=F32)
                acc_ref[:, cs] = acc_ref[:, cs] * alpha + pv
                m_ref[:, cs] = m_new
            return carry

        lax.fori_loop(0, n_steps, body, 0)


def _diff_kernel(fast_ref, lam_ref, sub_ref, qT_ref, k_ref, vT_ref, o_ref, m_ref, acc_ref, *,
                 lambda_init):
    lv = lam_ref[...]
    lam = (jnp.exp(jnp.sum(lv[0:1] * lv[1:2], axis=-1, keepdims=True))
           - jnp.exp(jnp.sum(lv[2:3] * lv[3:4], axis=-1, keepdims=True)) + lambda_init)
    _attn_core(fast_ref, qT_ref, k_ref, vT_ref, m_ref, acc_ref)

    tq = NQ // 2
    for qb in range(Q_BLOCKS):
        c1 = slice(qb * NQ, qb * NQ + tq)
        c2 = slice(qb * NQ + tq, (qb + 1) * NQ)
        o1 = acc_ref[0:LANES, c1] / acc_ref[LANES:LANES + 1, c1]
        o2 = acc_ref[0:LANES, c2] / acc_ref[LANES:LANES + 1, c2]
        d = o1 - lam * o2
        ms = jnp.mean(d * d, axis=0, keepdims=True)
        dn = d * lax.rsqrt(ms + 1e-5)
        o_ref[qb * tq:(qb + 1) * tq, :] = (
            dn.T * (sub_ref[...] * (1.0 - lambda_init))).astype(o_ref.dtype)


_SMEM_SPEC = pl.BlockSpec(memory_space=pltpu.SMEM)
_ATTN_SCRATCH = [pltpu.VMEM((1, NCOLS), F32), pltpu.VMEM((VROWS, NCOLS), F32)]


def _diff_call(fast, lamv, subln, dqT, dk, dvT, B, S, lambda_init):
    T = dk.shape[0]
    tq = Q_BLOCKS * NQ // 2
    nq = S // tq
    kern = functools.partial(_diff_kernel, lambda_init=lambda_init)
    return pl.pallas_call(
        kern, grid=(B, DIFF_HEADS, nq),
        in_specs=[
            _SMEM_SPEC, _resident(lamv.shape), _resident(subln.shape),
            pl.BlockSpec((Q_BLOCKS, None, LANES, NQ), lambda b, h, i: (b * nq + i, h, 0, 0)),
            pl.BlockSpec((S, LANES), lambda b, h, i: (b, h)),
            pl.BlockSpec((S // TM, None, VROWS, TM), lambda b, h, i: (b, h, 0, 0)),
        ],
        out_specs=pl.BlockSpec((tq, LANES), lambda b, h, i: (b * nq + i, h)),
        out_shape=jax.ShapeDtypeStruct((T, BRANCH_WIDTH), BF16),
        scratch_shapes=_ATTN_SCRATCH,
        compiler_params=_cparams(3), name="diff_attn",
    )(fast, lamv, subln, dqT, dk, dvT)


def _gqa_kernel(fast_ref, qT_ref, k_ref, vT_ref, o_ref, m_ref, acc_ref):
    _attn_core(fast_ref, qT_ref, k_ref, vT_ref, m_ref, acc_ref)
    for qb, c in [(qb, c) for qb in range(Q_BLOCKS) for c in range(GQA_Q_HEADS // 2)]:
        g = (2 * c) // GQA_GROUP
        rows = slice(g * HEAD64, (g + 1) * HEAD64)
        parts = []
        for hh in (2 * c, 2 * c + 1):
            cols = slice(qb * NQ + hh * LANES, qb * NQ + (hh + 1) * LANES)
            parts.append(acc_ref[rows, cols] / acc_ref[LANES:LANES + 1, cols])
        o_ref[qb * LANES:(qb + 1) * LANES, c * LANES:(c + 1) * LANES] = (
            jnp.concatenate(parts, axis=0).T.astype(o_ref.dtype))


def _gqa_call(fast, gqT, gk, gvT, B, S):
    T = gk.shape[0]
    tq = Q_BLOCKS * NQ // GQA_Q_HEADS
    nq = S // tq
    return pl.pallas_call(
        _gqa_kernel, grid=(B, nq),
        in_specs=[
            _SMEM_SPEC,
            pl.BlockSpec((Q_BLOCKS, LANES, NQ), lambda b, i: (b * nq + i, 0, 0)),
            pl.BlockSpec((S, LANES), lambda b, i: (b, 0)),
            pl.BlockSpec((S // TM, VROWS, TM), lambda b, i: (b, 0, 0)),
        ],
        out_specs=pl.BlockSpec((tq, BRANCH_WIDTH), lambda b, i: (b * nq + i, 0)),
        out_shape=jax.ShapeDtypeStruct((T, BRANCH_WIDTH), BF16),
        scratch_shapes=_ATTN_SCRATCH,
        compiler_params=_cparams(2), name="gqa_attn",
    )(fast, gqT, gk, gvT)


S_REGIONS = (0, 64, 128)
O_REGIONS = ((192, 208), (224, 240))
MXU_Q_BLOCKS = len(O_REGIONS)
POP_LAG = 2
PV_LAG = 3
P_RING = 6
TRIP_SLOTS = 30


def _gqa_mxu_kernel(qT_ref, k_ref, vT_ref, o_ref, p_ref, l_ref):
    per_block = 2 * (k_ref.shape[0] // MXU_DIM)
    n_slots = MXU_Q_BLOCKS * per_block
    head = per_block % TRIP_SLOTS
    assert head > PV_LAG and TRIP_SLOTS % P_RING == 0

    @pl.when((pl.program_id(0) == 0) & (pl.program_id(1) == 0))
    def _():
        for mxu in range(2):
            for region in S_REGIONS:
                pltpu.matmul_pop(region, (MXU_DIM, MXU_DIM), F32, mxu)
            for region in sum(O_REGIONS, ()):
                pltpu.matmul_pop(region, (HEAD64, MXU_DIM), F32, mxu)

    l_ref[...] = jnp.zeros(l_ref.shape, F32)

    def q_tile(mxu, n, blk):
        c = 2 * mxu + n % 2
        return qT_ref[blk, :, c * MXU_DIM:(c + 1) * MXU_DIM]

    def o_region(n):
        return O_REGIONS[n // per_block][n % 2]

    def l_cols(mxu, n):
        c0 = (n // per_block) * NQ + (2 * mxu + n % 2) * MXU_DIM
        return slice(c0, c0 + MXU_DIM)

    def slot(n, unit, unit_v, next_blk):
        start = unit * MXU_DIM
        if not isinstance(unit, int):
            start = pl.multiple_of(start, MXU_DIM)
        k = k_ref[pl.ds(start, MXU_DIM), :]
        for mxu in range(2):
            if n >= PV_LAG:
                pltpu.matmul_push_rhs(p_ref[mxu, (n - PV_LAG) % P_RING], staging_register=1,
                                      mxu_index=mxu)
            pltpu.matmul_acc_lhs(S_REGIONS[n % 3], k, mxu_index=mxu, load_staged_rhs=0)
            pltpu.matmul_push_rhs(q_tile(mxu, n + 1, next_blk), staging_register=0, mxu_index=mxu)
            if n >= PV_LAG:
                pltpu.matmul_acc_lhs(o_region(n - PV_LAG), vT_ref[unit_v, mxu],
                                     mxu_index=mxu, load_staged_rhs=1)
        for mxu in range(2):
            if n >= POP_LAG:
                pop_scores(mxu, n - POP_LAG)

    def pop_scores(mxu, n):
        s = pltpu.matmul_pop(S_REGIONS[n % 3], (MXU_DIM, MXU_DIM), F32, mxu)
        pf = jnp.exp2(s)
        l_ref[:, l_cols(mxu, n)] += jnp.sum(pf, axis=0, keepdims=True)
        p_ref[mxu, n % P_RING] = pf.astype(BF16)

    def values_only(mxu, n, unit_v):
        pltpu.matmul_push_rhs(p_ref[mxu, n % P_RING], staging_register=1, mxu_index=mxu)
        pltpu.matmul_acc_lhs(o_region(n), vT_ref[unit_v, mxu], mxu_index=mxu, load_staged_rhs=1)

    def unit_of(n):
        return (n % per_block) // 2

    def block_of(n):
        return jnp.minimum(n // per_block, MXU_Q_BLOCKS - 1)

    for mxu in range(2):
        pltpu.matmul_push_rhs(q_tile(mxu, 0, 0), staging_register=0, mxu_index=mxu)
    for blk in range(MXU_Q_BLOCKS):
        first = blk * per_block
        for n in range(first, first + head):
            slot(n, unit_of(n), unit_of(max(n - PV_LAG, 0)), blk)

        def trip(t, carry, first=first + head):
            for i in range(TRIP_SLOTS):
                base = first + TRIP_SLOTS * t + i
                slot(first + i, unit_of(base), unit_of(base - PV_LAG), block_of(base + 1))
            return carry

        lax.fori_loop(0, per_block // TRIP_SLOTS, trip, 0)

    for mxu in range(2):
        pltpu.matmul_acc_lhs(S_REGIONS[n_slots % 3], jnp.zeros((16, MXU_DIM), BF16),
                             mxu_index=mxu, load_staged_rhs=0)
    for n in range(n_slots - POP_LAG, n_slots):
        for mxu in range(2):
            pop_scores(mxu, n)
    for mxu in range(2):
        pltpu.matmul_pop(S_REGIONS[n_slots % 3], (16, MXU_DIM), F32, mxu)
    for n in range(n_slots - PV_LAG, n_slots):
        for mxu in range(2):
            values_only(mxu, n, unit_of(n))

    tq = NQ // GQA_Q_HEADS
    for blk in range(MXU_Q_BLOCKS):
        for mxu in range(2):
            for cc in range(2):
                n = blk * per_block + cc
                c = 2 * mxu + cc
                o = pltpu.matmul_pop(o_region(n), (HEAD64, MXU_DIM), F32, mxu)
                o = o / l_ref[:, l_cols(mxu, n)]
                pair = jnp.concatenate([o[:, 0:LANES], o[:, LANES:2 * LANES]], axis=0)
                o_ref[blk * tq:(blk + 1) * tq, c * LANES:(c + 1) * LANES] = pair.T.astype(o_ref.dtype)


def _gqa_mxu_call(gqT, gk, gvT2, B, S):
    T = gk.shape[0]
    tq = MXU_Q_BLOCKS * NQ // GQA_Q_HEADS
    nq = S // tq
    return pl.pallas_call(
        _gqa_mxu_kernel, grid=(B, nq),
        in_specs=[
            pl.BlockSpec((MXU_Q_BLOCKS, MXU_DIM, NQ), lambda b, i: (b * nq + i, 0, 0)),
            pl.BlockSpec((S, MXU_DIM), lambda b, i: (b, 0)),
            pl.BlockSpec((S // MXU_DIM, 2, HEAD64, MXU_DIM), lambda b, i: (b, 0, 0, 0)),
        ],
        out_specs=pl.BlockSpec((tq, BRANCH_WIDTH), lambda b, i: (b * nq + i, 0)),
        out_shape=jax.ShapeDtypeStruct((T, BRANCH_WIDTH), BF16),
        scratch_shapes=[pltpu.VMEM((2, P_RING, MXU_DIM, MXU_DIM), BF16),
                        pltpu.VMEM((1, MXU_Q_BLOCKS * NQ), F32)],
        compiler_params=_cparams(2), name="gqa_attn_mxu",
    )(gqT, gk, gvT2)


def _ret_kernel(dec_ref, gain_ref, q_ref, kT_ref, v_ref, g_ref, o_ref, sf_ref, sb_ref):
    C = RET_CHUNK
    hd = pl.program_id(1)
    lf = -jnp.exp(jnp.full((C, C), dec_ref[0, hd], F32))
    lb = -jnp.exp(jnp.full((C, C), dec_ref[1, hd], F32))
    ii = lax.broadcasted_iota(jnp.int32, (C, C), 0).astype(F32)
    jj = lax.broadcasted_iota(jnp.int32, (C, C), 1).astype(F32)
    diff = ii - jj
    decay = jnp.where(diff >= 0, jnp.exp(jnp.maximum(diff, 0.0) * lf),
                      jnp.exp(jnp.maximum(-diff, 0.0) * lb))
    xi_f, zeta_f = jnp.exp((ii + 1.0) * lf), jnp.exp((C - 1.0 - ii) * lf)
    xi_b, zeta_b = jnp.exp((C - ii) * lb), jnp.exp(ii * lb)
    dec_f, dec_b = jnp.exp(C * lf), jnp.exp(C * lb)
    n_tiles = kT_ref.shape[0]
    per_tile = TM // C

    def rows(t, c):
        return pl.ds(pl.multiple_of(t * TM + c * C, C), C)

    def chunk_kv(t, c, zeta):
        vz = (v_ref[rows(t, c), :].astype(F32) * zeta).astype(BF16)
        return jnp.dot(kT_ref[t, :, c * C:(c + 1) * C], vz, preferred_element_type=F32)

    def scans(i, states):
        st_f, st_b = states
        fwd = [(i * RET_SCAN_TILES + t, c) for t in range(RET_SCAN_TILES) for c in range(per_tile)]
        bwd = [(n_tiles - 1 - t, per_tile - 1 - c) for t, c in fwd]
        kv_f = [chunk_kv(t, c, zeta_f) for t, c in fwd]
        kv_b = [chunk_kv(t, c, zeta_b) for t, c in bwd]
        for (tf, cf), kf, (tb, cb), kb in zip(fwd, kv_f, bwd, kv_b):
            sf_ref[tf * per_tile + cf] = st_f.astype(BF16)
            st_f = st_f * dec_f + kf
            sb_ref[tb * per_tile + cb] = st_b.astype(BF16)
            st_b = st_b * dec_b + kb
        return st_f, st_b

    zero = jnp.zeros((C, C), F32)
    lax.fori_loop(0, n_tiles // RET_SCAN_TILES, scans, (zero, zero))

    def outputs(tt, carry):
        chunks = [(tt * RET_OUT_TILES + i, c) for i in range(RET_OUT_TILES) for c in range(per_tile)]
        rs = [rows(t, c) for t, c in chunks]
        qs = [q_ref[r, :] for r in rs]
        a = [jnp.dot(q, kT_ref[t, :, c * C:(c + 1) * C], preferred_element_type=F32)
             for q, (t, c) in zip(qs, chunks)]
        cross = [jnp.dot(q, sf_ref[t * per_tile + c], preferred_element_type=F32) * xi_f
                 + jnp.dot(q, sb_ref[t * per_tile + c], preferred_element_type=F32) * xi_b
                 for q, (t, c) in zip(qs, chunks)]
        a = [(x * decay).astype(BF16) for x in a]
        o = [jnp.dot(x, v_ref[r, :], preferred_element_type=F32) + y
             for x, r, y in zip(a, rs, cross)]
        mu = [jnp.mean(x, axis=-1, keepdims=True) for x in o]
        oc = [x - m for x, m in zip(o, mu)]
        var = [jnp.mean(x * x, axis=-1, keepdims=True) for x in oc]
        for x, s2, r in zip(oc, var, rs):
            y = x * lax.rsqrt(s2 + 1e-5) * gain_ref[...]
            g = g_ref[r, :].astype(F32)
            o_ref[r, :] = (g / (1.0 + jnp.exp(-g)) * y).astype(o_ref.dtype)
        return carry

    lax.fori_loop(0, n_tiles // RET_OUT_TILES, outputs, 0)


def _ret_call(dec, gain, rq, rkT, rv, rg, B, S):
    T = rq.shape[0]
    blk = pl.BlockSpec((S, LANES), lambda b, h: (b, h))
    n_chunks = S // RET_CHUNK
    return pl.pallas_call(
        _ret_kernel, grid=(B, RET_HEADS),
        in_specs=[pl.BlockSpec(memory_space=pltpu.SMEM),
                  pl.BlockSpec((1, LANES), lambda b, h: (0, h)), blk,
                  pl.BlockSpec((S // TM, None, LANES, TM), lambda b, h: (b, h, 0, 0)), blk, blk],
        out_specs=blk,
        out_shape=jax.ShapeDtypeStruct((T, BRANCH_WIDTH), BF16),
        scratch_shapes=[pltpu.VMEM((n_chunks, RET_CHUNK, LANES), BF16),
                        pltpu.VMEM((n_chunks, RET_CHUNK, LANES), BF16)],
        compiler_params=_cparams(2), name="retention",
    )(dec, gain, rq, rkT, rv, rg)


def _merge_mlp_kernel(h_ref, u_ref, a_ref, r_ref, c_ref, wg_ref, wb_ref, wo_ref, mn_ref,
                      w1_ref, w2_ref, o_ref):
    u = u_ref[...]
    merged = None
    for n, br in enumerate((a_ref, r_ref, c_ref)):
        logits = jnp.dot(u, wg_ref[:, n * D_MODEL:(n + 1) * D_MODEL], preferred_element_type=F32)
        gate = 1.0 / (1.0 + jnp.exp(-logits))
        term = gate * jnp.dot(br[...], wb_ref[n], preferred_element_type=F32)
        merged = term if merged is None else merged + term
    merged = merged.astype(BF16)
    halves = [slice(0, TM // 2), slice(TM // 2, TM)]
    hns = [h_ref[r, :] + jnp.dot(merged[r], wo_ref[...], preferred_element_type=F32) for r in halves]
    u2s = []
    for hn in hns:
        ms = jnp.mean(hn * hn, axis=-1, keepdims=True)
        u2s.append((hn * lax.rsqrt(ms + NORM_EPS) * mn_ref[...]).astype(BF16))
    u2 = jnp.concatenate(u2s, axis=0)
    acc = jnp.concatenate(hns, axis=0)
    for c in range(D_FF // D_MODEL):
        cs = slice(c * D_MODEL, (c + 1) * D_MODEL)
        m = jnp.maximum(jnp.dot(u2, w1_ref[:, cs], preferred_element_type=F32), 0.0)
        acc = acc + jnp.dot((m * m).astype(BF16), w2_ref[cs, :], preferred_element_type=F32)
    o_ref[...] = acc


def _merge_mlp_call(h, u, a, r, c, wg, wb, wo, mlp_norm, w1, w2):
    T = h.shape[0]
    row = lambda w: pl.BlockSpec((TM, w), lambda i: (i, 0))
    return pl.pallas_call(
        _merge_mlp_kernel, grid=(T // TM,),
        in_specs=[row(D_MODEL), row(D_MODEL), row(BRANCH_WIDTH), row(BRANCH_WIDTH), row(BRANCH_WIDTH),
                  _resident(wg.shape), _resident(wb.shape), _resident(wo.shape),
                  _resident((1, D_MODEL)), _resident(w1.shape), _resident(w2.shape)],
        out_specs=row(D_MODEL),
        out_shape=jax.ShapeDtypeStruct((T, D_MODEL), F32),
        compiler_params=_cparams(1), name="merge_mlp",
    )(h, u, a, r, c, wg, wb, wo, mlp_norm, w1, w2)


def _rope_tables(S):
    pos = jnp.arange(S, dtype=F32)
    lane = jnp.arange(LANES)
    l64 = lane % HEAD64

    inv = 500000.0 ** (-jnp.arange(0, 16, 2, dtype=F32) / 16)
    ang = pos[:, None] * inv[None, :]
    cos, sin = jnp.cos(ang)[:, l64 % 8], jnp.sin(ang)[:, l64 % 8]
    cd = jnp.where(l64 < 16, cos, 1.0)
    sd = jnp.where(l64 < 8, -sin, jnp.where(l64 < 16, sin, 0.0))

    inv = 10000.0 ** (-jnp.arange(0, 32, 2, dtype=F32) / 32)
    row_pos = jnp.floor(pos / GRID_W)
    col_pos = pos - row_pos * GRID_W
    p2 = jnp.where((l64 < 32)[None, :], row_pos[:, None], col_pos[:, None])
    ang = p2 * inv[l64 % 16][None, :]
    cg = jnp.cos(ang)
    sg = jnp.where(l64 % 32 < 16, -jnp.sin(ang), jnp.sin(ang))

    inv = 10000.0 ** (-jnp.arange(0, 128, 2, dtype=F32) / 128)
    ang = pos[:, None] * inv[l64][None, :]
    cr = jnp.cos(ang)
    sr = jnp.where(lane < LANES // 2, -jnp.sin(ang), jnp.sin(ang))
    return (cd, sd, cg, sg, cr, sr)


def _fast_flag(q_gain, k_gain):
    bound = (HEAD64 ** 0.5 * LOG2E * 1.02) * jnp.max(jnp.abs(q_gain)) * jnp.max(jnp.abs(k_gain))
    return (bound <= FAST_SOFTMAX_BOUND).astype(jnp.int32).reshape(1)


def kernel(x, attn_norm, w_in, diff_q_norm, diff_k_norm, diff_lam_q1, diff_lam_k1, diff_lam_q2, diff_lam_k2, diff_subln, ret_decay_fwd, ret_decay_bwd, ret_group_norm, gqa_q_norm, gqa_k_norm, w_branch, w_out, mlp_norm, w_mlp_in, w_mlp_out):
    B, S, D = x.shape
    T = B * S
    tabs = _rope_tables(S)
    blk = jnp.arange(2 * LANES) // HEAD64
    gmat = jnp.where(blk[:, None] == blk[None, :], 1.0 / HEAD64, 0.0).astype(BF16)
    tile2 = lambda v: jnp.concatenate([v, v]).astype(F32)

    h = x.reshape(T, D)
    for l in range(DEPTH):
        lambda_init = 0.8 - 0.6 * math.exp(-0.3 * l)
        w_attn = w_in[l, :, :ATTN_COLS].astype(BF16)
        w_gate = w_in[l, :, ATTN_COLS:].astype(BF16)
        nrm = jnp.stack([tile2(diff_q_norm[l]), tile2(diff_k_norm[l]),
                         tile2(gqa_q_norm[l]), tile2(gqa_k_norm[l])])
        (u, dqT, dk, dvT, rq, rkT, rv, rg, gqT, gk, gvT, gvT2) = _proj_call(
            h, attn_norm[l].reshape(1, D).astype(F32), w_attn, gmat, nrm, tabs)

        lamv = jnp.stack([diff_lam_q1[l], diff_lam_k1[l], diff_lam_q2[l], diff_lam_k2[l]]).astype(F32)
        a = _diff_call(_fast_flag(diff_q_norm[l], diff_k_norm[l]), lamv,
                       diff_subln[l].reshape(1, LANES).astype(F32), dqT, dk, dvT, B, S, lambda_init)
        dec = jnp.stack([ret_decay_fwd[l], ret_decay_bwd[l]]).astype(F32)
        r = _ret_call(dec, ret_group_norm[l].reshape(1, BRANCH_WIDTH).astype(F32), rq, rkT, rv, rg, B, S)
        gqa_fast = _fast_flag(gqa_q_norm[l], gqa_k_norm[l])
        c = lax.cond(gqa_fast[0] == 1,
                     lambda ops: _gqa_mxu_call(ops[0], ops[1], ops[3], B, S),
                     lambda ops: _gqa_call(gqa_fast, ops[0], ops[1], ops[2], B, S),
                     (gqT, gk, gvT, gvT2))

        h = _merge_mlp_call(h, u, a, r, c, w_gate, w_branch[l].astype(BF16), w_out[l].astype(BF16),
                            mlp_norm[l].reshape(1, D).astype(F32),
                            w_mlp_in[l].astype(BF16), w_mlp_out[l].astype(BF16))
    return h.reshape(B, S, D)
```
